```python
import math
import jax, jax.numpy as jnp
from jax import lax
import numpy as np

D_MODEL = 4096
BATCH = 8
SEQ = 4096
DEPTH = 1

CHUNK = 64

FOX_HEADS = 16
FOX_HEAD_DIM = 128
FOX_WIDTH = FOX_HEADS * FOX_HEAD_DIM
Q_BLOCK = 128

S5_GROUP = 16
S5_GROUPS = 64
S5_WIDTH = S5_GROUP * S5_GROUPS
S5_STATE = 64

D_FF = ((8 * D_MODEL + 3 * 256 - 1) // (3 * 256)) * 256

N_BRANCHES = 2
RMS_EPS = 1e-6
MASK_VALUE = -1e30

COL_Q = FOX_WIDTH
COL_K = COL_Q + FOX_WIDTH
COL_V = COL_K + FOX_WIDTH
COL_F = COL_V + FOX_HEADS
COL_S5 = COL_F + S5_WIDTH
COL_GATE_FOX = COL_S5 + D_MODEL
IN_COLS = COL_GATE_FOX + D_MODEL

kernel_name = "fox_s5_gated_hybrid_block"


def _rmsnorm(x, g):
    x32 = x.astype(jnp.float32)
    y = x32 * lax.rsqrt(jnp.mean(x32 * x32, axis=-1, keepdims=True) + RMS_EPS)
    return (y * g.astype(jnp.float32)).astype(x.dtype)


def _forgetting_attention(q, k, v, f_logit, q_norm, k_norm):
    B, S, _ = q.shape
    n_blk = S // Q_BLOCK
    q = _rmsnorm(q.reshape(B, S, FOX_HEADS, FOX_HEAD_DIM), q_norm) * (1.0 / math.sqrt(FOX_HEAD_DIM))
    k = _rmsnorm(k.reshape(B, S, FOX_HEADS, FOX_HEAD_DIM), k_norm)
    v = v.reshape(B, S, FOX_HEADS, FOX_HEAD_DIM)
    qh = q.transpose(0, 2, 1, 3)
    kh = k.transpose(0, 2, 1, 3)
    vh = v.transpose(0, 2, 1, 3)
    log_f = jax.nn.log_sigmoid(f_logit.astype(jnp.float32))
    F = jnp.cumsum(log_f, axis=1).transpose(0, 2, 1)
    q_blocks = qh.reshape(B, FOX_HEADS, n_blk, Q_BLOCK, FOX_HEAD_DIM).transpose(2, 0, 1, 3, 4)
    f_blocks = F.reshape(B, FOX_HEADS, n_blk, Q_BLOCK).transpose(2, 0, 1, 3)
    k_pos = jnp.arange(S)

    def one_block(args):
        qb, fb, blk = args
        q_pos = blk * Q_BLOCK + jnp.arange(Q_BLOCK)
        logits = jnp.einsum('bhqd,bhkd->bhqk', qb, kh).astype(jnp.float32)
        logits = logits + (fb[..., :, None] - F[:, :, None, :])
        logits = jnp.where(k_pos[None, :] <= q_pos[:, None], logits, MASK_VALUE)
        p = jax.nn.softmax(logits, axis=-1)
        return jnp.einsum('bhqk,bhkd->bhqd', p.astype(vh.dtype), vh)

    out = lax.map(one_block, (q_blocks, f_blocks, jnp.arange(n_blk)))
    return out.transpose(1, 0, 3, 2, 4).reshape(B, S, FOX_WIDTH)


def _cplx_scan_op(e1, e2):
    a1r, a1i, b1r, b1i = e1
    a2r, a2i, b2r, b2i = e2
    ar = a2r * a1r - a2i * a1i
    ai = a2r * a1i + a2i * a1r
    br = a2r * b1r - a2i * b1i + b2r
    bi = a2r * b1i + a2i * b1r + b2i
    return (ar, ai, br, bi)


def _s5(s5_in, lam_re, lam_im, log_step, b_re, b_im, c_re, c_im, d, w_glu, b_glu):
    B, S, _ = s5_in.shape
    u = s5_in.reshape(B, S, S5_GROUPS, S5_GROUP).astype(jnp.float32)
    lr = lam_re.astype(jnp.float32)
    li = lam_im.astype(jnp.float32)
    dt = jnp.exp(log_step.astype(jnp.float32))[:, None]
    mag = jnp.exp(lr * dt)
    lb_re = mag * jnp.cos(li * dt)
    lb_im = mag * jnp.sin(li * dt)
    denom = lr * lr + li * li
    num_re = lb_re - 1.0
    fac_re = (num_re * lr + lb_im * li) / denom
    fac_im = (lb_im * lr - num_re * li) / denom
    br = b_re.astype(jnp.float32)
    bi = b_im.astype(jnp.float32)
    bb_re = fac_re[..., None] * br - fac_im[..., None] * bi
    bb_im = fac_re[..., None] * bi + fac_im[..., None] * br
    bu_re = jnp.einsum('bsgi,gpi->bsgp', u, bb_re)
    bu_im = jnp.einsum('bsgi,gpi->bsgp', u, bb_im)
    a_re = jnp.broadcast_to(lb_re[None, None], (1, S, S5_GROUPS, S5_STATE))
    a_im = jnp.broadcast_to(lb_im[None, None], (1, S, S5_GROUPS, S5_STATE))
    _, _, x_re, x_im = lax.associative_scan(_cplx_scan_op, (a_re, a_im, bu_re, bu_im), axis=1)
    y = (jnp.einsum('bsgp,gip->bsgi', x_re, c_re.astype(jnp.float32))
         - jnp.einsum('bsgp,gip->bsgi', x_im, c_im.astype(jnp.float32))
         + d.astype(jnp.float32) * u)
    y = y.reshape(B, S, S5_WIDTH).astype(s5_in.dtype)
    g = jax.nn.gelu(y)
    return g * jax.nn.sigmoid(g @ w_glu + b_glu)


def _layer(x, g_mix, w_in, b_fgate, b_gates, q_norm, k_norm,
           s5_lambda_re, s5_lambda_im, s5_log_step, s5_b_re, s5_b_im, s5_c_re, s5_c_im, s5_d,
           w_glu, b_glu, w_proj_fox, w_proj_s5, w_out, g_ffn, w_gate_up, w_down):
    u = _rmsnorm(x, g_mix)
    z = u @ w_in
    q, k, v, f_logit, s5_in, gate_logits = jnp.split(
        z, [COL_Q, COL_K, COL_V, COL_F, COL_S5], axis=-1)
    gates = jax.nn.sigmoid(gate_logits.astype(jnp.float32) + b_gates.astype(jnp.float32))
    gate_fox, gate_s5 = jnp.split(gates, [D_MODEL], axis=-1)
    attn = _forgetting_attention(q, k, v, f_logit + b_fgate, q_norm, k_norm)
    ssm = _s5(s5_in, s5_lambda_re, s5_lambda_im, s5_log_step, s5_b_re, s5_b_im,
              s5_c_re, s5_c_im, s5_d, w_glu, b_glu)
    merged = (gate_fox * (attn @ w_proj_fox) + gate_s5 * (ssm @ w_proj_s5)).astype(x.dtype)
    h = x + merged @ w_out
    hn = _rmsnorm(h, g_ffn)
    gate, up = jnp.split(hn @ w_gate_up, [D_FF], axis=-1)
    return h + (jax.nn.silu(gate) * up) @ w_down


def _normal(key, shape, scale):
    return jax.random.normal(key, shape, jnp.float32) * scale


def _fwd_setup_inputs(seed: int = 0) -> dict:
    key = jax.random.key(seed)
    ks = jax.random.split(key, 24)
    L = DEPTH
    n_idx = jnp.arange(S5_STATE, dtype=jnp.float32)
    lam_re = -0.5 + _normal(ks[8], (L, S5_GROUPS, S5_STATE), 0.01)
    lam_im = math.pi * n_idx[None, None, :] + _normal(ks[9], (L, S5_GROUPS, S5_STATE), 0.01)
    log_step = jax.random.uniform(ks[10], (L, S5_GROUPS), jnp.float32,
                                  math.log(1e-3), math.log(1e-1))
    return {
        "x": _normal(ks[0], (BATCH, SEQ, D_MODEL), 1.0),
        "g_mix": 1.0 + _normal(ks[1], (L, D_MODEL), 0.02),
        "w_in": _normal(ks[2], (L, D_MODEL, IN_COLS), D_MODEL ** -0.5),
        "b_fgate": 3.0 + _normal(ks[3], (L, FOX_HEADS), 0.5),
        "b_gates": _normal(ks[4], (L, N_BRANCHES * D_MODEL), 0.02),
        "q_norm": 1.0 + _normal(ks[5], (L, FOX_HEAD_DIM), 0.02),
        "k_norm": 1.0 + _normal(ks[6], (L, FOX_HEAD_DIM), 0.02),
        "s5_lambda_re": lam_re,
        "s5_lambda_im": lam_im,
        "s5_log_step": log_step,
        "s5_b_re": _normal(ks[11], (L, S5_GROUPS, S5_STATE, S5_GROUP), (2 * S5_GROUP) ** -0.5),
        "s5_b_im": _normal(ks[12], (L, S5_GROUPS, S5_STATE, S5_GROUP), (2 * S5_GROUP) ** -0.5),
        "s5_c_re": _normal(ks[13], (L, S5_GROUPS, S5_GROUP, S5_STATE), S5_STATE ** -0.5),
        "s5_c_im": _normal(ks[14], (L, S5_GROUPS, S5_GROUP, S5_STATE), S5_STATE ** -0.5),
        "s5_d": _normal(ks[15], (L, S5_GROUPS, S5_GROUP), 1.0),
        "w_glu": _normal(ks[16], (L, S5_WIDTH, S5_WIDTH), S5_WIDTH ** -0.5),
        "b_glu": _normal(ks[17], (L, S5_WIDTH), 0.02),
        "w_proj_fox": _normal(ks[18], (L, FOX_WIDTH, D_MODEL), FOX_WIDTH ** -0.5),
        "w_proj_s5": _normal(ks[19], (L, S5_WIDTH, D_MODEL), S5_WIDTH ** -0.5),
        "w_out": _normal(ks[20], (L, D_MODEL, D_MODEL), D_MODEL ** -0.5),
        "g_ffn": 1.0 + _normal(ks[21], (L, D_MODEL), 0.02),
        "w_gate_up": _normal(ks[22], (L, D_MODEL, 2 * D_FF), D_MODEL ** -0.5),
        "w_down": _normal(ks[23], (L, D_FF, D_MODEL), D_FF ** -0.5),
    }


def _fwd_reference(x, g_mix, w_in, b_fgate, b_gates, q_norm, k_norm,
              s5_lambda_re, s5_lambda_im, s5_log_step, s5_b_re, s5_b_im, s5_c_re, s5_c_im, s5_d,
              w_glu, b_glu, w_proj_fox, w_proj_s5, w_out, g_ffn, w_gate_up, w_down):
    for l in range(DEPTH):
        x = _layer(x, g_mix[l], w_in[l], b_fgate[l], b_gates[l], q_norm[l], k_norm[l],
                   s5_lambda_re[l], s5_lambda_im[l], s5_log_step[l], s5_b_re[l], s5_b_im[l],
                   s5_c_re[l], s5_c_im[l], s5_d[l], w_glu[l], b_glu[l],
                   w_proj_fox[l], w_proj_s5[l], w_out[l], g_ffn[l], w_gate_up[l], w_down[l])
    return x


import jax as _jax
import jax.numpy as _jnp

TWIN_FORMAT = 'train_step'
FWD_PARAMS = ['x', 'g_mix', 'w_in', 'b_fgate', 'b_gates', 'q_norm', 'k_norm', 's5_lambda_re', 's5_lambda_im', 's5_log_step', 's5_b_re', 's5_b_im', 's5_c_re', 's5_c_im', 's5_d', 'w_glu', 'b_glu', 'w_proj_fox', 'w_proj_s5', 'w_out', 'g_ffn', 'w_gate_up', 'w_down']
TWIN_WEIGHTS = ['g_mix', 'w_in', 'b_fgate', 'b_gates', 'q_norm', 'k_norm', 's5_lambda_re', 's5_lambda_im', 's5_log_step', 's5_b_re', 's5_b_im', 's5_c_re', 's5_c_im', 's5_d', 'w_glu', 'b_glu', 'w_proj_fox', 'w_proj_s5', 'w_out', 'g_ffn', 'w_gate_up', 'w_down']
TWIN_DIFF_INPUT = 'x'
TWIN_INPUTS = ['x', 'g_mix', 'w_in', 'b_fgate', 'b_gates', 'q_norm', 'k_norm', 's5_lambda_re', 's5_lambda_im', 's5_log_step', 's5_b_re', 's5_b_im', 's5_c_re', 's5_c_im', 's5_d', 'w_glu', 'b_glu', 'w_proj_fox', 'w_proj_s5', 'w_out', 'g_ffn', 'w_gate_up', 'w_down', 'loss_target', 'm_g_mix', 'm_w_in', 'm_b_fgate', 'm_b_gates', 'm_q_norm', 'm_k_norm', 'm_s5_lambda_re', 'm_s5_lambda_im', 'm_s5_log_step', 'm_s5_b_re', 'm_s5_b_im', 'm_s5_c_re', 'm_s5_c_im', 'm_s5_d', 'm_w_glu', 'm_b_glu', 'm_w_proj_fox', 'm_w_proj_s5', 'm_w_out', 'm_g_ffn', 'm_w_gate_up', 'm_w_down', 'v_g_mix', 'v_w_in', 'v_b_fgate', 'v_b_gates', 'v_q_norm', 'v_k_norm', 'v_s5_lambda_re', 'v_s5_lambda_im', 'v_s5_log_step', 'v_s5_b_re', 'v_s5_b_im', 'v_s5_c_re', 'v_s5_c_im', 'v_s5_d', 'v_w_glu', 'v_b_glu', 'v_w_proj_fox', 'v_w_proj_s5', 'v_w_out', 'v_g_ffn', 'v_w_gate_up', 'v_w_down']
TWIN_OUTPUTS = ['loss', 'grad_x', 'grad_g_mix', 'grad_w_in', 'grad_b_fgate', 'grad_b_gates', 'grad_q_norm', 'grad_k_norm', 'grad_s5_lambda_re', 'grad_s5_lambda_im', 'grad_s5_log_step', 'grad_s5_b_re', 'grad_s5_b_im', 'grad_s5_c_re', 'grad_s5_c_im', 'grad_s5_d', 'grad_w_glu', 'grad_b_glu', 'grad_w_proj_fox', 'grad_w_proj_s5', 'grad_w_out', 'grad_g_ffn', 'grad_w_gate_up', 'grad_w_down', 'delta_g_mix', 'delta_w_in', 'delta_b_fgate', 'delta_b_gates', 'delta_q_norm', 'delta_k_norm', 'delta_s5_lambda_re', 'delta_s5_lambda_im', 'delta_s5_log_step', 'delta_s5_b_re', 'delta_s5_b_im', 'delta_s5_c_re', 'delta_s5_c_im', 'delta_s5_d', 'delta_w_glu', 'delta_b_glu', 'delta_w_proj_fox', 'delta_w_proj_s5', 'delta_w_out', 'delta_g_ffn', 'delta_w_gate_up', 'delta_w_down', 'new_m_g_mix', 'new_m_w_in', 'new_m_b_fgate', 'new_m_b_gates', 'new_m_q_norm', 'new_m_k_norm', 'new_m_s5_lambda_re', 'new_m_s5_lambda_im', 'new_m_s5_log_step', 'new_m_s5_b_re', 'new_m_s5_b_im', 'new_m_s5_c_re', 'new_m_s5_c_im', 'new_m_s5_d', 'new_m_w_glu', 'new_m_b_glu', 'new_m_w_proj_fox', 'new_m_w_proj_s5', 'new_m_w_out', 'new_m_g_ffn', 'new_m_w_gate_up', 'new_m_w_down', 'new_v_g_mix', 'new_v_w_in', 'new_v_b_fgate', 'new_v_b_gates', 'new_v_q_norm', 'new_v_k_norm', 'new_v_s5_lambda_re', 'new_v_s5_lambda_im', 'new_v_s5_log_step', 'new_v_s5_b_re', 'new_v_s5_b_im', 'new_v_s5_c_re', 'new_v_s5_c_im', 'new_v_s5_d', 'new_v_w_glu', 'new_v_b_glu', 'new_v_w_proj_fox', 'new_v_w_proj_s5', 'new_v_w_out', 'new_v_g_ffn', 'new_v_w_gate_up', 'new_v_w_down']
TWIN_LEAF_KINDS = {'loss': 'loss', 'grad_x': 'grad_x', 'grad_g_mix': 'grad_w', 'grad_w_in': 'grad_w', 'grad_b_fgate': 'grad_w', 'grad_b_gates': 'grad_w', 'grad_q_norm': 'grad_w', 'grad_k_norm': 'grad_w', 'grad_s5_lambda_re': 'grad_w', 'grad_s5_lambda_im': 'grad_w', 'grad_s5_log_step': 'grad_w', 'grad_s5_b_re': 'grad_w', 'grad_s5_b_im': 'grad_w', 'grad_s5_c_re': 'grad_w', 'grad_s5_c_im': 'grad_w', 'grad_s5_d': 'grad_w', 'grad_w_glu': 'grad_w', 'grad_b_glu': 'grad_w', 'grad_w_proj_fox': 'grad_w', 'grad_w_proj_s5': 'grad_w', 'grad_w_out': 'grad_w', 'grad_g_ffn': 'grad_w', 'grad_w_gate_up': 'grad_w', 'grad_w_down': 'grad_w', 'delta_g_mix': 'delta_w', 'delta_w_in': 'delta_w', 'delta_b_fgate': 'delta_w', 'delta_b_gates': 'delta_w', 'delta_q_norm': 'delta_w', 'delta_k_norm': 'delta_w', 'delta_s5_lambda_re': 'delta_w', 'delta_s5_lambda_im': 'delta_w', 'delta_s5_log_step': 'delta_w', 'delta_s5_b_re': 'delta_w', 'delta_s5_b_im': 'delta_w', 'delta_s5_c_re': 'delta_w', 'delta_s5_c_im': 'delta_w', 'delta_s5_d': 'delta_w', 'delta_w_glu': 'delta_w', 'delta_b_glu': 'delta_w', 'delta_w_proj_fox': 'delta_w', 'delta_w_proj_s5': 'delta_w', 'delta_w_out': 'delta_w', 'delta_g_ffn': 'delta_w', 'delta_w_gate_up': 'delta_w', 'delta_w_down': 'delta_w', 'new_m_g_mix': 'new_m', 'new_m_w_in': 'new_m', 'new_m_b_fgate': 'new_m', 'new_m_b_gates': 'new_m', 'new_m_q_norm': 'new_m', 'new_m_k_norm': 'new_m', 'new_m_s5_lambda_re': 'new_m', 'new_m_s5_lambda_im': 'new_m', 'new_m_s5_log_step': 'new_m', 'new_m_s5_b_re': 'new_m', 'new_m_s5_b_im': 'new_m', 'new_m_s5_c_re': 'new_m', 'new_m_s5_c_im': 'new_m', 'new_m_s5_d': 'new_m', 'new_m_w_glu': 'new_m', 'new_m_b_glu': 'new_m', 'new_m_w_proj_fox': 'new_m', 'new_m_w_proj_s5': 'new_m', 'new_m_w_out': 'new_m', 'new_m_g_ffn': 'new_m', 'new_m_w_gate_up': 'new_m', 'new_m_w_down': 'new_m', 'new_v_g_mix': 'new_v', 'new_v_w_in': 'new_v', 'new_v_b_fgate': 'new_v', 'new_v_b_gates': 'new_v', 'new_v_q_norm': 'new_v', 'new_v_k_norm': 'new_v', 'new_v_s5_lambda_re': 'new_v', 'new_v_s5_lambda_im': 'new_v', 'new_v_s5_log_step': 'new_v', 'new_v_s5_b_re': 'new_v', 'new_v_s5_b_im': 'new_v', 'new_v_s5_c_re': 'new_v', 'new_v_s5_c_im': 'new_v', 'new_v_s5_d': 'new_v', 'new_v_w_glu': 'new_v', 'new_v_b_glu': 'new_v', 'new_v_w_proj_fox': 'new_v', 'new_v_w_proj_s5': 'new_v', 'new_v_w_out': 'new_v', 'new_v_g_ffn': 'new_v', 'new_v_w_gate_up': 'new_v', 'new_v_w_down': 'new_v'}


def _forward(args):
    return _fwd_reference(*[args[k] for k in FWD_PARAMS])


def _output_shape():
    out = _jax.eval_shape(lambda: _forward(_fwd_setup_inputs(0)))
    return out.shape, out.dtype

N_MICROBATCH = 1
ADAM_LR = 0.001
ADAM_B1 = 0.9
ADAM_B2 = 0.999
ADAM_EPS = 1e-08
ADAM_WD = 0.01
ADAM_STEP = 10
PER_EXAMPLE_BATCH_AXIS = {'x': 0, 'loss_target': 0}
SHARED_INPUTS = []
_WEIGHT_DTYPES = {'g_mix': _jnp.float32, 'w_in': _jnp.float32, 'b_fgate': _jnp.float32, 'b_gates': _jnp.float32, 'q_norm': _jnp.float32, 'k_norm': _jnp.float32, 's5_lambda_re': _jnp.float32, 's5_lambda_im': _jnp.float32, 's5_log_step': _jnp.float32, 's5_b_re': _jnp.float32, 's5_b_im': _jnp.float32, 's5_c_re': _jnp.float32, 's5_c_im': _jnp.float32, 's5_d': _jnp.float32, 'w_glu': _jnp.float32, 'b_glu': _jnp.float32, 'w_proj_fox': _jnp.float32, 'w_proj_s5': _jnp.float32, 'w_out': _jnp.float32, 'g_ffn': _jnp.float32, 'w_gate_up': _jnp.float32, 'w_down': _jnp.float32}
MOMENT_SCALE = {'g_mix': 6.087992e-01, 'w_in': 2.136947e-02, 'b_fgate': 1.845623e+01, 'b_gates': 8.963850e-02, 'q_norm': 3.091974e+00, 'k_norm': 3.086741e+00, 's5_lambda_re': 4.869616e-03, 's5_lambda_im': 3.008444e-03, 's5_log_step': 2.998779e+00, 's5_b_re': 3.265983e-03, 's5_b_im': 3.158166e-03, 's5_c_re': 4.643541e-03, 's5_c_im': 4.222943e-03, 's5_d': 1.474218e+00, 'w_glu': 3.131131e-01, 'b_glu': 9.123849e-01, 'w_proj_fox': 1.807168e-02, 'w_proj_s5': 1.209193e-01, 'w_out': 9.444548e-02, 'g_ffn': 6.159070e+00, 'w_gate_up': 4.236388e-02, 'w_down': 4.655639e-02}


def _to_microbatches(a, axis):
    t = _jnp.moveaxis(a, axis, 0)
    t = t.reshape((N_MICROBATCH, t.shape[0] // N_MICROBATCH) + t.shape[1:])
    return _jnp.moveaxis(t, 1, axis + 1)


def setup_inputs(seed: int = 0) -> dict:
    inp = _fwd_setup_inputs(seed)
    key = _jax.random.fold_in(_jax.random.key(seed), 7919)
    shape, _ = _output_shape()
    out = dict(inp)
    out["loss_target"] = _jax.random.normal(_jax.random.fold_in(key, 0), shape, _jnp.float32)
    for i, name in enumerate(TWIN_WEIGHTS):
        w = inp[name].astype(_jnp.float32)
        if MOMENT_SCALE is None:
            s = _jnp.sqrt(_jnp.mean(_jnp.square(w)) + 1e-30)
        else:
            s = MOMENT_SCALE[name]
        km, kv = _jax.random.split(_jax.random.fold_in(key, i + 1))
        out[name] = w
        out["m_" + name] = s * _jax.random.normal(km, w.shape, _jnp.float32)
        out["v_" + name] = (s * s) * _jax.random.uniform(kv, w.shape, _jnp.float32, 0.5, 1.5)
    if N_MICROBATCH > 1:
        for name, axis in PER_EXAMPLE_BATCH_AXIS.items():
            out[name] = _to_microbatches(out[name], axis)
    return {'x': out['x'], 'g_mix': out['g_mix'], 'w_in': out['w_in'], 'b_fgate': out['b_fgate'], 'b_gates': out['b_gates'], 'q_norm': out['q_norm'], 'k_norm': out['k_norm'], 's5_lambda_re': out['s5_lambda_re'], 's5_lambda_im': out['s5_lambda_im'], 's5_log_step': out['s5_log_step'], 's5_b_re': out['s5_b_re'], 's5_b_im': out['s5_b_im'], 's5_c_re': out['s5_c_re'], 's5_c_im': out['s5_c_im'], 's5_d': out['s5_d'], 'w_glu': out['w_glu'], 'b_glu': out['b_glu'], 'w_proj_fox': out['w_proj_fox'], 'w_proj_s5': out['w_proj_s5'], 'w_out': out['w_out'], 'g_ffn': out['g_ffn'], 'w_gate_up': out['w_gate_up'], 'w_down': out['w_down'], 'loss_target': out['loss_target'], 'm_g_mix': out['m_g_mix'], 'm_w_in': out['m_w_in'], 'm_b_fgate': out['m_b_fgate'], 'm_b_gates': out['m_b_gates'], 'm_q_norm': out['m_q_norm'], 'm_k_norm': out['m_k_norm'], 'm_s5_lambda_re': out['m_s5_lambda_re'], 'm_s5_lambda_im': out['m_s5_lambda_im'], 'm_s5_log_step': out['m_s5_log_step'], 'm_s5_b_re': out['m_s5_b_re'], 'm_s5_b_im': out['m_s5_b_im'], 'm_s5_c_re': out['m_s5_c_re'], 'm_s5_c_im': out['m_s5_c_im'], 'm_s5_d': out['m_s5_d'], 'm_w_glu': out['m_w_glu'], 'm_b_glu': out['m_b_glu'], 'm_w_proj_fox': out['m_w_proj_fox'], 'm_w_proj_s5': out['m_w_proj_s5'], 'm_w_out': out['m_w_out'], 'm_g_ffn': out['m_g_ffn'], 'm_w_gate_up': out['m_w_gate_up'], 'm_w_down': out['m_w_down'], 'v_g_mix': out['v_g_mix'], 'v_w_in': out['v_w_in'], 'v_b_fgate': out['v_b_fgate'], 'v_b_gates': out['v_b_gates'], 'v_q_norm': out['v_q_norm'], 'v_k_norm': out['v_k_norm'], 'v_s5_lambda_re': out['v_s5_lambda_re'], 'v_s5_lambda_im': out['v_s5_lambda_im'], 'v_s5_log_step': out['v_s5_log_step'], 'v_s5_b_re': out['v_s5_b_re'], 'v_s5_b_im': out['v_s5_b_im'], 'v_s5_c_re': out['v_s5_c_re'], 'v_s5_c_im': out['v_s5_c_im'], 'v_s5_d': out['v_s5_d'], 'v_w_glu': out['v_w_glu'], 'v_b_glu': out['v_b_glu'], 'v_w_proj_fox': out['v_w_proj_fox'], 'v_w_proj_s5': out['v_w_proj_s5'], 'v_w_out': out['v_w_out'], 'v_g_ffn': out['v_g_ffn'], 'v_w_gate_up': out['v_w_gate_up'], 'v_w_down': out['v_w_down']}


def _loss(weights, diff, rest, loss_target):
    with _jax.named_scope("forward"):
        args = {**rest, TWIN_DIFF_INPUT: diff, **{k: w.astype(_WEIGHT_DTYPES[k]) for k, w in weights.items()}}
        y = _forward(args)
    with _jax.named_scope("loss_head"):
        err = _jnp.square(y.astype(_jnp.float32) - loss_target)
        return 0.5 * _jnp.sum(_jnp.mean(err, axis=-1)) if err.ndim else 0.5 * err


def _adamw(w, g, m, v):
    m = ADAM_B1 * m + (1.0 - ADAM_B1) * g
    v = ADAM_B2 * v + (1.0 - ADAM_B2) * _jnp.square(g)
    m_hat = m / (1.0 - ADAM_B1 ** ADAM_STEP)
    v_hat = v / (1.0 - ADAM_B2 ** ADAM_STEP)
    delta = -ADAM_LR * (m_hat / (_jnp.sqrt(v_hat) + ADAM_EPS) + ADAM_WD * w)
    return delta, m, v


def reference(x, g_mix, w_in, b_fgate, b_gates, q_norm, k_norm, s5_lambda_re, s5_lambda_im, s5_log_step, s5_b_re, s5_b_im, s5_c_re, s5_c_im, s5_d, w_glu, b_glu, w_proj_fox, w_proj_s5, w_out, g_ffn, w_gate_up, w_down, loss_target, m_g_mix, m_w_in, m_b_fgate, m_b_gates, m_q_norm, m_k_norm, m_s5_lambda_re, m_s5_lambda_im, m_s5_log_step, m_s5_b_re, m_s5_b_im, m_s5_c_re, m_s5_c_im, m_s5_d, m_w_glu, m_b_glu, m_w_proj_fox, m_w_proj_s5, m_w_out, m_g_ffn, m_w_gate_up, m_w_down, v_g_mix, v_w_in, v_b_fgate, v_b_gates, v_q_norm, v_k_norm, v_s5_lambda_re, v_s5_lambda_im, v_s5_log_step, v_s5_b_re, v_s5_b_im, v_s5_c_re, v_s5_c_im, v_s5_d, v_w_glu, v_b_glu, v_w_proj_fox, v_w_proj_s5, v_w_out, v_g_ffn, v_w_gate_up, v_w_down):
    given = dict(x=x, g_mix=g_mix, w_in=w_in, b_fgate=b_fgate, b_gates=b_gates, q_norm=q_norm, k_norm=k_norm, s5_lambda_re=s5_lambda_re, s5_lambda_im=s5_lambda_im, s5_log_step=s5_log_step, s5_b_re=s5_b_re, s5_b_im=s5_b_im, s5_c_re=s5_c_re, s5_c_im=s5_c_im, s5_d=s5_d, w_glu=w_glu, b_glu=b_glu, w_proj_fox=w_proj_fox, w_proj_s5=w_proj_s5, w_out=w_out, g_ffn=g_ffn, w_gate_up=w_gate_up, w_down=w_down, loss_target=loss_target, m_g_mix=m_g_mix, m_w_in=m_w_in, m_b_fgate=m_b_fgate, m_b_gates=m_b_gates, m_q_norm=m_q_norm, m_k_norm=m_k_norm, m_s5_lambda_re=m_s5_lambda_re, m_s5_lambda_im=m_s5_lambda_im, m_s5_log_step=m_s5_log_step, m_s5_b_re=m_s5_b_re, m_s5_b_im=m_s5_b_im, m_s5_c_re=m_s5_c_re, m_s5_c_im=m_s5_c_im, m_s5_d=m_s5_d, m_w_glu=m_w_glu, m_b_glu=m_b_glu, m_w_proj_fox=m_w_proj_fox, m_w_proj_s5=m_w_proj_s5, m_w_out=m_w_out, m_g_ffn=m_g_ffn, m_w_gate_up=m_w_gate_up, m_w_down=m_w_down, v_g_mix=v_g_mix, v_w_in=v_w_in, v_b_fgate=v_b_fgate, v_b_gates=v_b_gates, v_q_norm=v_q_norm, v_k_norm=v_k_norm, v_s5_lambda_re=v_s5_lambda_re, v_s5_lambda_im=v_s5_lambda_im, v_s5_log_step=v_s5_log_step, v_s5_b_re=v_s5_b_re, v_s5_b_im=v_s5_b_im, v_s5_c_re=v_s5_c_re, v_s5_c_im=v_s5_c_im, v_s5_d=v_s5_d, v_w_glu=v_w_glu, v_b_glu=v_b_glu, v_w_proj_fox=v_w_proj_fox, v_w_proj_s5=v_w_proj_s5, v_w_out=v_w_out, v_g_ffn=v_g_ffn, v_w_gate_up=v_w_gate_up, v_w_down=v_w_down)
    weights = {n: given[n] for n in TWIN_WEIGHTS}
    shared = {n: given[n] for n in SHARED_INPUTS}
    per_example = {n: given[n] for n in ['x']}
    grad_fn = _jax.value_and_grad(_loss, argnums=(0, 1))

    def one_microbatch(ex, loss_target):
        ex = dict(ex)
        diff = ex.pop(TWIN_DIFF_INPUT)
        return grad_fn(weights, diff, {**shared, **ex}, loss_target)

    if N_MICROBATCH == 1:
        loss, (grad_w, grad_x) = one_microbatch(per_example, given["loss_target"])
    else:
        def body(carry, xs):
            loss_sum, grad_sum = carry
            l_k, (gw_k, gx_k) = one_microbatch(xs[0], xs[1])
            with _jax.named_scope("update"):
                return (loss_sum + l_k, _jax.tree.map(_jnp.add, grad_sum, gw_k)), gx_k

        init = (_jnp.zeros((), _jnp.float32), _jax.tree.map(_jnp.zeros_like, weights))
        (loss, grad_w), grad_x = _jax.lax.scan(body, init, (per_example, given["loss_target"]))
    with _jax.named_scope("update"):
        delta_w, new_m, new_v = {}, {}, {}
        for n in TWIN_WEIGHTS:
            delta_w[n], new_m[n], new_v[n] = _adamw(weights[n], grad_w[n], given["m_" + n], given["v_" + n])
    return (loss, grad_x, *[grad_w[n] for n in TWIN_WEIGHTS], *[delta_w[n] for n in TWIN_WEIGHTS],
            *[new_m[n] for n in TWIN_WEIGHTS], *[new_v[n] for n in TWIN_WEIGHTS])
```

```python
import functools
import math

import jax
import jax.numpy as jnp
from jax import lax
from jax.experimental import pallas as pl
from jax.experimental.pallas import tpu as pltpu

F32 = jnp.float32
BF16 = jnp.bfloat16

V7X_VMEM_LIMIT = 56 * 2**20
LANES = 128
N_DEV = 8
MESH = pl.DeviceIdType.MESH

RMS_EPS = 1e-6
MASK_VALUE = -1e30
ADAM_LR, ADAM_B1, ADAM_B2, ADAM_EPS, ADAM_WD, ADAM_STEP = 0.001, 0.9, 0.999, 1e-08, 0.01, 10
GELU_C = math.sqrt(2.0 / math.pi)
GELU_A = 0.044715


def _cp(sem=None):
    return pltpu.CompilerParams(dimension_semantics=sem, vmem_limit_bytes=V7X_VMEM_LIMIT)


def _tile(n, pref, unit=LANES):
    if n <= pref:
        return n
    t = (pref // unit) * unit
    while t >= unit:
        if n % t == 0:
            return t
        t -= unit
    raise ValueError(f"no tile for {n}")


def _row_tile(rows, cols, bytes_per_row_elem=4, target=2 * 2**20, unit=16):
    best = None
    for t in range(unit, rows + 1, unit):
        if rows % t == 0 and t * cols * bytes_per_row_elem <= target:
            best = t
    if best is None:
        best = unit if rows % unit == 0 else rows
    return best


def _sigmoid(x):
    return 1.0 / (1.0 + jnp.exp(-x))


def _position():
    return lax.axis_index("x"), lax.axis_index("y"), lax.axis_index("c")


def _other_chips(x, y):
    return [(1 - x, y), (x, 1 - y), (1 - x, 1 - y)]


def _all_gather(shard, name):
    rows, cols = shard.shape

    def body(x_ref, out_ref, send_sems, recv_sems, local_sem):
        x, y, c = _position()
        me, sibling = (x, y, c), (x, y, 1 - c)
        chips = _other_chips(x, y)

        def slot(px, py, pc):
            return out_ref.at[4 * px + 2 * py + pc]

        def copy(k, block, to, src=None):
            return pltpu.make_async_remote_copy(
                src_ref=slot(*block) if src is None else src, dst_ref=slot(*block),
                send_sem=send_sems.at[k], recv_sem=recv_sems.at[k], device_id=to, device_id_type=MESH)

        mine = pltpu.make_async_copy(x_ref, slot(*me), local_sem)
        mine.start()
        first = [copy(0, me, sibling, src=x_ref)]
        first += [copy(1 + j, me, (*chip, c), src=x_ref) for j, chip in enumerate(chips)]
        for cp in first:
            cp.start()
        passed = [copy(4 + j, (*chip, c), sibling) for j, chip in enumerate(chips)]
        for j, chip in enumerate(chips):
            copy(1 + j, (*chip, c), me).wait_recv()
            passed[j].start()
        copy(0, sibling, me).wait_recv()
        for j, chip in enumerate(chips):
            copy(4 + j, (*chip, 1 - c), me).wait_recv()
        for cp in first + passed:
            cp.wait_send()
        mine.wait()

    return pl.pallas_call(
        body, name=name,
        out_shape=jax.ShapeDtypeStruct((N_DEV, rows, cols), shard.dtype),
        in_specs=[pl.BlockSpec(memory_space=pltpu.HBM)],
        out_specs=pl.BlockSpec(memory_space=pltpu.HBM),
        scratch_shapes=[pltpu.SemaphoreType.DMA((7,)), pltpu.SemaphoreType.DMA((7,)), pltpu.SemaphoreType.DMA],
    )(shard)


def _swap_with_sibling(parts, name):
    _, rows, cols = parts.shape

    def body(p_ref, out_ref, send_sems, recv_sems):
        x, y, c = _position()
        copies = []
        for j in range(4):
            copies.append(pltpu.make_async_remote_copy(
                src_ref=p_ref.at[2 * j + (1 - c)], dst_ref=out_ref.at[j],
                send_sem=send_sems.at[j], recv_sem=recv_sems.at[j], device_id=(x, y, 1 - c), device_id_type=MESH))
        for cp in copies:
            cp.start()
        for cp in copies:
            cp.wait()

    return pl.pallas_call(
        body, name=name,
        out_shape=jax.ShapeDtypeStruct((4, rows, cols), parts.dtype),
        in_specs=[pl.BlockSpec(memory_space=pltpu.HBM)],
        out_specs=pl.BlockSpec(memory_space=pltpu.HBM),
        scratch_shapes=[pltpu.SemaphoreType.DMA((4,)), pltpu.SemaphoreType.DMA((4,))],
    )(parts)


def _swap_with_chips(sums, name):
    _, rows, cols = sums.shape

    def body(s_ref, out_ref, send_sems, recv_sems):
        x, y, c = _position()
        copies = []
        for k, (px, py) in enumerate(_other_chips(x, y)):
            copies.append(pltpu.make_async_remote_copy(
                src_ref=s_ref.at[2 * px + py], dst_ref=out_ref.at[k],
                send_sem=send_sems.at[k], recv_sem=recv_sems.at[k], device_id=(px, py, c), device_id_type=MESH))
        for cp in copies:
            cp.start()
        for cp in copies:
            cp.wait()

    return pl.pallas_call(
        body, name=name,
        out_shape=jax.ShapeDtypeStruct((3, rows, cols), sums.dtype),
        in_specs=[pl.BlockSpec(memory_space=pltpu.HBM)],
        out_specs=pl.BlockSpec(memory_space=pltpu.HBM),
        scratch_shapes=[pltpu.SemaphoreType.DMA((3,)), pltpu.SemaphoreType.DMA((3,))],
    )(sums)


def _add_sibling(parts, got, core, name):
    _, rows, cols = parts.shape
    tr = _row_tile(rows, cols)

    def body(core_ref, a_ref, b_ref, o_ref):
        o_ref[...] = (a_ref[...].astype(F32) + b_ref[...].astype(F32)).astype(o_ref.dtype)

    return pl.pallas_call(
        body, name=name,
        grid_spec=pltpu.PrefetchScalarGridSpec(
            num_scalar_prefetch=1, grid=(4, rows // tr),
            in_specs=[pl.BlockSpec((None, tr, cols), lambda j, i, core_ref: (2 * j + core_ref[0], i, 0)),
                      pl.BlockSpec((None, tr, cols), lambda j, i, core_ref: (j, i, 0))],
            out_specs=pl.BlockSpec((None, tr, cols), lambda j, i, core_ref: (j, i, 0))),
        out_shape=jax.ShapeDtypeStruct((4, rows, cols), BF16),
        compiler_params=_cp(("parallel", "parallel")),
    )(core, parts, got)


def _all_reduce_small(packed, name):
    rows, cols = packed.shape

    def body(x_ref, out_ref, gathered, send_sems, recv_sems):
        x, y, c = _position()
        me, sibling = (x, y, c), (x, y, 1 - c)
        chips = _other_chips(x, y)

        def slot(px, py, pc):
            return gathered.at[4 * px + 2 * py + pc]

        def copy(k, block, to, src=None):
            return pltpu.make_async_remote_copy(
                src_ref=slot(*block) if src is None else src, dst_ref=slot(*block),
                send_sem=send_sems.at[k], recv_sem=recv_sems.at[k], device_id=to, device_id_type=MESH)

        first = [copy(0, me, sibling, src=x_ref)]
        first += [copy(1 + j, me, (*chip, c), src=x_ref) for j, chip in enumerate(chips)]
        for cp in first:
            cp.start()
        passed = [copy(4 + j, (*chip, c), sibling) for j, chip in enumerate(chips)]
        for j, chip in enumerate(chips):
            copy(1 + j, (*chip, c), me).wait_recv()
            passed[j].start()
        copy(0, sibling, me).wait_recv()
        for j, chip in enumerate(chips):
            copy(4 + j, (*chip, 1 - c), me).wait_recv()
        for cp in first + passed:
            cp.wait_send()
        gathered[4 * x + 2 * y + c] = x_ref[...]
        total = gathered[0]
        for k in range(1, N_DEV):
            total = total + gathered[k]
        out_ref[...] = total

    return pl.pallas_call(
        body, name=name,
        out_shape=jax.ShapeDtypeStruct((rows, cols), F32),
        in_specs=[pl.BlockSpec(memory_space=pltpu.VMEM)],
        out_specs=pl.BlockSpec(memory_space=pltpu.VMEM),
        scratch_shapes=[pltpu.VMEM((N_DEV, rows, cols), F32),
                        pltpu.SemaphoreType.DMA((7,)), pltpu.SemaphoreType.DMA((7,))],
        compiler_params=pltpu.CompilerParams(vmem_limit_bytes=V7X_VMEM_LIMIT),
    )(packed)


def _mm(a, b, mode, out_dtype, name, tm=1024, tn=512, tk=512):
    if mode == "nn":
        (m, k), (k2, n) = a.shape, b.shape
    elif mode == "nt":
        (m, k), (n, k2) = a.shape, b.shape
    else:
        (k, m), (k2, n) = a.shape, b.shape
    assert k == k2, (name, a.shape, b.shape)
    tm, tn, tk = _tile(m, tm), _tile(n, tn), _tile(k, tk)
    nk = k // tk
    if mode == "tn":
        a_spec = pl.BlockSpec((tk, tm), lambda i, j, l: (l, i))
        dims = (((0,), (0,)), ((), ()))
    else:
        a_spec = pl.BlockSpec((tm, tk), lambda i, j, l: (i, l))
        dims = (((1,), (1,)), ((), ())) if mode == "nt" else (((1,), (0,)), ((), ()))
    if mode == "nt":
        b_spec = pl.BlockSpec((tn, tk), lambda i, j, l: (j, l))
    else:
        b_spec = pl.BlockSpec((tk, tn), lambda i, j, l: (l, j))

    def body(a_ref, b_ref, o_ref, acc_ref):
        l = pl.program_id(2)

        @pl.when(l == 0)
        def _():
            acc_ref[...] = jnp.zeros_like(acc_ref)

        acc_ref[...] += lax.dot_general(a_ref[...].astype(BF16), b_ref[...].astype(BF16), dims,
                                        preferred_element_type=F32)

        @pl.when(l == nk - 1)
        def _():
            o_ref[...] = acc_ref[...].astype(o_ref.dtype)

    return pl.pallas_call(
        body, name=name, grid=(m // tm, n // tn, nk),
        in_specs=[a_spec, b_spec],
        out_specs=pl.BlockSpec((tm, tn), lambda i, j, l: (i, j)),
        out_shape=jax.ShapeDtypeStruct((m, n), out_dtype),
        scratch_shapes=[pltpu.VMEM((tm, tn), F32)],
        compiler_params=_cp(("parallel", "parallel", "arbitrary")),
    )(a, b)


def _rms_fwd(x, g, name):
    t, d = x.shape
    tm = _tile(t, 256, 16)

    def body(x_ref, g_ref, u_ref, r_ref):
        xv = x_ref[...]
        r = lax.rsqrt(jnp.mean(xv * xv, axis=-1, keepdims=True) + RMS_EPS)
        u_ref[...] = (xv * r * g_ref[...]).astype(u_ref.dtype)
        r_ref[...] = r

    return pl.pallas_call(
        body, name=name, grid=(t // tm,),
        in_specs=[pl.BlockSpec((tm, d), lambda i: (i, 0)), pl.BlockSpec((1, d), lambda i: (0, 0))],
        out_specs=[pl.BlockSpec((tm, d), lambda i: (i, 0)), pl.BlockSpec((tm, 1), lambda i: (i, 0))],
        out_shape=[jax.ShapeDtypeStruct((t, d), BF16), jax.ShapeDtypeStruct((t, 1), F32)],
        compiler_params=_cp(("parallel",)),
    )(x, g)


def _rms_bwd(dn_parts, x, r, g, extra, name):
    t, d = x.shape
    tm = _tile(t, 128, 16)
    n_dn, n_extra = len(dn_parts), len(extra)

    def body(*refs):
        dn_refs = refs[:n_dn]
        x_ref, r_ref, g_ref = refs[n_dn:n_dn + 3]
        extra_refs = refs[n_dn + 3:n_dn + 3 + n_extra]
        dx_ref, dg_ref = refs[n_dn + 3 + n_extra:]
        xhat = x_ref[...] * r_ref[...]
        dnv = dn_refs[0][...].astype(F32)
        for p in dn_refs[1:]:
            dnv = dnv + p[...].astype(F32)
        gd = dnv * g_ref[...]
        dx = r_ref[...] * (gd - xhat * jnp.mean(gd * xhat, axis=-1, keepdims=True))
        for e in extra_refs:
            dx = dx + e[...].astype(F32)
        dx_ref[...] = dx

        @pl.when(pl.program_id(0) == 0)
        def _():
            dg_ref[...] = jnp.zeros_like(dg_ref)

        dg_ref[...] += jnp.sum(dnv * xhat, axis=0, keepdims=True)

    row = pl.BlockSpec((tm, d), lambda i: (i, 0))
    return pl.pallas_call(
        body, name=name, grid=(t // tm,),
        in_specs=[row] * n_dn + [row, pl.BlockSpec((tm, 1), lambda i: (i, 0)), pl.BlockSpec((1, d), lambda i: (0, 0))]
        + [row] * n_extra,
        out_specs=[row, pl.BlockSpec((1, d), lambda i: (0, 0))],
        out_shape=[jax.ShapeDtypeStruct((t, d), F32), jax.ShapeDtypeStruct((1, d), F32)],
        compiler_params=_cp(("arbitrary",)),
    )(*dn_parts, x, r, g, *extra)


def _gate_merge_fwd(z, gate_col, b_gates, pf, ps, name):
    t, d = pf.shape
    tm, tn = _tile(t, 512, 16), _tile(math.gcd(d, gate_col), 512)
    nj, off = d // tn, gate_col // tn
    assert gate_col % tn == 0

    def body(zf_ref, zs_ref, bf_ref, bs_ref, pf_ref, ps_ref, o_ref):
        gf = _sigmoid(zf_ref[...].astype(F32) + bf_ref[...])
        gs = _sigmoid(zs_ref[...].astype(F32) + bs_ref[...])
        o_ref[...] = (gf * pf_ref[...].astype(F32) + gs * ps_ref[...].astype(F32)).astype(o_ref.dtype)

    blk = pl.BlockSpec((tm, tn), lambda i, j: (i, j))
    return pl.pallas_call(
        body, name=name, grid=(t // tm, nj),
        in_specs=[pl.BlockSpec((tm, tn), lambda i, j: (i, off + j)), pl.BlockSpec((tm, tn), lambda i, j: (i, off + nj + j)),
                  pl.BlockSpec((1, tn), lambda i, j: (0, j)), pl.BlockSpec((1, tn), lambda i, j: (0, nj + j)), blk, blk],
        out_specs=blk,
        out_shape=jax.ShapeDtypeStruct((t, d), BF16),
        compiler_params=_cp(("parallel", "parallel")),
    )(z, z, b_gates, b_gates, pf, ps)


def _gate_merge_bwd(dm, z, gate_col, b_gates, pf, ps, name):
    t, d = pf.shape
    tm, tn = _tile(t, 512, 16), _tile(math.gcd(d, gate_col), 512)
    nj, off = d // tn, gate_col // tn

    def body(dm_ref, zf_ref, zs_ref, bf_ref, bs_ref, pf_ref, ps_ref, dpf_ref, dps_ref, dzf_ref, dzs_ref, dbf_ref, dbs_ref):
        gf = _sigmoid(zf_ref[...].astype(F32) + bf_ref[...])
        gs = _sigmoid(zs_ref[...].astype(F32) + bs_ref[...])
        dmv = dm_ref[...].astype(F32)
        dpf_ref[...] = (dmv * gf).astype(dpf_ref.dtype)
        dps_ref[...] = (dmv * gs).astype(dps_ref.dtype)
        dzf = dmv * pf_ref[...].astype(F32) * gf * (1.0 - gf)
        dzs = dmv * ps_ref[...].astype(F32) * gs * (1.0 - gs)
        dzf_ref[...] = dzf.astype(dzf_ref.dtype)
        dzs_ref[...] = dzs.astype(dzs_ref.dtype)

        @pl.when(pl.program_id(1) == 0)
        def _():
            dbf_ref[...] = jnp.zeros_like(dbf_ref)
            dbs_ref[...] = jnp.zeros_like(dbs_ref)

        dbf_ref[...] += jnp.sum(dzf, axis=0, keepdims=True)
        dbs_ref[...] += jnp.sum(dzs, axis=0, keepdims=True)

    blk = pl.BlockSpec((tm, tn), lambda j, i: (i, j))
    lo = pl.BlockSpec((1, tn), lambda j, i: (0, j))
    hi = pl.BlockSpec((1, tn), lambda j, i: (0, nj + j))
    return pl.pallas_call(
        body, name=name, grid=(nj, t // tm),
        in_specs=[blk, pl.BlockSpec((tm, tn), lambda j, i: (i, off + j)), pl.BlockSpec((tm, tn), lambda j, i: (i, off + nj + j)),
                  lo, hi, blk, blk],
        out_specs=[blk, blk, blk, blk, lo, lo],
        out_shape=[jax.ShapeDtypeStruct((t, d), BF16)] * 4 + [jax.ShapeDtypeStruct((1, d), F32)] * 2,
        compiler_params=_cp(("parallel", "arbitrary")),
    )(dm, z, z, b_gates, b_gates, pf, ps)


def _resid_rms(x, mo, g, name):
    t, d = x.shape
    tm = _tile(t, 256, 16)

    def body(x_ref, mo_ref, g_ref, h_ref, hn_ref, r_ref):
        h = x_ref[...] + mo_ref[...].astype(F32)
        r = lax.rsqrt(jnp.mean(h * h, axis=-1, keepdims=True) + RMS_EPS)
        h_ref[...] = h
        hn_ref[...] = (h * r * g_ref[...]).astype(hn_ref.dtype)
        r_ref[...] = r

    row = pl.BlockSpec((tm, d), lambda i: (i, 0))
    col = pl.BlockSpec((tm, 1), lambda i: (i, 0))
    return pl.pallas_call(
        body, name=name, grid=(t // tm,),
        in_specs=[row, row, pl.BlockSpec((1, d), lambda i: (0, 0))],
        out_specs=[row, row, col],
        out_shape=[jax.ShapeDtypeStruct((t, d), F32), jax.ShapeDtypeStruct((t, d), BF16), jax.ShapeDtypeStruct((t, 1), F32)],
        compiler_params=_cp(("parallel",)),
    )(x, mo, g)


def _swiglu_fwd(gu, name):
    t, f2 = gu.shape
    f = f2 // 2
    tm, tn = _tile(t, 512, 16), _tile(f, 1024)
    nj = f // tn

    def body(g_ref, u_ref, o_ref):
        gate = g_ref[...].astype(F32)
        o_ref[...] = (gate * _sigmoid(gate) * u_ref[...].astype(F32)).astype(o_ref.dtype)

    return pl.pallas_call(
        body, name=name, grid=(t // tm, nj),
        in_specs=[pl.BlockSpec((tm, tn), lambda i, j: (i, j)), pl.BlockSpec((tm, tn), lambda i, j: (i, nj + j))],
        out_specs=pl.BlockSpec((tm, tn), lambda i, j: (i, j)),
        out_shape=jax.ShapeDtypeStruct((t, f), BF16),
        compiler_params=_cp(("parallel", "parallel")),
    )(gu, gu)


def _swiglu_bwd(gu, dact, name):
    t, f2 = gu.shape
    f = f2 // 2
    tm, tn = _tile(t, 512, 16), _tile(f, 1024)
    nj = f // tn

    def body(g_ref, u_ref, da_ref, dg_ref, du_ref):
        gate = g_ref[...].astype(F32)
        s = _sigmoid(gate)
        da = da_ref[...].astype(F32)
        dg_ref[...] = (da * u_ref[...].astype(F32) * s * (1.0 + gate * (1.0 - s))).astype(dg_ref.dtype)
        du_ref[...] = (da * gate * s).astype(du_ref.dtype)

    lo = pl.BlockSpec((tm, tn), lambda i, j: (i, j))
    dgate, dup = pl.pallas_call(
        body, name=name, grid=(t // tm, nj),
        in_specs=[lo, pl.BlockSpec((tm, tn), lambda i, j: (i, nj + j)), lo],
        out_specs=[lo, lo],
        out_shape=[jax.ShapeDtypeStruct((t, f), BF16)] * 2,
        compiler_params=_cp(("parallel", "parallel")),
    )(gu, gu, dact)
    return dgate, dup


def _loss_head(h, dn, target, name):
    t, d = h.shape
    tm = _tile(t, 256, 16)

    def body(h_ref, dn_ref, t_ref, loss_ref, dy_ref):
        err = h_ref[...] + dn_ref[...].astype(F32) - t_ref[...]
        dy_ref[...] = err * (1.0 / d)

        @pl.when(pl.program_id(0) == 0)
        def _():
            loss_ref[...] = jnp.zeros_like(loss_ref)

        loss_ref[...] += 0.5 * jnp.sum(jnp.mean(err * err, axis=-1, keepdims=True))

    row = pl.BlockSpec((tm, d), lambda i: (i, 0))
    return pl.pallas_call(
        body, name=name, grid=(t // tm,),
        in_specs=[row, row, row],
        out_specs=[pl.BlockSpec((8, LANES), lambda i: (0, 0)), row],
        out_shape=[jax.ShapeDtypeStruct((8, LANES), F32), jax.ShapeDtypeStruct((t, d), F32)],
        compiler_params=_cp(("arbitrary",)),
    )(h, dn, target)


def _qk_prep(z, heads, dh, q_norm, k_norm, name):
    t = z.shape[0]
    tq = _tile(t, 512, 16)
    scale = 1.0 / math.sqrt(dh)

    def body(q_ref, k_ref, gq_ref, gk_ref, qn_ref, kn_ref, rq_ref, rk_ref):
        q = q_ref[...].astype(F32)
        k = k_ref[...].astype(F32)
        rq = lax.rsqrt(jnp.mean(q * q, axis=-1, keepdims=True) + RMS_EPS)
        rk = lax.rsqrt(jnp.mean(k * k, axis=-1, keepdims=True) + RMS_EPS)
        qn_ref[...] = (q * rq * gq_ref[...] * scale).astype(qn_ref.dtype)
        kn_ref[...] = (k * rk * gk_ref[...]).astype(kn_ref.dtype)
        rq_ref[...] = rq
        rk_ref[...] = rk

    blk = pl.BlockSpec((tq, dh), lambda i, h: (i, h))
    vec = pl.BlockSpec((1, dh), lambda i, h: (0, 0))
    col = pl.BlockSpec((None, tq, 1), lambda i, h: (h, i, 0))
    return pl.pallas_call(
        body, name=name, grid=(t // tq, heads),
        in_specs=[blk, pl.BlockSpec((tq, dh), lambda i, h: (i, heads + h)), vec, vec],
        out_specs=[blk, blk, col, col],
        out_shape=[jax.ShapeDtypeStruct((t, heads * dh), BF16)] * 2 + [jax.ShapeDtypeStruct((heads, t, 1), F32)] * 2,
        compiler_params=_cp(("parallel", "parallel")),
    )(z, z, q_norm, k_norm)


def _qk_prep_bwd(dqn, dkn, z, heads, dh, q_norm, k_norm, rq, rk, name):
    t = z.shape[0]
    tq = _tile(t, 512, 16)
    scale = 1.0 / math.sqrt(dh)

    def norm_bwd(dy, xv, r, g):
        xhat = xv * r
        gd = dy * g
        return r * (gd - xhat * jnp.mean(gd * xhat, axis=-1, keepdims=True)), jnp.sum(dy * xhat, axis=0, keepdims=True)

    def body(dqn_ref, dkn_ref, q_ref, k_ref, gq_ref, gk_ref, rq_ref, rk_ref, dq_ref, dk_ref, dgq_ref, dgk_ref):
        dq, dgq = norm_bwd(dqn_ref[...].astype(F32) * scale, q_ref[...].astype(F32), rq_ref[...], gq_ref[...])
        dk, dgk = norm_bwd(dkn_ref[...].astype(F32), k_ref[...].astype(F32), rk_ref[...], gk_ref[...])
        dq_ref[...] = dq.astype(dq_ref.dtype)
        dk_ref[...] = dk.astype(dk_ref.dtype)

        @pl.when((pl.program_id(0) == 0) & (pl.program_id(1) == 0))
        def _():
            dgq_ref[...] = jnp.zeros_like(dgq_ref)
            dgk_ref[...] = jnp.zeros_like(dgk_ref)

        dgq_ref[...] += dgq
        dgk_ref[...] += dgk

    blk = pl.BlockSpec((tq, dh), lambda i, h: (i, h))
    vec = pl.BlockSpec((1, dh), lambda i, h: (0, 0))
    col = pl.BlockSpec((None, tq, 1), lambda i, h: (h, i, 0))
    return pl.pallas_call(
        body, name=name, grid=(t // tq, heads),
        in_specs=[blk, blk, blk, pl.BlockSpec((tq, dh), lambda i, h: (i, heads + h)), vec, vec, col, col],
        out_specs=[blk, blk, vec, vec],
        out_shape=[jax.ShapeDtypeStruct((t, heads * dh), BF16)] * 2 + [jax.ShapeDtypeStruct((1, dh), F32)] * 2,
        compiler_params=_cp(("arbitrary", "arbitrary")),
    )(dqn, dkn, z, z, q_norm, k_norm, rq, rk)


def _tri_ones(n, upper):
    row = lax.broadcasted_iota(jnp.int32, (n, n), 0)
    col = lax.broadcasted_iota(jnp.int32, (n, n), 1)
    return jnp.where((col >= row) if upper else (col <= row), 1.0, 0.0).astype(F32)


def _forget_fwd(f, b, name):
    t, w = f.shape
    blk = _tile(t, 256, 8)
    nb = t // blk

    def body(f_ref, b_ref, out_ref):
        tri = _tri_ones(blk, upper=False)

        def step(i, carry):
            rows = pl.ds(pl.multiple_of(i * blk, blk), blk)
            logf = jax.nn.log_sigmoid(f_ref[rows, :] + b_ref[...])
            acc = jnp.dot(tri, logf, precision=lax.Precision.HIGHEST, preferred_element_type=F32) + carry
            out_ref[rows, :] = acc
            return acc[blk - 1:blk, :]

        lax.fori_loop(0, nb, step, jnp.zeros((1, w), F32))

    return pl.pallas_call(
        body, name=name,
        in_specs=[pl.BlockSpec(memory_space=pltpu.VMEM)] * 2,
        out_specs=pl.BlockSpec(memory_space=pltpu.VMEM),
        out_shape=jax.ShapeDtypeStruct((t, w), F32),
        compiler_params=_cp(),
    )(f, b)


def _forget_bwd(d_query, d_key, f, b, name):
    t, w = f.shape
    blk = _tile(t, 256, 8)
    nb = t // blk

    def body(dq_ref, dk_ref, f_ref, b_ref, df_ref, db_ref):
        tri = _tri_ones(blk, upper=True)

        def step(i, carry):
            suffix, db = carry
            rows = pl.ds(pl.multiple_of((nb - 1 - i) * blk, blk), blk)
            dcum = dq_ref[rows, :] - dk_ref[rows, :]
            dlog = jnp.dot(tri, dcum, precision=lax.Precision.HIGHEST, preferred_element_type=F32) + suffix
            df = dlog * _sigmoid(-(f_ref[rows, :] + b_ref[...]))
            df_ref[rows, :] = df
            return dlog[0:1, :], db + jnp.sum(df, axis=0, keepdims=True)

        _, db = lax.fori_loop(0, nb, step, (jnp.zeros((1, w), F32), jnp.zeros((1, w), F32)))
        db_ref[...] = db

    return pl.pallas_call(
        body, name=name,
        in_specs=[pl.BlockSpec(memory_space=pltpu.VMEM)] * 4,
        out_specs=[pl.BlockSpec(memory_space=pltpu.VMEM)] * 2,
        out_shape=[jax.ShapeDtypeStruct((t, w), F32), jax.ShapeDtypeStruct((1, w), F32)],
        compiler_params=_cp(),
    )(d_query, d_key, f, b)


def _attn_logits(q_ref, k_ref, fc_ref, fr_ref, qi, ki, tq, tk):
    s = lax.dot_general(q_ref[...], k_ref[...], (((1,), (1,)), ((), ())), preferred_element_type=F32)
    s = s + fc_ref[...] - fr_ref[...]
    row = qi * tq + lax.broadcasted_iota(jnp.int32, (tq, tk), 0)
    col = ki * tk + lax.broadcasted_iota(jnp.int32, (tq, tk), 1)
    return jnp.where(col <= row, s, MASK_VALUE)


def _attn_fwd(qn, kn, v_src, v_col, fcol, frow, heads, dh, name):
    t = qn.shape[0]
    tq = tk = _tile(t, 512)
    nq = t // tq

    def body(q_ref, k_ref, v_ref, fc_ref, fr_ref, o_ref, lse_ref, m_sc, l_sc, acc_sc):
        qi, ki = pl.program_id(1), pl.program_id(2)

        @pl.when(ki == 0)
        def _():
            m_sc[...] = jnp.full_like(m_sc, MASK_VALUE)
            l_sc[...] = jnp.zeros_like(l_sc)
            acc_sc[...] = jnp.zeros_like(acc_sc)

        @pl.when(ki <= qi)
        def _():
            s = _attn_logits(q_ref, k_ref, fc_ref, fr_ref, qi, ki, tq, tk)
            m_new = jnp.maximum(m_sc[...], jnp.max(s, axis=-1, keepdims=True))
            alpha = jnp.exp(m_sc[...] - m_new)
            p = jnp.exp(s - m_new)
            l_sc[...] = alpha * l_sc[...] + jnp.sum(p, axis=-1, keepdims=True)
            acc_sc[...] = alpha * acc_sc[...] + jnp.dot(p.astype(BF16), v_ref[...].astype(BF16), preferred_element_type=F32)
            m_sc[...] = m_new

        @pl.when(ki == nq - 1)
        def _():
            o_ref[...] = (acc_sc[...] / l_sc[...]).astype(o_ref.dtype)
            lse_ref[...] = m_sc[...] + jnp.log(l_sc[...])

    qblk = pl.BlockSpec((tq, dh), lambda h, i, j: (i, h))
    qcol = pl.BlockSpec((None, tq, 1), lambda h, i, j: (h, i, 0))
    return pl.pallas_call(
        body, name=name, grid=(heads, nq, nq),
        in_specs=[qblk,
                  pl.BlockSpec((tk, dh), lambda h, i, j: (jnp.minimum(j, i), h)),
                  pl.BlockSpec((tk, dh), lambda h, i, j: (jnp.minimum(j, i), v_col + h)),
                  qcol,
                  pl.BlockSpec((None, 1, tk), lambda h, i, j: (h, 0, jnp.minimum(j, i)))],
        out_specs=[qblk, qcol],
        out_shape=[jax.ShapeDtypeStruct((t, heads * dh), BF16), jax.ShapeDtypeStruct((heads, t, 1), F32)],
        scratch_shapes=[pltpu.VMEM((tq, 1), F32), pltpu.VMEM((tq, 1), F32), pltpu.VMEM((tq, dh), F32)],
        compiler_params=_cp(("parallel", "parallel", "arbitrary")),
    )(qn, kn, v_src, fcol, frow)


def _attn_delta(o, do, heads, dh, name):
    t = o.shape[0]
    tq = _tile(t, 512, 16)

    def body(o_ref, do_ref, out_ref):
        out_ref[...] = jnp.sum(o_ref[...].astype(F32) * do_ref[...].astype(F32), axis=-1, keepdims=True)

    blk = pl.BlockSpec((tq, dh), lambda i, h: (i, h))
    return pl.pallas_call(
        body, name=name, grid=(t // tq, heads),
        in_specs=[blk, blk],
        out_specs=pl.BlockSpec((None, tq, 1), lambda i, h: (h, i, 0)),
        out_shape=jax.ShapeDtypeStruct((heads, t, 1), F32),
        compiler_params=_cp(("parallel", "parallel")),
    )(o, do)


def _attn_bwd_q(qn, kn, v_src, v_col, do, fcol, frow, lse, delta, heads, dh, name):
    t = qn.shape[0]
    tq = tk = _tile(t, 512)
    nq = t // tq

    def body(q_ref, k_ref, v_ref, do_ref, fc_ref, fr_ref, lse_ref, dl_ref, dq_ref, dfq_ref, dq_sc, dfq_sc):
        qi, ki = pl.program_id(1), pl.program_id(2)

        @pl.when(ki == 0)
        def _():
            dq_sc[...] = jnp.zeros_like(dq_sc)
            dfq_sc[...] = jnp.zeros_like(dfq_sc)

        @pl.when(ki <= qi)
        def _():
            s = _attn_logits(q_ref, k_ref, fc_ref, fr_ref, qi, ki, tq, tk)
            p = jnp.exp(s - lse_ref[...])
            dp = lax.dot_general(do_ref[...].astype(BF16), v_ref[...].astype(BF16), (((1,), (1,)), ((), ())),
                                 preferred_element_type=F32)
            ds = p * (dp - dl_ref[...])
            dq_sc[...] += jnp.dot(ds.astype(BF16), k_ref[...], preferred_element_type=F32)
            dfq_sc[...] += jnp.sum(ds, axis=-1, keepdims=True)

        @pl.when(ki == nq - 1)
        def _():
            dq_ref[...] = dq_sc[...]
            dfq_ref[...] = dfq_sc[...]

    qblk = pl.BlockSpec((tq, dh), lambda h, i, j: (i, h))
    qcol = pl.BlockSpec((None, tq, 1), lambda h, i, j: (h, i, 0))
    return pl.pallas_call(
        body, name=name, grid=(heads, nq, nq),
        in_specs=[qblk,
                  pl.BlockSpec((tk, dh), lambda h, i, j: (jnp.minimum(j, i), h)),
                  pl.BlockSpec((tk, dh), lambda h, i, j: (jnp.minimum(j, i), v_col + h)),
                  qblk, qcol,
                  pl.BlockSpec((None, 1, tk), lambda h, i, j: (h, 0, jnp.minimum(j, i))),
                  qcol, qcol],
        out_specs=[qblk, qcol],
        out_shape=[jax.ShapeDtypeStruct((t, heads * dh), F32), jax.ShapeDtypeStruct((heads, t, 1), F32)],
        scratch_shapes=[pltpu.VMEM((tq, dh), F32), pltpu.VMEM((tq, 1), F32)],
        compiler_params=_cp(("parallel", "parallel", "arbitrary")),
    )(qn, kn, v_src, do, fcol, frow, lse, delta)


def _attn_bwd_kv(qn, kn, v_src, v_col, do, fcol, frow, lse, delta, heads, dh, name):
    t = qn.shape[0]
    tq = tk = _tile(t, 512)
    nq = t // tq

    def body(q_ref, k_ref, v_ref, do_ref, fc_ref, fr_ref, lse_ref, dl_ref, dk_ref, dv_ref, dfk_ref, dk_sc, dv_sc, dfk_sc):
        ki, qi = pl.program_id(1), pl.program_id(2)

        @pl.when(qi == 0)
        def _():
            dk_sc[...] = jnp.zeros_like(dk_sc)
            dv_sc[...] = jnp.zeros_like(dv_sc)
            dfk_sc[...] = jnp.zeros_like(dfk_sc)

        @pl.when(qi >= ki)
        def _():
            s = _attn_logits(q_ref, k_ref, fc_ref, fr_ref, qi, ki, tq, tk)
            p = jnp.exp(s - lse_ref[...])
            dob = do_ref[...].astype(BF16)
            dp = lax.dot_general(dob, v_ref[...].astype(BF16), (((1,), (1,)), ((), ())), preferred_element_type=F32)
            ds = p * (dp - dl_ref[...])
            dv_sc[...] += lax.dot_general(p.astype(BF16), dob, (((0,), (0,)), ((), ())), preferred_element_type=F32)
            dk_sc[...] += lax.dot_general(ds.astype(BF16), q_ref[...], (((0,), (0,)), ((), ())), preferred_element_type=F32)
            dfk_sc[...] += jnp.sum(ds, axis=0, keepdims=True)

        @pl.when(qi == nq - 1)
        def _():
            dk_ref[...] = dk_sc[...]
            dv_ref[...] = dv_sc[...].astype(dv_ref.dtype)
            dfk_ref[...] = dfk_sc[...]

    qblk = pl.BlockSpec((tq, dh), lambda h, j, i: (jnp.maximum(i, j), h))
    qcol = pl.BlockSpec((None, tq, 1), lambda h, j, i: (h, jnp.maximum(i, j), 0))
    kblk = pl.BlockSpec((tk, dh), lambda h, j, i: (j, h))
    krow = pl.BlockSpec((None, 1, tk), lambda h, j, i: (h, 0, j))
    return pl.pallas_call(
        body, name=name, grid=(heads, nq, nq),
        in_specs=[qblk, kblk, pl.BlockSpec((tk, dh), lambda h, j, i: (j, v_col + h)), qblk, qcol, krow, qcol, qcol],
        out_specs=[kblk, kblk, krow],
        out_shape=[jax.ShapeDtypeStruct((t, heads * dh), F32), jax.ShapeDtypeStruct((t, heads * dh), BF16),
                   jax.ShapeDtypeStruct((heads, 1, t), F32)],
        scratch_shapes=[pltpu.VMEM((tk, dh), F32), pltpu.VMEM((tk, dh), F32), pltpu.VMEM((1, tk), F32)],
        compiler_params=_cp(("parallel", "parallel", "arbitrary")),
    )(qn, kn, v_src, do, fcol, frow, lse, delta)


TIME_TILE = 8


def _s5_discretize(lam_re, lam_im, log_step, b_re, b_im):
    dt = jnp.exp(log_step)
    mag = jnp.exp(lam_re * dt)
    lb_re = mag * jnp.cos(lam_im * dt)
    lb_im = mag * jnp.sin(lam_im * dt)
    denom = lam_re * lam_re + lam_im * lam_im
    num_re = lb_re - 1.0
    fac_re = (num_re * lam_re + lb_im * lam_im) / denom
    fac_im = (lb_im * lam_re - num_re * lam_im) / denom
    return lb_re, lb_im, fac_re * b_re - fac_im * b_im, fac_re * b_im + fac_im * b_re


def _s5_prep(lam_re, lam_im, log_step, b_re, b_im, name):
    gp, width = b_re.shape

    def body(lr, li, ls, br, bi, o_lr, o_li, o_br, o_bi):
        res = _s5_discretize(lr[...], li[...], ls[...], br[...], bi[...])
        for ref, val in zip((o_lr, o_li, o_br, o_bi), res):
            ref[...] = val

    vm = pl.BlockSpec(memory_space=pltpu.VMEM)
    return pl.pallas_call(
        body, name=name, in_specs=[vm] * 5, out_specs=[vm] * 4,
        out_shape=[jax.ShapeDtypeStruct((gp, 1), F32)] * 2 + [jax.ShapeDtypeStruct((gp, width), F32)] * 2,
        compiler_params=_cp(),
    )(lam_re, lam_im, log_step, b_re, b_im)


def _s5_prep_bwd(lam_re, lam_im, log_step, b_re, b_im, d_lb_re, d_lb_im, d_bb_re, d_bb_im, groups, name):
    gp, width = b_re.shape
    states = gp // groups
    tr = _tile(gp, 512, 8)

    def body(lr, li, ls, br, bi, g_lr, g_li, g_br, g_bi, o_lr, o_li, o_ls, o_br, o_bi):
        _, vjp = jax.vjp(_s5_discretize, lr[...], li[...], ls[...], br[...], bi[...])
        d_lr, d_li, d_ls, d_br, d_bi = vjp((g_lr[...], g_li[...], g_br[...], g_bi[...]))
        o_lr[...] = d_lr
        o_li[...] = d_li
        o_br[...] = d_br
        o_bi[...] = d_bi
        row_group = (pl.program_id(0) * tr + lax.broadcasted_iota(jnp.int32, (tr, groups), 0)) // states
        col_group = lax.broadcasted_iota(jnp.int32, (tr, groups), 1)

        @pl.when(pl.program_id(0) == 0)
        def _():
            o_ls[...] = jnp.zeros_like(o_ls)

        o_ls[...] += jnp.sum(jnp.where(row_group == col_group, d_ls, 0.0), axis=0, keepdims=True)

    col = pl.BlockSpec((tr, 1), lambda i: (i, 0))
    mat = pl.BlockSpec((tr, width), lambda i: (i, 0))
    return pl.pallas_call(
        body, name=name, grid=(gp // tr,),
        in_specs=[col, col, col, mat, mat, col, col, mat, mat],
        out_specs=[col, col, pl.BlockSpec((1, groups), lambda i: (0, 0)), mat, mat],
        out_shape=[jax.ShapeDtypeStruct((gp, 1), F32)] * 2 + [jax.ShapeDtypeStruct((1, groups), F32)]
        + [jax.ShapeDtypeStruct((gp, width), F32)] * 2,
        compiler_params=_cp(("arbitrary",)),
    )(lam_re, lam_im, log_step, b_re, b_im, d_lb_re, d_lb_im, d_bb_re, d_bb_im)


def _shift_time(v, s, reverse):
    row = lax.broadcasted_iota(jnp.int32, v.shape, 0)
    if reverse:
        return jnp.where(row < TIME_TILE - s, pltpu.roll(v, TIME_TILE - s, 0), 0.0)
    return jnp.where(row >= s, pltpu.roll(v, s, 0), 0.0)


def _cmul(ar, ai, br, bi):
    return ar * br - ai * bi, ar * bi + ai * br


def _scan_time(xr_ref, xi_ref, ar, ai, reverse):
    t = xr_ref.shape[0]
    n_tiles = t // TIME_TILE
    powers = [(ar, ai)]
    for _ in range(TIME_TILE - 1):
        powers.append(_cmul(*powers[-1], ar, ai))
    order = powers[::-1] if reverse else powers
    carry_r = jnp.concatenate([p[0] for p in order], axis=0)
    carry_i = jnp.concatenate([p[1] for p in order], axis=0)
    levels = [(1, powers[0]), (2, powers[1]), (4, powers[3])]
    last = 0 if reverse else TIME_TILE - 1

    def tile(i, carry):
        cr, ci = carry
        idx = (n_tiles - 1 - i) if reverse else i
        rows = pl.ds(pl.multiple_of(idx * TIME_TILE, TIME_TILE), TIME_TILE)
        br, bi = xr_ref[rows, :], xi_ref[rows, :]
        for s, (pr, pi) in levels:
            sr, si = _cmul(pr, pi, _shift_time(br, s, reverse), _shift_time(bi, s, reverse))
            br, bi = br + sr, bi + si
        kr, ki = _cmul(carry_r, carry_i, cr, ci)
        br, bi = br + kr, bi + ki
        xr_ref[rows, :] = br
        xi_ref[rows, :] = bi
        return br[last:last + 1, :], bi[last:last + 1, :]

    zero = jnp.zeros_like(ar)
    lax.fori_loop(0, n_tiles, tile, (zero, zero))


def _s5_states(u_ref, bbr_ref, bbi_ref, ar_ref, ai_ref, xr, xi, chunk):
    t = u_ref.shape[0]
    for r0 in range(0, t, chunk):
        rows = pl.ds(r0, chunk)
        xr[rows, :] = jnp.dot(u_ref[rows, :], bbr_ref[...], preferred_element_type=F32)
        xi[rows, :] = jnp.dot(u_ref[rows, :], bbi_ref[...], preferred_element_type=F32)
    _scan_time(xr, xi, ar_ref[...], ai_ref[...], reverse=False)


def _s5_specs(t, nb_lanes, state_lanes):
    tok = pl.BlockSpec((t, nb_lanes), lambda j: (0, j))
    bb = pl.BlockSpec((None, nb_lanes, state_lanes), lambda j: (j, 0, 0))
    cc = pl.BlockSpec((None, state_lanes, nb_lanes), lambda j: (j, 0, 0))
    dvec = pl.BlockSpec((1, nb_lanes), lambda j: (0, j))
    avec = pl.BlockSpec((1, state_lanes), lambda j: (0, j))
    return tok, bb, cc, dvec, avec


def _s5_fwd(u5, bbr, bbi, ccr, cci, dskip, ar, ai, name):
    t, w = u5.shape
    nb, nb_lanes, state_lanes = bbr.shape
    chunk = _tile(t, 512, 16)

    def body(u_ref, bbr_ref, bbi_ref, cr_ref, ci_ref, d_ref, ar_ref, ai_ref, y_ref, xr, xi):
        _s5_states(u_ref, bbr_ref, bbi_ref, ar_ref, ai_ref, xr, xi, chunk)
        for r0 in range(0, t, chunk):
            rows = pl.ds(r0, chunk)
            y = jnp.dot(xr[rows, :].astype(BF16), cr_ref[...], preferred_element_type=F32)
            y = y - jnp.dot(xi[rows, :].astype(BF16), ci_ref[...], preferred_element_type=F32)
            y_ref[rows, :] = y + d_ref[...] * u_ref[rows, :].astype(F32)

    tok, bb, cc, dvec, avec = _s5_specs(t, nb_lanes, state_lanes)
    return pl.pallas_call(
        body, name=name, grid=(nb,),
        in_specs=[tok, bb, bb, cc, cc, dvec, avec, avec],
        out_specs=tok,
        out_shape=jax.ShapeDtypeStruct((t, w), F32),
        scratch_shapes=[pltpu.VMEM((t, state_lanes), F32)] * 2,
        compiler_params=_cp(("parallel",)),
    )(u5, bbr, bbi, ccr, cci, dskip, ar, ai)


def _s5_bwd(u5, dy, bbr, bbi, ccr, cci, dskip, ar, ai, name):
    t, w = u5.shape
    nb, nb_lanes, state_lanes = bbr.shape
    chunk = _tile(t, 512, 16)
    nt_dims = (((1,), (1,)), ((), ()))
    tn_dims = (((0,), (0,)), ((), ()))

    def body(u_ref, dy_ref, bbr_ref, bbi_ref, cr_ref, ci_ref, d_ref, ar_ref, ai_ref,
             du_ref, dbbr_ref, dbbi_ref, dcr_ref, dci_ref, dar_ref, dai_ref, dd_ref, xr, xi, gr, gi):
        _s5_states(u_ref, bbr_ref, bbi_ref, ar_ref, ai_ref, xr, xi, chunk)
        for r0 in range(0, t, chunk):
            rows = pl.ds(r0, chunk)
            dyb = dy_ref[rows, :].astype(BF16)
            gr[rows, :] = lax.dot_general(dyb, cr_ref[...], nt_dims, preferred_element_type=F32)
            gi[rows, :] = -lax.dot_general(dyb, ci_ref[...], nt_dims, preferred_element_type=F32)
        _scan_time(gr, gi, ar_ref[...], -ai_ref[...], reverse=True)

        dcr = jnp.zeros((state_lanes, nb_lanes), F32)
        dci = jnp.zeros((state_lanes, nb_lanes), F32)
        dbr = jnp.zeros((nb_lanes, state_lanes), F32)
        dbi = jnp.zeros((nb_lanes, state_lanes), F32)
        dd = jnp.zeros((1, nb_lanes), F32)
        for r0 in range(0, t, chunk):
            rows = pl.ds(r0, chunk)
            u = u_ref[rows, :]
            dyv = dy_ref[rows, :]
            dyb = dyv.astype(BF16)
            lr, li = gr[rows, :].astype(BF16), gi[rows, :].astype(BF16)
            dcr = dcr + lax.dot_general(xr[rows, :].astype(BF16), dyb, tn_dims, preferred_element_type=F32)
            dci = dci - lax.dot_general(xi[rows, :].astype(BF16), dyb, tn_dims, preferred_element_type=F32)
            dbr = dbr + lax.dot_general(u, lr, tn_dims, preferred_element_type=F32)
            dbi = dbi + lax.dot_general(u, li, tn_dims, preferred_element_type=F32)
            du = lax.dot_general(lr, bbr_ref[...], nt_dims, preferred_element_type=F32)
            du = du + lax.dot_general(li, bbi_ref[...], nt_dims, preferred_element_type=F32)
            du_ref[rows, :] = du + d_ref[...] * dyv
            dd = dd + jnp.sum(dyv * u.astype(F32), axis=0, keepdims=True)
        dcr_ref[...] = dcr
        dci_ref[...] = dci
        dbbr_ref[...] = dbr
        dbbi_ref[...] = dbi
        dd_ref[...] = dd

        first_row = lax.broadcasted_iota(jnp.int32, (TIME_TILE, state_lanes), 0) == 0

        def tile(i, carry):
            pr, pi, acc_r, acc_i = carry
            rows = pl.ds(pl.multiple_of(i * TIME_TILE, TIME_TILE), TIME_TILE)
            x_r, x_i, l_r, l_i = xr[rows, :], xi[rows, :], gr[rows, :], gi[rows, :]
            prev_r = jnp.where(first_row, pr, pltpu.roll(x_r, 1, 0))
            prev_i = jnp.where(first_row, pi, pltpu.roll(x_i, 1, 0))
            acc_r = acc_r + l_r * prev_r + l_i * prev_i
            acc_i = acc_i + l_i * prev_r - l_r * prev_i
            return x_r[TIME_TILE - 1:, :], x_i[TIME_TILE - 1:, :], acc_r, acc_i

        zrow = jnp.zeros((1, state_lanes), F32)
        ztile = jnp.zeros((TIME_TILE, state_lanes), F32)
        _, _, acc_r, acc_i = lax.fori_loop(0, t // TIME_TILE, tile, (zrow, zrow, ztile, ztile))
        dar_ref[...] = jnp.sum(acc_r, axis=0, keepdims=True)
        dai_ref[...] = jnp.sum(acc_i, axis=0, keepdims=True)

    tok, bb, cc, dvec, avec = _s5_specs(t, nb_lanes, state_lanes)
    return pl.pallas_call(
        body, name=name, grid=(nb,),
        in_specs=[tok, tok, bb, bb, cc, cc, dvec, avec, avec],
        out_specs=[tok, bb, bb, cc, cc, avec, avec, dvec],
        out_shape=[jax.ShapeDtypeStruct((t, w), F32)]
        + [jax.ShapeDtypeStruct((nb, nb_lanes, state_lanes), F32)] * 2
        + [jax.ShapeDtypeStruct((nb, state_lanes, nb_lanes), F32)] * 2
        + [jax.ShapeDtypeStruct((1, nb * state_lanes), F32)] * 2
        + [jax.ShapeDtypeStruct((1, w), F32)],
        scratch_shapes=[pltpu.VMEM((t, state_lanes), F32)] * 4,
        compiler_params=_cp(("parallel",)),
    )(u5, dy, bbr, bbi, ccr, cci, dskip, ar, ai)


def _gelu(x):
    return 0.5 * x * (1.0 + jnp.tanh(GELU_C * (x + GELU_A * x * x * x)))


def _gelu_grad(x):
    th = jnp.tanh(GELU_C * (x + GELU_A * x * x * x))
    return 0.5 * (1.0 + th) + 0.5 * x * (1.0 - th * th) * GELU_C * (1.0 + 3.0 * GELU_A * x * x)


def _glu_fwd(y5, w, b, name):
    t, width = y5.shape
    tm = _tile(t, 512, 16)

    def body(y_ref, w_ref, b_ref, o_ref):
        g = _gelu(y_ref[...])
        a = jnp.dot(g.astype(BF16), w_ref[...], preferred_element_type=F32) + b_ref[...]
        o_ref[...] = (g * _sigmoid(a)).astype(o_ref.dtype)

    row = pl.BlockSpec((tm, width), lambda i: (i, 0))
    return pl.pallas_call(
        body, name=name, grid=(t // tm,),
        in_specs=[row, pl.BlockSpec((width, width), lambda i: (0, 0)), pl.BlockSpec((1, width), lambda i: (0, 0))],
        out_specs=row,
        out_shape=jax.ShapeDtypeStruct((t, width), BF16),
        compiler_params=_cp(("parallel",)),
    )(y5, w, b)


def _glu_bwd(y5, dout, w, b, name):
    t, width = y5.shape
    tm = _tile(t, 512, 16)

    def body(y_ref, do_ref, w_ref, b_ref, dy_ref, g_ref, da_ref, db_ref):
        y = y_ref[...]
        g = _gelu(y)
        s = _sigmoid(jnp.dot(g.astype(BF16), w_ref[...], preferred_element_type=F32) + b_ref[...])
        dout_v = do_ref[...].astype(F32)
        da = dout_v * g * s * (1.0 - s)
        dg = dout_v * s + lax.dot_general(da.astype(BF16), w_ref[...], (((1,), (1,)), ((), ())),
                                          preferred_element_type=F32)
        dy_ref[...] = dg * _gelu_grad(y)
        g_ref[...] = g.astype(g_ref.dtype)
        da_ref[...] = da.astype(da_ref.dtype)

        @pl.when(pl.program_id(0) == 0)
        def _():
            db_ref[...] = jnp.zeros_like(db_ref)

        db_ref[...] += jnp.sum(da, axis=0, keepdims=True)

    row = pl.BlockSpec((tm, width), lambda i: (i, 0))
    vec = pl.BlockSpec((1, width), lambda i: (0, 0))
    return pl.pallas_call(
        body, name=name, grid=(t // tm,),
        in_specs=[row, row, pl.BlockSpec((width, width), lambda i: (0, 0)), vec],
        out_specs=[row, row, row, vec],
        out_shape=[jax.ShapeDtypeStruct((t, width), F32), jax.ShapeDtypeStruct((t, width), BF16),
                   jax.ShapeDtypeStruct((t, width), BF16), jax.ShapeDtypeStruct((1, width), F32)],
        compiler_params=_cp(("arbitrary",)),
    )(y5, dout, w, b)


def _adamw(w, g, m, v):
    m = ADAM_B1 * m + (1.0 - ADAM_B1) * g
    v = ADAM_B2 * v + (1.0 - ADAM_B2) * (g * g)
    m_hat = m / (1.0 - ADAM_B1 ** ADAM_STEP)
    v_hat = v / (1.0 - ADAM_B2 ** ADAM_STEP)
    return -ADAM_LR * (m_hat / (jnp.sqrt(v_hat) + ADAM_EPS) + ADAM_WD * w), m, v


def _adamw_shard(w, m, v, sums, got, chip, name):
    rows, cols = w.shape
    tr = _row_tile(rows, cols, target=2**20)

    def body(chip_ref, w_ref, m_ref, v_ref, s_ref, g0_ref, g1_ref, g2_ref, g_out, d_out, m_out, v_out):
        g = s_ref[...].astype(F32) + g0_ref[...].astype(F32) + g1_ref[...].astype(F32) + g2_ref[...].astype(F32)
        delta, m_new, v_new = _adamw(w_ref[...], g, m_ref[...], v_ref[...])
        g_out[...] = g
        d_out[...] = delta
        m_out[...] = m_new
        v_out[...] = v_new

    blk = pl.BlockSpec((tr, cols), lambda i, chip_ref: (i, 0))

    def part(k):
        return pl.BlockSpec((None, tr, cols), lambda i, chip_ref: (k, i, 0))

    return pl.pallas_call(
        body, name=name,
        grid_spec=pltpu.PrefetchScalarGridSpec(
            num_scalar_prefetch=1, grid=(rows // tr,),
            in_specs=[blk, blk, blk, pl.BlockSpec((None, tr, cols), lambda i, chip_ref: (chip_ref[0], i, 0)),
                      part(0), part(1), part(2)],
            out_specs=[blk] * 4),
        out_shape=[jax.ShapeDtypeStruct((rows, cols), F32)] * 4,
        compiler_params=_cp(("parallel",)),
    )(chip, w, m, v, sums, got, got, got)


def _adamw_packed(w, m, v, g, name):
    def body(w_ref, m_ref, v_ref, g_ref, d_out, m_out, v_out):
        delta, m_new, v_new = _adamw(w_ref[...], g_ref[...], m_ref[...], v_ref[...])
        d_out[...] = delta
        m_out[...] = m_new
        v_out[...] = v_new

    vm = pl.BlockSpec(memory_space=pltpu.VMEM)
    return pl.pallas_call(
        body, name=name, in_specs=[vm] * 4, out_specs=[vm] * 3,
        out_shape=[jax.ShapeDtypeStruct(w.shape, F32)] * 3,
        compiler_params=_cp(),
    )(w, m, v, g)


WEIGHTS = ("g_mix", "w_in", "b_fgate", "b_gates", "q_norm", "k_norm", "s5_lambda_re", "s5_lambda_im", "s5_log_step",
           "s5_b_re", "s5_b_im", "s5_c_re", "s5_c_im", "s5_d", "w_glu", "b_glu", "w_proj_fox", "w_proj_s5", "w_out",
           "g_ffn", "w_gate_up", "w_down")
COLUMN_SHARDED = ("w_in", "w_proj_fox", "w_proj_s5", "w_gate_up")
ROW_SHARDED = ("w_glu", "w_out", "w_down")
PACK_ROWS = 8 * LANES


def _pack(arrays):
    flat = jnp.concatenate([a.reshape(-1).astype(F32) for a in arrays])
    flat = jnp.pad(flat, (0, (-flat.shape[0]) % PACK_ROWS))
    return flat.reshape(-1, LANES)


def _unpack(packed, like):
    flat, out, at = packed.reshape(-1), [], 0
    for a in like:
        out.append(flat[at:at + a.size].reshape(a.shape))
        at += a.size
    return out


def _pad_lanes(a):
    return jnp.pad(a, ((0, 0), (0, LANES - a.shape[1])))


def kernel(x, g_mix, w_in, b_fgate, b_gates, q_norm, k_norm, s5_lambda_re, s5_lambda_im, s5_log_step, s5_b_re, s5_b_im,
           s5_c_re, s5_c_im, s5_d, w_glu, b_glu, w_proj_fox, w_proj_s5, w_out, g_ffn, w_gate_up, w_down,
           loss_target, m_g_mix, m_w_in, m_b_fgate, m_b_gates, m_q_norm, m_k_norm, m_s5_lambda_re,
           m_s5_lambda_im, m_s5_log_step, m_s5_b_re, m_s5_b_im, m_s5_c_re, m_s5_c_im, m_s5_d, m_w_glu,
           m_b_glu, m_w_proj_fox, m_w_proj_s5, m_w_out, m_g_ffn, m_w_gate_up, m_w_down, v_g_mix, v_w_in,
           v_b_fgate, v_b_gates, v_q_norm, v_k_norm, v_s5_lambda_re, v_s5_lambda_im, v_s5_log_step, v_s5_b_re,
           v_s5_b_im, v_s5_c_re, v_s5_c_im, v_s5_d, v_w_glu, v_b_glu, v_w_proj_fox, v_w_proj_s5, v_w_out,
           v_g_ffn, v_w_gate_up, v_w_down):
    given = dict(locals())
    weights = {n: given[n] for n in WEIGHTS}
    mom_m = {n: given["m_" + n] for n in WEIGHTS}
    mom_v = {n: given["v_" + n] for n in WEIGHTS}

    pos_x, pos_y, pos_c = _position()
    core = jnp.reshape(pos_c, (1,)).astype(jnp.int32)
    chip = jnp.reshape(2 * pos_x + pos_y, (1,)).astype(jnp.int32)

    xs, target = x[0], loss_target[0]
    t, d = xs.shape
    heads, dh = b_fgate.shape[-1], q_norm.shape[-1]
    fw = heads * dh
    groups, states, gwidth = s5_b_re.shape[1:]
    sw = groups * gwidth
    gp = groups * states
    assert dh == LANES and sw % LANES == 0 and LANES % gwidth == 0
    col_v, col_f, col_s5 = 3 * fw, 3 * fw + heads, 3 * fw + heads + sw

    full = {}
    for n in COLUMN_SHARDED:
        ag = _all_gather(weights[n][0].astype(BF16), "ag_" + n)
        full[n] = ag.transpose(1, 0, 2).reshape(ag.shape[1], N_DEV * ag.shape[2])
    for n in ROW_SHARDED:
        ag = _all_gather(weights[n][0].astype(BF16), "ag_" + n)
        full[n] = ag.reshape(N_DEV * ag.shape[1], ag.shape[2])
    w_main = jnp.concatenate([full["w_in"][:, :col_v], full["w_in"][:, col_f:]], axis=1)
    w_forget = _pad_lanes(full["w_in"][:, col_v:col_f])
    z_s5, z_gate = 3 * fw, 3 * fw + sw

    u, r_mix = _rms_fwd(xs, g_mix, "rms_mix")
    z = _mm(u, w_main, "nn", BF16, "mm_z")
    zf = _mm(u, w_forget, "nn", F32, "mm_zf")
    qn, kn, r_q, r_k = _qk_prep(z, heads, dh, q_norm, k_norm, "qk_prep")
    b_forget = _pad_lanes(b_fgate)
    cum = _forget_fwd(zf, b_forget, "forget_fwd")
    cum_t = cum[:, :heads].T
    fcol, frow = cum_t[:, :, None], cum_t[:, None, :]
    attn, lse = _attn_fwd(qn, kn, z, 2 * heads, fcol, frow, heads, dh, "attn_fwd")

    lam_re, lam_im = s5_lambda_re.reshape(gp, 1), s5_lambda_im.reshape(gp, 1)
    log_step = jnp.repeat(s5_log_step.reshape(groups, 1), states, axis=1).reshape(gp, 1)
    b_re, b_im = s5_b_re.reshape(gp, gwidth), s5_b_im.reshape(gp, gwidth)
    lb_re, lb_im, bb_re, bb_im = _s5_prep(lam_re, lam_im, log_step, b_re, b_im, "s5_prep")
    nb, per = sw // LANES, LANES // gwidth
    eye = jnp.eye(per, dtype=F32)

    def diag_b(bb):
        return jnp.einsum("napi,ab->naibp", bb.reshape(nb, per, states, gwidth), eye).reshape(nb, LANES, per * states)

    def diag_c(c):
        return jnp.einsum("naip,ab->nbpai", c.reshape(nb, per, gwidth, states), eye).reshape(nb, per * states, LANES)

    def undiag_b(g):
        return jnp.einsum("naibp,ab->napi", g.reshape(nb, per, gwidth, per, states), eye).reshape(gp, gwidth)

    def undiag_c(g):
        return jnp.einsum("nbpai,ab->naip", g.reshape(nb, per, states, per, gwidth), eye).reshape(1, groups, gwidth, states)

    bbr, bbi = diag_b(bb_re).astype(BF16), diag_b(bb_im).astype(BF16)
    ccr, cci = diag_c(s5_c_re[0]).astype(BF16), diag_c(s5_c_im[0]).astype(BF16)
    a_re, a_im = lb_re.reshape(1, gp), lb_im.reshape(1, gp)
    d_skip = s5_d.reshape(1, sw)
    u5 = z[:, z_s5:z_s5 + sw]
    y5 = _s5_fwd(u5, bbr, bbi, ccr, cci, d_skip, a_re, a_im, "s5_fwd")
    ssm = _glu_fwd(y5, full["w_glu"], b_glu, "glu_fwd")

    pf = _mm(attn, full["w_proj_fox"], "nn", BF16, "mm_pf")
    ps = _mm(ssm, full["w_proj_s5"], "nn", BF16, "mm_ps")
    merged = _gate_merge_fwd(z, z_gate, b_gates, pf, ps, "merge_fwd")
    mo = _mm(merged, full["w_out"], "nn", F32, "mm_out")
    h, hn, r_ffn = _resid_rms(xs, mo, g_ffn, "resid_rms")
    gu = _mm(hn, full["w_gate_up"], "nn", BF16, "mm_gu")
    act = _swiglu_fwd(gu, "swiglu_fwd")
    dn = _mm(act, full["w_down"], "nn", F32, "mm_down")
    loss_blk, dy = _loss_head(h, dn, target, "loss_head")
    loss = lax.psum(loss_blk[0, 0], ("x", "y", "c"))

    grad = {}
    dact = _mm(dy, full["w_down"], "nt", BF16, "mm_dact")
    grad["w_down"] = _mm(act, dy, "tn", BF16, "mm_gw_down")
    dgate, dup = _swiglu_bwd(gu, dact, "swiglu_bwd")
    dgu = jnp.concatenate([dgate, dup], axis=1)
    dhn = _mm(dgu, full["w_gate_up"], "nt", F32, "mm_dhn")
    grad["w_gate_up"] = _mm(hn, dgu, "tn", BF16, "mm_gw_gu")
    dh_, grad["g_ffn"] = _rms_bwd([dhn], h, r_ffn, g_ffn, [dy], "rms_ffn_bwd")
    dmerged = _mm(dh_, full["w_out"], "nt", BF16, "mm_dmerged")
    grad["w_out"] = _mm(merged, dh_, "tn", BF16, "mm_gw_out")
    dpf, dps, dz_gf, dz_gs, db_gf, db_gs = _gate_merge_bwd(dmerged, z, z_gate, b_gates, pf, ps, "merge_bwd")
    grad["b_gates"] = jnp.concatenate([db_gf, db_gs], axis=1)
    dattn = _mm(dpf, full["w_proj_fox"], "nt", BF16, "mm_dattn")
    grad["w_proj_fox"] = _mm(attn, dpf, "tn", BF16, "mm_gw_pf")
    dssm = _mm(dps, full["w_proj_s5"], "nt", BF16, "mm_dssm")
    grad["w_proj_s5"] = _mm(ssm, dps, "tn", BF16, "mm_gw_ps")

    dy5, g5, da5, grad["b_glu"] = _glu_bwd(y5, dssm, full["w_glu"], b_glu, "glu_bwd")
    grad["w_glu"] = _mm(g5, da5, "tn", BF16, "mm_gw_glu")
    du5, d_bbr, d_bbi, d_ccr, d_cci, d_are, d_aim, d_dskip = _s5_bwd(
        u5, dy5, bbr, bbi, ccr, cci, d_skip, a_re, a_im, "s5_bwd")
    d_lre, d_lim, d_lstep, d_bre, d_bim = _s5_prep_bwd(
        lam_re, lam_im, log_step, b_re, b_im, d_are.reshape(gp, 1), d_aim.reshape(gp, 1),
        undiag_b(d_bbr), undiag_b(d_bbi), groups, "s5_prep_bwd")
    grad["s5_lambda_re"], grad["s5_lambda_im"] = d_lre.reshape(1, groups, states), d_lim.reshape(1, groups, states)
    grad["s5_log_step"] = d_lstep
    grad["s5_b_re"], grad["s5_b_im"] = d_bre.reshape(s5_b_re.shape), d_bim.reshape(s5_b_im.shape)
    grad["s5_c_re"], grad["s5_c_im"] = undiag_c(d_ccr), undiag_c(d_cci)
    grad["s5_d"] = d_dskip.reshape(s5_d.shape)

    delta = _attn_delta(attn, dattn, heads, dh, "attn_delta")
    dqn, df_q = _attn_bwd_q(qn, kn, z, 2 * heads, dattn, fcol, frow, lse, delta, heads, dh, "attn_bwd_q")
    dkn, dv, df_k = _attn_bwd_kv(qn, kn, z, 2 * heads, dattn, fcol, frow, lse, delta, heads, dh, "attn_bwd_kv")
    dq, dk, grad["q_norm"], grad["k_norm"] = _qk_prep_bwd(dqn, dkn, z, heads, dh, q_norm, k_norm, r_q, r_k, "qk_prep_bwd")
    dzf, db_forget = _forget_bwd(_pad_lanes(df_q[:, :, 0].T), _pad_lanes(df_k[:, 0, :].T), zf, b_forget, "forget_bwd")
    grad["b_fgate"] = db_forget[:, :heads]

    dz = jnp.concatenate([dq, dk, dv, du5.astype(BF16), dz_gf, dz_gs], axis=1)
    du = _mm(dz, w_main, "nt", F32, "mm_du")
    du_f = _mm(dzf, w_forget, "nt", F32, "mm_du_f")
    gw_main = _mm(u, dz, "tn", BF16, "mm_gw_main")
    gw_forget = _mm(u, dzf, "tn", BF16, "mm_gw_forget")
    grad["w_in"] = jnp.concatenate([gw_main[:, :col_v], gw_forget[:, :heads], gw_main[:, col_v:]], axis=1)
    dx, grad["g_mix"] = _rms_bwd([du, du_f], xs, r_mix, g_mix, [dh_], "rms_mix_bwd")

    out_g, out_d, out_m, out_v = {}, {}, {}, {}
    for n in COLUMN_SHARDED + ROW_SHARDED:
        g_full = grad[n]
        if n in COLUMN_SHARDED:
            parts = g_full.reshape(g_full.shape[0], N_DEV, g_full.shape[1] // N_DEV).transpose(1, 0, 2)
        else:
            parts = g_full.reshape(N_DEV, g_full.shape[0] // N_DEV, g_full.shape[1])
        got = _swap_with_sibling(parts, "rs_sibling_" + n)
        sums = _add_sibling(parts, got, core, "rs_add_" + n)
        got2 = _swap_with_chips(sums, "rs_chips_" + n)
        res = _adamw_shard(weights[n][0], mom_m[n][0], mom_v[n][0], sums, got2, chip, "adamw_" + n)
        out_g[n], out_d[n], out_m[n], out_v[n] = (r[None] for r in res)

    small = [n for n in WEIGHTS if n not in COLUMN_SHARDED + ROW_SHARDED]
    g_small = _all_reduce_small(_pack([grad[n] for n in small]), "ar_small")
    like = [weights[n] for n in small]
    res = _adamw_packed(_pack(like), _pack([mom_m[n] for n in small]), _pack([mom_v[n] for n in small]), g_small,
                        "adamw_small")
    for store, packed in zip((out_g, out_d, out_m, out_v), (g_small, *res)):
        for n, a in zip(small, _unpack(packed, like)):
            store[n] = a

    return (loss, dx[None], *[out_g[n] for n in WEIGHTS], *[out_d[n] for n in WEIGHTS],
            *[out_m[n] for n in WEIGHTS], *[out_v[n] for n in WEIGHTS])
```

```python
import functools
import math

import jax
import jax.numpy as jnp
from jax import lax
from jax.experimental import pallas as pl
from jax.experimental.pallas import tpu as pltpu

F32 = jnp.float32
BF16 = jnp.bfloat16

V7X_VMEM_LIMIT = 56 * 2**20
LANES = 128
N_DEV = 8
MESH = pl.DeviceIdType.MESH

RMS_EPS = 1e-6
MASK_VALUE = -1e30
ADAM_LR, ADAM_B1, ADAM_B2, ADAM_EPS, ADAM_WD, ADAM_STEP = 0.001, 0.9, 0.999, 1e-08, 0.01, 10
GELU_C = math.sqrt(2.0 / math.pi)
GELU_A = 0.044715


def _cp(sem=None):
    return pltpu.CompilerParams(dimension_semantics=sem, vmem_limit_bytes=V7X_VMEM_LIMIT)


def _tile(n, pref, unit=LANES):
    if n <= pref:
        return n
    t = (pref // unit) * unit
    while t >= unit:
        if n % t == 0:
            return t
        t -= unit
    raise ValueError(f"no tile for {n}")


def _row_tile(rows, cols, bytes_per_row_elem=4, target=2 * 2**20, unit=16):
    best = None
    for t in range(unit, rows + 1, unit):
        if rows % t == 0 and t * cols * bytes_per_row_elem <= target:
            best = t
    if best is None:
        best = unit if rows % unit == 0 else rows
    return best


def _sigmoid(x):
    return 1.0 / (1.0 + jnp.exp(-x))


def _position():
    return lax.axis_index("x"), lax.axis_index("y"), lax.axis_index("c")


def _other_chips(x, y):
    return [(1 - x, y), (x, 1 - y), (1 - x, 1 - y)]


def _all_gather(shard, name):
    rows, cols = shard.shape

    def body(x_ref, out_ref, send_sems, recv_sems, local_sem):
        x, y, c = _position()
        me, sibling = (x, y, c), (x, y, 1 - c)
        chips = _other_chips(x, y)

        def slot(px, py, pc):
            return out_ref.at[4 * px + 2 * py + pc]

        def copy(k, block, to, src=None):
            return pltpu.make_async_remote_copy(
                src_ref=slot(*block) if src is None else src, dst_ref=slot(*block),
                send_sem=send_sems.at[k], recv_sem=recv_sems.at[k], device_id=to, device_id_type=MESH)

        mine = pltpu.make_async_copy(x_ref, slot(*me), local_sem)
        mine.start()
        first = [copy(0, me, sibling, src=x_ref)]
        first += [copy(1 + j, me, (*chip, c), src=x_ref) for j, chip in enumerate(chips)]
        for cp in first:
            cp.start()
        passed = [copy(4 + j, (*chip, c), sibling) for j, chip in enumerate(chips)]
        for j, chip in enumerate(chips):
            copy(1 + j, (*chip, c), me).wait_recv()
            passed[j].start()
        copy(0, sibling, me).wait_recv()
        for j, chip in enumerate(chips):
            copy(4 + j, (*chip, 1 - c), me).wait_recv()
        for cp in first + passed:
            cp.wait_send()
        mine.wait()

    return pl.pallas_call(
        body, name=name,
        out_shape=jax.ShapeDtypeStruct((N_DEV, rows, cols), shard.dtype),
        in_specs=[pl.BlockSpec(memory_space=pltpu.HBM)],
        out_specs=pl.BlockSpec(memory_space=pltpu.HBM),
        scratch_shapes=[pltpu.SemaphoreType.DMA((7,)), pltpu.SemaphoreType.DMA((7,)), pltpu.SemaphoreType.DMA],
    )(shard)


def _swap_with_sibling(parts, name):
    _, rows, cols = parts.shape

    def body(p_ref, out_ref, send_sems, recv_sems):
        x, y, c = _position()
        copies = []
        for j in range(4):
            copies.append(pltpu.make_async_remote_copy(
                src_ref=p_ref.at[2 * j + (1 - c)], dst_ref=out_ref.at[j],
                send_sem=send_sems.at[j], recv_sem=recv_sems.at[j], device_id=(x, y, 1 - c), device_id_type=MESH))
        for cp in copies:
            cp.start()
        for cp in copies:
            cp.wait()

    return pl.pallas_call(
        body, name=name,
        out_shape=jax.ShapeDtypeStruct((4, rows, cols), parts.dtype),
        in_specs=[pl.BlockSpec(memory_space=pltpu.HBM)],
        out_specs=pl.BlockSpec(memory_space=pltpu.HBM),
        scratch_shapes=[pltpu.SemaphoreType.DMA((4,)), pltpu.SemaphoreType.DMA((4,))],
    )(parts)


def _swap_with_chips(sums, name):
    _, rows, cols = sums.shape

    def body(s_ref, out_ref, send_sems, recv_sems):
        x, y, c = _position()
        copies = []
        for k, (px, py) in enumerate(_other_chips(x, y)):
            copies.append(pltpu.make_async_remote_copy(
                src_ref=s_ref.at[2 * px + py], dst_ref=out_ref.at[k],
                send_sem=send_sems.at[k], recv_sem=recv_sems.at[k], device_id=(px, py, c), device_id_type=MESH))
        for cp in copies:
            cp.start()
        for cp in copies:
            cp.wait()

    return pl.pallas_call(
        body, name=name,
        out_shape=jax.ShapeDtypeStruct((3, rows, cols), sums.dtype),
        in_specs=[pl.BlockSpec(memory_space=pltpu.HBM)],
        out_specs=pl.BlockSpec(memory_space=pltpu.HBM),
        scratch_shapes=[pltpu.SemaphoreType.DMA((3,)), pltpu.SemaphoreType.DMA((3,))],
    )(sums)


def _add_sibling(parts, got, core, name):
    _, rows, cols = parts.shape
    tr = _row_tile(rows, cols)

    def body(core_ref, a_ref, b_ref, o_ref):
        o_ref[...] = (a_ref[...].astype(F32) + b_ref[...].astype(F32)).astype(o_ref.dtype)

    return pl.pallas_call(
        body, name=name,
        grid_spec=pltpu.PrefetchScalarGridSpec(
            num_scalar_prefetch=1, grid=(4, rows // tr),
            in_specs=[pl.BlockSpec((None, tr, cols), lambda j, i, core_ref: (2 * j + core_ref[0], i, 0)),
                      pl.BlockSpec((None, tr, cols), lambda j, i, core_ref: (j, i, 0))],
            out_specs=pl.BlockSpec((None, tr, cols), lambda j, i, core_ref: (j, i, 0))),
        out_shape=jax.ShapeDtypeStruct((4, rows, cols), BF16),
        compiler_params=_cp(("parallel", "parallel")),
    )(core, parts, got)


def _all_reduce_small(packed, name):
    rows, cols = packed.shape

    def body(x_ref, out_ref, gathered, send_sems, recv_sems):
        x, y, c = _position()
        me, sibling = (x, y, c), (x, y, 1 - c)
        chips = _other_chips(x, y)

        def slot(px, py, pc):
            return gathered.at[4 * px + 2 * py + pc]

        def copy(k, block, to, src=None):
            return pltpu.make_async_remote_copy(
                src_ref=slot(*block) if src is None else src, dst_ref=slot(*block),
                send_sem=send_sems.at[k], recv_sem=recv_sems.at[k], device_id=to, device_id_type=MESH)

        first = [copy(0, me, sibling, src=x_ref)]
        first += [copy(1 + j, me, (*chip, c), src=x_ref) for j, chip in enumerate(chips)]
        for cp in first:
            cp.start()
        passed = [copy(4 + j, (*chip, c), sibling) for j, chip in enumerate(chips)]
        for j, chip in enumerate(chips):
            copy(1 + j, (*chip, c), me).wait_recv()
            passed[j].start()
        copy(0, sibling, me).wait_recv()
        for j, chip in enumerate(chips):
            copy(4 + j, (*chip, 1 - c), me).wait_recv()
        for cp in first + passed:
            cp.wait_send()
        gathered[4 * x + 2 * y + c] = x_ref[...]
        total = gathered[0]
        for k in range(1, N_DEV):
            total = total + gathered[k]
        out_ref[...] = total

    return pl.pallas_call(
        body, name=name,
        out_shape=jax.ShapeDtypeStruct((rows, cols), F32),
        in_specs=[pl.BlockSpec(memory_space=pltpu.VMEM)],
        out_specs=pl.BlockSpec(memory_space=pltpu.VMEM),
        scratch_shapes=[pltpu.VMEM((N_DEV, rows, cols), F32),
                        pltpu.SemaphoreType.DMA((7,)), pltpu.SemaphoreType.DMA((7,))],
        compiler_params=pltpu.CompilerParams(vmem_limit_bytes=V7X_VMEM_LIMIT),
    )(packed)


MM_VMEM_BUDGET = 40 * 2**20


def _mm_tiles(m, n, k, a_bytes, b_bytes, o_bytes):
    for pref_m, pref_n in ((1024, 1024), (1024, 512), (512, 512), (256, 256)):
        tm, tn = _tile(m, pref_m), _tile(n, pref_n)
        for nk in range(1, k // LANES + 1):
            if k % nk or (k // nk) % LANES:
                continue
            tk = k // nk
            need = 2 * (tm * tk * a_bytes + tk * tn * b_bytes) + 2 * tm * tn * o_bytes + (tm * tn * 4 if nk > 1 else 0)
            if need <= MM_VMEM_BUDGET:
                return tm, tn, tk
    raise ValueError(f"no matmul tiles for {(m, n, k)}")


def _mm(a, b, mode, out_dtype, name):
    if mode == "nn":
        (m, k), (k2, n) = a.shape, b.shape
    elif mode == "nt":
        (m, k), (n, k2) = a.shape, b.shape
    else:
        (k, m), (k2, n) = a.shape, b.shape
    assert k == k2, (name, a.shape, b.shape)
    tm, tn, tk = _mm_tiles(m, n, k, a.dtype.itemsize, b.dtype.itemsize, jnp.dtype(out_dtype).itemsize)
    nk = k // tk
    if mode == "tn":
        a_spec = pl.BlockSpec((tk, tm), lambda i, j, l: (l, i))
        dims = (((0,), (0,)), ((), ()))
    else:
        a_spec = pl.BlockSpec((tm, tk), lambda i, j, l: (i, l))
        dims = (((1,), (1,)), ((), ())) if mode == "nt" else (((1,), (0,)), ((), ()))
    if mode == "nt":
        b_spec = pl.BlockSpec((tn, tk), lambda i, j, l: (j, l))
    else:
        b_spec = pl.BlockSpec((tk, tn), lambda i, j, l: (l, j))

    def product(a_ref, b_ref):
        return lax.dot_general(a_ref[...].astype(BF16), b_ref[...].astype(BF16), dims, preferred_element_type=F32)

    def body_whole_k(a_ref, b_ref, o_ref):
        o_ref[...] = product(a_ref, b_ref).astype(o_ref.dtype)

    def body_split_k(a_ref, b_ref, o_ref, acc_ref):
        l = pl.program_id(2)

        @pl.when(l == 0)
        def _():
            acc_ref[...] = product(a_ref, b_ref)

        @pl.when(l > 0)
        def _():
            acc_ref[...] += product(a_ref, b_ref)

        @pl.when(l == nk - 1)
        def _():
            o_ref[...] = acc_ref[...].astype(o_ref.dtype)

    return pl.pallas_call(
        body_whole_k if nk == 1 else body_split_k, name=name, grid=(m // tm, n // tn, nk),
        in_specs=[a_spec, b_spec],
        out_specs=pl.BlockSpec((tm, tn), lambda i, j, l: (i, j)),
        out_shape=jax.ShapeDtypeStruct((m, n), out_dtype),
        scratch_shapes=[] if nk == 1 else [pltpu.VMEM((tm, tn), F32)],
        compiler_params=_cp(("parallel", "parallel", "arbitrary")),
    )(a, b)


def _rms_fwd(x, g, name):
    t, d = x.shape
    tm = _tile(t, 256, 16)

    def body(x_ref, g_ref, u_ref, r_ref):
        xv = x_ref[...]
        r = lax.rsqrt(jnp.mean(xv * xv, axis=-1, keepdims=True) + RMS_EPS)
        u_ref[...] = (xv * r * g_ref[...]).astype(u_ref.dtype)
        r_ref[...] = r

    return pl.pallas_call(
        body, name=name, grid=(t // tm,),
        in_specs=[pl.BlockSpec((tm, d), lambda i: (i, 0)), pl.BlockSpec((1, d), lambda i: (0, 0))],
        out_specs=[pl.BlockSpec((tm, d), lambda i: (i, 0)), pl.BlockSpec((tm, 1), lambda i: (i, 0))],
        out_shape=[jax.ShapeDtypeStruct((t, d), BF16), jax.ShapeDtypeStruct((t, 1), F32)],
        compiler_params=_cp(("parallel",)),
    )(x, g)


def _rms_bwd(dn_parts, x, r, g, extra, name):
    t, d = x.shape
    tm = _tile(t, 128, 16)
    n_dn, n_extra = len(dn_parts), len(extra)

    def body(*refs):
        dn_refs = refs[:n_dn]
        x_ref, r_ref, g_ref = refs[n_dn:n_dn + 3]
        extra_refs = refs[n_dn + 3:n_dn + 3 + n_extra]
        dx_ref, dxb_ref, dg_ref = refs[n_dn + 3 + n_extra:]
        xhat = x_ref[...] * r_ref[...]
        dnv = dn_refs[0][...].astype(F32)
        for p in dn_refs[1:]:
            dnv = dnv + p[...].astype(F32)
        gd = dnv * g_ref[...]
        dx = r_ref[...] * (gd - xhat * jnp.mean(gd * xhat, axis=-1, keepdims=True))
        for e in extra_refs:
            dx = dx + e[...].astype(F32)
        dx_ref[...] = dx
        dxb_ref[...] = dx.astype(dxb_ref.dtype)

        @pl.when(pl.program_id(0) == 0)
        def _():
            dg_ref[...] = jnp.zeros_like(dg_ref)

        dg_ref[...] += jnp.sum(dnv * xhat, axis=0, keepdims=True)

    row = pl.BlockSpec((tm, d), lambda i: (i, 0))
    return pl.pallas_call(
        body, name=name, grid=(t // tm,),
        in_specs=[row] * n_dn + [row, pl.BlockSpec((tm, 1), lambda i: (i, 0)), pl.BlockSpec((1, d), lambda i: (0, 0))]
        + [row] * n_extra,
        out_specs=[row, row, pl.BlockSpec((1, d), lambda i: (0, 0))],
        out_shape=[jax.ShapeDtypeStruct((t, d), F32), jax.ShapeDtypeStruct((t, d), BF16), jax.ShapeDtypeStruct((1, d), F32)],
        compiler_params=_cp(("arbitrary",)),
    )(*dn_parts, x, r, g, *extra)


def _gate_merge_fwd(z, gate_col, b_gates, pf, ps, name):
    t, d = pf.shape
    tm, tn = _tile(t, 512, 16), _tile(math.gcd(d, gate_col), 512)
    nj, off = d // tn, gate_col // tn
    assert gate_col % tn == 0

    def body(zf_ref, zs_ref, bf_ref, bs_ref, pf_ref, ps_ref, o_ref):
        gf = _sigmoid(zf_ref[...].astype(F32) + bf_ref[...])
        gs = _sigmoid(zs_ref[...].astype(F32) + bs_ref[...])
        o_ref[...] = (gf * pf_ref[...].astype(F32) + gs * ps_ref[...].astype(F32)).astype(o_ref.dtype)

    blk = pl.BlockSpec((tm, tn), lambda i, j: (i, j))
    return pl.pallas_call(
        body, name=name, grid=(t // tm, nj),
        in_specs=[pl.BlockSpec((tm, tn), lambda i, j: (i, off + j)), pl.BlockSpec((tm, tn), lambda i, j: (i, off + nj + j)),
                  pl.BlockSpec((1, tn), lambda i, j: (0, j)), pl.BlockSpec((1, tn), lambda i, j: (0, nj + j)), blk, blk],
        out_specs=blk,
        out_shape=jax.ShapeDtypeStruct((t, d), BF16),
        compiler_params=_cp(("parallel", "parallel")),
    )(z, z, b_gates, b_gates, pf, ps)


def _gate_merge_bwd(dm, z, gate_col, b_gates, pf, ps, name):
    t, d = pf.shape
    tm, tn = _tile(t, 512, 16), _tile(math.gcd(d, gate_col), 512)
    nj, off = d // tn, gate_col // tn

    def body(dm_ref, zf_ref, zs_ref, bf_ref, bs_ref, pf_ref, ps_ref, dpf_ref, dps_ref, dzf_ref, dzs_ref, dbf_ref, dbs_ref):
        gf = _sigmoid(zf_ref[...].astype(F32) + bf_ref[...])
        gs = _sigmoid(zs_ref[...].astype(F32) + bs_ref[...])
        dmv = dm_ref[...].astype(F32)
        dpf_ref[...] = (dmv * gf).astype(dpf_ref.dtype)
        dps_ref[...] = (dmv * gs).astype(dps_ref.dtype)
        dzf = dmv * pf_ref[...].astype(F32) * gf * (1.0 - gf)
        dzs = dmv * ps_ref[...].astype(F32) * gs * (1.0 - gs)
        dzf_ref[...] = dzf.astype(dzf_ref.dtype)
        dzs_ref[...] = dzs.astype(dzs_ref.dtype)

        @pl.when(pl.program_id(1) == 0)
        def _():
            dbf_ref[...] = jnp.zeros_like(dbf_ref)
            dbs_ref[...] = jnp.zeros_like(dbs_ref)

        dbf_ref[...] += jnp.sum(dzf, axis=0, keepdims=True)
        dbs_ref[...] += jnp.sum(dzs, axis=0, keepdims=True)

    blk = pl.BlockSpec((tm, tn), lambda j, i: (i, j))
    lo = pl.BlockSpec((1, tn), lambda j, i: (0, j))
    hi = pl.BlockSpec((1, tn), lambda j, i: (0, nj + j))
    return pl.pallas_call(
        body, name=name, grid=(nj, t // tm),
        in_specs=[blk, pl.BlockSpec((tm, tn), lambda j, i: (i, off + j)), pl.BlockSpec((tm, tn), lambda j, i: (i, off + nj + j)),
                  lo, hi, blk, blk],
        out_specs=[blk, blk, blk, blk, lo, lo],
        out_shape=[jax.ShapeDtypeStruct((t, d), BF16)] * 4 + [jax.ShapeDtypeStruct((1, d), F32)] * 2,
        compiler_params=_cp(("parallel", "arbitrary")),
    )(dm, z, z, b_gates, b_gates, pf, ps)


def _resid_rms(x, mo, g, name):
    t, d = x.shape
    tm = _tile(t, 256, 16)

    def body(x_ref, mo_ref, g_ref, h_ref, hn_ref, r_ref):
        h = x_ref[...] + mo_ref[...].astype(F32)
        r = lax.rsqrt(jnp.mean(h * h, axis=-1, keepdims=True) + RMS_EPS)
        h_ref[...] = h
        hn_ref[...] = (h * r * g_ref[...]).astype(hn_ref.dtype)
        r_ref[...] = r

    row = pl.BlockSpec((tm, d), lambda i: (i, 0))
    col = pl.BlockSpec((tm, 1), lambda i: (i, 0))
    return pl.pallas_call(
        body, name=name, grid=(t // tm,),
        in_specs=[row, row, pl.BlockSpec((1, d), lambda i: (0, 0))],
        out_specs=[row, row, col],
        out_shape=[jax.ShapeDtypeStruct((t, d), F32), jax.ShapeDtypeStruct((t, d), BF16), jax.ShapeDtypeStruct((t, 1), F32)],
        compiler_params=_cp(("parallel",)),
    )(x, mo, g)


def _swiglu_fwd(gu, name):
    t, f2 = gu.shape
    f = f2 // 2
    tm, tn = _tile(t, 512, 16), _tile(f, 1024)
    nj = f // tn

    def body(g_ref, u_ref, o_ref):
        gate = g_ref[...].astype(F32)
        o_ref[...] = (gate * _sigmoid(gate) * u_ref[...].astype(F32)).astype(o_ref.dtype)

    return pl.pallas_call(
        body, name=name, grid=(t // tm, nj),
        in_specs=[pl.BlockSpec((tm, tn), lambda i, j: (i, j)), pl.BlockSpec((tm, tn), lambda i, j: (i, nj + j))],
        out_specs=pl.BlockSpec((tm, tn), lambda i, j: (i, j)),
        out_shape=jax.ShapeDtypeStruct((t, f), BF16),
        compiler_params=_cp(("parallel", "parallel")),
    )(gu, gu)


def _swiglu_bwd(gu, dact, name):
    t, f2 = gu.shape
    f = f2 // 2
    tm, tn = _tile(t, 512, 16), _tile(f, 1024)
    nj = f // tn

    def body(g_ref, u_ref, da_ref, dg_ref, du_ref):
        gate = g_ref[...].astype(F32)
        s = _sigmoid(gate)
        da = da_ref[...].astype(F32)
        dg_ref[...] = (da * u_ref[...].astype(F32) * s * (1.0 + gate * (1.0 - s))).astype(dg_ref.dtype)
        du_ref[...] = (da * gate * s).astype(du_ref.dtype)

    lo = pl.BlockSpec((tm, tn), lambda i, j: (i, j))
    dgate, dup = pl.pallas_call(
        body, name=name, grid=(t // tm, nj),
        in_specs=[lo, pl.BlockSpec((tm, tn), lambda i, j: (i, nj + j)), lo],
        out_specs=[lo, lo],
        out_shape=[jax.ShapeDtypeStruct((t, f), BF16)] * 2,
        compiler_params=_cp(("parallel", "parallel")),
    )(gu, gu, dact)
    return dgate, dup


def _loss_head(h, dn, target, name):
    t, d = h.shape
    tm = _tile(t, 256, 16)

    def body(h_ref, dn_ref, t_ref, loss_ref, dy_ref, dyb_ref):
        err = h_ref[...] + dn_ref[...].astype(F32) - t_ref[...]
        dy_ref[...] = err * (1.0 / d)
        dyb_ref[...] = (err * (1.0 / d)).astype(dyb_ref.dtype)

        @pl.when(pl.program_id(0) == 0)
        def _():
            loss_ref[...] = jnp.zeros_like(loss_ref)

        loss_ref[...] += 0.5 * jnp.sum(jnp.mean(err * err, axis=-1, keepdims=True))

    row = pl.BlockSpec((tm, d), lambda i: (i, 0))
    return pl.pallas_call(
        body, name=name, grid=(t // tm,),
        in_specs=[row, row, row],
        out_specs=[pl.BlockSpec((8, LANES), lambda i: (0, 0)), row, row],
        out_shape=[jax.ShapeDtypeStruct((8, LANES), F32), jax.ShapeDtypeStruct((t, d), F32), jax.ShapeDtypeStruct((t, d), BF16)],
        compiler_params=_cp(("arbitrary",)),
    )(h, dn, target)


def _qk_prep(z, heads, dh, q_norm, k_norm, name):
    t = z.shape[0]
    tq = _tile(t, 512, 16)
    scale = 1.0 / math.sqrt(dh)

    def body(q_ref, k_ref, gq_ref, gk_ref, qn_ref, kn_ref, rq_ref, rk_ref):
        q = q_ref[...].astype(F32)
        k = k_ref[...].astype(F32)
        rq = lax.rsqrt(jnp.mean(q * q, axis=-1, keepdims=True) + RMS_EPS)
        rk = lax.rsqrt(jnp.mean(k * k, axis=-1, keepdims=True) + RMS_EPS)
        qn_ref[...] = (q * rq * gq_ref[...] * scale).astype(qn_ref.dtype)
        kn_ref[...] = (k * rk * gk_ref[...]).astype(kn_ref.dtype)
        rq_ref[...] = rq
        rk_ref[...] = rk

    blk = pl.BlockSpec((tq, dh), lambda i, h: (i, h))
    vec = pl.BlockSpec((1, dh), lambda i, h: (0, 0))
    col = pl.BlockSpec((None, tq, 1), lambda i, h: (h, i, 0))
    return pl.pallas_call(
        body, name=name, grid=(t // tq, heads),
        in_specs=[blk, pl.BlockSpec((tq, dh), lambda i, h: (i, heads + h)), vec, vec],
        out_specs=[blk, blk, col, col],
        out_shape=[jax.ShapeDtypeStruct((t, heads * dh), BF16)] * 2 + [jax.ShapeDtypeStruct((heads, t, 1), F32)] * 2,
        compiler_params=_cp(("parallel", "parallel")),
    )(z, z, q_norm, k_norm)


def _qk_prep_bwd(dqn, dkn, z, heads, dh, q_norm, k_norm, rq, rk, name):
    t = z.shape[0]
    tq = _tile(t, 512, 16)
    scale = 1.0 / math.sqrt(dh)

    def norm_bwd(dy, xv, r, g):
        xhat = xv * r
        gd = dy * g
        return r * (gd - xhat * jnp.mean(gd * xhat, axis=-1, keepdims=True)), jnp.sum(dy * xhat, axis=0, keepdims=True)

    def body(dqn_ref, dkn_ref, q_ref, k_ref, gq_ref, gk_ref, rq_ref, rk_ref, dq_ref, dk_ref, dgq_ref, dgk_ref):
        dq, dgq = norm_bwd(dqn_ref[...].astype(F32) * scale, q_ref[...].astype(F32), rq_ref[...], gq_ref[...])
        dk, dgk = norm_bwd(dkn_ref[...].astype(F32), k_ref[...].astype(F32), rk_ref[...], gk_ref[...])
        dq_ref[...] = dq.astype(dq_ref.dtype)
        dk_ref[...] = dk.astype(dk_ref.dtype)

        @pl.when((pl.program_id(0) == 0) & (pl.program_id(1) == 0))
        def _():
            dgq_ref[...] = jnp.zeros_like(dgq_ref)
            dgk_ref[...] = jnp.zeros_like(dgk_ref)

        dgq_ref[...] += dgq
        dgk_ref[...] += dgk

    blk = pl.BlockSpec((tq, dh), lambda i, h: (i, h))
    vec = pl.BlockSpec((1, dh), lambda i, h: (0, 0))
    col = pl.BlockSpec((None, tq, 1), lambda i, h: (h, i, 0))
    return pl.pallas_call(
        body, name=name, grid=(t // tq, heads),
        in_specs=[blk, blk, blk, pl.BlockSpec((tq, dh), lambda i, h: (i, heads + h)), vec, vec, col, col],
        out_specs=[blk, blk, vec, vec],
        out_shape=[jax.ShapeDtypeStruct((t, heads * dh), BF16)] * 2 + [jax.ShapeDtypeStruct((1, dh), F32)] * 2,
        compiler_params=_cp(("arbitrary", "arbitrary")),
    )(dqn, dkn, z, z, q_norm, k_norm, rq, rk)


def _tri_ones(n, upper):
    row = lax.broadcasted_iota(jnp.int32, (n, n), 0)
    col = lax.broadcasted_iota(jnp.int32, (n, n), 1)
    return jnp.where((col >= row) if upper else (col <= row), 1.0, 0.0).astype(F32)


def _forget_fwd(f, b, name):
    t, w = f.shape
    blk = _tile(t, 256, 8)
    nb = t // blk

    def body(f_ref, b_ref, out_ref):
        tri = _tri_ones(blk, upper=False)

        def step(i, carry):
            rows = pl.ds(pl.multiple_of(i * blk, blk), blk)
            logf = jax.nn.log_sigmoid(f_ref[rows, :] + b_ref[...])
            acc = jnp.dot(tri, logf, precision=lax.Precision.HIGHEST, preferred_element_type=F32) + carry
            out_ref[rows, :] = acc
            return acc[blk - 1:blk, :]

        lax.fori_loop(0, nb, step, jnp.zeros((1, w), F32))

    return pl.pallas_call(
        body, name=name,
        in_specs=[pl.BlockSpec(memory_space=pltpu.VMEM)] * 2,
        out_specs=pl.BlockSpec(memory_space=pltpu.VMEM),
        out_shape=jax.ShapeDtypeStruct((t, w), F32),
        compiler_params=_cp(),
    )(f, b)


def _forget_bwd(d_query, d_key, f, b, name):
    t, w = f.shape
    blk = _tile(t, 256, 8)
    nb = t // blk

    def body(dq_ref, dk_ref, f_ref, b_ref, df_ref, db_ref):
        tri = _tri_ones(blk, upper=True)

        def step(i, carry):
            suffix, db = carry
            rows = pl.ds(pl.multiple_of((nb - 1 - i) * blk, blk), blk)
            dcum = dq_ref[rows, :] - dk_ref[rows, :]
            dlog = jnp.dot(tri, dcum, precision=lax.Precision.HIGHEST, preferred_element_type=F32) + suffix
            df = dlog * _sigmoid(-(f_ref[rows, :] + b_ref[...]))
            df_ref[rows, :] = df
            return dlog[0:1, :], db + jnp.sum(df, axis=0, keepdims=True)

        _, db = lax.fori_loop(0, nb, step, (jnp.zeros((1, w), F32), jnp.zeros((1, w), F32)))
        db_ref[...] = db

    return pl.pallas_call(
        body, name=name,
        in_specs=[pl.BlockSpec(memory_space=pltpu.VMEM)] * 4,
        out_specs=[pl.BlockSpec(memory_space=pltpu.VMEM)] * 2,
        out_shape=[jax.ShapeDtypeStruct((t, w), F32), jax.ShapeDtypeStruct((1, w), F32)],
        compiler_params=_cp(),
    )(d_query, d_key, f, b)


def _attn_logits(q_ref, k_ref, fc_ref, fr_ref, qi, ki, tq, tk):
    s = lax.dot_general(q_ref[...], k_ref[...], (((1,), (1,)), ((), ())), preferred_element_type=F32)
    s = s + fc_ref[...] - fr_ref[...]
    row = qi * tq + lax.broadcasted_iota(jnp.int32, (tq, tk), 0)
    col = ki * tk + lax.broadcasted_iota(jnp.int32, (tq, tk), 1)
    return jnp.where(col <= row, s, MASK_VALUE)


def _attn_fwd(qn, kn, v_src, v_col, fcol, frow, heads, dh, name):
    t = qn.shape[0]
    tq = tk = _tile(t, 512)
    nq = t // tq

    def body(q_ref, k_ref, v_ref, fc_ref, fr_ref, o_ref, lse_ref, m_sc, l_sc, acc_sc):
        qi, ki = pl.program_id(1), pl.program_id(2)

        @pl.when(ki == 0)
        def _():
            m_sc[...] = jnp.full_like(m_sc, MASK_VALUE)
            l_sc[...] = jnp.zeros_like(l_sc)
            acc_sc[...] = jnp.zeros_like(acc_sc)

        @pl.when(ki <= qi)
        def _():
            s = _attn_logits(q_ref, k_ref, fc_ref, fr_ref, qi, ki, tq, tk)
            m_new = jnp.maximum(m_sc[...], jnp.max(s, axis=-1, keepdims=True))
            alpha = jnp.exp(m_sc[...] - m_new)
            p = jnp.exp(s - m_new)
            l_sc[...] = alpha * l_sc[...] + jnp.sum(p, axis=-1, keepdims=True)
            acc_sc[...] = alpha * acc_sc[...] + jnp.dot(p.astype(BF16), v_ref[...].astype(BF16), preferred_element_type=F32)
            m_sc[...] = m_new

        @pl.when(ki == nq - 1)
        def _():
            o_ref[...] = (acc_sc[...] / l_sc[...]).astype(o_ref.dtype)
            lse_ref[...] = m_sc[...] + jnp.log(l_sc[...])

    qblk = pl.BlockSpec((tq, dh), lambda h, i, j: (i, h))
    qcol = pl.BlockSpec((None, tq, 1), lambda h, i, j: (h, i, 0))
    return pl.pallas_call(
        body, name=name, grid=(heads, nq, nq),
        in_specs=[qblk,
                  pl.BlockSpec((tk, dh), lambda h, i, j: (jnp.minimum(j, i), h)),
                  pl.BlockSpec((tk, dh), lambda h, i, j: (jnp.minimum(j, i), v_col + h)),
                  qcol,
                  pl.BlockSpec((None, 1, tk), lambda h, i, j: (h, 0, jnp.minimum(j, i)))],
        out_specs=[qblk, qcol],
        out_shape=[jax.ShapeDtypeStruct((t, heads * dh), BF16), jax.ShapeDtypeStruct((heads, t, 1), F32)],
        scratch_shapes=[pltpu.VMEM((tq, 1), F32), pltpu.VMEM((tq, 1), F32), pltpu.VMEM((tq, dh), F32)],
        compiler_params=_cp(("parallel", "parallel", "arbitrary")),
    )(qn, kn, v_src, fcol, frow)


def _attn_delta(o, do, heads, dh, name):
    t = o.shape[0]
    tq = _tile(t, 512, 16)

    def body(o_ref, do_ref, out_ref):
        out_ref[...] = jnp.sum(o_ref[...].astype(F32) * do_ref[...].astype(F32), axis=-1, keepdims=True)

    blk = pl.BlockSpec((tq, dh), lambda i, h: (i, h))
    return pl.pallas_call(
        body, name=name, grid=(t // tq, heads),
        in_specs=[blk, blk],
        out_specs=pl.BlockSpec((None, tq, 1), lambda i, h: (h, i, 0)),
        out_shape=jax.ShapeDtypeStruct((heads, t, 1), F32),
        compiler_params=_cp(("parallel", "parallel")),
    )(o, do)


def _attn_bwd_q(qn, kn, v_src, v_col, do, fcol, frow, lse, delta, heads, dh, name):
    t = qn.shape[0]
    tq = tk = _tile(t, 512)
    nq = t // tq

    def body(q_ref, k_ref, v_ref, do_ref, fc_ref, fr_ref, lse_ref, dl_ref, dq_ref, dfq_ref, dq_sc, dfq_sc):
        qi, ki = pl.program_id(1), pl.program_id(2)

        @pl.when(ki == 0)
        def _():
            dq_sc[...] = jnp.zeros_like(dq_sc)
            dfq_sc[...] = jnp.zeros_like(dfq_sc)

        @pl.when(ki <= qi)
        def _():
            s = _attn_logits(q_ref, k_ref, fc_ref, fr_ref, qi, ki, tq, tk)
            p = jnp.exp(s - lse_ref[...])
            dp = lax.dot_general(do_ref[...].astype(BF16), v_ref[...].astype(BF16), (((1,), (1,)), ((), ())),
                                 preferred_element_type=F32)
            ds = p * (dp - dl_ref[...])
            dq_sc[...] += jnp.dot(ds.astype(BF16), k_ref[...], preferred_element_type=F32)
            dfq_sc[...] += jnp.sum(ds, axis=-1, keepdims=True)

        @pl.when(ki == nq - 1)
        def _():
            dq_ref[...] = dq_sc[...]
            dfq_ref[...] = dfq_sc[...]

    qblk = pl.BlockSpec((tq, dh), lambda h, i, j: (i, h))
    qcol = pl.BlockSpec((None, tq, 1), lambda h, i, j: (h, i, 0))
    return pl.pallas_call(
        body, name=name, grid=(heads, nq, nq),
        in_specs=[qblk,
                  pl.BlockSpec((tk, dh), lambda h, i, j: (jnp.minimum(j, i), h)),
                  pl.BlockSpec((tk, dh), lambda h, i, j: (jnp.minimum(j, i), v_col + h)),
                  qblk, qcol,
                  pl.BlockSpec((None, 1, tk), lambda h, i, j: (h, 0, jnp.minimum(j, i))),
                  qcol, qcol],
        out_specs=[qblk, qcol],
        out_shape=[jax.ShapeDtypeStruct((t, heads * dh), F32), jax.ShapeDtypeStruct((heads, t, 1), F32)],
        scratch_shapes=[pltpu.VMEM((tq, dh), F32), pltpu.VMEM((tq, 1), F32)],
        compiler_params=_cp(("parallel", "parallel", "arbitrary")),
    )(qn, kn, v_src, do, fcol, frow, lse, delta)


def _attn_bwd_kv(qn, kn, v_src, v_col, do, fcol, frow, lse, delta, heads, dh, name):
    t = qn.shape[0]
    tq = tk = _tile(t, 512)
    nq = t // tq

    def body(q_ref, k_ref, v_ref, do_ref, fc_ref, fr_ref, lse_ref, dl_ref, dk_ref, dv_ref, dfk_ref, dk_sc, dv_sc, dfk_sc):
        ki, qi = pl.program_id(1), pl.program_id(2)

        @pl.when(qi == 0)
        def _():
            dk_sc[...] = jnp.zeros_like(dk_sc)
            dv_sc[...] = jnp.zeros_like(dv_sc)
            dfk_sc[...] = jnp.zeros_like(dfk_sc)

        @pl.when(qi >= ki)
        def _():
            s = _attn_logits(q_ref, k_ref, fc_ref, fr_ref, qi, ki, tq, tk)
            p = jnp.exp(s - lse_ref[...])
            dob = do_ref[...].astype(BF16)
            dp = lax.dot_general(dob, v_ref[...].astype(BF16), (((1,), (1,)), ((), ())), preferred_element_type=F32)
            ds = p * (dp - dl_ref[...])
            dv_sc[...] += lax.dot_general(p.astype(BF16), dob, (((0,), (0,)), ((), ())), preferred_element_type=F32)
            dk_sc[...] += lax.dot_general(ds.astype(BF16), q_ref[...], (((0,), (0,)), ((), ())), preferred_element_type=F32)
            dfk_sc[...] += jnp.sum(ds, axis=0, keepdims=True)

        @pl.when(qi == nq - 1)
        def _():
            dk_ref[...] = dk_sc[...]
            dv_ref[...] = dv_sc[...].astype(dv_ref.dtype)
            dfk_ref[...] = dfk_sc[...]

    qblk = pl.BlockSpec((tq, dh), lambda h, j, i: (jnp.maximum(i, j), h))
    qcol = pl.BlockSpec((None, tq, 1), lambda h, j, i: (h, jnp.maximum(i, j), 0))
    kblk = pl.BlockSpec((tk, dh), lambda h, j, i: (j, h))
    krow = pl.BlockSpec((None, 1, tk), lambda h, j, i: (h, 0, j))
    return pl.pallas_call(
        body, name=name, grid=(heads, nq, nq),
        in_specs=[qblk, kblk, pl.BlockSpec((tk, dh), lambda h, j, i: (j, v_col + h)), qblk, qcol, krow, qcol, qcol],
        out_specs=[kblk, kblk, krow],
        out_shape=[jax.ShapeDtypeStruct((t, heads * dh), F32), jax.ShapeDtypeStruct((t, heads * dh), BF16),
                   jax.ShapeDtypeStruct((heads, 1, t), F32)],
        scratch_shapes=[pltpu.VMEM((tk, dh), F32), pltpu.VMEM((tk, dh), F32), pltpu.VMEM((1, tk), F32)],
        compiler_params=_cp(("parallel", "parallel", "arbitrary")),
    )(qn, kn, v_src, do, fcol, frow, lse, delta)


TIME_TILE = 8


def _s5_discretize(lam_re, lam_im, log_step, b_re, b_im):
    dt = jnp.exp(log_step)
    mag = jnp.exp(lam_re * dt)
    lb_re = mag * jnp.cos(lam_im * dt)
    lb_im = mag * jnp.sin(lam_im * dt)
    denom = lam_re * lam_re + lam_im * lam_im
    num_re = lb_re - 1.0
    fac_re = (num_re * lam_re + lb_im * lam_im) / denom
    fac_im = (lb_im * lam_re - num_re * lam_im) / denom
    return lb_re, lb_im, fac_re * b_re - fac_im * b_im, fac_re * b_im + fac_im * b_re


def _s5_prep(lam_re, lam_im, log_step, b_re, b_im, name):
    gp, width = b_re.shape

    def body(lr, li, ls, br, bi, o_lr, o_li, o_br, o_bi):
        res = _s5_discretize(lr[...], li[...], ls[...], br[...], bi[...])
        for ref, val in zip((o_lr, o_li, o_br, o_bi), res):
            ref[...] = val

    vm = pl.BlockSpec(memory_space=pltpu.VMEM)
    return pl.pallas_call(
        body, name=name, in_specs=[vm] * 5, out_specs=[vm] * 4,
        out_shape=[jax.ShapeDtypeStruct((gp, 1), F32)] * 2 + [jax.ShapeDtypeStruct((gp, width), F32)] * 2,
        compiler_params=_cp(),
    )(lam_re, lam_im, log_step, b_re, b_im)


def _s5_prep_bwd(lam_re, lam_im, log_step, b_re, b_im, d_lb_re, d_lb_im, d_bb_re, d_bb_im, groups, name):
    gp, width = b_re.shape
    states = gp // groups
    tr = _tile(gp, 512, 8)

    def body(lr, li, ls, br, bi, g_lr, g_li, g_br, g_bi, o_lr, o_li, o_ls, o_br, o_bi):
        _, vjp = jax.vjp(_s5_discretize, lr[...], li[...], ls[...], br[...], bi[...])
        d_lr, d_li, d_ls, d_br, d_bi = vjp((g_lr[...], g_li[...], g_br[...], g_bi[...]))
        o_lr[...] = d_lr
        o_li[...] = d_li
        o_br[...] = d_br
        o_bi[...] = d_bi
        row_group = (pl.program_id(0) * tr + lax.broadcasted_iota(jnp.int32, (tr, groups), 0)) // states
        col_group = lax.broadcasted_iota(jnp.int32, (tr, groups), 1)

        @pl.when(pl.program_id(0) == 0)
        def _():
            o_ls[...] = jnp.zeros_like(o_ls)

        o_ls[...] += jnp.sum(jnp.where(row_group == col_group, d_ls, 0.0), axis=0, keepdims=True)

    col = pl.BlockSpec((tr, 1), lambda i: (i, 0))
    mat = pl.BlockSpec((tr, width), lambda i: (i, 0))
    return pl.pallas_call(
        body, name=name, grid=(gp // tr,),
        in_specs=[col, col, col, mat, mat, col, col, mat, mat],
        out_specs=[col, col, pl.BlockSpec((1, groups), lambda i: (0, 0)), mat, mat],
        out_shape=[jax.ShapeDtypeStruct((gp, 1), F32)] * 2 + [jax.ShapeDtypeStruct((1, groups), F32)]
        + [jax.ShapeDtypeStruct((gp, width), F32)] * 2,
        compiler_params=_cp(("arbitrary",)),
    )(lam_re, lam_im, log_step, b_re, b_im, d_lb_re, d_lb_im, d_bb_re, d_bb_im)


def _shift_time(v, s, reverse):
    row = lax.broadcasted_iota(jnp.int32, v.shape, 0)
    if reverse:
        return jnp.where(row < TIME_TILE - s, pltpu.roll(v, TIME_TILE - s, 0), 0.0)
    return jnp.where(row >= s, pltpu.roll(v, s, 0), 0.0)


def _cmul(ar, ai, br, bi):
    return ar * br - ai * bi, ar * bi + ai * br


def _scan_time(xr_ref, xi_ref, ar, ai, reverse):
    t = xr_ref.shape[0]
    n_tiles = t // TIME_TILE
    powers = [(ar, ai)]
    for _ in range(TIME_TILE - 1):
        powers.append(_cmul(*powers[-1], ar, ai))
    order = powers[::-1] if reverse else powers
    carry_r = jnp.concatenate([p[0] for p in order], axis=0)
    carry_i = jnp.concatenate([p[1] for p in order], axis=0)
    levels = [(1, powers[0]), (2, powers[1]), (4, powers[3])]
    last = 0 if reverse else TIME_TILE - 1

    def tile(i, carry):
        cr, ci = carry
        idx = (n_tiles - 1 - i) if reverse else i
        rows = pl.ds(pl.multiple_of(idx * TIME_TILE, TIME_TILE), TIME_TILE)
        br, bi = xr_ref[rows, :], xi_ref[rows, :]
        for s, (pr, pi) in levels:
            sr, si = _cmul(pr, pi, _shift_time(br, s, reverse), _shift_time(bi, s, reverse))
            br, bi = br + sr, bi + si
        kr, ki = _cmul(carry_r, carry_i, cr, ci)
        br, bi = br + kr, bi + ki
        xr_ref[rows, :] = br
        xi_ref[rows, :] = bi
        return br[last:last + 1, :], bi[last:last + 1, :]

    zero = jnp.zeros_like(ar)
    lax.fori_loop(0, n_tiles, tile, (zero, zero))


def _s5_states(u_ref, bbr_ref, bbi_ref, ar_ref, ai_ref, xr, xi, chunk):
    t = u_ref.shape[0]
    for r0 in range(0, t, chunk):
        rows = pl.ds(r0, chunk)
        xr[rows, :] = jnp.dot(u_ref[rows, :], bbr_ref[...], preferred_element_type=F32)
        xi[rows, :] = jnp.dot(u_ref[rows, :], bbi_ref[...], preferred_element_type=F32)
    _scan_time(xr, xi, ar_ref[...], ai_ref[...], reverse=False)


def _s5_specs(t, nb_lanes, state_lanes):
    tok = pl.BlockSpec((t, nb_lanes), lambda j: (0, j))
    bb = pl.BlockSpec((None, nb_lanes, state_lanes), lambda j: (j, 0, 0))
    cc = pl.BlockSpec((None, state_lanes, nb_lanes), lambda j: (j, 0, 0))
    dvec = pl.BlockSpec((1, nb_lanes), lambda j: (0, j))
    avec = pl.BlockSpec((1, state_lanes), lambda j: (0, j))
    return tok, bb, cc, dvec, avec


def _s5_fwd(u5, bbr, bbi, ccr, cci, dskip, ar, ai, name):
    t, w = u5.shape
    nb, nb_lanes, state_lanes = bbr.shape
    chunk = _tile(t, 512, 16)

    def body(u_ref, bbr_ref, bbi_ref, cr_ref, ci_ref, d_ref, ar_ref, ai_ref, y_ref, xr, xi):
        _s5_states(u_ref, bbr_ref, bbi_ref, ar_ref, ai_ref, xr, xi, chunk)
        for r0 in range(0, t, chunk):
            rows = pl.ds(r0, chunk)
            y = jnp.dot(xr[rows, :].astype(BF16), cr_ref[...], preferred_element_type=F32)
            y = y - jnp.dot(xi[rows, :].astype(BF16), ci_ref[...], preferred_element_type=F32)
            y_ref[rows, :] = y + d_ref[...] * u_ref[rows, :].astype(F32)

    tok, bb, cc, dvec, avec = _s5_specs(t, nb_lanes, state_lanes)
    return pl.pallas_call(
        body, name=name, grid=(nb,),
        in_specs=[tok, bb, bb, cc, cc, dvec, avec, avec],
        out_specs=tok,
        out_shape=jax.ShapeDtypeStruct((t, w), F32),
        scratch_shapes=[pltpu.VMEM((t, state_lanes), F32)] * 2,
        compiler_params=_cp(("parallel",)),
    )(u5, bbr, bbi, ccr, cci, dskip, ar, ai)


def _s5_bwd(u5, dy, bbr, bbi, ccr, cci, dskip, ar, ai, name):
    t, w = u5.shape
    nb, nb_lanes, state_lanes = bbr.shape
    chunk = _tile(t, 512, 16)
    nt_dims = (((1,), (1,)), ((), ()))
    tn_dims = (((0,), (0,)), ((), ()))

    def body(u_ref, dy_ref, bbr_ref, bbi_ref, cr_ref, ci_ref, d_ref, ar_ref, ai_ref,
             du_ref, dbbr_ref, dbbi_ref, dcr_ref, dci_ref, dar_ref, dai_ref, dd_ref, xr, xi, gr, gi):
        _s5_states(u_ref, bbr_ref, bbi_ref, ar_ref, ai_ref, xr, xi, chunk)
        for r0 in range(0, t, chunk):
            rows = pl.ds(r0, chunk)
            dyb = dy_ref[rows, :].astype(BF16)
            gr[rows, :] = lax.dot_general(dyb, cr_ref[...], nt_dims, preferred_element_type=F32)
            gi[rows, :] = -lax.dot_general(dyb, ci_ref[...], nt_dims, preferred_element_type=F32)
        _scan_time(gr, gi, ar_ref[...], -ai_ref[...], reverse=True)

        dcr = jnp.zeros((state_lanes, nb_lanes), F32)
        dci = jnp.zeros((state_lanes, nb_lanes), F32)
        dbr = jnp.zeros((nb_lanes, state_lanes), F32)
        dbi = jnp.zeros((nb_lanes, state_lanes), F32)
        dd = jnp.zeros((1, nb_lanes), F32)
        for r0 in range(0, t, chunk):
            rows = pl.ds(r0, chunk)
            u = u_ref[rows, :]
            dyv = dy_ref[rows, :]
            dyb = dyv.astype(BF16)
            lr, li = gr[rows, :].astype(BF16), gi[rows, :].astype(BF16)
            dcr = dcr + lax.dot_general(xr[rows, :].astype(BF16), dyb, tn_dims, preferred_element_type=F32)
            dci = dci - lax.dot_general(xi[rows, :].astype(BF16), dyb, tn_dims, preferred_element_type=F32)
            dbr = dbr + lax.dot_general(u, lr, tn_dims, preferred_element_type=F32)
            dbi = dbi + lax.dot_general(u, li, tn_dims, preferred_element_type=F32)
            du = lax.dot_general(lr, bbr_ref[...], nt_dims, preferred_element_type=F32)
            du = du + lax.dot_general(li, bbi_ref[...], nt_dims, preferred_element_type=F32)
            du_ref[rows, :] = du + d_ref[...] * dyv
            dd = dd + jnp.sum(dyv * u.astype(F32), axis=0, keepdims=True)
        dcr_ref[...] = dcr
        dci_ref[...] = dci
        dbbr_ref[...] = dbr
        dbbi_ref[...] = dbi
        dd_ref[...] = dd

        first_row = lax.broadcasted_iota(jnp.int32, (TIME_TILE, state_lanes), 0) == 0

        def tile(i, carry):
            pr, pi, acc_r, acc_i = carry
            rows = pl.ds(pl.multiple_of(i * TIME_TILE, TIME_TILE), TIME_TILE)
            x_r, x_i, l_r, l_i = xr[rows, :], xi[rows, :], gr[rows, :], gi[rows, :]
            prev_r = jnp.where(first_row, pr, pltpu.roll(x_r, 1, 0))
            prev_i = jnp.where(first_row, pi, pltpu.roll(x_i, 1, 0))
            acc_r = acc_r + l_r * prev_r + l_i * prev_i
            acc_i = acc_i + l_i * prev_r - l_r * prev_i
            return x_r[TIME_TILE - 1:, :], x_i[TIME_TILE - 1:, :], acc_r, acc_i

        zrow = jnp.zeros((1, state_lanes), F32)
        ztile = jnp.zeros((TIME_TILE, state_lanes), F32)
        _, _, acc_r, acc_i = lax.fori_loop(0, t // TIME_TILE, tile, (zrow, zrow, ztile, ztile))
        dar_ref[...] = jnp.sum(acc_r, axis=0, keepdims=True)
        dai_ref[...] = jnp.sum(acc_i, axis=0, keepdims=True)

    tok, bb, cc, dvec, avec = _s5_specs(t, nb_lanes, state_lanes)
    return pl.pallas_call(
        body, name=name, grid=(nb,),
        in_specs=[tok, tok, bb, bb, cc, cc, dvec, avec, avec],
        out_specs=[tok, bb, bb, cc, cc, avec, avec, dvec],
        out_shape=[jax.ShapeDtypeStruct((t, w), F32)]
        + [jax.ShapeDtypeStruct((nb, nb_lanes, state_lanes), F32)] * 2
        + [jax.ShapeDtypeStruct((nb, state_lanes, nb_lanes), F32)] * 2
        + [jax.ShapeDtypeStruct((1, nb * state_lanes), F32)] * 2
        + [jax.ShapeDtypeStruct((1, w), F32)],
        scratch_shapes=[pltpu.VMEM((t, state_lanes), F32)] * 4,
        compiler_params=_cp(("parallel",)),
    )(u5, dy, bbr, bbi, ccr, cci, dskip, ar, ai)


def _gelu(x):
    return 0.5 * x * (1.0 + jnp.tanh(GELU_C * (x + GELU_A * x * x * x)))


def _gelu_grad(x):
    th = jnp.tanh(GELU_C * (x + GELU_A * x * x * x))
    return 0.5 * (1.0 + th) + 0.5 * x * (1.0 - th * th) * GELU_C * (1.0 + 3.0 * GELU_A * x * x)


def _glu_fwd(y5, w, b, name):
    t, width = y5.shape
    tm = _tile(t, 512, 16)

    def body(y_ref, w_ref, b_ref, o_ref):
        g = _gelu(y_ref[...])
        a = jnp.dot(g.astype(BF16), w_ref[...], preferred_element_type=F32) + b_ref[...]
        o_ref[...] = (g * _sigmoid(a)).astype(o_ref.dtype)

    row = pl.BlockSpec((tm, width), lambda i: (i, 0))
    return pl.pallas_call(
        body, name=name, grid=(t // tm,),
        in_specs=[row, pl.BlockSpec((width, width), lambda i: (0, 0)), pl.BlockSpec((1, width), lambda i: (0, 0))],
        out_specs=row,
        out_shape=jax.ShapeDtypeStruct((t, width), BF16),
        compiler_params=_cp(("parallel",)),
    )(y5, w, b)


def _glu_bwd(y5, dout, w, b, name):
    t, width = y5.shape
    tm = _tile(t, 512, 16)

    def body(y_ref, do_ref, w_ref, b_ref, dy_ref, g_ref, da_ref, db_ref):
        y = y_ref[...]
        g = _gelu(y)
        s = _sigmoid(jnp.dot(g.astype(BF16), w_ref[...], preferred_element_type=F32) + b_ref[...])
        dout_v = do_ref[...].astype(F32)
        da = dout_v * g * s * (1.0 - s)
        dg = dout_v * s + lax.dot_general(da.astype(BF16), w_ref[...], (((1,), (1,)), ((), ())),
                                          preferred_element_type=F32)
        dy_ref[...] = dg * _gelu_grad(y)
        g_ref[...] = g.astype(g_ref.dtype)
        da_ref[...] = da.astype(da_ref.dtype)

        @pl.when(pl.program_id(0) == 0)
        def _():
            db_ref[...] = jnp.zeros_like(db_ref)

        db_ref[...] += jnp.sum(da, axis=0, keepdims=True)

    row = pl.BlockSpec((tm, width), lambda i: (i, 0))
    vec = pl.BlockSpec((1, width), lambda i: (0, 0))
    return pl.pallas_call(
        body, name=name, grid=(t // tm,),
        in_specs=[row, row, pl.BlockSpec((width, width), lambda i: (0, 0)), vec],
        out_specs=[row, row, row, vec],
        out_shape=[jax.ShapeDtypeStruct((t, width), F32), jax.ShapeDtypeStruct((t, width), BF16),
                   jax.ShapeDtypeStruct((t, width), BF16), jax.ShapeDtypeStruct((1, width), F32)],
        compiler_params=_cp(("arbitrary",)),
    )(y5, dout, w, b)


def _adamw(w, g, m, v):
    m = ADAM_B1 * m + (1.0 - ADAM_B1) * g
    v = ADAM_B2 * v + (1.0 - ADAM_B2) * (g * g)
    m_hat = m / (1.0 - ADAM_B1 ** ADAM_STEP)
    v_hat = v / (1.0 - ADAM_B2 ** ADAM_STEP)
    return -ADAM_LR * (m_hat / (jnp.sqrt(v_hat) + ADAM_EPS) + ADAM_WD * w), m, v


def _adamw_shard(w, m, v, sums, got, chip, name):
    rows, cols = w.shape
    tr = _row_tile(rows, cols, target=2**20)

    def body(chip_ref, w_ref, m_ref, v_ref, s_ref, g0_ref, g1_ref, g2_ref, g_out, d_out, m_out, v_out):
        g = s_ref[...].astype(F32) + g0_ref[...].astype(F32) + g1_ref[...].astype(F32) + g2_ref[...].astype(F32)
        delta, m_new, v_new = _adamw(w_ref[...], g, m_ref[...], v_ref[...])
        g_out[...] = g
        d_out[...] = delta
        m_out[...] = m_new
        v_out[...] = v_new

    blk = pl.BlockSpec((tr, cols), lambda i, chip_ref: (i, 0))

    def part(k):
        return pl.BlockSpec((None, tr, cols), lambda i, chip_ref: (k, i, 0))

    return pl.pallas_call(
        body, name=name,
        grid_spec=pltpu.PrefetchScalarGridSpec(
            num_scalar_prefetch=1, grid=(rows // tr,),
            in_specs=[blk, blk, blk, pl.BlockSpec((None, tr, cols), lambda i, chip_ref: (chip_ref[0], i, 0)),
                      part(0), part(1), part(2)],
            out_specs=[blk] * 4),
        out_shape=[jax.ShapeDtypeStruct((rows, cols), F32)] * 4,
        compiler_params=_cp(("parallel",)),
    )(chip, w, m, v, sums, got, got, got)


def _adamw_packed(w, m, v, g, name):
    def body(w_ref, m_ref, v_ref, g_ref, d_out, m_out, v_out):
        delta, m_new, v_new = _adamw(w_ref[...], g_ref[...], m_ref[...], v_ref[...])
        d_out[...] = delta
        m_out[...] = m_new
        v_out[...] = v_new

    vm = pl.BlockSpec(memory_space=pltpu.VMEM)
    return pl.pallas_call(
        body, name=name, in_specs=[vm] * 4, out_specs=[vm] * 3,
        out_shape=[jax.ShapeDtypeStruct(w.shape, F32)] * 3,
        compiler_params=_cp(),
    )(w, m, v, g)


WEIGHTS = ("g_mix", "w_in", "b_fgate", "b_gates", "q_norm", "k_norm", "s5_lambda_re", "s5_lambda_im", "s5_log_step",
           "s5_b_re", "s5_b_im", "s5_c_re", "s5_c_im", "s5_d", "w_glu", "b_glu", "w_proj_fox", "w_proj_s5", "w_out",
           "g_ffn", "w_gate_up", "w_down")
COLUMN_SHARDED = ("w_in", "w_proj_fox", "w_proj_s5", "w_gate_up")
ROW_SHARDED = ("w_glu", "w_out", "w_down")
PACK_ROWS = 8 * LANES
FF_ALIGN = 1024


def _pack(arrays):
    flat = jnp.concatenate([a.reshape(-1).astype(F32) for a in arrays])
    flat = jnp.pad(flat, (0, (-flat.shape[0]) % PACK_ROWS))
    return flat.reshape(-1, LANES)


def _unpack(packed, like):
    flat, out, at = packed.reshape(-1), [], 0
    for a in like:
        out.append(flat[at:at + a.size].reshape(a.shape))
        at += a.size
    return out


def _pad_lanes(a):
    return jnp.pad(a, ((0, 0), (0, LANES - a.shape[1])))


def kernel(x, g_mix, w_in, b_fgate, b_gates, q_norm, k_norm, s5_lambda_re, s5_lambda_im, s5_log_step, s5_b_re, s5_b_im,
           s5_c_re, s5_c_im, s5_d, w_glu, b_glu, w_proj_fox, w_proj_s5, w_out, g_ffn, w_gate_up, w_down,
           loss_target, m_g_mix, m_w_in, m_b_fgate, m_b_gates, m_q_norm, m_k_norm, m_s5_lambda_re,
           m_s5_lambda_im, m_s5_log_step, m_s5_b_re, m_s5_b_im, m_s5_c_re, m_s5_c_im, m_s5_d, m_w_glu,
           m_b_glu, m_w_proj_fox, m_w_proj_s5, m_w_out, m_g_ffn, m_w_gate_up, m_w_down, v_g_mix, v_w_in,
           v_b_fgate, v_b_gates, v_q_norm, v_k_norm, v_s5_lambda_re, v_s5_lambda_im, v_s5_log_step, v_s5_b_re,
           v_s5_b_im, v_s5_c_re, v_s5_c_im, v_s5_d, v_w_glu, v_b_glu, v_w_proj_fox, v_w_proj_s5, v_w_out,
           v_g_ffn, v_w_gate_up, v_w_down):
    given = dict(locals())
    weights = {n: given[n] for n in WEIGHTS}
    mom_m = {n: given["m_" + n] for n in WEIGHTS}
    mom_v = {n: given["v_" + n] for n in WEIGHTS}

    pos_x, pos_y, pos_c = _position()
    core = jnp.reshape(pos_c, (1,)).astype(jnp.int32)
    chip = jnp.reshape(2 * pos_x + pos_y, (1,)).astype(jnp.int32)

    xs, target = x[0], loss_target[0]
    t, d = xs.shape
    heads, dh = b_fgate.shape[-1], q_norm.shape[-1]
    fw = heads * dh
    groups, states, gwidth = s5_b_re.shape[1:]
    sw = groups * gwidth
    gp = groups * states
    assert dh == LANES and sw % LANES == 0 and LANES % gwidth == 0
    col_v, col_f, col_s5 = 3 * fw, 3 * fw + heads, 3 * fw + heads + sw

    full = {}
    for n in COLUMN_SHARDED:
        ag = _all_gather(weights[n][0].astype(BF16), "ag_" + n)
        full[n] = ag.transpose(1, 0, 2).reshape(ag.shape[1], N_DEV * ag.shape[2])
    for n in ROW_SHARDED:
        ag = _all_gather(weights[n][0].astype(BF16), "ag_" + n)
        full[n] = ag.reshape(N_DEV * ag.shape[1], ag.shape[2])
    ff = full["w_down"].shape[0]
    ff_pad = -(-ff // FF_ALIGN) * FF_ALIGN
    full["w_gate_up"] = jnp.pad(full["w_gate_up"].reshape(d, 2, ff), ((0, 0), (0, 0), (0, ff_pad - ff))).reshape(d, 2 * ff_pad)
    full["w_down"] = jnp.pad(full["w_down"], ((0, ff_pad - ff), (0, 0)))
    w_main = jnp.concatenate([full["w_in"][:, :col_v], full["w_in"][:, col_f:]], axis=1)
    w_forget = _pad_lanes(full["w_in"][:, col_v:col_f])
    z_s5, z_gate = 3 * fw, 3 * fw + sw

    u, r_mix = _rms_fwd(xs, g_mix, "rms_mix")
    z = _mm(u, w_main, "nn", BF16, "mm_z")
    zf = _mm(u, w_forget, "nn", F32, "mm_zf")
    qn, kn, r_q, r_k = _qk_prep(z, heads, dh, q_norm, k_norm, "qk_prep")
    b_forget = _pad_lanes(b_fgate)
    cum = _forget_fwd(zf, b_forget, "forget_fwd")
    cum_t = cum[:, :heads].T
    fcol, frow = cum_t[:, :, None], cum_t[:, None, :]
    attn, lse = _attn_fwd(qn, kn, z, 2 * heads, fcol, frow, heads, dh, "attn_fwd")

    lam_re, lam_im = s5_lambda_re.reshape(gp, 1), s5_lambda_im.reshape(gp, 1)
    log_step = jnp.repeat(s5_log_step.reshape(groups, 1), states, axis=1).reshape(gp, 1)
    b_re, b_im = s5_b_re.reshape(gp, gwidth), s5_b_im.reshape(gp, gwidth)
    lb_re, lb_im, bb_re, bb_im = _s5_prep(lam_re, lam_im, log_step, b_re, b_im, "s5_prep")
    nb, per = sw // LANES, LANES // gwidth
    eye = jnp.eye(per, dtype=F32)

    def diag_b(bb):
        return jnp.einsum("napi,ab->naibp", bb.reshape(nb, per, states, gwidth), eye).reshape(nb, LANES, per * states)

    def diag_c(c):
        return jnp.einsum("naip,ab->nbpai", c.reshape(nb, per, gwidth, states), eye).reshape(nb, per * states, LANES)

    def undiag_b(g):
        return jnp.einsum("naibp,ab->napi", g.reshape(nb, per, gwidth, per, states), eye).reshape(gp, gwidth)

    def undiag_c(g):
        return jnp.einsum("nbpai,ab->naip", g.reshape(nb, per, states, per, gwidth), eye).reshape(1, groups, gwidth, states)

    bbr, bbi = diag_b(bb_re).astype(BF16), diag_b(bb_im).astype(BF16)
    ccr, cci = diag_c(s5_c_re[0]).astype(BF16), diag_c(s5_c_im[0]).astype(BF16)
    a_re, a_im = lb_re.reshape(1, gp), lb_im.reshape(1, gp)
    d_skip = s5_d.reshape(1, sw)
    u5 = z[:, z_s5:z_s5 + sw]
    y5 = _s5_fwd(u5, bbr, bbi, ccr, cci, d_skip, a_re, a_im, "s5_fwd")
    ssm = _glu_fwd(y5, full["w_glu"], b_glu, "glu_fwd")

    pf = _mm(attn, full["w_proj_fox"], "nn", BF16, "mm_pf")
    ps = _mm(ssm, full["w_proj_s5"], "nn", BF16, "mm_ps")
    merged = _gate_merge_fwd(z, z_gate, b_gates, pf, ps, "merge_fwd")
    mo = _mm(merged, full["w_out"], "nn", F32, "mm_out")
    h, hn, r_ffn = _resid_rms(xs, mo, g_ffn, "resid_rms")
    gu = _mm(hn, full["w_gate_up"], "nn", BF16, "mm_gu")
    act = _swiglu_fwd(gu, "swiglu_fwd")
    dn = _mm(act, full["w_down"], "nn", F32, "mm_down")
    loss_blk, dy, dy_b = _loss_head(h, dn, target, "loss_head")
    loss = lax.psum(loss_blk[0, 0], ("x", "y", "c"))

    grad = {}
    dact = _mm(dy_b, full["w_down"], "nt", BF16, "mm_dact")
    grad["w_down"] = _mm(act, dy_b, "tn", BF16, "mm_gw_down")[:ff]
    dgate, dup = _swiglu_bwd(gu, dact, "swiglu_bwd")
    dgu = jnp.concatenate([dgate, dup], axis=1)
    dhn = _mm(dgu, full["w_gate_up"], "nt", F32, "mm_dhn")
    grad["w_gate_up"] = _mm(hn, dgu, "tn", BF16, "mm_gw_gu").reshape(d, 2, ff_pad)[:, :, :ff].reshape(d, 2 * ff)
    dh_, dh_b, grad["g_ffn"] = _rms_bwd([dhn], h, r_ffn, g_ffn, [dy], "rms_ffn_bwd")
    dmerged = _mm(dh_b, full["w_out"], "nt", BF16, "mm_dmerged")
    grad["w_out"] = _mm(merged, dh_b, "tn", BF16, "mm_gw_out")
    dpf, dps, dz_gf, dz_gs, db_gf, db_gs = _gate_merge_bwd(dmerged, z, z_gate, b_gates, pf, ps, "merge_bwd")
    grad["b_gates"] = jnp.concatenate([db_gf, db_gs], axis=1)
    dattn = _mm(dpf, full["w_proj_fox"], "nt", BF16, "mm_dattn")
    grad["w_proj_fox"] = _mm(attn, dpf, "tn", BF16, "mm_gw_pf")
    dssm = _mm(dps, full["w_proj_s5"], "nt", BF16, "mm_dssm")
    grad["w_proj_s5"] = _mm(ssm, dps, "tn", BF16, "mm_gw_ps")

    dy5, g5, da5, grad["b_glu"] = _glu_bwd(y5, dssm, full["w_glu"], b_glu, "glu_bwd")
    grad["w_glu"] = _mm(g5, da5, "tn", BF16, "mm_gw_glu")
    du5, d_bbr, d_bbi, d_ccr, d_cci, d_are, d_aim, d_dskip = _s5_bwd(
        u5, dy5, bbr, bbi, ccr, cci, d_skip, a_re, a_im, "s5_bwd")
    d_lre, d_lim, d_lstep, d_bre, d_bim = _s5_prep_bwd(
        lam_re, lam_im, log_step, b_re, b_im, d_are.reshape(gp, 1), d_aim.reshape(gp, 1),
        undiag_b(d_bbr), undiag_b(d_bbi), groups, "s5_prep_bwd")
    grad["s5_lambda_re"], grad["s5_lambda_im"] = d_lre.reshape(1, groups, states), d_lim.reshape(1, groups, states)
    grad["s5_log_step"] = d_lstep
    grad["s5_b_re"], grad["s5_b_im"] = d_bre.reshape(s5_b_re.shape), d_bim.reshape(s5_b_im.shape)
    grad["s5_c_re"], grad["s5_c_im"] = undiag_c(d_ccr), undiag_c(d_cci)
    grad["s5_d"] = d_dskip.reshape(s5_d.shape)

    delta = _attn_delta(attn, dattn, heads, dh, "attn_delta")
    dqn, df_q = _attn_bwd_q(qn, kn, z, 2 * heads, dattn, fcol, frow, lse, delta, heads, dh, "attn_bwd_q")
    dkn, dv, df_k = _attn_bwd_kv(qn, kn, z, 2 * heads, dattn, fcol, frow, lse, delta, heads, dh, "attn_bwd_kv")
    dq, dk, grad["q_norm"], grad["k_norm"] = _qk_prep_bwd(dqn, dkn, z, heads, dh, q_norm, k_norm, r_q, r_k, "qk_prep_bwd")
    dzf, db_forget = _forget_bwd(_pad_lanes(df_q[:, :, 0].T), _pad_lanes(df_k[:, 0, :].T), zf, b_forget, "forget_bwd")
    grad["b_fgate"] = db_forget[:, :heads]

    dz = jnp.concatenate([dq, dk, dv, du5.astype(BF16), dz_gf, dz_gs], axis=1)
    du = _mm(dz, w_main, "nt", F32, "mm_du")
    du_f = _mm(dzf, w_forget, "nt", F32, "mm_du_f")
    gw_main = _mm(u, dz, "tn", BF16, "mm_gw_main")
    gw_forget = _mm(u, dzf, "tn", BF16, "mm_gw_forget")
    grad["w_in"] = jnp.concatenate([gw_main[:, :col_v], gw_forget[:, :heads], gw_main[:, col_v:]], axis=1)
    dx, _, grad["g_mix"] = _rms_bwd([du, du_f], xs, r_mix, g_mix, [dh_], "rms_mix_bwd")

    out_g, out_d, out_m, out_v = {}, {}, {}, {}
    for n in COLUMN_SHARDED + ROW_SHARDED:
        g_full = grad[n]
        if n in COLUMN_SHARDED:
            parts = g_full.reshape(g_full.shape[0], N_DEV, g_full.shape[1] // N_DEV).transpose(1, 0, 2)
        else:
            parts = g_full.reshape(N_DEV, g_full.shape[0] // N_DEV, g_full.shape[1])
        got = _swap_with_sibling(parts, "rs_sibling_" + n)
        sums = _add_sibling(parts, got, core, "rs_add_" + n)
        got2 = _swap_with_chips(sums, "rs_chips_" + n)
        res = _adamw_shard(weights[n][0], mom_m[n][0], mom_v[n][0], sums, got2, chip, "adamw_" + n)
        out_g[n], out_d[n], out_m[n], out_v[n] = (r[None] for r in res)

    small = [n for n in WEIGHTS if n not in COLUMN_SHARDED + ROW_SHARDED]
    g_small = _all_reduce_small(_pack([grad[n] for n in small]), "ar_small")
    like = [weights[n] for n in small]
    res = _adamw_packed(_pack(like), _pack([mom_m[n] for n in small]), _pack([mom_v[n] for n in small]), g_small,
                        "adamw_small")
    for store, packed in zip((out_g, out_d, out_m, out_v), (g_small, *res)):
        for n, a in zip(small, _unpack(packed, like)):
            store[n] = a

    return (loss, dx[None], *[out_g[n] for n in WEIGHTS], *[out_d[n] for n in WEIGHTS],
            *[out_m[n] for n in WEIGHTS], *[out_v[n] for n in WEIGHTS])
```

```python
import functools
import math

import jax
import jax.numpy as jnp
from jax import lax
from jax.experimental import pallas as pl
from jax.experimental.pallas import tpu as pltpu

F32 = jnp.float32
BF16 = jnp.bfloat16

V7X_VMEM_LIMIT = 56 * 2**20
LANES = 128
N_DEV = 8
MESH = pl.DeviceIdType.MESH

RMS_EPS = 1e-6
MASK_VALUE = -1e30
ADAM_LR, ADAM_B1, ADAM_B2, ADAM_EPS, ADAM_WD, ADAM_STEP = 0.001, 0.9, 0.999, 1e-08, 0.01, 10
GELU_C = math.sqrt(2.0 / math.pi)
GELU_A = 0.044715


def _cp(sem=None):
    return pltpu.CompilerParams(dimension_semantics=sem, vmem_limit_bytes=V7X_VMEM_LIMIT)


def _tile(n, pref, unit=LANES):
    if n <= pref:
        return n
    t = (pref // unit) * unit
    while t >= unit:
        if n % t == 0:
            return t
        t -= unit
    raise ValueError(f"no tile for {n}")


def _row_tile(rows, cols, bytes_per_row_elem=4, target=2 * 2**20, unit=16):
    best = None
    for t in range(unit, rows + 1, unit):
        if rows % t == 0 and t * cols * bytes_per_row_elem <= target:
            best = t
    if best is None:
        best = unit if rows % unit == 0 else rows
    return best


def _sigmoid(x):
    return 1.0 / (1.0 + jnp.exp(-x))


def _position():
    return lax.axis_index("x"), lax.axis_index("y"), lax.axis_index("c")


def _other_chips(x, y):
    return [(1 - x, y), (x, 1 - y), (1 - x, 1 - y)]


def _all_gather(shard, name):
    rows, cols = shard.shape

    def body(x_ref, out_ref, send_sems, recv_sems, local_sem):
        x, y, c = _position()
        me, sibling = (x, y, c), (x, y, 1 - c)
        chips = _other_chips(x, y)

        def slot(px, py, pc):
            return out_ref.at[4 * px + 2 * py + pc]

        def copy(k, block, to, src=None):
            return pltpu.make_async_remote_copy(
                src_ref=slot(*block) if src is None else src, dst_ref=slot(*block),
                send_sem=send_sems.at[k], recv_sem=recv_sems.at[k], device_id=to, device_id_type=MESH)

        mine = pltpu.make_async_copy(x_ref, slot(*me), local_sem)
        mine.start()
        first = [copy(0, me, sibling, src=x_ref)]
        first += [copy(1 + j, me, (*chip, c), src=x_ref) for j, chip in enumerate(chips)]
        for cp in first:
            cp.start()
        passed = [copy(4 + j, (*chip, c), sibling) for j, chip in enumerate(chips)]
        for j, chip in enumerate(chips):
            copy(1 + j, (*chip, c), me).wait_recv()
            passed[j].start()
        copy(0, sibling, me).wait_recv()
        for j, chip in enumerate(chips):
            copy(4 + j, (*chip, 1 - c), me).wait_recv()
        for cp in first + passed:
            cp.wait_send()
        mine.wait()

    return pl.pallas_call(
        body, name=name,
        out_shape=jax.ShapeDtypeStruct((N_DEV, rows, cols), shard.dtype),
        in_specs=[pl.BlockSpec(memory_space=pltpu.HBM)],
        out_specs=pl.BlockSpec(memory_space=pltpu.HBM),
        scratch_shapes=[pltpu.SemaphoreType.DMA((7,)), pltpu.SemaphoreType.DMA((7,)), pltpu.SemaphoreType.DMA],
    )(shard)


def _swap_with_sibling(parts, name):
    _, rows, cols = parts.shape

    def body(p_ref, out_ref, send_sems, recv_sems):
        x, y, c = _position()
        copies = []
        for j in range(4):
            copies.append(pltpu.make_async_remote_copy(
                src_ref=p_ref.at[2 * j + (1 - c)], dst_ref=out_ref.at[j],
                send_sem=send_sems.at[j], recv_sem=recv_sems.at[j], device_id=(x, y, 1 - c), device_id_type=MESH))
        for cp in copies:
            cp.start()
        for cp in copies:
            cp.wait()

    return pl.pallas_call(
        body, name=name,
        out_shape=jax.ShapeDtypeStruct((4, rows, cols), parts.dtype),
        in_specs=[pl.BlockSpec(memory_space=pltpu.HBM)],
        out_specs=pl.BlockSpec(memory_space=pltpu.HBM),
        scratch_shapes=[pltpu.SemaphoreType.DMA((4,)), pltpu.SemaphoreType.DMA((4,))],
    )(parts)


def _add_sibling(parts, got, core, name):
    _, rows, cols = parts.shape
    tr = _row_tile(rows, cols)

    def body(core_ref, a_ref, b_ref, o_ref):
        o_ref[...] = (a_ref[...].astype(F32) + b_ref[...].astype(F32)).astype(o_ref.dtype)

    return pl.pallas_call(
        body, name=name,
        grid_spec=pltpu.PrefetchScalarGridSpec(
            num_scalar_prefetch=1, grid=(4, rows // tr),
            in_specs=[pl.BlockSpec((None, tr, cols), lambda j, i, core_ref: (2 * j + core_ref[0], i, 0)),
                      pl.BlockSpec((None, tr, cols), lambda j, i, core_ref: (j, i, 0))],
            out_specs=pl.BlockSpec((None, tr, cols), lambda j, i, core_ref: (j, i, 0))),
        out_shape=jax.ShapeDtypeStruct((4, rows, cols), BF16),
        compiler_params=_cp(("parallel", "parallel")),
    )(core, parts, got)


def _all_reduce_small(packed, name):
    rows, cols = packed.shape

    def body(x_ref, out_ref, gathered, send_sems, recv_sems):
        x, y, c = _position()
        me, sibling = (x, y, c), (x, y, 1 - c)
        chips = _other_chips(x, y)

        def slot(px, py, pc):
            return gathered.at[4 * px + 2 * py + pc]

        def copy(k, block, to, src=None):
            return pltpu.make_async_remote_copy(
                src_ref=slot(*block) if src is None else src, dst_ref=slot(*block),
                send_sem=send_sems.at[k], recv_sem=recv_sems.at[k], device_id=to, device_id_type=MESH)

        first = [copy(0, me, sibling, src=x_ref)]
        first += [copy(1 + j, me, (*chip, c), src=x_ref) for j, chip in enumerate(chips)]
        for cp in first:
            cp.start()
        passed = [copy(4 + j, (*chip, c), sibling) for j, chip in enumerate(chips)]
        for j, chip in enumerate(chips):
            copy(1 + j, (*chip, c), me).wait_recv()
            passed[j].start()
        copy(0, sibling, me).wait_recv()
        for j, chip in enumerate(chips):
            copy(4 + j, (*chip, 1 - c), me).wait_recv()
        for cp in first + passed:
            cp.wait_send()
        gathered[4 * x + 2 * y + c] = x_ref[...]
        total = gathered[0]
        for k in range(1, N_DEV):
            total = total + gathered[k]
        out_ref[...] = total

    return pl.pallas_call(
        body, name=name,
        out_shape=jax.ShapeDtypeStruct((rows, cols), F32),
        in_specs=[pl.BlockSpec(memory_space=pltpu.VMEM)],
        out_specs=pl.BlockSpec(memory_space=pltpu.VMEM),
        scratch_shapes=[pltpu.VMEM((N_DEV, rows, cols), F32),
                        pltpu.SemaphoreType.DMA((7,)), pltpu.SemaphoreType.DMA((7,))],
        compiler_params=pltpu.CompilerParams(vmem_limit_bytes=V7X_VMEM_LIMIT),
    )(packed)


class _GatherJob:
    def __init__(self, shards):
        self.inputs = list(shards)
        self.out_shape = [jax.ShapeDtypeStruct((N_DEV,) + s.shape, s.dtype) for s in shards]
        n = len(shards)
        self.scratch = [pltpu.SemaphoreType.DMA((7 * n,)), pltpu.SemaphoreType.DMA((7 * n,)),
                        pltpu.SemaphoreType.DMA((n,))]

    def _plan(self, ins, outs, scratch):
        send_sems, recv_sems, local_sems = scratch
        x, y, c = _position()
        me, sibling = (x, y, c), (x, y, 1 - c)
        chips = _other_chips(x, y)

        def slot(i, px, py, pc):
            return outs[i].at[4 * px + 2 * py + pc]

        def copy(i, k, block, to, src=None):
            return pltpu.make_async_remote_copy(
                src_ref=slot(i, *block) if src is None else src, dst_ref=slot(i, *block),
                send_sem=send_sems.at[7 * i + k], recv_sem=recv_sems.at[7 * i + k], device_id=to, device_id_type=MESH)

        def mine(i):
            return pltpu.make_async_copy(ins[i], slot(i, *me), local_sems.at[i])

        return c, me, sibling, chips, copy, mine

    def begin(self, ins, outs, scratch):
        c, me, sibling, chips, copy, mine = self._plan(ins, outs, scratch)
        for i, x_ref in enumerate(ins):
            mine(i).start()
            copy(i, 0, me, sibling, src=x_ref).start()
            for j, chip in enumerate(chips):
                copy(i, 1 + j, me, (*chip, c), src=x_ref).start()

    def middle(self, ins, outs, scratch):
        c, me, sibling, chips, copy, mine = self._plan(ins, outs, scratch)
        for i in range(len(ins)):
            for j, chip in enumerate(chips):
                copy(i, 1 + j, (*chip, c), me).wait_recv()
                copy(i, 4 + j, (*chip, c), sibling).start()

    def end(self, ins, outs, scratch):
        c, me, sibling, chips, copy, mine = self._plan(ins, outs, scratch)
        for i, x_ref in enumerate(ins):
            copy(i, 0, sibling, me).wait_recv()
            for j, chip in enumerate(chips):
                copy(i, 4 + j, (*chip, 1 - c), me).wait_recv()
            copy(i, 0, me, sibling, src=x_ref).wait_send()
            for j, chip in enumerate(chips):
                copy(i, 1 + j, me, (*chip, c), src=x_ref).wait_send()
                copy(i, 4 + j, (*chip, c), sibling).wait_send()
            mine(i).wait()


class _ChipSwapJob:
    def __init__(self, sums):
        self.inputs = list(sums)
        self.out_shape = [jax.ShapeDtypeStruct((3,) + s.shape[1:], s.dtype) for s in sums]
        n = len(sums)
        self.scratch = [pltpu.SemaphoreType.DMA((3 * n,)), pltpu.SemaphoreType.DMA((3 * n,))]

    def _copies(self, ins, outs, scratch):
        send_sems, recv_sems = scratch
        x, y, c = _position()
        return [pltpu.make_async_remote_copy(
            src_ref=ins[i].at[2 * px + py], dst_ref=outs[i].at[k], send_sem=send_sems.at[3 * i + k],
            recv_sem=recv_sems.at[3 * i + k], device_id=(px, py, c), device_id_type=MESH)
            for i in range(len(ins)) for k, (px, py) in enumerate(_other_chips(x, y))]

    def begin(self, ins, outs, scratch):
        for cp in self._copies(ins, outs, scratch):
            cp.start()

    def middle(self, ins, outs, scratch):
        pass

    def end(self, ins, outs, scratch):
        for cp in self._copies(ins, outs, scratch):
            cp.wait()


def _call(body, *, name, grid, in_specs, out_specs, out_shape, scratch_shapes, semantics, operands, job=None):
    if job is None:
        return pl.pallas_call(
            body, name=name, grid=grid, in_specs=in_specs, out_specs=out_specs, out_shape=out_shape,
            scratch_shapes=scratch_shapes, compiler_params=_cp(semantics))(*operands)
    n_in, n_out, n_scr = len(in_specs), len(out_specs), len(scratch_shapes)
    j_in, j_out = len(job.inputs), len(job.out_shape)
    n_steps = math.prod(grid)
    hbm = pl.BlockSpec(memory_space=pltpu.HBM)

    def carrier(*refs):
        ins, refs = refs[:n_in], refs[n_in:]
        job_ins, refs = refs[:j_in], refs[j_in:]
        outs, refs = refs[:n_out], refs[n_out:]
        job_outs, refs = refs[:j_out], refs[j_out:]
        scr, job_scr = refs[:n_scr], refs[n_scr:]
        step = pl.program_id(0)
        for axis in range(1, len(grid)):
            step = step * grid[axis] + pl.program_id(axis)

        @pl.when(step == 0)
        def _():
            job.begin(job_ins, job_outs, job_scr)

        body(*ins, *outs, *scr)

        @pl.when(step == (3 * n_steps) // 4)
        def _():
            job.middle(job_ins, job_outs, job_scr)

        @pl.when(step == n_steps - 1)
        def _():
            job.end(job_ins, job_outs, job_scr)

    res = pl.pallas_call(
        carrier, name=name, grid=grid,
        in_specs=list(in_specs) + [hbm] * j_in, out_specs=list(out_specs) + [hbm] * j_out,
        out_shape=list(out_shape) + job.out_shape, scratch_shapes=list(scratch_shapes) + job.scratch,
        compiler_params=_cp(("arbitrary",) * len(grid)))(*operands, *job.inputs)
    return res[:n_out], res[n_out:]


MM_VMEM_BUDGET = 40 * 2**20


def _mm_tiles(m, n, k, a_bytes, b_bytes, o_bytes):
    for pref_m, pref_n in ((1024, 1024), (1024, 512), (512, 512), (256, 256)):
        tm, tn = _tile(m, pref_m), _tile(n, pref_n)
        for nk in range(1, k // LANES + 1):
            if k % nk or (k // nk) % LANES:
                continue
            tk = k // nk
            need = 2 * (tm * tk * a_bytes + tk * tn * b_bytes) + 2 * tm * tn * o_bytes + (tm * tn * 4 if nk > 1 else 0)
            if need <= MM_VMEM_BUDGET:
                return tm, tn, tk
    raise ValueError(f"no matmul tiles for {(m, n, k)}")


def _mm(a, b, mode, out_dtype, name, job=None):
    if mode == "nn":
        (m, k), (k2, n) = a.shape, b.shape
    elif mode == "nt":
        (m, k), (n, k2) = a.shape, b.shape
    else:
        (k, m), (k2, n) = a.shape, b.shape
    assert k == k2, (name, a.shape, b.shape)
    tm, tn, tk = _mm_tiles(m, n, k, a.dtype.itemsize, b.dtype.itemsize, jnp.dtype(out_dtype).itemsize)
    nk = k // tk
    if mode == "tn":
        a_spec = pl.BlockSpec((tk, tm), lambda i, j, l: (l, i))
        dims = (((0,), (0,)), ((), ()))
    else:
        a_spec = pl.BlockSpec((tm, tk), lambda i, j, l: (i, l))
        dims = (((1,), (1,)), ((), ())) if mode == "nt" else (((1,), (0,)), ((), ()))
    if mode == "nt":
        b_spec = pl.BlockSpec((tn, tk), lambda i, j, l: (j, l))
    else:
        b_spec = pl.BlockSpec((tk, tn), lambda i, j, l: (l, j))

    def product(a_ref, b_ref):
        return lax.dot_general(a_ref[...].astype(BF16), b_ref[...].astype(BF16), dims, preferred_element_type=F32)

    def body_whole_k(a_ref, b_ref, o_ref):
        o_ref[...] = product(a_ref, b_ref).astype(o_ref.dtype)

    def body_split_k(a_ref, b_ref, o_ref, acc_ref):
        l = pl.program_id(2)

        @pl.when(l == 0)
        def _():
            acc_ref[...] = product(a_ref, b_ref)

        @pl.when(l > 0)
        def _():
            acc_ref[...] += product(a_ref, b_ref)

        @pl.when(l == nk - 1)
        def _():
            o_ref[...] = acc_ref[...].astype(o_ref.dtype)

    res = _call(
        body_whole_k if nk == 1 else body_split_k, name=name, grid=(m // tm, n // tn, nk),
        in_specs=[a_spec, b_spec],
        out_specs=[pl.BlockSpec((tm, tn), lambda i, j, l: (i, j))],
        out_shape=[jax.ShapeDtypeStruct((m, n), out_dtype)],
        scratch_shapes=[] if nk == 1 else [pltpu.VMEM((tm, tn), F32)],
        semantics=("parallel", "parallel", "arbitrary"), operands=(a, b), job=job)
    return res[0] if job is None else (res[0][0], res[1])


def _rms_fwd(x, g, name):
    t, d = x.shape
    tm = _tile(t, 256, 16)

    def body(x_ref, g_ref, u_ref, r_ref):
        xv = x_ref[...]
        r = lax.rsqrt(jnp.mean(xv * xv, axis=-1, keepdims=True) + RMS_EPS)
        u_ref[...] = (xv * r * g_ref[...]).astype(u_ref.dtype)
        r_ref[...] = r

    return pl.pallas_call(
        body, name=name, grid=(t // tm,),
        in_specs=[pl.BlockSpec((tm, d), lambda i: (i, 0)), pl.BlockSpec((1, d), lambda i: (0, 0))],
        out_specs=[pl.BlockSpec((tm, d), lambda i: (i, 0)), pl.BlockSpec((tm, 1), lambda i: (i, 0))],
        out_shape=[jax.ShapeDtypeStruct((t, d), BF16), jax.ShapeDtypeStruct((t, 1), F32)],
        compiler_params=_cp(("parallel",)),
    )(x, g)


def _rms_bwd(dn_parts, x, r, g, extra, name):
    t, d = x.shape
    tm = _tile(t, 128, 16)
    n_dn, n_extra = len(dn_parts), len(extra)

    def body(*refs):
        dn_refs = refs[:n_dn]
        x_ref, r_ref, g_ref = refs[n_dn:n_dn + 3]
        extra_refs = refs[n_dn + 3:n_dn + 3 + n_extra]
        dx_ref, dxb_ref, dg_ref = refs[n_dn + 3 + n_extra:]
        xhat = x_ref[...] * r_ref[...]
        dnv = dn_refs[0][...].astype(F32)
        for p in dn_refs[1:]:
            dnv = dnv + p[...].astype(F32)
        gd = dnv * g_ref[...]
        dx = r_ref[...] * (gd - xhat * jnp.mean(gd * xhat, axis=-1, keepdims=True))
        for e in extra_refs:
            dx = dx + e[...].astype(F32)
        dx_ref[...] = dx
        dxb_ref[...] = dx.astype(dxb_ref.dtype)

        @pl.when(pl.program_id(0) == 0)
        def _():
            dg_ref[...] = jnp.zeros_like(dg_ref)

        dg_ref[...] += jnp.sum(dnv * xhat, axis=0, keepdims=True)

    row = pl.BlockSpec((tm, d), lambda i: (i, 0))
    return pl.pallas_call(
        body, name=name, grid=(t // tm,),
        in_specs=[row] * n_dn + [row, pl.BlockSpec((tm, 1), lambda i: (i, 0)), pl.BlockSpec((1, d), lambda i: (0, 0))]
        + [row] * n_extra,
        out_specs=[row, row, pl.BlockSpec((1, d), lambda i: (0, 0))],
        out_shape=[jax.ShapeDtypeStruct((t, d), F32), jax.ShapeDtypeStruct((t, d), BF16), jax.ShapeDtypeStruct((1, d), F32)],
        compiler_params=_cp(("arbitrary",)),
    )(*dn_parts, x, r, g, *extra)


def _gate_merge_fwd(z, gate_col, b_gates, pf, ps, name):
    t, d = pf.shape
    tm, tn = _tile(t, 512, 16), _tile(math.gcd(d, gate_col), 512)
    nj, off = d // tn, gate_col // tn
    assert gate_col % tn == 0

    def body(zf_ref, zs_ref, bf_ref, bs_ref, pf_ref, ps_ref, o_ref):
        gf = _sigmoid(zf_ref[...].astype(F32) + bf_ref[...])
        gs = _sigmoid(zs_ref[...].astype(F32) + bs_ref[...])
        o_ref[...] = (gf * pf_ref[...].astype(F32) + gs * ps_ref[...].astype(F32)).astype(o_ref.dtype)

    blk = pl.BlockSpec((tm, tn), lambda i, j: (i, j))
    return pl.pallas_call(
        body, name=name, grid=(t // tm, nj),
        in_specs=[pl.BlockSpec((tm, tn), lambda i, j: (i, off + j)), pl.BlockSpec((tm, tn), lambda i, j: (i, off + nj + j)),
                  pl.BlockSpec((1, tn), lambda i, j: (0, j)), pl.BlockSpec((1, tn), lambda i, j: (0, nj + j)), blk, blk],
        out_specs=blk,
        out_shape=jax.ShapeDtypeStruct((t, d), BF16),
        compiler_params=_cp(("parallel", "parallel")),
    )(z, z, b_gates, b_gates, pf, ps)


def _gate_merge_bwd(dm, z, gate_col, b_gates, pf, ps, name):
    t, d = pf.shape
    tm, tn = _tile(t, 512, 16), _tile(math.gcd(d, gate_col), 512)
    nj, off = d // tn, gate_col // tn

    def body(dm_ref, zf_ref, zs_ref, bf_ref, bs_ref, pf_ref, ps_ref, dpf_ref, dps_ref, dzf_ref, dzs_ref, dbf_ref, dbs_ref):
        gf = _sigmoid(zf_ref[...].astype(F32) + bf_ref[...])
        gs = _sigmoid(zs_ref[...].astype(F32) + bs_ref[...])
        dmv = dm_ref[...].astype(F32)
        dpf_ref[...] = (dmv * gf).astype(dpf_ref.dtype)
        dps_ref[...] = (dmv * gs).astype(dps_ref.dtype)
        dzf = dmv * pf_ref[...].astype(F32) * gf * (1.0 - gf)
        dzs = dmv * ps_ref[...].astype(F32) * gs * (1.0 - gs)
        dzf_ref[...] = dzf.astype(dzf_ref.dtype)
        dzs_ref[...] = dzs.astype(dzs_ref.dtype)

        @pl.when(pl.program_id(1) == 0)
        def _():
            dbf_ref[...] = jnp.zeros_like(dbf_ref)
            dbs_ref[...] = jnp.zeros_like(dbs_ref)

        dbf_ref[...] += jnp.sum(dzf, axis=0, keepdims=True)
        dbs_ref[...] += jnp.sum(dzs, axis=0, keepdims=True)

    blk = pl.BlockSpec((tm, tn), lambda j, i: (i, j))
    lo = pl.BlockSpec((1, tn), lambda j, i: (0, j))
    hi = pl.BlockSpec((1, tn), lambda j, i: (0, nj + j))
    return pl.pallas_call(
        body, name=name, grid=(nj, t // tm),
        in_specs=[blk, pl.BlockSpec((tm, tn), lambda j, i: (i, off + j)), pl.BlockSpec((tm, tn), lambda j, i: (i, off + nj + j)),
                  lo, hi, blk, blk],
        out_specs=[blk, blk, blk, blk, lo, lo],
        out_shape=[jax.ShapeDtypeStruct((t, d), BF16)] * 4 + [jax.ShapeDtypeStruct((1, d), F32)] * 2,
        compiler_params=_cp(("parallel", "arbitrary")),
    )(dm, z, z, b_gates, b_gates, pf, ps)


def _resid_rms(x, mo, g, name):
    t, d = x.shape
    tm = _tile(t, 256, 16)

    def body(x_ref, mo_ref, g_ref, h_ref, hn_ref, r_ref):
        h = x_ref[...] + mo_ref[...].astype(F32)
        r = lax.rsqrt(jnp.mean(h * h, axis=-1, keepdims=True) + RMS_EPS)
        h_ref[...] = h
        hn_ref[...] = (h * r * g_ref[...]).astype(hn_ref.dtype)
        r_ref[...] = r

    row = pl.BlockSpec((tm, d), lambda i: (i, 0))
    col = pl.BlockSpec((tm, 1), lambda i: (i, 0))
    return pl.pallas_call(
        body, name=name, grid=(t // tm,),
        in_specs=[row, row, pl.BlockSpec((1, d), lambda i: (0, 0))],
        out_specs=[row, row, col],
        out_shape=[jax.ShapeDtypeStruct((t, d), F32), jax.ShapeDtypeStruct((t, d), BF16), jax.ShapeDtypeStruct((t, 1), F32)],
        compiler_params=_cp(("parallel",)),
    )(x, mo, g)


def _swiglu_fwd(gu, name):
    t, f2 = gu.shape
    f = f2 // 2
    tm, tn = _tile(t, 512, 16), _tile(f, 1024)
    nj = f // tn

    def body(g_ref, u_ref, o_ref):
        gate = g_ref[...].astype(F32)
        o_ref[...] = (gate * _sigmoid(gate) * u_ref[...].astype(F32)).astype(o_ref.dtype)

    return pl.pallas_call(
        body, name=name, grid=(t // tm, nj),
        in_specs=[pl.BlockSpec((tm, tn), lambda i, j: (i, j)), pl.BlockSpec((tm, tn), lambda i, j: (i, nj + j))],
        out_specs=pl.BlockSpec((tm, tn), lambda i, j: (i, j)),
        out_shape=jax.ShapeDtypeStruct((t, f), BF16),
        compiler_params=_cp(("parallel", "parallel")),
    )(gu, gu)


def _swiglu_bwd(gu, dact, name):
    t, f2 = gu.shape
    f = f2 // 2
    tm, tn = _tile(t, 512, 16), _tile(f, 1024)
    nj = f // tn

    def body(g_ref, u_ref, da_ref, dg_ref, du_ref):
        gate = g_ref[...].astype(F32)
        s = _sigmoid(gate)
        da = da_ref[...].astype(F32)
        dg_ref[...] = (da * u_ref[...].astype(F32) * s * (1.0 + gate * (1.0 - s))).astype(dg_ref.dtype)
        du_ref[...] = (da * gate * s).astype(du_ref.dtype)

    lo = pl.BlockSpec((tm, tn), lambda i, j: (i, j))
    dgate, dup = pl.pallas_call(
        body, name=name, grid=(t // tm, nj),
        in_specs=[lo, pl.BlockSpec((tm, tn), lambda i, j: (i, nj + j)), lo],
        out_specs=[lo, lo],
        out_shape=[jax.ShapeDtypeStruct((t, f), BF16)] * 2,
        compiler_params=_cp(("parallel", "parallel")),
    )(gu, gu, dact)
    return dgate, dup


def _loss_head(h, dn, target, name):
    t, d = h.shape
    tm = _tile(t, 256, 16)

    def body(h_ref, dn_ref, t_ref, loss_ref, dy_ref, dyb_ref):
        err = h_ref[...] + dn_ref[...].astype(F32) - t_ref[...]
        dy_ref[...] = err * (1.0 / d)
        dyb_ref[...] = (err * (1.0 / d)).astype(dyb_ref.dtype)

        @pl.when(pl.program_id(0) == 0)
        def _():
            loss_ref[...] = jnp.zeros_like(loss_ref)

        loss_ref[...] += 0.5 * jnp.sum(jnp.mean(err * err, axis=-1, keepdims=True))

    row = pl.BlockSpec((tm, d), lambda i: (i, 0))
    return pl.pallas_call(
        body, name=name, grid=(t // tm,),
        in_specs=[row, row, row],
        out_specs=[pl.BlockSpec((8, LANES), lambda i: (0, 0)), row, row],
        out_shape=[jax.ShapeDtypeStruct((8, LANES), F32), jax.ShapeDtypeStruct((t, d), F32), jax.ShapeDtypeStruct((t, d), BF16)],
        compiler_params=_cp(("arbitrary",)),
    )(h, dn, target)


def _qk_prep(z, heads, dh, q_norm, k_norm, name):
    t = z.shape[0]
    tq = _tile(t, 512, 16)
    scale = 1.0 / math.sqrt(dh)

    def body(q_ref, k_ref, gq_ref, gk_ref, qn_ref, kn_ref, rq_ref, rk_ref):
        q = q_ref[...].astype(F32)
        k = k_ref[...].astype(F32)
        rq = lax.rsqrt(jnp.mean(q * q, axis=-1, keepdims=True) + RMS_EPS)
        rk = lax.rsqrt(jnp.mean(k * k, axis=-1, keepdims=True) + RMS_EPS)
        qn_ref[...] = (q * rq * gq_ref[...] * scale).astype(qn_ref.dtype)
        kn_ref[...] = (k * rk * gk_ref[...]).astype(kn_ref.dtype)
        rq_ref[...] = rq
        rk_ref[...] = rk

    blk = pl.BlockSpec((tq, dh), lambda i, h: (i, h))
    vec = pl.BlockSpec((1, dh), lambda i, h: (0, 0))
    col = pl.BlockSpec((None, tq, 1), lambda i, h: (h, i, 0))
    return pl.pallas_call(
        body, name=name, grid=(t // tq, heads),
        in_specs=[blk, pl.BlockSpec((tq, dh), lambda i, h: (i, heads + h)), vec, vec],
        out_specs=[blk, blk, col, col],
        out_shape=[jax.ShapeDtypeStruct((t, heads * dh), BF16)] * 2 + [jax.ShapeDtypeStruct((heads, t, 1), F32)] * 2,
        compiler_params=_cp(("parallel", "parallel")),
    )(z, z, q_norm, k_norm)


def _qk_prep_bwd(dqn, dkn, z, heads, dh, q_norm, k_norm, rq, rk, name):
    t = z.shape[0]
    tq = _tile(t, 512, 16)
    scale = 1.0 / math.sqrt(dh)

    def norm_bwd(dy, xv, r, g):
        xhat = xv * r
        gd = dy * g
        return r * (gd - xhat * jnp.mean(gd * xhat, axis=-1, keepdims=True)), jnp.sum(dy * xhat, axis=0, keepdims=True)

    def body(dqn_ref, dkn_ref, q_ref, k_ref, gq_ref, gk_ref, rq_ref, rk_ref, dq_ref, dk_ref, dgq_ref, dgk_ref):
        dq, dgq = norm_bwd(dqn_ref[...].astype(F32) * scale, q_ref[...].astype(F32), rq_ref[...], gq_ref[...])
        dk, dgk = norm_bwd(dkn_ref[...].astype(F32), k_ref[...].astype(F32), rk_ref[...], gk_ref[...])
        dq_ref[...] = dq.astype(dq_ref.dtype)
        dk_ref[...] = dk.astype(dk_ref.dtype)

        @pl.when((pl.program_id(0) == 0) & (pl.program_id(1) == 0))
        def _():
            dgq_ref[...] = jnp.zeros_like(dgq_ref)
            dgk_ref[...] = jnp.zeros_like(dgk_ref)

        dgq_ref[...] += dgq
        dgk_ref[...] += dgk

    blk = pl.BlockSpec((tq, dh), lambda i, h: (i, h))
    vec = pl.BlockSpec((1, dh), lambda i, h: (0, 0))
    col = pl.BlockSpec((None, tq, 1), lambda i, h: (h, i, 0))
    return pl.pallas_call(
        body, name=name, grid=(t // tq, heads),
        in_specs=[blk, blk, blk, pl.BlockSpec((tq, dh), lambda i, h: (i, heads + h)), vec, vec, col, col],
        out_specs=[blk, blk, vec, vec],
        out_shape=[jax.ShapeDtypeStruct((t, heads * dh), BF16)] * 2 + [jax.ShapeDtypeStruct((1, dh), F32)] * 2,
        compiler_params=_cp(("arbitrary", "arbitrary")),
    )(dqn, dkn, z, z, q_norm, k_norm, rq, rk)


def _tri_ones(n, upper):
    row = lax.broadcasted_iota(jnp.int32, (n, n), 0)
    col = lax.broadcasted_iota(jnp.int32, (n, n), 1)
    return jnp.where((col >= row) if upper else (col <= row), 1.0, 0.0).astype(F32)


def _forget_fwd(f, b, name):
    t, w = f.shape
    blk = _tile(t, 256, 8)
    nb = t // blk

    def body(f_ref, b_ref, out_ref):
        tri = _tri_ones(blk, upper=False)

        def step(i, carry):
            rows = pl.ds(pl.multiple_of(i * blk, blk), blk)
            logf = jax.nn.log_sigmoid(f_ref[rows, :] + b_ref[...])
            acc = jnp.dot(tri, logf, precision=lax.Precision.HIGHEST, preferred_element_type=F32) + carry
            out_ref[rows, :] = acc
            return acc[blk - 1:blk, :]

        lax.fori_loop(0, nb, step, jnp.zeros((1, w), F32))

    return pl.pallas_call(
        body, name=name,
        in_specs=[pl.BlockSpec(memory_space=pltpu.VMEM)] * 2,
        out_specs=pl.BlockSpec(memory_space=pltpu.VMEM),
        out_shape=jax.ShapeDtypeStruct((t, w), F32),
        compiler_params=_cp(),
    )(f, b)


def _forget_bwd(d_query, d_key, f, b, name):
    t, w = f.shape
    blk = _tile(t, 256, 8)
    nb = t // blk

    def body(dq_ref, dk_ref, f_ref, b_ref, df_ref, db_ref):
        tri = _tri_ones(blk, upper=True)

        def step(i, carry):
            suffix, db = carry
            rows = pl.ds(pl.multiple_of((nb - 1 - i) * blk, blk), blk)
            dcum = dq_ref[rows, :] - dk_ref[rows, :]
            dlog = jnp.dot(tri, dcum, precision=lax.Precision.HIGHEST, preferred_element_type=F32) + suffix
            df = dlog * _sigmoid(-(f_ref[rows, :] + b_ref[...]))
            df_ref[rows, :] = df
            return dlog[0:1, :], db + jnp.sum(df, axis=0, keepdims=True)

        _, db = lax.fori_loop(0, nb, step, (jnp.zeros((1, w), F32), jnp.zeros((1, w), F32)))
        db_ref[...] = db

    return pl.pallas_call(
        body, name=name,
        in_specs=[pl.BlockSpec(memory_space=pltpu.VMEM)] * 4,
        out_specs=[pl.BlockSpec(memory_space=pltpu.VMEM)] * 2,
        out_shape=[jax.ShapeDtypeStruct((t, w), F32), jax.ShapeDtypeStruct((1, w), F32)],
        compiler_params=_cp(),
    )(d_query, d_key, f, b)


def _attn_logits(q_ref, k_ref, fc_ref, fr_ref, qi, ki, tq, tk):
    s = lax.dot_general(q_ref[...], k_ref[...], (((1,), (1,)), ((), ())), preferred_element_type=F32)
    s = s + fc_ref[...] - fr_ref[...]
    row = qi * tq + lax.broadcasted_iota(jnp.int32, (tq, tk), 0)
    col = ki * tk + lax.broadcasted_iota(jnp.int32, (tq, tk), 1)
    return jnp.where(col <= row, s, MASK_VALUE)


def _attn_fwd(qn, kn, v_src, v_col, fcol, frow, heads, dh, name, job=None):
    t = qn.shape[0]
    tq = tk = _tile(t, 512)
    nq = t // tq

    def body(q_ref, k_ref, v_ref, fc_ref, fr_ref, o_ref, lse_ref, m_sc, l_sc, acc_sc):
        qi, ki = pl.program_id(1), pl.program_id(2)

        @pl.when(ki == 0)
        def _():
            m_sc[...] = jnp.full_like(m_sc, MASK_VALUE)
            l_sc[...] = jnp.zeros_like(l_sc)
            acc_sc[...] = jnp.zeros_like(acc_sc)

        @pl.when(ki <= qi)
        def _():
            s = _attn_logits(q_ref, k_ref, fc_ref, fr_ref, qi, ki, tq, tk)
            m_new = jnp.maximum(m_sc[...], jnp.max(s, axis=-1, keepdims=True))
            alpha = jnp.exp(m_sc[...] - m_new)
            p = jnp.exp(s - m_new)
            l_sc[...] = alpha * l_sc[...] + jnp.sum(p, axis=-1, keepdims=True)
            acc_sc[...] = alpha * acc_sc[...] + jnp.dot(p.astype(BF16), v_ref[...].astype(BF16), preferred_element_type=F32)
            m_sc[...] = m_new

        @pl.when(ki == nq - 1)
        def _():
            o_ref[...] = (acc_sc[...] / l_sc[...]).astype(o_ref.dtype)
            lse_ref[...] = m_sc[...] + jnp.log(l_sc[...])

    qblk = pl.BlockSpec((tq, dh), lambda h, i, j: (i, h))
    qcol = pl.BlockSpec((None, tq, 1), lambda h, i, j: (h, i, 0))
    return _call(
        body, name=name, grid=(heads, nq, nq),
        in_specs=[qblk,
                  pl.BlockSpec((tk, dh), lambda h, i, j: (jnp.minimum(j, i), h)),
                  pl.BlockSpec((tk, dh), lambda h, i, j: (jnp.minimum(j, i), v_col + h)),
                  qcol,
                  pl.BlockSpec((None, 1, tk), lambda h, i, j: (h, 0, jnp.minimum(j, i)))],
        out_specs=[qblk, qcol],
        out_shape=[jax.ShapeDtypeStruct((t, heads * dh), BF16), jax.ShapeDtypeStruct((heads, t, 1), F32)],
        scratch_shapes=[pltpu.VMEM((tq, 1), F32), pltpu.VMEM((tq, 1), F32), pltpu.VMEM((tq, dh), F32)],
        semantics=("parallel", "parallel", "arbitrary"), operands=(qn, kn, v_src, fcol, frow), job=job)


def _attn_delta(o, do, heads, dh, name):
    t = o.shape[0]
    tq = _tile(t, 512, 16)

    def body(o_ref, do_ref, out_ref):
        out_ref[...] = jnp.sum(o_ref[...].astype(F32) * do_ref[...].astype(F32), axis=-1, keepdims=True)

    blk = pl.BlockSpec((tq, dh), lambda i, h: (i, h))
    return pl.pallas_call(
        body, name=name, grid=(t // tq, heads),
        in_specs=[blk, blk],
        out_specs=pl.BlockSpec((None, tq, 1), lambda i, h: (h, i, 0)),
        out_shape=jax.ShapeDtypeStruct((heads, t, 1), F32),
        compiler_params=_cp(("parallel", "parallel")),
    )(o, do)


def _attn_bwd_q(qn, kn, v_src, v_col, do, fcol, frow, lse, delta, heads, dh, name, job=None):
    t = qn.shape[0]
    tq = tk = _tile(t, 512)
    nq = t // tq

    def body(q_ref, k_ref, v_ref, do_ref, fc_ref, fr_ref, lse_ref, dl_ref, dq_ref, dfq_ref, dq_sc, dfq_sc):
        qi, ki = pl.program_id(1), pl.program_id(2)

        @pl.when(ki == 0)
        def _():
            dq_sc[...] = jnp.zeros_like(dq_sc)
            dfq_sc[...] = jnp.zeros_like(dfq_sc)

        @pl.when(ki <= qi)
        def _():
            s = _attn_logits(q_ref, k_ref, fc_ref, fr_ref, qi, ki, tq, tk)
            p = jnp.exp(s - lse_ref[...])
            dp = lax.dot_general(do_ref[...].astype(BF16), v_ref[...].astype(BF16), (((1,), (1,)), ((), ())),
                                 preferred_element_type=F32)
            ds = p * (dp - dl_ref[...])
            dq_sc[...] += jnp.dot(ds.astype(BF16), k_ref[...], preferred_element_type=F32)
            dfq_sc[...] += jnp.sum(ds, axis=-1, keepdims=True)

        @pl.when(ki == nq - 1)
        def _():
            dq_ref[...] = dq_sc[...]
            dfq_ref[...] = dfq_sc[...]

    qblk = pl.BlockSpec((tq, dh), lambda h, i, j: (i, h))
    qcol = pl.BlockSpec((None, tq, 1), lambda h, i, j: (h, i, 0))
    return _call(
        body, name=name, grid=(heads, nq, nq),
        in_specs=[qblk,
                  pl.BlockSpec((tk, dh), lambda h, i, j: (jnp.minimum(j, i), h)),
                  pl.BlockSpec((tk, dh), lambda h, i, j: (jnp.minimum(j, i), v_col + h)),
                  qblk, qcol,
                  pl.BlockSpec((None, 1, tk), lambda h, i, j: (h, 0, jnp.minimum(j, i))),
                  qcol, qcol],
        out_specs=[qblk, qcol],
        out_shape=[jax.ShapeDtypeStruct((t, heads * dh), F32), jax.ShapeDtypeStruct((heads, t, 1), F32)],
        scratch_shapes=[pltpu.VMEM((tq, dh), F32), pltpu.VMEM((tq, 1), F32)],
        semantics=("parallel", "parallel", "arbitrary"),
        operands=(qn, kn, v_src, do, fcol, frow, lse, delta), job=job)


def _attn_bwd_kv(qn, kn, v_src, v_col, do, fcol, frow, lse, delta, heads, dh, name, job=None):
    t = qn.shape[0]
    tq = tk = _tile(t, 512)
    nq = t // tq

    def body(q_ref, k_ref, v_ref, do_ref, fc_ref, fr_ref, lse_ref, dl_ref, dk_ref, dv_ref, dfk_ref, dk_sc, dv_sc, dfk_sc):
        ki, qi = pl.program_id(1), pl.program_id(2)

        @pl.when(qi == 0)
        def _():
            dk_sc[...] = jnp.zeros_like(dk_sc)
            dv_sc[...] = jnp.zeros_like(dv_sc)
            dfk_sc[...] = jnp.zeros_like(dfk_sc)

        @pl.when(qi >= ki)
        def _():
            s = _attn_logits(q_ref, k_ref, fc_ref, fr_ref, qi, ki, tq, tk)
            p = jnp.exp(s - lse_ref[...])
            dob = do_ref[...].astype(BF16)
            dp = lax.dot_general(dob, v_ref[...].astype(BF16), (((1,), (1,)), ((), ())), preferred_element_type=F32)
            ds = p * (dp - dl_ref[...])
            dv_sc[...] += lax.dot_general(p.astype(BF16), dob, (((0,), (0,)), ((), ())), preferred_element_type=F32)
            dk_sc[...] += lax.dot_general(ds.astype(BF16), q_ref[...], (((0,), (0,)), ((), ())), preferred_element_type=F32)
            dfk_sc[...] += jnp.sum(ds, axis=0, keepdims=True)

        @pl.when(qi == nq - 1)
        def _():
            dk_ref[...] = dk_sc[...]
            dv_ref[...] = dv_sc[...].astype(dv_ref.dtype)
            dfk_ref[...] = dfk_sc[...]

    qblk = pl.BlockSpec((tq, dh), lambda h, j, i: (jnp.maximum(i, j), h))
    qcol = pl.BlockSpec((None, tq, 1), lambda h, j, i: (h, jnp.maximum(i, j), 0))
    kblk = pl.BlockSpec((tk, dh), lambda h, j, i: (j, h))
    krow = pl.BlockSpec((None, 1, tk), lambda h, j, i: (h, 0, j))
    return _call(
        body, name=name, grid=(heads, nq, nq),
        in_specs=[qblk, kblk, pl.BlockSpec((tk, dh), lambda h, j, i: (j, v_col + h)), qblk, qcol, krow, qcol, qcol],
        out_specs=[kblk, kblk, krow],
        out_shape=[jax.ShapeDtypeStruct((t, heads * dh), F32), jax.ShapeDtypeStruct((t, heads * dh), BF16),
                   jax.ShapeDtypeStruct((heads, 1, t), F32)],
        scratch_shapes=[pltpu.VMEM((tk, dh), F32), pltpu.VMEM((tk, dh), F32), pltpu.VMEM((1, tk), F32)],
        semantics=("parallel", "parallel", "arbitrary"),
        operands=(qn, kn, v_src, do, fcol, frow, lse, delta), job=job)


TIME_TILE = 8


def _s5_discretize(lam_re, lam_im, log_step, b_re, b_im):
    dt = jnp.exp(log_step)
    mag = jnp.exp(lam_re * dt)
    lb_re = mag * jnp.cos(lam_im * dt)
    lb_im = mag * jnp.sin(lam_im * dt)
    denom = lam_re * lam_re + lam_im * lam_im
    num_re = lb_re - 1.0
    fac_re = (num_re * lam_re + lb_im * lam_im) / denom
    fac_im = (lb_im * lam_re - num_re * lam_im) / denom
    return lb_re, lb_im, fac_re * b_re - fac_im * b_im, fac_re * b_im + fac_im * b_re


def _s5_prep(lam_re, lam_im, log_step, b_re, b_im, name):
    gp, width = b_re.shape

    def body(lr, li, ls, br, bi, o_lr, o_li, o_br, o_bi):
        res = _s5_discretize(lr[...], li[...], ls[...], br[...], bi[...])
        for ref, val in zip((o_lr, o_li, o_br, o_bi), res):
            ref[...] = val

    vm = pl.BlockSpec(memory_space=pltpu.VMEM)
    return pl.pallas_call(
        body, name=name, in_specs=[vm] * 5, out_specs=[vm] * 4,
        out_shape=[jax.ShapeDtypeStruct((gp, 1), F32)] * 2 + [jax.ShapeDtypeStruct((gp, width), F32)] * 2,
        compiler_params=_cp(),
    )(lam_re, lam_im, log_step, b_re, b_im)


def _s5_prep_bwd(lam_re, lam_im, log_step, b_re, b_im, d_lb_re, d_lb_im, d_bb_re, d_bb_im, groups, name):
    gp, width = b_re.shape
    states = gp // groups
    tr = _tile(gp, 512, 8)

    def body(lr, li, ls, br, bi, g_lr, g_li, g_br, g_bi, o_lr, o_li, o_ls, o_br, o_bi):
        _, vjp = jax.vjp(_s5_discretize, lr[...], li[...], ls[...], br[...], bi[...])
        d_lr, d_li, d_ls, d_br, d_bi = vjp((g_lr[...], g_li[...], g_br[...], g_bi[...]))
        o_lr[...] = d_lr
        o_li[...] = d_li
        o_br[...] = d_br
        o_bi[...] = d_bi
        row_group = (pl.program_id(0) * tr + lax.broadcasted_iota(jnp.int32, (tr, groups), 0)) // states
        col_group = lax.broadcasted_iota(jnp.int32, (tr, groups), 1)

        @pl.when(pl.program_id(0) == 0)
        def _():
            o_ls[...] = jnp.zeros_like(o_ls)

        o_ls[...] += jnp.sum(jnp.where(row_group == col_group, d_ls, 0.0), axis=0, keepdims=True)

    col = pl.BlockSpec((tr, 1), lambda i: (i, 0))
    mat = pl.BlockSpec((tr, width), lambda i: (i, 0))
    return pl.pallas_call(
        body, name=name, grid=(gp // tr,),
        in_specs=[col, col, col, mat, mat, col, col, mat, mat],
        out_specs=[col, col, pl.BlockSpec((1, groups), lambda i: (0, 0)), mat, mat],
        out_shape=[jax.ShapeDtypeStruct((gp, 1), F32)] * 2 + [jax.ShapeDtypeStruct((1, groups), F32)]
        + [jax.ShapeDtypeStruct((gp, width), F32)] * 2,
        compiler_params=_cp(("arbitrary",)),
    )(lam_re, lam_im, log_step, b_re, b_im, d_lb_re, d_lb_im, d_bb_re, d_bb_im)


def _shift_time(v, s, reverse):
    row = lax.broadcasted_iota(jnp.int32, v.shape, 0)
    if reverse:
        return jnp.where(row < TIME_TILE - s, pltpu.roll(v, TIME_TILE - s, 0), 0.0)
    return jnp.where(row >= s, pltpu.roll(v, s, 0), 0.0)


def _cmul(ar, ai, br, bi):
    return ar * br - ai * bi, ar * bi + ai * br


def _scan_time(xr_ref, xi_ref, ar, ai, reverse):
    t = xr_ref.shape[0]
    n_tiles = t // TIME_TILE
    powers = [(ar, ai)]
    for _ in range(TIME_TILE - 1):
        powers.append(_cmul(*powers[-1], ar, ai))
    order = powers[::-1] if reverse else powers
    carry_r = jnp.concatenate([p[0] for p in order], axis=0)
    carry_i = jnp.concatenate([p[1] for p in order], axis=0)
    levels = [(1, powers[0]), (2, powers[1]), (4, powers[3])]
    last = 0 if reverse else TIME_TILE - 1

    def tile(i, carry):
        cr, ci = carry
        idx = (n_tiles - 1 - i) if reverse else i
        rows = pl.ds(pl.multiple_of(idx * TIME_TILE, TIME_TILE), TIME_TILE)
        br, bi = xr_ref[rows, :], xi_ref[rows, :]
        for s, (pr, pi) in levels:
            sr, si = _cmul(pr, pi, _shift_time(br, s, reverse), _shift_time(bi, s, reverse))
            br, bi = br + sr, bi + si
        kr, ki = _cmul(carry_r, carry_i, cr, ci)
        br, bi = br + kr, bi + ki
        xr_ref[rows, :] = br
        xi_ref[rows, :] = bi
        return br[last:last + 1, :], bi[last:last + 1, :]

    zero = jnp.zeros_like(ar)
    lax.fori_loop(0, n_tiles, tile, (zero, zero))


def _s5_states(u_ref, bbr_ref, bbi_ref, ar_ref, ai_ref, xr, xi, chunk):
    t = u_ref.shape[0]
    for r0 in range(0, t, chunk):
        rows = pl.ds(r0, chunk)
        xr[rows, :] = jnp.dot(u_ref[rows, :], bbr_ref[...], preferred_element_type=F32)
        xi[rows, :] = jnp.dot(u_ref[rows, :], bbi_ref[...], preferred_element_type=F32)
    _scan_time(xr, xi, ar_ref[...], ai_ref[...], reverse=False)


def _s5_specs(t, nb_lanes, state_lanes):
    tok = pl.BlockSpec((t, nb_lanes), lambda j: (0, j))
    bb = pl.BlockSpec((None, nb_lanes, state_lanes), lambda j: (j, 0, 0))
    cc = pl.BlockSpec((None, state_lanes, nb_lanes), lambda j: (j, 0, 0))
    dvec = pl.BlockSpec((1, nb_lanes), lambda j: (0, j))
    avec = pl.BlockSpec((1, state_lanes), lambda j: (0, j))
    return tok, bb, cc, dvec, avec


def _s5_fwd(u5, bbr, bbi, ccr, cci, dskip, ar, ai, name):
    t, w = u5.shape
    nb, nb_lanes, state_lanes = bbr.shape
    chunk = _tile(t, 512, 16)

    def body(u_ref, bbr_ref, bbi_ref, cr_ref, ci_ref, d_ref, ar_ref, ai_ref, y_ref, xr, xi):
        _s5_states(u_ref, bbr_ref, bbi_ref, ar_ref, ai_ref, xr, xi, chunk)
        for r0 in range(0, t, chunk):
            rows = pl.ds(r0, chunk)
            y = jnp.dot(xr[rows, :].astype(BF16), cr_ref[...], preferred_element_type=F32)
            y = y - jnp.dot(xi[rows, :].astype(BF16), ci_ref[...], preferred_element_type=F32)
            y_ref[rows, :] = y + d_ref[...] * u_ref[rows, :].astype(F32)

    tok, bb, cc, dvec, avec = _s5_specs(t, nb_lanes, state_lanes)
    return pl.pallas_call(
        body, name=name, grid=(nb,),
        in_specs=[tok, bb, bb, cc, cc, dvec, avec, avec],
        out_specs=tok,
        out_shape=jax.ShapeDtypeStruct((t, w), F32),
        scratch_shapes=[pltpu.VMEM((t, state_lanes), F32)] * 2,
        compiler_params=_cp(("parallel",)),
    )(u5, bbr, bbi, ccr, cci, dskip, ar, ai)


def _s5_bwd(u5, dy, bbr, bbi, ccr, cci, dskip, ar, ai, name):
    t, w = u5.shape
    nb, nb_lanes, state_lanes = bbr.shape
    chunk = _tile(t, 512, 16)
    nt_dims = (((1,), (1,)), ((), ()))
    tn_dims = (((0,), (0,)), ((), ()))

    def body(u_ref, dy_ref, bbr_ref, bbi_ref, cr_ref, ci_ref, d_ref, ar_ref, ai_ref,
             du_ref, dbbr_ref, dbbi_ref, dcr_ref, dci_ref, dar_ref, dai_ref, dd_ref, xr, xi, gr, gi):
        _s5_states(u_ref, bbr_ref, bbi_ref, ar_ref, ai_ref, xr, xi, chunk)
        for r0 in range(0, t, chunk):
            rows = pl.ds(r0, chunk)
            dyb = dy_ref[rows, :].astype(BF16)
            gr[rows, :] = lax.dot_general(dyb, cr_ref[...], nt_dims, preferred_element_type=F32)
            gi[rows, :] = -lax.dot_general(dyb, ci_ref[...], nt_dims, preferred_element_type=F32)
        _scan_time(gr, gi, ar_ref[...], -ai_ref[...], reverse=True)

        dcr = jnp.zeros((state_lanes, nb_lanes), F32)
        dci = jnp.zeros((state_lanes, nb_lanes), F32)
        dbr = jnp.zeros((nb_lanes, state_lanes), F32)
        dbi = jnp.zeros((nb_lanes, state_lanes), F32)
        dd = jnp.zeros((1, nb_lanes), F32)
        for r0 in range(0, t, chunk):
            rows = pl.ds(r0, chunk)
            u = u_ref[rows, :]
            dyv = dy_ref[rows, :]
            dyb = dyv.astype(BF16)
            lr, li = gr[rows, :].astype(BF16), gi[rows, :].astype(BF16)
            dcr = dcr + lax.dot_general(xr[rows, :].astype(BF16), dyb, tn_dims, preferred_element_type=F32)
            dci = dci - lax.dot_general(xi[rows, :].astype(BF16), dyb, tn_dims, preferred_element_type=F32)
            dbr = dbr + lax.dot_general(u, lr, tn_dims, preferred_element_type=F32)
            dbi = dbi + lax.dot_general(u, li, tn_dims, preferred_element_type=F32)
            du = lax.dot_general(lr, bbr_ref[...], nt_dims, preferred_element_type=F32)
            du = du + lax.dot_general(li, bbi_ref[...], nt_dims, preferred_element_type=F32)
            du_ref[rows, :] = du + d_ref[...] * dyv
            dd = dd + jnp.sum(dyv * u.astype(F32), axis=0, keepdims=True)
        dcr_ref[...] = dcr
        dci_ref[...] = dci
        dbbr_ref[...] = dbr
        dbbi_ref[...] = dbi
        dd_ref[...] = dd

        first_row = lax.broadcasted_iota(jnp.int32, (TIME_TILE, state_lanes), 0) == 0

        def tile(i, carry):
            pr, pi, acc_r, acc_i = carry
            rows = pl.ds(pl.multiple_of(i * TIME_TILE, TIME_TILE), TIME_TILE)
            x_r, x_i, l_r, l_i = xr[rows, :], xi[rows, :], gr[rows, :], gi[rows, :]
            prev_r = jnp.where(first_row, pr, pltpu.roll(x_r, 1, 0))
            prev_i = jnp.where(first_row, pi, pltpu.roll(x_i, 1, 0))
            acc_r = acc_r + l_r * prev_r + l_i * prev_i
            acc_i = acc_i + l_i * prev_r - l_r * prev_i
            return x_r[TIME_TILE - 1:, :], x_i[TIME_TILE - 1:, :], acc_r, acc_i

        zrow = jnp.zeros((1, state_lanes), F32)
        ztile = jnp.zeros((TIME_TILE, state_lanes), F32)
        _, _, acc_r, acc_i = lax.fori_loop(0, t // TIME_TILE, tile, (zrow, zrow, ztile, ztile))
        dar_ref[...] = jnp.sum(acc_r, axis=0, keepdims=True)
        dai_ref[...] = jnp.sum(acc_i, axis=0, keepdims=True)

    tok, bb, cc, dvec, avec = _s5_specs(t, nb_lanes, state_lanes)
    return pl.pallas_call(
        body, name=name, grid=(nb,),
        in_specs=[tok, tok, bb, bb, cc, cc, dvec, avec, avec],
        out_specs=[tok, bb, bb, cc, cc, avec, avec, dvec],
        out_shape=[jax.ShapeDtypeStruct((t, w), F32)]
        + [jax.ShapeDtypeStruct((nb, nb_lanes, state_lanes), F32)] * 2
        + [jax.ShapeDtypeStruct((nb, state_lanes, nb_lanes), F32)] * 2
        + [jax.ShapeDtypeStruct((1, nb * state_lanes), F32)] * 2
        + [jax.ShapeDtypeStruct((1, w), F32)],
        scratch_shapes=[pltpu.VMEM((t, state_lanes), F32)] * 4,
        compiler_params=_cp(("parallel",)),
    )(u5, dy, bbr, bbi, ccr, cci, dskip, ar, ai)


def _gelu(x):
    return 0.5 * x * (1.0 + jnp.tanh(GELU_C * (x + GELU_A * x * x * x)))


def _gelu_grad(x):
    th = jnp.tanh(GELU_C * (x + GELU_A * x * x * x))
    return 0.5 * (1.0 + th) + 0.5 * x * (1.0 - th * th) * GELU_C * (1.0 + 3.0 * GELU_A * x * x)


def _glu_fwd(y5, w, b, name):
    t, width = y5.shape
    tm = _tile(t, 512, 16)

    def body(y_ref, w_ref, b_ref, o_ref):
        g = _gelu(y_ref[...])
        a = jnp.dot(g.astype(BF16), w_ref[...], preferred_element_type=F32) + b_ref[...]
        o_ref[...] = (g * _sigmoid(a)).astype(o_ref.dtype)

    row = pl.BlockSpec((tm, width), lambda i: (i, 0))
    return pl.pallas_call(
        body, name=name, grid=(t // tm,),
        in_specs=[row, pl.BlockSpec((width, width), lambda i: (0, 0)), pl.BlockSpec((1, width), lambda i: (0, 0))],
        out_specs=row,
        out_shape=jax.ShapeDtypeStruct((t, width), BF16),
        compiler_params=_cp(("parallel",)),
    )(y5, w, b)


def _glu_bwd(y5, dout, w, b, name):
    t, width = y5.shape
    tm = _tile(t, 512, 16)

    def body(y_ref, do_ref, w_ref, b_ref, dy_ref, g_ref, da_ref, db_ref):
        y = y_ref[...]
        g = _gelu(y)
        s = _sigmoid(jnp.dot(g.astype(BF16), w_ref[...], preferred_element_type=F32) + b_ref[...])
        dout_v = do_ref[...].astype(F32)
        da = dout_v * g * s * (1.0 - s)
        dg = dout_v * s + lax.dot_general(da.astype(BF16), w_ref[...], (((1,), (1,)), ((), ())),
                                          preferred_element_type=F32)
        dy_ref[...] = dg * _gelu_grad(y)
        g_ref[...] = g.astype(g_ref.dtype)
        da_ref[...] = da.astype(da_ref.dtype)

        @pl.when(pl.program_id(0) == 0)
        def _():
            db_ref[...] = jnp.zeros_like(db_ref)

        db_ref[...] += jnp.sum(da, axis=0, keepdims=True)

    row = pl.BlockSpec((tm, width), lambda i: (i, 0))
    vec = pl.BlockSpec((1, width), lambda i: (0, 0))
    return pl.pallas_call(
        body, name=name, grid=(t // tm,),
        in_specs=[row, row, pl.BlockSpec((width, width), lambda i: (0, 0)), vec],
        out_specs=[row, row, row, vec],
        out_shape=[jax.ShapeDtypeStruct((t, width), F32), jax.ShapeDtypeStruct((t, width), BF16),
                   jax.ShapeDtypeStruct((t, width), BF16), jax.ShapeDtypeStruct((1, width), F32)],
        compiler_params=_cp(("arbitrary",)),
    )(y5, dout, w, b)


def _adamw(w, g, m, v):
    m = ADAM_B1 * m + (1.0 - ADAM_B1) * g
    v = ADAM_B2 * v + (1.0 - ADAM_B2) * (g * g)
    m_hat = m / (1.0 - ADAM_B1 ** ADAM_STEP)
    v_hat = v / (1.0 - ADAM_B2 ** ADAM_STEP)
    return -ADAM_LR * (m_hat / (jnp.sqrt(v_hat) + ADAM_EPS) + ADAM_WD * w), m, v


def _adamw_shard(w, m, v, sums, got, chip, name):
    rows, cols = w.shape
    tr = _row_tile(rows, cols, target=2**20)

    def body(chip_ref, w_ref, m_ref, v_ref, s_ref, g0_ref, g1_ref, g2_ref, g_out, d_out, m_out, v_out):
        g = s_ref[...].astype(F32) + g0_ref[...].astype(F32) + g1_ref[...].astype(F32) + g2_ref[...].astype(F32)
        delta, m_new, v_new = _adamw(w_ref[...], g, m_ref[...], v_ref[...])
        g_out[...] = g
        d_out[...] = delta
        m_out[...] = m_new
        v_out[...] = v_new

    blk = pl.BlockSpec((tr, cols), lambda i, chip_ref: (i, 0))

    def part(k):
        return pl.BlockSpec((None, tr, cols), lambda i, chip_ref: (k, i, 0))

    return pl.pallas_call(
        body, name=name,
        grid_spec=pltpu.PrefetchScalarGridSpec(
            num_scalar_prefetch=1, grid=(rows // tr,),
            in_specs=[blk, blk, blk, pl.BlockSpec((None, tr, cols), lambda i, chip_ref: (chip_ref[0], i, 0)),
                      part(0), part(1), part(2)],
            out_specs=[blk] * 4),
        out_shape=[jax.ShapeDtypeStruct((rows, cols), F32)] * 4,
        compiler_params=_cp(("parallel",)),
    )(chip, w, m, v, sums, got, got, got)


def _adamw_packed(w, m, v, g, name):
    def body(w_ref, m_ref, v_ref, g_ref, d_out, m_out, v_out):
        delta, m_new, v_new = _adamw(w_ref[...], g_ref[...], m_ref[...], v_ref[...])
        d_out[...] = delta
        m_out[...] = m_new
        v_out[...] = v_new

    vm = pl.BlockSpec(memory_space=pltpu.VMEM)
    return pl.pallas_call(
        body, name=name, in_specs=[vm] * 4, out_specs=[vm] * 3,
        out_shape=[jax.ShapeDtypeStruct(w.shape, F32)] * 3,
        compiler_params=_cp(),
    )(w, m, v, g)


WEIGHTS = ("g_mix", "w_in", "b_fgate", "b_gates", "q_norm", "k_norm", "s5_lambda_re", "s5_lambda_im", "s5_log_step",
           "s5_b_re", "s5_b_im", "s5_c_re", "s5_c_im", "s5_d", "w_glu", "b_glu", "w_proj_fox", "w_proj_s5", "w_out",
           "g_ffn", "w_gate_up", "w_down")
COLUMN_SHARDED = ("w_in", "w_proj_fox", "w_proj_s5", "w_gate_up")
ROW_SHARDED = ("w_glu", "w_out", "w_down")
PACK_ROWS = 8 * LANES
FF_ALIGN = 1024


def _pack(arrays):
    flat = jnp.concatenate([a.reshape(-1).astype(F32) for a in arrays])
    flat = jnp.pad(flat, (0, (-flat.shape[0]) % PACK_ROWS))
    return flat.reshape(-1, LANES)


def _unpack(packed, like):
    flat, out, at = packed.reshape(-1), [], 0
    for a in like:
        out.append(flat[at:at + a.size].reshape(a.shape))
        at += a.size
    return out


def _pad_lanes(a):
    return jnp.pad(a, ((0, 0), (0, LANES - a.shape[1])))


def kernel(x, g_mix, w_in, b_fgate, b_gates, q_norm, k_norm, s5_lambda_re, s5_lambda_im, s5_log_step, s5_b_re, s5_b_im,
           s5_c_re, s5_c_im, s5_d, w_glu, b_glu, w_proj_fox, w_proj_s5, w_out, g_ffn, w_gate_up, w_down,
           loss_target, m_g_mix, m_w_in, m_b_fgate, m_b_gates, m_q_norm, m_k_norm, m_s5_lambda_re,
           m_s5_lambda_im, m_s5_log_step, m_s5_b_re, m_s5_b_im, m_s5_c_re, m_s5_c_im, m_s5_d, m_w_glu,
           m_b_glu, m_w_proj_fox, m_w_proj_s5, m_w_out, m_g_ffn, m_w_gate_up, m_w_down, v_g_mix, v_w_in,
           v_b_fgate, v_b_gates, v_q_norm, v_k_norm, v_s5_lambda_re, v_s5_lambda_im, v_s5_log_step, v_s5_b_re,
           v_s5_b_im, v_s5_c_re, v_s5_c_im, v_s5_d, v_w_glu, v_b_glu, v_w_proj_fox, v_w_proj_s5, v_w_out,
           v_g_ffn, v_w_gate_up, v_w_down):
    given = dict(locals())
    weights = {n: given[n] for n in WEIGHTS}
    mom_m = {n: given["m_" + n] for n in WEIGHTS}
    mom_v = {n: given["v_" + n] for n in WEIGHTS}

    pos_x, pos_y, pos_c = _position()
    core = jnp.reshape(pos_c, (1,)).astype(jnp.int32)
    chip = jnp.reshape(2 * pos_x + pos_y, (1,)).astype(jnp.int32)

    xs, target = x[0], loss_target[0]
    t, d = xs.shape
    heads, dh = b_fgate.shape[-1], q_norm.shape[-1]
    fw = heads * dh
    groups, states, gwidth = s5_b_re.shape[1:]
    sw = groups * gwidth
    gp = groups * states
    assert dh == LANES and sw % LANES == 0 and LANES % gwidth == 0
    col_v, col_f, col_s5 = 3 * fw, 3 * fw + heads, 3 * fw + heads + sw

    shard = {n: weights[n][0].astype(BF16) for n in COLUMN_SHARDED + ROW_SHARDED}

    def whole(n, ag):
        if n in COLUMN_SHARDED:
            return ag.transpose(1, 0, 2).reshape(ag.shape[1], N_DEV * ag.shape[2])
        return ag.reshape(N_DEV * ag.shape[1], ag.shape[2])

    full = {"w_in": whole("w_in", _all_gather(shard["w_in"], "ag_w_in"))}
    ff = w_down.shape[1] * N_DEV
    ff_pad = -(-ff // FF_ALIGN) * FF_ALIGN
    w_main = jnp.concatenate([full["w_in"][:, :col_v], full["w_in"][:, col_f:]], axis=1)
    w_forget = _pad_lanes(full["w_in"][:, col_v:col_f])
    z_s5, z_gate = 3 * fw, 3 * fw + sw

    u, r_mix = _rms_fwd(xs, g_mix, "rms_mix")
    early = ("w_proj_fox", "w_proj_s5", "w_glu", "w_out")
    z, got = _mm(u, w_main, "nn", BF16, "mm_z", job=_GatherJob([shard[n] for n in early]))
    for n, ag in zip(early, got):
        full[n] = whole(n, ag)
    zf = _mm(u, w_forget, "nn", F32, "mm_zf")
    qn, kn, r_q, r_k = _qk_prep(z, heads, dh, q_norm, k_norm, "qk_prep")
    b_forget = _pad_lanes(b_fgate)
    cum = _forget_fwd(zf, b_forget, "forget_fwd")
    cum_t = cum[:, :heads].T
    fcol, frow = cum_t[:, :, None], cum_t[:, None, :]
    (attn, lse), got = _attn_fwd(qn, kn, z, 2 * heads, fcol, frow, heads, dh, "attn_fwd",
                                 job=_GatherJob([shard["w_gate_up"]]))
    full["w_gate_up"] = jnp.pad(whole("w_gate_up", got[0]).reshape(d, 2, ff),
                                ((0, 0), (0, 0), (0, ff_pad - ff))).reshape(d, 2 * ff_pad)

    lam_re, lam_im = s5_lambda_re.reshape(gp, 1), s5_lambda_im.reshape(gp, 1)
    log_step = jnp.repeat(s5_log_step.reshape(groups, 1), states, axis=1).reshape(gp, 1)
    b_re, b_im = s5_b_re.reshape(gp, gwidth), s5_b_im.reshape(gp, gwidth)
    lb_re, lb_im, bb_re, bb_im = _s5_prep(lam_re, lam_im, log_step, b_re, b_im, "s5_prep")
    nb, per = sw // LANES, LANES // gwidth
    eye = jnp.eye(per, dtype=F32)

    def diag_b(bb):
        return jnp.einsum("napi,ab->naibp", bb.reshape(nb, per, states, gwidth), eye).reshape(nb, LANES, per * states)

    def diag_c(c):
        return jnp.einsum("naip,ab->nbpai", c.reshape(nb, per, gwidth, states), eye).reshape(nb, per * states, LANES)

    def undiag_b(g):
        return jnp.einsum("naibp,ab->napi", g.reshape(nb, per, gwidth, per, states), eye).reshape(gp, gwidth)

    def undiag_c(g):
        return jnp.einsum("nbpai,ab->naip", g.reshape(nb, per, states, per, gwidth), eye).reshape(1, groups, gwidth, states)

    bbr, bbi = diag_b(bb_re).astype(BF16), diag_b(bb_im).astype(BF16)
    ccr, cci = diag_c(s5_c_re[0]).astype(BF16), diag_c(s5_c_im[0]).astype(BF16)
    a_re, a_im = lb_re.reshape(1, gp), lb_im.reshape(1, gp)
    d_skip = s5_d.reshape(1, sw)
    u5 = z[:, z_s5:z_s5 + sw]
    y5 = _s5_fwd(u5, bbr, bbi, ccr, cci, d_skip, a_re, a_im, "s5_fwd")
    ssm = _glu_fwd(y5, full["w_glu"], b_glu, "glu_fwd")

    pf = _mm(attn, full["w_proj_fox"], "nn", BF16, "mm_pf")
    ps = _mm(ssm, full["w_proj_s5"], "nn", BF16, "mm_ps")
    merged = _gate_merge_fwd(z, z_gate, b_gates, pf, ps, "merge_fwd")
    mo = _mm(merged, full["w_out"], "nn", F32, "mm_out")
    h, hn, r_ffn = _resid_rms(xs, mo, g_ffn, "resid_rms")
    gu, got = _mm(hn, full["w_gate_up"], "nn", BF16, "mm_gu", job=_GatherJob([shard["w_down"]]))
    full["w_down"] = jnp.pad(whole("w_down", got[0]), ((0, ff_pad - ff), (0, 0)))
    act = _swiglu_fwd(gu, "swiglu_fwd")
    dn = _mm(act, full["w_down"], "nn", F32, "mm_down")
    loss_blk, dy, dy_b = _loss_head(h, dn, target, "loss_head")
    loss = lax.psum(loss_blk[0, 0], ("x", "y", "c"))

    grad, sums, from_chips = {}, {}, {}

    def pair_sums(n):
        g_full = grad[n]
        if n in COLUMN_SHARDED:
            parts = g_full.reshape(g_full.shape[0], N_DEV, g_full.shape[1] // N_DEV).transpose(1, 0, 2)
        else:
            parts = g_full.reshape(N_DEV, g_full.shape[0] // N_DEV, g_full.shape[1])
        got = _swap_with_sibling(parts, "rs_sibling_" + n)
        return _add_sibling(parts, got, core, "rs_add_" + n)

    dact = _mm(dy_b, full["w_down"], "nt", BF16, "mm_dact")
    grad["w_down"] = _mm(act, dy_b, "tn", BF16, "mm_gw_down")[:ff]
    sums["w_down"] = pair_sums("w_down")
    dgate, dup = _swiglu_bwd(gu, dact, "swiglu_bwd")
    dgu = jnp.concatenate([dgate, dup], axis=1)
    dhn, (from_chips["w_down"],) = _mm(dgu, full["w_gate_up"], "nt", F32, "mm_dhn", job=_ChipSwapJob([sums["w_down"]]))
    grad["w_gate_up"] = _mm(hn, dgu, "tn", BF16, "mm_gw_gu").reshape(d, 2, ff_pad)[:, :, :ff].reshape(d, 2 * ff)
    sums["w_gate_up"] = pair_sums("w_gate_up")
    dh_, dh_b, grad["g_ffn"] = _rms_bwd([dhn], h, r_ffn, g_ffn, [dy], "rms_ffn_bwd")
    dmerged = _mm(dh_b, full["w_out"], "nt", BF16, "mm_dmerged")
    grad["w_out"] = _mm(merged, dh_b, "tn", BF16, "mm_gw_out")
    dpf, dps, dz_gf, dz_gs, db_gf, db_gs = _gate_merge_bwd(dmerged, z, z_gate, b_gates, pf, ps, "merge_bwd")
    grad["b_gates"] = jnp.concatenate([db_gf, db_gs], axis=1)
    dattn = _mm(dpf, full["w_proj_fox"], "nt", BF16, "mm_dattn")
    grad["w_proj_fox"] = _mm(attn, dpf, "tn", BF16, "mm_gw_pf")
    dssm = _mm(dps, full["w_proj_s5"], "nt", BF16, "mm_dssm")
    grad["w_proj_s5"] = _mm(ssm, dps, "tn", BF16, "mm_gw_ps")

    dy5, g5, da5, grad["b_glu"] = _glu_bwd(y5, dssm, full["w_glu"], b_glu, "glu_bwd")
    grad["w_glu"] = _mm(g5, da5, "tn", BF16, "mm_gw_glu")
    for n in early:
        sums[n] = pair_sums(n)
    du5, d_bbr, d_bbi, d_ccr, d_cci, d_are, d_aim, d_dskip = _s5_bwd(
        u5, dy5, bbr, bbi, ccr, cci, d_skip, a_re, a_im, "s5_bwd")
    d_lre, d_lim, d_lstep, d_bre, d_bim = _s5_prep_bwd(
        lam_re, lam_im, log_step, b_re, b_im, d_are.reshape(gp, 1), d_aim.reshape(gp, 1),
        undiag_b(d_bbr), undiag_b(d_bbi), groups, "s5_prep_bwd")
    grad["s5_lambda_re"], grad["s5_lambda_im"] = d_lre.reshape(1, groups, states), d_lim.reshape(1, groups, states)
    grad["s5_log_step"] = d_lstep
    grad["s5_b_re"], grad["s5_b_im"] = d_bre.reshape(s5_b_re.shape), d_bim.reshape(s5_b_im.shape)
    grad["s5_c_re"], grad["s5_c_im"] = undiag_c(d_ccr), undiag_c(d_cci)
    grad["s5_d"] = d_dskip.reshape(s5_d.shape)

    delta = _attn_delta(attn, dattn, heads, dh, "attn_delta")
    (dqn, df_q), got = _attn_bwd_q(qn, kn, z, 2 * heads, dattn, fcol, frow, lse, delta, heads, dh, "attn_bwd_q",
                                   job=_ChipSwapJob([sums[n] for n in early]))
    from_chips.update(zip(early, got))
    (dkn, dv, df_k), (from_chips["w_gate_up"],) = _attn_bwd_kv(
        qn, kn, z, 2 * heads, dattn, fcol, frow, lse, delta, heads, dh, "attn_bwd_kv",
        job=_ChipSwapJob([sums["w_gate_up"]]))
    dq, dk, grad["q_norm"], grad["k_norm"] = _qk_prep_bwd(dqn, dkn, z, heads, dh, q_norm, k_norm, r_q, r_k, "qk_prep_bwd")
    dzf, db_forget = _forget_bwd(_pad_lanes(df_q[:, :, 0].T), _pad_lanes(df_k[:, 0, :].T), zf, b_forget, "forget_bwd")
    grad["b_fgate"] = db_forget[:, :heads]

    dz = jnp.concatenate([dq, dk, dv, du5.astype(BF16), dz_gf, dz_gs], axis=1)
    gw_main = _mm(u, dz, "tn", BF16, "mm_gw_main")
    gw_forget = _mm(u, dzf, "tn", BF16, "mm_gw_forget")
    grad["w_in"] = jnp.concatenate([gw_main[:, :col_v], gw_forget[:, :heads], gw_main[:, col_v:]], axis=1)
    sums["w_in"] = pair_sums("w_in")
    du, (from_chips["w_in"],) = _mm(dz, w_main, "nt", F32, "mm_du", job=_ChipSwapJob([sums["w_in"]]))
    du_f = _mm(dzf, w_forget, "nt", F32, "mm_du_f")
    dx, _, grad["g_mix"] = _rms_bwd([du, du_f], xs, r_mix, g_mix, [dh_], "rms_mix_bwd")

    out_g, out_d, out_m, out_v = {}, {}, {}, {}
    for n in COLUMN_SHARDED + ROW_SHARDED:
        res = _adamw_shard(weights[n][0], mom_m[n][0], mom_v[n][0], sums[n], from_chips[n], chip, "adamw_" + n)
        out_g[n], out_d[n], out_m[n], out_v[n] = (r[None] for r in res)

    small = [n for n in WEIGHTS if n not in COLUMN_SHARDED + ROW_SHARDED]
    g_small = _all_reduce_small(_pack([grad[n] for n in small]), "ar_small")
    like = [weights[n] for n in small]
    res = _adamw_packed(_pack(like), _pack([mom_m[n] for n in small]), _pack([mom_v[n] for n in small]), g_small,
                        "adamw_small")
    for store, packed in zip((out_g, out_d, out_m, out_v), (g_small, *res)):
        for n, a in zip(small, _unpack(packed, like)):
            store[n] = a

    return (loss, dx[None], *[out_g[n] for n in WEIGHTS], *[out_d[n] for n in WEIGHTS],
            *[out_m[n] for n in WEIGHTS], *[out_v[n] for n in WEIGHTS])
```

```python
import functools
import math

import jax
import jax.numpy as jnp
from jax import lax
from jax.experimental import pallas as pl
from jax.experimental.pallas import tpu as pltpu

F32 = jnp.float32
BF16 = jnp.bfloat16

V7X_VMEM_LIMIT = 56 * 2**20
LANES = 128
N_DEV = 8
MESH = pl.DeviceIdType.MESH

RMS_EPS = 1e-6
MASK_VALUE = -1e30
ADAM_LR, ADAM_B1, ADAM_B2, ADAM_EPS, ADAM_WD, ADAM_STEP = 0.001, 0.9, 0.999, 1e-08, 0.01, 10
GELU_C = math.sqrt(2.0 / math.pi)
GELU_A = 0.044715


def _cp(sem=None):
    return pltpu.CompilerParams(dimension_semantics=sem, vmem_limit_bytes=V7X_VMEM_LIMIT)


def _tile(n, pref, unit=LANES):
    if n <= pref:
        return n
    t = (pref // unit) * unit
    while t >= unit:
        if n % t == 0:
            return t
        t -= unit
    raise ValueError(f"no tile for {n}")


def _row_tile(rows, cols, bytes_per_row_elem=4, target=2 * 2**20, unit=16):
    best = None
    for t in range(unit, rows + 1, unit):
        if rows % t == 0 and t * cols * bytes_per_row_elem <= target:
            best = t
    if best is None:
        best = unit if rows % unit == 0 else rows
    return best


def _sigmoid(x):
    return 1.0 / (1.0 + jnp.exp(-x))


def _position():
    return lax.axis_index("x"), lax.axis_index("y"), lax.axis_index("c")


def _other_chips(x, y):
    return [(1 - x, y), (x, 1 - y), (1 - x, 1 - y)]


def _all_gather(shard, name):
    rows, cols = shard.shape

    def body(x_ref, out_ref, send_sems, recv_sems, local_sem):
        x, y, c = _position()
        me, sibling = (x, y, c), (x, y, 1 - c)
        chips = _other_chips(x, y)

        def slot(px, py, pc):
            return out_ref.at[4 * px + 2 * py + pc]

        def copy(k, block, to, src=None):
            return pltpu.make_async_remote_copy(
                src_ref=slot(*block) if src is None else src, dst_ref=slot(*block),
                send_sem=send_sems.at[k], recv_sem=recv_sems.at[k], device_id=to, device_id_type=MESH)

        mine = pltpu.make_async_copy(x_ref, slot(*me), local_sem)
        mine.start()
        first = [copy(0, me, sibling, src=x_ref)]
        first += [copy(1 + j, me, (*chip, c), src=x_ref) for j, chip in enumerate(chips)]
        for cp in first:
            cp.start()
        passed = [copy(4 + j, (*chip, c), sibling) for j, chip in enumerate(chips)]
        for j, chip in enumerate(chips):
            copy(1 + j, (*chip, c), me).wait_recv()
            passed[j].start()
        copy(0, sibling, me).wait_recv()
        for j, chip in enumerate(chips):
            copy(4 + j, (*chip, 1 - c), me).wait_recv()
        for cp in first + passed:
            cp.wait_send()
        mine.wait()

    return pl.pallas_call(
        body, name=name,
        out_shape=jax.ShapeDtypeStruct((N_DEV, rows, cols), shard.dtype),
        in_specs=[pl.BlockSpec(memory_space=pltpu.HBM)],
        out_specs=pl.BlockSpec(memory_space=pltpu.HBM),
        scratch_shapes=[pltpu.SemaphoreType.DMA((7,)), pltpu.SemaphoreType.DMA((7,)), pltpu.SemaphoreType.DMA],
    )(shard)


def _swap_with_sibling(parts, name):
    _, rows, cols = parts.shape

    def body(p_ref, out_ref, send_sems, recv_sems):
        x, y, c = _position()
        copies = []
        for j in range(4):
            copies.append(pltpu.make_async_remote_copy(
                src_ref=p_ref.at[2 * j + (1 - c)], dst_ref=out_ref.at[j],
                send_sem=send_sems.at[j], recv_sem=recv_sems.at[j], device_id=(x, y, 1 - c), device_id_type=MESH))
        for cp in copies:
            cp.start()
        for cp in copies:
            cp.wait()

    return pl.pallas_call(
        body, name=name,
        out_shape=jax.ShapeDtypeStruct((4, rows, cols), parts.dtype),
        in_specs=[pl.BlockSpec(memory_space=pltpu.HBM)],
        out_specs=pl.BlockSpec(memory_space=pltpu.HBM),
        scratch_shapes=[pltpu.SemaphoreType.DMA((4,)), pltpu.SemaphoreType.DMA((4,))],
    )(parts)


def _add_sibling(parts, got, core, name):
    _, rows, cols = parts.shape
    tr = _row_tile(rows, cols)

    def body(core_ref, a_ref, b_ref, o_ref):
        o_ref[...] = (a_ref[...].astype(F32) + b_ref[...].astype(F32)).astype(o_ref.dtype)

    return pl.pallas_call(
        body, name=name,
        grid_spec=pltpu.PrefetchScalarGridSpec(
            num_scalar_prefetch=1, grid=(4, rows // tr),
            in_specs=[pl.BlockSpec((None, tr, cols), lambda j, i, core_ref: (2 * j + core_ref[0], i, 0)),
                      pl.BlockSpec((None, tr, cols), lambda j, i, core_ref: (j, i, 0))],
            out_specs=pl.BlockSpec((None, tr, cols), lambda j, i, core_ref: (j, i, 0))),
        out_shape=jax.ShapeDtypeStruct((4, rows, cols), BF16),
        compiler_params=_cp(("parallel", "parallel")),
    )(core, parts, got)


def _all_reduce_small(packed, name):
    rows, cols = packed.shape

    def body(x_ref, out_ref, gathered, send_sems, recv_sems):
        x, y, c = _position()
        me, sibling = (x, y, c), (x, y, 1 - c)
        chips = _other_chips(x, y)

        def slot(px, py, pc):
            return gathered.at[4 * px + 2 * py + pc]

        def copy(k, block, to, src=None):
            return pltpu.make_async_remote_copy(
                src_ref=slot(*block) if src is None else src, dst_ref=slot(*block),
                send_sem=send_sems.at[k], recv_sem=recv_sems.at[k], device_id=to, device_id_type=MESH)

        first = [copy(0, me, sibling, src=x_ref)]
        first += [copy(1 + j, me, (*chip, c), src=x_ref) for j, chip in enumerate(chips)]
        for cp in first:
            cp.start()
        passed = [copy(4 + j, (*chip, c), sibling) for j, chip in enumerate(chips)]
        for j, chip in enumerate(chips):
            copy(1 + j, (*chip, c), me).wait_recv()
            passed[j].start()
        copy(0, sibling, me).wait_recv()
        for j, chip in enumerate(chips):
            copy(4 + j, (*chip, 1 - c), me).wait_recv()
        for cp in first + passed:
            cp.wait_send()
        gathered[4 * x + 2 * y + c] = x_ref[...]
        total = gathered[0]
        for k in range(1, N_DEV):
            total = total + gathered[k]
        out_ref[...] = total

    return pl.pallas_call(
        body, name=name,
        out_shape=jax.ShapeDtypeStruct((rows, cols), F32),
        in_specs=[pl.BlockSpec(memory_space=pltpu.VMEM)],
        out_specs=pl.BlockSpec(memory_space=pltpu.VMEM),
        scratch_shapes=[pltpu.VMEM((N_DEV, rows, cols), F32),
                        pltpu.SemaphoreType.DMA((7,)), pltpu.SemaphoreType.DMA((7,))],
        compiler_params=pltpu.CompilerParams(vmem_limit_bytes=V7X_VMEM_LIMIT),
    )(packed)


class _GatherJob:
    def __init__(self, shards):
        self.inputs = list(shards)
        self.out_shape = [jax.ShapeDtypeStruct((N_DEV,) + s.shape, s.dtype) for s in shards]
        n = len(shards)
        self.scratch = [pltpu.SemaphoreType.DMA((7 * n,)), pltpu.SemaphoreType.DMA((7 * n,)),
                        pltpu.SemaphoreType.DMA((n,))]

    def _plan(self, ins, outs, scratch):
        send_sems, recv_sems, local_sems = scratch
        x, y, c = _position()
        me, sibling = (x, y, c), (x, y, 1 - c)
        chips = _other_chips(x, y)

        def slot(i, px, py, pc):
            return outs[i].at[4 * px + 2 * py + pc]

        def copy(i, k, block, to, src=None):
            return pltpu.make_async_remote_copy(
                src_ref=slot(i, *block) if src is None else src, dst_ref=slot(i, *block),
                send_sem=send_sems.at[7 * i + k], recv_sem=recv_sems.at[7 * i + k], device_id=to, device_id_type=MESH)

        def mine(i):
            return pltpu.make_async_copy(ins[i], slot(i, *me), local_sems.at[i])

        return c, me, sibling, chips, copy, mine

    def begin(self, ins, outs, scratch):
        c, me, sibling, chips, copy, mine = self._plan(ins, outs, scratch)
        for i, x_ref in enumerate(ins):
            mine(i).start()
            copy(i, 0, me, sibling, src=x_ref).start()
            for j, chip in enumerate(chips):
                copy(i, 1 + j, me, (*chip, c), src=x_ref).start()

    def middle(self, ins, outs, scratch):
        c, me, sibling, chips, copy, mine = self._plan(ins, outs, scratch)
        for i in range(len(ins)):
            for j, chip in enumerate(chips):
                copy(i, 1 + j, (*chip, c), me).wait_recv()
                copy(i, 4 + j, (*chip, c), sibling).start()

    def end(self, ins, outs, scratch):
        c, me, sibling, chips, copy, mine = self._plan(ins, outs, scratch)
        for i, x_ref in enumerate(ins):
            copy(i, 0, sibling, me).wait_recv()
            for j, chip in enumerate(chips):
                copy(i, 4 + j, (*chip, 1 - c), me).wait_recv()
            copy(i, 0, me, sibling, src=x_ref).wait_send()
            for j, chip in enumerate(chips):
                copy(i, 1 + j, me, (*chip, c), src=x_ref).wait_send()
                copy(i, 4 + j, (*chip, c), sibling).wait_send()
            mine(i).wait()


class _ChipSwapJob:
    def __init__(self, sums):
        self.inputs = list(sums)
        self.out_shape = [jax.ShapeDtypeStruct((3,) + s.shape[1:], s.dtype) for s in sums]
        n = len(sums)
        self.scratch = [pltpu.SemaphoreType.DMA((3 * n,)), pltpu.SemaphoreType.DMA((3 * n,))]

    def _copies(self, ins, outs, scratch):
        send_sems, recv_sems = scratch
        x, y, c = _position()
        return [pltpu.make_async_remote_copy(
            src_ref=ins[i].at[2 * px + py], dst_ref=outs[i].at[k], send_sem=send_sems.at[3 * i + k],
            recv_sem=recv_sems.at[3 * i + k], device_id=(px, py, c), device_id_type=MESH)
            for i in range(len(ins)) for k, (px, py) in enumerate(_other_chips(x, y))]

    def begin(self, ins, outs, scratch):
        for cp in self._copies(ins, outs, scratch):
            cp.start()

    def middle(self, ins, outs, scratch):
        pass

    def end(self, ins, outs, scratch):
        for cp in self._copies(ins, outs, scratch):
            cp.wait()


def _call(body, *, name, grid, in_specs, out_specs, out_shape, scratch_shapes, semantics, operands, job=None):
    if job is None:
        return pl.pallas_call(
            body, name=name, grid=grid, in_specs=in_specs, out_specs=out_specs, out_shape=out_shape,
            scratch_shapes=scratch_shapes, compiler_params=_cp(semantics))(*operands)
    n_in, n_out, n_scr = len(in_specs), len(out_specs), len(scratch_shapes)
    j_in, j_out = len(job.inputs), len(job.out_shape)
    n_steps = math.prod(grid)
    hbm = pl.BlockSpec(memory_space=pltpu.HBM)

    def carrier(*refs):
        ins, refs = refs[:n_in], refs[n_in:]
        job_ins, refs = refs[:j_in], refs[j_in:]
        outs, refs = refs[:n_out], refs[n_out:]
        job_outs, refs = refs[:j_out], refs[j_out:]
        scr, job_scr = refs[:n_scr], refs[n_scr:]
        step = pl.program_id(0)
        for axis in range(1, len(grid)):
            step = step * grid[axis] + pl.program_id(axis)

        @pl.when(step == 0)
        def _():
            job.begin(job_ins, job_outs, job_scr)

        body(*ins, *outs, *scr)

        @pl.when(step == (3 * n_steps) // 4)
        def _():
            job.middle(job_ins, job_outs, job_scr)

        @pl.when(step == n_steps - 1)
        def _():
            job.end(job_ins, job_outs, job_scr)

    res = pl.pallas_call(
        carrier, name=name, grid=grid,
        in_specs=list(in_specs) + [hbm] * j_in, out_specs=list(out_specs) + [hbm] * j_out,
        out_shape=list(out_shape) + job.out_shape, scratch_shapes=list(scratch_shapes) + job.scratch,
        compiler_params=_cp(("arbitrary",) * len(grid)))(*operands, *job.inputs)
    return res[:n_out], res[n_out:]


MM_VMEM_BUDGET = 44 * 2**20
MM_TILE_CAP = 1536
MM_MIN_INTENSITY = 340


def _divisor_tiles(n, cap):
    return [t for t in range(LANES, min(n, cap) + 1, LANES) if n % t == 0] or [n]


def _mm_tiles(m, n_unit, k_unit, whole_k, a_bytes, b_bytes, o_bytes):
    best, best_key = None, None
    for tm in _divisor_tiles(m, 1024):
        for tn in _divisor_tiles(n_unit, MM_TILE_CAP):
            for tk in _divisor_tiles(k_unit, k_unit):
                one_block = whole_k and tk == k_unit
                need = (2 * (tm * tk * a_bytes + tk * tn * b_bytes) + 2 * tm * tn * o_bytes + tm * tn * 4
                        + (0 if one_block else tm * tn * 4))
                intensity = tm * tn / (tm + tn)
                key = (intensity >= MM_MIN_INTENSITY, one_block, intensity, tk)
                if need <= MM_VMEM_BUDGET and (best_key is None or key > best_key):
                    best, best_key = (tm, tn, tk), key
    if best is None:
        raise ValueError(f"no matmul tiles for {(m, n_unit, k_unit)}")
    return best


def _mm(a, b, mode, out_dtype, name, job=None, b_stacked=False, out_stack=None):
    b_rows, b_cols = (b.shape[1], b.shape[0] * b.shape[2]) if b_stacked else b.shape
    b_unit = b.shape[2] if b_stacked else b_cols
    if mode == "nn":
        (m, k), (k2, n) = a.shape, (b_rows, b_cols)
    elif mode == "nt":
        (m, k), (n, k2) = a.shape, (b_rows, b_cols)
    else:
        (k, m), (k2, n) = a.shape, (b_rows, b_cols)
    assert k == k2, (name, a.shape, b.shape)
    n_unit = n // out_stack if out_stack else (b_unit if b_stacked and mode != "nt" else n)
    k_unit = b_unit if b_stacked and mode == "nt" else k
    tm, tn, tk = _mm_tiles(m, n_unit, k_unit, k_unit == k, a.dtype.itemsize, b.dtype.itemsize,
                           jnp.dtype(out_dtype).itemsize)
    nk = k // tk
    per_n, per_k = n_unit // tn, k_unit // tk
    if mode == "tn":
        a_spec = pl.BlockSpec((tk, tm), lambda i, j, l: (l, i))
        dims = (((0,), (0,)), ((), ()))
    else:
        a_spec = pl.BlockSpec((tm, tk), lambda i, j, l: (i, l))
        dims = (((1,), (1,)), ((), ())) if mode == "nt" else (((1,), (0,)), ((), ()))
    if mode == "nt" and b_stacked:
        b_spec = pl.BlockSpec((None, tn, tk), lambda i, j, l: (l // per_k, j, l % per_k))
    elif mode == "nt":
        b_spec = pl.BlockSpec((tn, tk), lambda i, j, l: (j, l))
    elif b_stacked:
        b_spec = pl.BlockSpec((None, tk, tn), lambda i, j, l: (j // per_n, l, j % per_n))
    else:
        b_spec = pl.BlockSpec((tk, tn), lambda i, j, l: (l, j))
    if out_stack:
        o_spec = pl.BlockSpec((None, tm, tn), lambda i, j, l: (j // per_n, i, j % per_n))
        o_shape = jax.ShapeDtypeStruct((out_stack, m, n_unit), out_dtype)
    else:
        o_spec = pl.BlockSpec((tm, tn), lambda i, j, l: (i, j))
        o_shape = jax.ShapeDtypeStruct((m, n), out_dtype)

    def product(a_ref, b_ref):
        return lax.dot_general(a_ref[...].astype(BF16), b_ref[...].astype(BF16), dims, preferred_element_type=F32)

    def body_whole_k(a_ref, b_ref, o_ref):
        o_ref[...] = product(a_ref, b_ref).astype(o_ref.dtype)

    def body_split_k(a_ref, b_ref, o_ref, acc_ref):
        l = pl.program_id(2)

        @pl.when(l == 0)
        def _():
            acc_ref[...] = product(a_ref, b_ref)

        @pl.when(l > 0)
        def _():
            acc_ref[...] += product(a_ref, b_ref)

        @pl.when(l == nk - 1)
        def _():
            o_ref[...] = acc_ref[...].astype(o_ref.dtype)

    res = _call(
        body_whole_k if nk == 1 else body_split_k, name=name, grid=(m // tm, n // tn, nk),
        in_specs=[a_spec, b_spec],
        out_specs=[o_spec],
        out_shape=[o_shape],
        scratch_shapes=[] if nk == 1 else [pltpu.VMEM((tm, tn), F32)],
        semantics=("parallel", "parallel", "arbitrary"), operands=(a, b), job=job)
    return res[0] if job is None else (res[0][0], res[1])


def _rms_fwd(x, g, name):
    t, d = x.shape
    tm = _tile(t, 256, 16)

    def body(x_ref, g_ref, u_ref, r_ref):
        xv = x_ref[...]
        r = lax.rsqrt(jnp.mean(xv * xv, axis=-1, keepdims=True) + RMS_EPS)
        u_ref[...] = (xv * r * g_ref[...]).astype(u_ref.dtype)
        r_ref[...] = r

    return pl.pallas_call(
        body, name=name, grid=(t // tm,),
        in_specs=[pl.BlockSpec((tm, d), lambda i: (i, 0)), pl.BlockSpec((1, d), lambda i: (0, 0))],
        out_specs=[pl.BlockSpec((tm, d), lambda i: (i, 0)), pl.BlockSpec((tm, 1), lambda i: (i, 0))],
        out_shape=[jax.ShapeDtypeStruct((t, d), BF16), jax.ShapeDtypeStruct((t, 1), F32)],
        compiler_params=_cp(("parallel",)),
    )(x, g)


def _rms_bwd(dn_parts, x, r, g, extra, name):
    t, d = x.shape
    tm = _tile(t, 128, 16)
    n_dn, n_extra = len(dn_parts), len(extra)

    def body(*refs):
        dn_refs = refs[:n_dn]
        x_ref, r_ref, g_ref = refs[n_dn:n_dn + 3]
        extra_refs = refs[n_dn + 3:n_dn + 3 + n_extra]
        dx_ref, dxb_ref, dg_ref = refs[n_dn + 3 + n_extra:]
        xhat = x_ref[...] * r_ref[...]
        dnv = dn_refs[0][...].astype(F32)
        for p in dn_refs[1:]:
            dnv = dnv + p[...].astype(F32)
        gd = dnv * g_ref[...]
        dx = r_ref[...] * (gd - xhat * jnp.mean(gd * xhat, axis=-1, keepdims=True))
        for e in extra_refs:
            dx = dx + e[...].astype(F32)
        dx_ref[...] = dx
        dxb_ref[...] = dx.astype(dxb_ref.dtype)

        @pl.when(pl.program_id(0) == 0)
        def _():
            dg_ref[...] = jnp.zeros_like(dg_ref)

        dg_ref[...] += jnp.sum(dnv * xhat, axis=0, keepdims=True)

    row = pl.BlockSpec((tm, d), lambda i: (i, 0))
    return pl.pallas_call(
        body, name=name, grid=(t // tm,),
        in_specs=[row] * n_dn + [row, pl.BlockSpec((tm, 1), lambda i: (i, 0)), pl.BlockSpec((1, d), lambda i: (0, 0))]
        + [row] * n_extra,
        out_specs=[row, row, pl.BlockSpec((1, d), lambda i: (0, 0))],
        out_shape=[jax.ShapeDtypeStruct((t, d), F32), jax.ShapeDtypeStruct((t, d), BF16), jax.ShapeDtypeStruct((1, d), F32)],
        compiler_params=_cp(("arbitrary",)),
    )(*dn_parts, x, r, g, *extra)


def _gate_merge_fwd(z, gate_col, b_gates, pf, ps, name):
    t, d = pf.shape
    tm, tn = _tile(t, 512, 16), _tile(math.gcd(d, gate_col), 512)
    nj, off = d // tn, gate_col // tn
    assert gate_col % tn == 0

    def body(zf_ref, zs_ref, bf_ref, bs_ref, pf_ref, ps_ref, o_ref):
        gf = _sigmoid(zf_ref[...].astype(F32) + bf_ref[...])
        gs = _sigmoid(zs_ref[...].astype(F32) + bs_ref[...])
        o_ref[...] = (gf * pf_ref[...].astype(F32) + gs * ps_ref[...].astype(F32)).astype(o_ref.dtype)

    blk = pl.BlockSpec((tm, tn), lambda i, j: (i, j))
    return pl.pallas_call(
        body, name=name, grid=(t // tm, nj),
        in_specs=[pl.BlockSpec((tm, tn), lambda i, j: (i, off + j)), pl.BlockSpec((tm, tn), lambda i, j: (i, off + nj + j)),
                  pl.BlockSpec((1, tn), lambda i, j: (0, j)), pl.BlockSpec((1, tn), lambda i, j: (0, nj + j)), blk, blk],
        out_specs=blk,
        out_shape=jax.ShapeDtypeStruct((t, d), BF16),
        compiler_params=_cp(("parallel", "parallel")),
    )(z, z, b_gates, b_gates, pf, ps)


def _gate_merge_bwd(dm, z, gate_col, b_gates, pf, ps, name):
    t, d = pf.shape
    tm, tn = _tile(t, 512, 16), _tile(math.gcd(d, gate_col), 512)
    nj, off = d // tn, gate_col // tn

    def body(dm_ref, zf_ref, zs_ref, bf_ref, bs_ref, pf_ref, ps_ref, dpf_ref, dps_ref, dzf_ref, dzs_ref, dbf_ref, dbs_ref):
        gf = _sigmoid(zf_ref[...].astype(F32) + bf_ref[...])
        gs = _sigmoid(zs_ref[...].astype(F32) + bs_ref[...])
        dmv = dm_ref[...].astype(F32)
        dpf_ref[...] = (dmv * gf).astype(dpf_ref.dtype)
        dps_ref[...] = (dmv * gs).astype(dps_ref.dtype)
        dzf = dmv * pf_ref[...].astype(F32) * gf * (1.0 - gf)
        dzs = dmv * ps_ref[...].astype(F32) * gs * (1.0 - gs)
        dzf_ref[...] = dzf.astype(dzf_ref.dtype)
        dzs_ref[...] = dzs.astype(dzs_ref.dtype)

        @pl.when(pl.program_id(1) == 0)
        def _():
            dbf_ref[...] = jnp.zeros_like(dbf_ref)
            dbs_ref[...] = jnp.zeros_like(dbs_ref)

        dbf_ref[...] += jnp.sum(dzf, axis=0, keepdims=True)
        dbs_ref[...] += jnp.sum(dzs, axis=0, keepdims=True)

    blk = pl.BlockSpec((tm, tn), lambda j, i: (i, j))
    lo = pl.BlockSpec((1, tn), lambda j, i: (0, j))
    hi = pl.BlockSpec((1, tn), lambda j, i: (0, nj + j))
    return pl.pallas_call(
        body, name=name, grid=(nj, t // tm),
        in_specs=[blk, pl.BlockSpec((tm, tn), lambda j, i: (i, off + j)), pl.BlockSpec((tm, tn), lambda j, i: (i, off + nj + j)),
                  lo, hi, blk, blk],
        out_specs=[blk, blk, blk, blk, lo, lo],
        out_shape=[jax.ShapeDtypeStruct((t, d), BF16)] * 4 + [jax.ShapeDtypeStruct((1, d), F32)] * 2,
        compiler_params=_cp(("parallel", "arbitrary")),
    )(dm, z, z, b_gates, b_gates, pf, ps)


def _resid_rms(x, mo, g, name):
    t, d = x.shape
    tm = _tile(t, 256, 16)

    def body(x_ref, mo_ref, g_ref, h_ref, hn_ref, r_ref):
        h = x_ref[...] + mo_ref[...].astype(F32)
        r = lax.rsqrt(jnp.mean(h * h, axis=-1, keepdims=True) + RMS_EPS)
        h_ref[...] = h
        hn_ref[...] = (h * r * g_ref[...]).astype(hn_ref.dtype)
        r_ref[...] = r

    row = pl.BlockSpec((tm, d), lambda i: (i, 0))
    col = pl.BlockSpec((tm, 1), lambda i: (i, 0))
    return pl.pallas_call(
        body, name=name, grid=(t // tm,),
        in_specs=[row, row, pl.BlockSpec((1, d), lambda i: (0, 0))],
        out_specs=[row, row, col],
        out_shape=[jax.ShapeDtypeStruct((t, d), F32), jax.ShapeDtypeStruct((t, d), BF16), jax.ShapeDtypeStruct((t, 1), F32)],
        compiler_params=_cp(("parallel",)),
    )(x, mo, g)


def _swiglu_fwd(gu, name):
    t, f2 = gu.shape
    f = f2 // 2
    tm, tn = _tile(t, 512, 16), _tile(f, 1024)
    nj = f // tn

    def body(g_ref, u_ref, o_ref):
        gate = g_ref[...].astype(F32)
        o_ref[...] = (gate * _sigmoid(gate) * u_ref[...].astype(F32)).astype(o_ref.dtype)

    return pl.pallas_call(
        body, name=name, grid=(t // tm, nj),
        in_specs=[pl.BlockSpec((tm, tn), lambda i, j: (i, j)), pl.BlockSpec((tm, tn), lambda i, j: (i, nj + j))],
        out_specs=pl.BlockSpec((tm, tn), lambda i, j: (i, j)),
        out_shape=jax.ShapeDtypeStruct((t, f), BF16),
        compiler_params=_cp(("parallel", "parallel")),
    )(gu, gu)


def _swiglu_bwd(gu, dact, name):
    t, f2 = gu.shape
    f = f2 // 2
    tm, tn = _tile(t, 512, 16), _tile(f, 1024)
    nj = f // tn

    def body(g_ref, u_ref, da_ref, dg_ref, du_ref):
        gate = g_ref[...].astype(F32)
        s = _sigmoid(gate)
        da = da_ref[...].astype(F32)
        dg_ref[...] = (da * u_ref[...].astype(F32) * s * (1.0 + gate * (1.0 - s))).astype(dg_ref.dtype)
        du_ref[...] = (da * gate * s).astype(du_ref.dtype)

    lo = pl.BlockSpec((tm, tn), lambda i, j: (i, j))
    dgate, dup = pl.pallas_call(
        body, name=name, grid=(t // tm, nj),
        in_specs=[lo, pl.BlockSpec((tm, tn), lambda i, j: (i, nj + j)), lo],
        out_specs=[lo, lo],
        out_shape=[jax.ShapeDtypeStruct((t, f), BF16)] * 2,
        compiler_params=_cp(("parallel", "parallel")),
    )(gu, gu, dact)
    return dgate, dup


def _loss_head(h, dn, target, name):
    t, d = h.shape
    tm = _tile(t, 256, 16)

    def body(h_ref, dn_ref, t_ref, loss_ref, dy_ref, dyb_ref):
        err = h_ref[...] + dn_ref[...].astype(F32) - t_ref[...]
        dy_ref[...] = err * (1.0 / d)
        dyb_ref[...] = (err * (1.0 / d)).astype(dyb_ref.dtype)

        @pl.when(pl.program_id(0) == 0)
        def _():
            loss_ref[...] = jnp.zeros_like(loss_ref)

        loss_ref[...] += 0.5 * jnp.sum(jnp.mean(err * err, axis=-1, keepdims=True))

    row = pl.BlockSpec((tm, d), lambda i: (i, 0))
    return pl.pallas_call(
        body, name=name, grid=(t // tm,),
        in_specs=[row, row, row],
        out_specs=[pl.BlockSpec((8, LANES), lambda i: (0, 0)), row, row],
        out_shape=[jax.ShapeDtypeStruct((8, LANES), F32), jax.ShapeDtypeStruct((t, d), F32), jax.ShapeDtypeStruct((t, d), BF16)],
        compiler_params=_cp(("arbitrary",)),
    )(h, dn, target)


def _qk_prep(z, heads, dh, q_norm, k_norm, name):
    t = z.shape[0]
    tq = _tile(t, 512, 16)
    scale = 1.0 / math.sqrt(dh)

    def body(q_ref, k_ref, gq_ref, gk_ref, qn_ref, kn_ref, rq_ref, rk_ref):
        q = q_ref[...].astype(F32)
        k = k_ref[...].astype(F32)
        rq = lax.rsqrt(jnp.mean(q * q, axis=-1, keepdims=True) + RMS_EPS)
        rk = lax.rsqrt(jnp.mean(k * k, axis=-1, keepdims=True) + RMS_EPS)
        qn_ref[...] = (q * rq * gq_ref[...] * scale).astype(qn_ref.dtype)
        kn_ref[...] = (k * rk * gk_ref[...]).astype(kn_ref.dtype)
        rq_ref[...] = rq
        rk_ref[...] = rk

    blk = pl.BlockSpec((tq, dh), lambda i, h: (i, h))
    vec = pl.BlockSpec((1, dh), lambda i, h: (0, 0))
    col = pl.BlockSpec((None, tq, 1), lambda i, h: (h, i, 0))
    return pl.pallas_call(
        body, name=name, grid=(t // tq, heads),
        in_specs=[blk, pl.BlockSpec((tq, dh), lambda i, h: (i, heads + h)), vec, vec],
        out_specs=[blk, blk, col, col],
        out_shape=[jax.ShapeDtypeStruct((t, heads * dh), BF16)] * 2 + [jax.ShapeDtypeStruct((heads, t, 1), F32)] * 2,
        compiler_params=_cp(("parallel", "parallel")),
    )(z, z, q_norm, k_norm)


def _qk_prep_bwd(dqn, dkn, z, heads, dh, q_norm, k_norm, rq, rk, name):
    t = z.shape[0]
    tq = _tile(t, 512, 16)
    scale = 1.0 / math.sqrt(dh)

    def norm_bwd(dy, xv, r, g):
        xhat = xv * r
        gd = dy * g
        return r * (gd - xhat * jnp.mean(gd * xhat, axis=-1, keepdims=True)), jnp.sum(dy * xhat, axis=0, keepdims=True)

    def body(dqn_ref, dkn_ref, q_ref, k_ref, gq_ref, gk_ref, rq_ref, rk_ref, dq_ref, dk_ref, dgq_ref, dgk_ref):
        dq, dgq = norm_bwd(dqn_ref[...].astype(F32) * scale, q_ref[...].astype(F32), rq_ref[...], gq_ref[...])
        dk, dgk = norm_bwd(dkn_ref[...].astype(F32), k_ref[...].astype(F32), rk_ref[...], gk_ref[...])
        dq_ref[...] = dq.astype(dq_ref.dtype)
        dk_ref[...] = dk.astype(dk_ref.dtype)

        @pl.when((pl.program_id(0) == 0) & (pl.program_id(1) == 0))
        def _():
            dgq_ref[...] = jnp.zeros_like(dgq_ref)
            dgk_ref[...] = jnp.zeros_like(dgk_ref)

        dgq_ref[...] += dgq
        dgk_ref[...] += dgk

    blk = pl.BlockSpec((tq, dh), lambda i, h: (i, h))
    vec = pl.BlockSpec((1, dh), lambda i, h: (0, 0))
    col = pl.BlockSpec((None, tq, 1), lambda i, h: (h, i, 0))
    return pl.pallas_call(
        body, name=name, grid=(t // tq, heads),
        in_specs=[blk, blk, blk, pl.BlockSpec((tq, dh), lambda i, h: (i, heads + h)), vec, vec, col, col],
        out_specs=[blk, blk, vec, vec],
        out_shape=[jax.ShapeDtypeStruct((t, heads * dh), BF16)] * 2 + [jax.ShapeDtypeStruct((1, dh), F32)] * 2,
        compiler_params=_cp(("arbitrary", "arbitrary")),
    )(dqn, dkn, z, z, q_norm, k_norm, rq, rk)


def _tri_ones(n, upper):
    row = lax.broadcasted_iota(jnp.int32, (n, n), 0)
    col = lax.broadcasted_iota(jnp.int32, (n, n), 1)
    return jnp.where((col >= row) if upper else (col <= row), 1.0, 0.0).astype(F32)


def _forget_fwd(f, b, name):
    t, w = f.shape
    blk = _tile(t, 256, 8)
    nb = t // blk

    def body(f_ref, b_ref, out_ref):
        tri = _tri_ones(blk, upper=False)

        def step(i, carry):
            rows = pl.ds(pl.multiple_of(i * blk, blk), blk)
            logf = jax.nn.log_sigmoid(f_ref[rows, :] + b_ref[...])
            acc = jnp.dot(tri, logf, precision=lax.Precision.HIGHEST, preferred_element_type=F32) + carry
            out_ref[rows, :] = acc
            return acc[blk - 1:blk, :]

        lax.fori_loop(0, nb, step, jnp.zeros((1, w), F32))

    return pl.pallas_call(
        body, name=name,
        in_specs=[pl.BlockSpec(memory_space=pltpu.VMEM)] * 2,
        out_specs=pl.BlockSpec(memory_space=pltpu.VMEM),
        out_shape=jax.ShapeDtypeStruct((t, w), F32),
        compiler_params=_cp(),
    )(f, b)


def _forget_bwd(d_key, f, b, name):
    t, w = f.shape
    blk = _tile(t, 256, 8)
    nb = t // blk

    def body(dk_ref, f_ref, b_ref, df_ref, db_ref):
        tri = _tri_ones(blk, upper=True)

        def step(i, carry):
            suffix, db = carry
            rows = pl.ds(pl.multiple_of((nb - 1 - i) * blk, blk), blk)
            dlog = suffix - jnp.dot(tri, dk_ref[rows, :], precision=lax.Precision.HIGHEST, preferred_element_type=F32)
            df = dlog * _sigmoid(-(f_ref[rows, :] + b_ref[...]))
            df_ref[rows, :] = df
            return dlog[0:1, :], db + jnp.sum(df, axis=0, keepdims=True)

        _, db = lax.fori_loop(0, nb, step, (jnp.zeros((1, w), F32), jnp.zeros((1, w), F32)))
        db_ref[...] = db

    return pl.pallas_call(
        body, name=name,
        in_specs=[pl.BlockSpec(memory_space=pltpu.VMEM)] * 3,
        out_specs=[pl.BlockSpec(memory_space=pltpu.VMEM)] * 2,
        out_shape=[jax.ShapeDtypeStruct((t, w), F32), jax.ShapeDtypeStruct((1, w), F32)],
        compiler_params=_cp(),
    )(d_key, f, b)


def _attn_logits(q_ref, k_ref, fc_ref, fr_ref, diagonal):
    s = lax.dot_general(q_ref[...], k_ref[...], (((1,), (1,)), ((), ())), preferred_element_type=F32)
    s = s - (fr_ref[...] - fc_ref[0:1, :])
    if diagonal:
        row = lax.broadcasted_iota(jnp.int32, s.shape, 0)
        col = lax.broadcasted_iota(jnp.int32, s.shape, 1)
        s = jnp.where(col <= row, s, MASK_VALUE)
    return s


def _on_causal_blocks(q_block, k_block, step):
    @pl.when(k_block < q_block)
    def _():
        step(False)

    @pl.when(k_block == q_block)
    def _():
        step(True)


def _attn_fwd(qn, kn, v_src, v_col, fcol, frow, heads, dh, name, job=None):
    t = qn.shape[0]
    tq = tk = _tile(t, 512)
    nq = t // tq

    def body(q_ref, k_ref, v_ref, fc_ref, fr_ref, o_ref, lse_ref, m_sc, l_sc, acc_sc):
        qi, ki = pl.program_id(1), pl.program_id(2)

        @pl.when(ki == 0)
        def _():
            m_sc[...] = jnp.full_like(m_sc, MASK_VALUE)
            l_sc[...] = jnp.zeros_like(l_sc)
            acc_sc[...] = jnp.zeros_like(acc_sc)

        def step(diagonal):
            s = _attn_logits(q_ref, k_ref, fc_ref, fr_ref, diagonal)
            m_new = jnp.maximum(m_sc[...], jnp.max(s, axis=-1, keepdims=True))
            alpha = jnp.exp(m_sc[...] - m_new)
            p = jnp.exp(s - m_new)
            l_sc[...] = alpha * l_sc[...] + jnp.sum(p, axis=-1, keepdims=True)
            acc_sc[...] = alpha * acc_sc[...] + jnp.dot(p.astype(BF16), v_ref[...].astype(BF16), preferred_element_type=F32)
            m_sc[...] = m_new

        _on_causal_blocks(qi, ki, step)

        @pl.when(ki == nq - 1)
        def _():
            o_ref[...] = (acc_sc[...] / l_sc[...]).astype(o_ref.dtype)
            lse_ref[...] = m_sc[...] + jnp.log(l_sc[...])

    qblk = pl.BlockSpec((tq, dh), lambda h, i, j: (i, h))
    qcol = pl.BlockSpec((None, tq, 1), lambda h, i, j: (h, i, 0))
    return _call(
        body, name=name, grid=(heads, nq, nq),
        in_specs=[qblk,
                  pl.BlockSpec((tk, dh), lambda h, i, j: (jnp.minimum(j, i), h)),
                  pl.BlockSpec((tk, dh), lambda h, i, j: (jnp.minimum(j, i), v_col + h)),
                  qcol,
                  pl.BlockSpec((None, 1, tk), lambda h, i, j: (h, 0, jnp.minimum(j, i)))],
        out_specs=[qblk, qcol],
        out_shape=[jax.ShapeDtypeStruct((t, heads * dh), BF16), jax.ShapeDtypeStruct((heads, t, 1), F32)],
        scratch_shapes=[pltpu.VMEM((tq, 1), F32), pltpu.VMEM((tq, 1), F32), pltpu.VMEM((tq, dh), F32)],
        semantics=("parallel", "parallel", "arbitrary"), operands=(qn, kn, v_src, fcol, frow), job=job)


def _attn_delta(o, do, heads, dh, name):
    t = o.shape[0]
    tq = _tile(t, 512, 16)

    def body(o_ref, do_ref, out_ref):
        out_ref[...] = jnp.sum(o_ref[...].astype(F32) * do_ref[...].astype(F32), axis=-1, keepdims=True)

    blk = pl.BlockSpec((tq, dh), lambda i, h: (i, h))
    return pl.pallas_call(
        body, name=name, grid=(t // tq, heads),
        in_specs=[blk, blk],
        out_specs=pl.BlockSpec((None, tq, 1), lambda i, h: (h, i, 0)),
        out_shape=jax.ShapeDtypeStruct((heads, t, 1), F32),
        compiler_params=_cp(("parallel", "parallel")),
    )(o, do)


def _attn_bwd_q(qn, kn, v_src, v_col, do, fcol, frow, lse, delta, heads, dh, name, job=None):
    t = qn.shape[0]
    tq = tk = _tile(t, 512)
    nq = t // tq

    def body(q_ref, k_ref, v_ref, do_ref, fc_ref, fr_ref, lse_ref, dl_ref, dq_ref, dq_sc):
        qi, ki = pl.program_id(1), pl.program_id(2)

        @pl.when(ki == 0)
        def _():
            dq_sc[...] = jnp.zeros_like(dq_sc)

        def step(diagonal):
            s = _attn_logits(q_ref, k_ref, fc_ref, fr_ref, diagonal)
            p = jnp.exp(s - lse_ref[...])
            dp = lax.dot_general(do_ref[...].astype(BF16), v_ref[...].astype(BF16), (((1,), (1,)), ((), ())),
                                 preferred_element_type=F32)
            ds = p * (dp - dl_ref[...])
            dq_sc[...] += jnp.dot(ds.astype(BF16), k_ref[...], preferred_element_type=F32)

        _on_causal_blocks(qi, ki, step)

        @pl.when(ki == nq - 1)
        def _():
            dq_ref[...] = dq_sc[...]

    qblk = pl.BlockSpec((tq, dh), lambda h, i, j: (i, h))
    qcol = pl.BlockSpec((None, tq, 1), lambda h, i, j: (h, i, 0))
    return _call(
        body, name=name, grid=(heads, nq, nq),
        in_specs=[qblk,
                  pl.BlockSpec((tk, dh), lambda h, i, j: (jnp.minimum(j, i), h)),
                  pl.BlockSpec((tk, dh), lambda h, i, j: (jnp.minimum(j, i), v_col + h)),
                  qblk, qcol,
                  pl.BlockSpec((None, 1, tk), lambda h, i, j: (h, 0, jnp.minimum(j, i))),
                  qcol, qcol],
        out_specs=[qblk],
        out_shape=[jax.ShapeDtypeStruct((t, heads * dh), F32)],
        scratch_shapes=[pltpu.VMEM((tq, dh), F32)],
        semantics=("parallel", "parallel", "arbitrary"),
        operands=(qn, kn, v_src, do, fcol, frow, lse, delta), job=job)


def _attn_bwd_kv(qn, kn, v_src, v_col, do, fcol, frow, lse, delta, heads, dh, name, job=None):
    t = qn.shape[0]
    tq = tk = _tile(t, 512)
    nq = t // tq

    def body(q_ref, k_ref, v_ref, do_ref, fc_ref, fr_ref, lse_ref, dl_ref, dk_ref, dv_ref, dfk_ref, dk_sc, dv_sc, dfk_sc):
        ki, qi = pl.program_id(1), pl.program_id(2)

        @pl.when(qi == 0)
        def _():
            dk_sc[...] = jnp.zeros_like(dk_sc)
            dv_sc[...] = jnp.zeros_like(dv_sc)
            dfk_sc[...] = jnp.zeros_like(dfk_sc)

        def step(diagonal):
            s = _attn_logits(q_ref, k_ref, fc_ref, fr_ref, diagonal)
            p = jnp.exp(s - lse_ref[...])
            dob = do_ref[...].astype(BF16)
            dp = lax.dot_general(dob, v_ref[...].astype(BF16), (((1,), (1,)), ((), ())), preferred_element_type=F32)
            ds = p * (dp - dl_ref[...])
            dv_sc[...] += lax.dot_general(p.astype(BF16), dob, (((0,), (0,)), ((), ())), preferred_element_type=F32)
            dk_sc[...] += lax.dot_general(ds.astype(BF16), q_ref[...], (((0,), (0,)), ((), ())), preferred_element_type=F32)
            dfk_sc[...] += jnp.sum(ds, axis=0, keepdims=True)

        _on_causal_blocks(qi, ki, step)

        @pl.when(qi == nq - 1)
        def _():
            dk_ref[...] = dk_sc[...]
            dv_ref[...] = dv_sc[...].astype(dv_ref.dtype)
            dfk_ref[...] = dfk_sc[...]

    qblk = pl.BlockSpec((tq, dh), lambda h, j, i: (jnp.maximum(i, j), h))
    qcol = pl.BlockSpec((None, tq, 1), lambda h, j, i: (h, jnp.maximum(i, j), 0))
    kblk = pl.BlockSpec((tk, dh), lambda h, j, i: (j, h))
    krow = pl.BlockSpec((None, 1, tk), lambda h, j, i: (h, 0, j))
    return _call(
        body, name=name, grid=(heads, nq, nq),
        in_specs=[qblk, kblk, pl.BlockSpec((tk, dh), lambda h, j, i: (j, v_col + h)), qblk, qcol, krow, qcol, qcol],
        out_specs=[kblk, kblk, krow],
        out_shape=[jax.ShapeDtypeStruct((t, heads * dh), F32), jax.ShapeDtypeStruct((t, heads * dh), BF16),
                   jax.ShapeDtypeStruct((heads, 1, t), F32)],
        scratch_shapes=[pltpu.VMEM((tk, dh), F32), pltpu.VMEM((tk, dh), F32), pltpu.VMEM((1, tk), F32)],
        semantics=("parallel", "parallel", "arbitrary"),
        operands=(qn, kn, v_src, do, fcol, frow, lse, delta), job=job)


TIME_TILE = 8


def _s5_discretize(lam_re, lam_im, log_step, b_re, b_im):
    dt = jnp.exp(log_step)
    mag = jnp.exp(lam_re * dt)
    lb_re = mag * jnp.cos(lam_im * dt)
    lb_im = mag * jnp.sin(lam_im * dt)
    denom = lam_re * lam_re + lam_im * lam_im
    num_re = lb_re - 1.0
    fac_re = (num_re * lam_re + lb_im * lam_im) / denom
    fac_im = (lb_im * lam_re - num_re * lam_im) / denom
    return lb_re, lb_im, fac_re * b_re - fac_im * b_im, fac_re * b_im + fac_im * b_re


def _s5_prep(lam_re, lam_im, log_step, b_re, b_im, name):
    gp, width = b_re.shape

    def body(lr, li, ls, br, bi, o_lr, o_li, o_br, o_bi):
        res = _s5_discretize(lr[...], li[...], ls[...], br[...], bi[...])
        for ref, val in zip((o_lr, o_li, o_br, o_bi), res):
            ref[...] = val

    vm = pl.BlockSpec(memory_space=pltpu.VMEM)
    return pl.pallas_call(
        body, name=name, in_specs=[vm] * 5, out_specs=[vm] * 4,
        out_shape=[jax.ShapeDtypeStruct((gp, 1), F32)] * 2 + [jax.ShapeDtypeStruct((gp, width), F32)] * 2,
        compiler_params=_cp(),
    )(lam_re, lam_im, log_step, b_re, b_im)


def _s5_prep_bwd(lam_re, lam_im, log_step, b_re, b_im, d_lb_re, d_lb_im, d_bb_re, d_bb_im, groups, name):
    gp, width = b_re.shape
    states = gp // groups
    tr = _tile(gp, 512, 8)

    def body(lr, li, ls, br, bi, g_lr, g_li, g_br, g_bi, o_lr, o_li, o_ls, o_br, o_bi):
        _, vjp = jax.vjp(_s5_discretize, lr[...], li[...], ls[...], br[...], bi[...])
        d_lr, d_li, d_ls, d_br, d_bi = vjp((g_lr[...], g_li[...], g_br[...], g_bi[...]))
        o_lr[...] = d_lr
        o_li[...] = d_li
        o_br[...] = d_br
        o_bi[...] = d_bi
        row_group = (pl.program_id(0) * tr + lax.broadcasted_iota(jnp.int32, (tr, groups), 0)) // states
        col_group = lax.broadcasted_iota(jnp.int32, (tr, groups), 1)

        @pl.when(pl.program_id(0) == 0)
        def _():
            o_ls[...] = jnp.zeros_like(o_ls)

        o_ls[...] += jnp.sum(jnp.where(row_group == col_group, d_ls, 0.0), axis=0, keepdims=True)

    col = pl.BlockSpec((tr, 1), lambda i: (i, 0))
    mat = pl.BlockSpec((tr, width), lambda i: (i, 0))
    return pl.pallas_call(
        body, name=name, grid=(gp // tr,),
        in_specs=[col, col, col, mat, mat, col, col, mat, mat],
        out_specs=[col, col, pl.BlockSpec((1, groups), lambda i: (0, 0)), mat, mat],
        out_shape=[jax.ShapeDtypeStruct((gp, 1), F32)] * 2 + [jax.ShapeDtypeStruct((1, groups), F32)]
        + [jax.ShapeDtypeStruct((gp, width), F32)] * 2,
        compiler_params=_cp(("arbitrary",)),
    )(lam_re, lam_im, log_step, b_re, b_im, d_lb_re, d_lb_im, d_bb_re, d_bb_im)


def _shift_time(v, s, reverse):
    row = lax.broadcasted_iota(jnp.int32, v.shape, 0)
    if reverse:
        return jnp.where(row < TIME_TILE - s, pltpu.roll(v, TIME_TILE - s, 0), 0.0)
    return jnp.where(row >= s, pltpu.roll(v, s, 0), 0.0)


def _cmul(ar, ai, br, bi):
    return ar * br - ai * bi, ar * bi + ai * br


def _scan_time(xr_ref, xi_ref, ar, ai, reverse):
    t = xr_ref.shape[0]
    n_tiles = t // TIME_TILE
    powers = [(ar, ai)]
    for _ in range(TIME_TILE - 1):
        powers.append(_cmul(*powers[-1], ar, ai))
    order = powers[::-1] if reverse else powers
    carry_r = jnp.concatenate([p[0] for p in order], axis=0)
    carry_i = jnp.concatenate([p[1] for p in order], axis=0)
    levels = [(1, powers[0]), (2, powers[1]), (4, powers[3])]
    last = 0 if reverse else TIME_TILE - 1

    def tile(i, carry):
        cr, ci = carry
        idx = (n_tiles - 1 - i) if reverse else i
        rows = pl.ds(pl.multiple_of(idx * TIME_TILE, TIME_TILE), TIME_TILE)
        br, bi = xr_ref[rows, :], xi_ref[rows, :]
        for s, (pr, pi) in levels:
            sr, si = _cmul(pr, pi, _shift_time(br, s, reverse), _shift_time(bi, s, reverse))
            br, bi = br + sr, bi + si
        kr, ki = _cmul(carry_r, carry_i, cr, ci)
        br, bi = br + kr, bi + ki
        xr_ref[rows, :] = br
        xi_ref[rows, :] = bi
        return br[last:last + 1, :], bi[last:last + 1, :]

    zero = jnp.zeros_like(ar)
    lax.fori_loop(0, n_tiles, tile, (zero, zero))


def _s5_states(u_ref, bbr_ref, bbi_ref, ar_ref, ai_ref, xr, xi, chunk):
    t = u_ref.shape[0]
    for r0 in range(0, t, chunk):
        rows = pl.ds(r0, chunk)
        xr[rows, :] = jnp.dot(u_ref[rows, :], bbr_ref[...], preferred_element_type=F32)
        xi[rows, :] = jnp.dot(u_ref[rows, :], bbi_ref[...], preferred_element_type=F32)
    _scan_time(xr, xi, ar_ref[...], ai_ref[...], reverse=False)


def _s5_specs(t, nb_lanes, state_lanes):
    tok = pl.BlockSpec((t, nb_lanes), lambda j: (0, j))
    bb = pl.BlockSpec((None, nb_lanes, state_lanes), lambda j: (j, 0, 0))
    cc = pl.BlockSpec((None, state_lanes, nb_lanes), lambda j: (j, 0, 0))
    dvec = pl.BlockSpec((1, nb_lanes), lambda j: (0, j))
    avec = pl.BlockSpec((1, state_lanes), lambda j: (0, j))
    return tok, bb, cc, dvec, avec


def _s5_fwd(u5, bbr, bbi, ccr, cci, dskip, ar, ai, name):
    t, w = u5.shape
    nb, nb_lanes, state_lanes = bbr.shape
    chunk = _tile(t, 512, 16)

    def body(u_ref, bbr_ref, bbi_ref, cr_ref, ci_ref, d_ref, ar_ref, ai_ref, y_ref, xr, xi):
        _s5_states(u_ref, bbr_ref, bbi_ref, ar_ref, ai_ref, xr, xi, chunk)
        for r0 in range(0, t, chunk):
            rows = pl.ds(r0, chunk)
            y = jnp.dot(xr[rows, :].astype(BF16), cr_ref[...], preferred_element_type=F32)
            y = y - jnp.dot(xi[rows, :].astype(BF16), ci_ref[...], preferred_element_type=F32)
            y_ref[rows, :] = y + d_ref[...] * u_ref[rows, :].astype(F32)

    tok, bb, cc, dvec, avec = _s5_specs(t, nb_lanes, state_lanes)
    return pl.pallas_call(
        body, name=name, grid=(nb,),
        in_specs=[tok, bb, bb, cc, cc, dvec, avec, avec],
        out_specs=tok,
        out_shape=jax.ShapeDtypeStruct((t, w), F32),
        scratch_shapes=[pltpu.VMEM((t, state_lanes), F32)] * 2,
        compiler_params=_cp(("parallel",)),
    )(u5, bbr, bbi, ccr, cci, dskip, ar, ai)


def _s5_bwd(u5, dy, bbr, bbi, ccr, cci, dskip, ar, ai, name):
    t, w = u5.shape
    nb, nb_lanes, state_lanes = bbr.shape
    chunk = _tile(t, 512, 16)
    nt_dims = (((1,), (1,)), ((), ()))
    tn_dims = (((0,), (0,)), ((), ()))

    def body(u_ref, dy_ref, bbr_ref, bbi_ref, cr_ref, ci_ref, d_ref, ar_ref, ai_ref,
             du_ref, dbbr_ref, dbbi_ref, dcr_ref, dci_ref, dar_ref, dai_ref, dd_ref, xr, xi, gr, gi):
        _s5_states(u_ref, bbr_ref, bbi_ref, ar_ref, ai_ref, xr, xi, chunk)
        for r0 in range(0, t, chunk):
            rows = pl.ds(r0, chunk)
            dyb = dy_ref[rows, :].astype(BF16)
            gr[rows, :] = lax.dot_general(dyb, cr_ref[...], nt_dims, preferred_element_type=F32)
            gi[rows, :] = -lax.dot_general(dyb, ci_ref[...], nt_dims, preferred_element_type=F32)
        _scan_time(gr, gi, ar_ref[...], -ai_ref[...], reverse=True)

        dcr = jnp.zeros((state_lanes, nb_lanes), F32)
        dci = jnp.zeros((state_lanes, nb_lanes), F32)
        dbr = jnp.zeros((nb_lanes, state_lanes), F32)
        dbi = jnp.zeros((nb_lanes, state_lanes), F32)
        dd = jnp.zeros((1, nb_lanes), F32)
        for r0 in range(0, t, chunk):
            rows = pl.ds(r0, chunk)
            u = u_ref[rows, :]
            dyv = dy_ref[rows, :]
            dyb = dyv.astype(BF16)
            lr, li = gr[rows, :].astype(BF16), gi[rows, :].astype(BF16)
            dcr = dcr + lax.dot_general(xr[rows, :].astype(BF16), dyb, tn_dims, preferred_element_type=F32)
            dci = dci - lax.dot_general(xi[rows, :].astype(BF16), dyb, tn_dims, preferred_element_type=F32)
            dbr = dbr + lax.dot_general(u, lr, tn_dims, preferred_element_type=F32)
            dbi = dbi + lax.dot_general(u, li, tn_dims, preferred_element_type=F32)
            du = lax.dot_general(lr, bbr_ref[...], nt_dims, preferred_element_type=F32)
            du = du + lax.dot_general(li, bbi_ref[...], nt_dims, preferred_element_type=F32)
            du_ref[rows, :] = du + d_ref[...] * dyv
            dd = dd + jnp.sum(dyv * u.astype(F32), axis=0, keepdims=True)
        dcr_ref[...] = dcr
        dci_ref[...] = dci
        dbbr_ref[...] = dbr
        dbbi_ref[...] = dbi
        dd_ref[...] = dd

        first_row = lax.broadcasted_iota(jnp.int32, (TIME_TILE, state_lanes), 0) == 0

        def tile(i, carry):
            pr, pi, acc_r, acc_i = carry
            rows = pl.ds(pl.multiple_of(i * TIME_TILE, TIME_TILE), TIME_TILE)
            x_r, x_i, l_r, l_i = xr[rows, :], xi[rows, :], gr[rows, :], gi[rows, :]
            prev_r = jnp.where(first_row, pr, pltpu.roll(x_r, 1, 0))
            prev_i = jnp.where(first_row, pi, pltpu.roll(x_i, 1, 0))
            acc_r = acc_r + l_r * prev_r + l_i * prev_i
            acc_i = acc_i + l_i * prev_r - l_r * prev_i
            return x_r[TIME_TILE - 1:, :], x_i[TIME_TILE - 1:, :], acc_r, acc_i

        zrow = jnp.zeros((1, state_lanes), F32)
        ztile = jnp.zeros((TIME_TILE, state_lanes), F32)
        _, _, acc_r, acc_i = lax.fori_loop(0, t // TIME_TILE, tile, (zrow, zrow, ztile, ztile))
        dar_ref[...] = jnp.sum(acc_r, axis=0, keepdims=True)
        dai_ref[...] = jnp.sum(acc_i, axis=0, keepdims=True)

    tok, bb, cc, dvec, avec = _s5_specs(t, nb_lanes, state_lanes)
    return pl.pallas_call(
        body, name=name, grid=(nb,),
        in_specs=[tok, tok, bb, bb, cc, cc, dvec, avec, avec],
        out_specs=[tok, bb, bb, cc, cc, avec, avec, dvec],
        out_shape=[jax.ShapeDtypeStruct((t, w), F32)]
        + [jax.ShapeDtypeStruct((nb, nb_lanes, state_lanes), F32)] * 2
        + [jax.ShapeDtypeStruct((nb, state_lanes, nb_lanes), F32)] * 2
        + [jax.ShapeDtypeStruct((1, nb * state_lanes), F32)] * 2
        + [jax.ShapeDtypeStruct((1, w), F32)],
        scratch_shapes=[pltpu.VMEM((t, state_lanes), F32)] * 4,
        compiler_params=_cp(("parallel",)),
    )(u5, dy, bbr, bbi, ccr, cci, dskip, ar, ai)


def _gelu(x):
    return 0.5 * x * (1.0 + jnp.tanh(GELU_C * (x + GELU_A * x * x * x)))


def _gelu_grad(x):
    th = jnp.tanh(GELU_C * (x + GELU_A * x * x * x))
    return 0.5 * (1.0 + th) + 0.5 * x * (1.0 - th * th) * GELU_C * (1.0 + 3.0 * GELU_A * x * x)


def _glu_fwd(y5, w, b, name):
    t, width = y5.shape
    tm = _tile(t, 512, 16)

    def body(y_ref, w_ref, b_ref, o_ref):
        g = _gelu(y_ref[...])
        a = jnp.dot(g.astype(BF16), w_ref[...], preferred_element_type=F32) + b_ref[...]
        o_ref[...] = (g * _sigmoid(a)).astype(o_ref.dtype)

    row = pl.BlockSpec((tm, width), lambda i: (i, 0))
    return pl.pallas_call(
        body, name=name, grid=(t // tm,),
        in_specs=[row, pl.BlockSpec((width, width), lambda i: (0, 0)), pl.BlockSpec((1, width), lambda i: (0, 0))],
        out_specs=row,
        out_shape=jax.ShapeDtypeStruct((t, width), BF16),
        compiler_params=_cp(("parallel",)),
    )(y5, w, b)


def _glu_bwd(y5, dout, w, b, name):
    t, width = y5.shape
    tm = _tile(t, 512, 16)

    def body(y_ref, do_ref, w_ref, b_ref, dy_ref, g_ref, da_ref, db_ref):
        y = y_ref[...]
        g = _gelu(y)
        s = _sigmoid(jnp.dot(g.astype(BF16), w_ref[...], preferred_element_type=F32) + b_ref[...])
        dout_v = do_ref[...].astype(F32)
        da = dout_v * g * s * (1.0 - s)
        dg = dout_v * s + lax.dot_general(da.astype(BF16), w_ref[...], (((1,), (1,)), ((), ())),
                                          preferred_element_type=F32)
        dy_ref[...] = dg * _gelu_grad(y)
        g_ref[...] = g.astype(g_ref.dtype)
        da_ref[...] = da.astype(da_ref.dtype)

        @pl.when(pl.program_id(0) == 0)
        def _():
            db_ref[...] = jnp.zeros_like(db_ref)

        db_ref[...] += jnp.sum(da, axis=0, keepdims=True)

    row = pl.BlockSpec((tm, width), lambda i: (i, 0))
    vec = pl.BlockSpec((1, width), lambda i: (0, 0))
    return pl.pallas_call(
        body, name=name, grid=(t // tm,),
        in_specs=[row, row, pl.BlockSpec((width, width), lambda i: (0, 0)), vec],
        out_specs=[row, row, row, vec],
        out_shape=[jax.ShapeDtypeStruct((t, width), F32), jax.ShapeDtypeStruct((t, width), BF16),
                   jax.ShapeDtypeStruct((t, width), BF16), jax.ShapeDtypeStruct((1, width), F32)],
        compiler_params=_cp(("arbitrary",)),
    )(y5, dout, w, b)


def _adamw(w, g, m, v):
    m = ADAM_B1 * m + (1.0 - ADAM_B1) * g
    v = ADAM_B2 * v + (1.0 - ADAM_B2) * (g * g)
    m_hat = m / (1.0 - ADAM_B1 ** ADAM_STEP)
    v_hat = v / (1.0 - ADAM_B2 ** ADAM_STEP)
    return -ADAM_LR * (m_hat / (jnp.sqrt(v_hat) + ADAM_EPS) + ADAM_WD * w), m, v


def _adamw_shard(w, m, v, sums, got, chip, name):
    rows, cols = w.shape
    wide = sums.shape[2]
    tr = _row_tile(rows, wide, target=2**20)

    def body(chip_ref, w_ref, m_ref, v_ref, s_ref, g0_ref, g1_ref, g2_ref, g_out, d_out, m_out, v_out):
        g = s_ref[...].astype(F32) + g0_ref[...].astype(F32) + g1_ref[...].astype(F32) + g2_ref[...].astype(F32)
        g = g[:, :cols]
        delta, m_new, v_new = _adamw(w_ref[...], g, m_ref[...], v_ref[...])
        g_out[...] = g
        d_out[...] = delta
        m_out[...] = m_new
        v_out[...] = v_new

    blk = pl.BlockSpec((tr, cols), lambda i, chip_ref: (i, 0))

    def part(k):
        return pl.BlockSpec((None, tr, wide), lambda i, chip_ref: (k, i, 0))

    return pl.pallas_call(
        body, name=name,
        grid_spec=pltpu.PrefetchScalarGridSpec(
            num_scalar_prefetch=1, grid=(rows // tr,),
            in_specs=[blk, blk, blk, pl.BlockSpec((None, tr, wide), lambda i, chip_ref: (chip_ref[0], i, 0)),
                      part(0), part(1), part(2)],
            out_specs=[blk] * 4),
        out_shape=[jax.ShapeDtypeStruct((rows, cols), F32)] * 4,
        compiler_params=_cp(("parallel",)),
    )(chip, w, m, v, sums, got, got, got)


def _adamw_packed(w, m, v, g, name):
    def body(w_ref, m_ref, v_ref, g_ref, d_out, m_out, v_out):
        delta, m_new, v_new = _adamw(w_ref[...], g_ref[...], m_ref[...], v_ref[...])
        d_out[...] = delta
        m_out[...] = m_new
        v_out[...] = v_new

    vm = pl.BlockSpec(memory_space=pltpu.VMEM)
    return pl.pallas_call(
        body, name=name, in_specs=[vm] * 4, out_specs=[vm] * 3,
        out_shape=[jax.ShapeDtypeStruct(w.shape, F32)] * 3,
        compiler_params=_cp(),
    )(w, m, v, g)


WEIGHTS = ("g_mix", "w_in", "b_fgate", "b_gates", "q_norm", "k_norm", "s5_lambda_re", "s5_lambda_im", "s5_log_step",
           "s5_b_re", "s5_b_im", "s5_c_re", "s5_c_im", "s5_d", "w_glu", "b_glu", "w_proj_fox", "w_proj_s5", "w_out",
           "g_ffn", "w_gate_up", "w_down")
COLUMN_SHARDED = ("w_in", "w_proj_fox", "w_proj_s5", "w_gate_up")
ROW_SHARDED = ("w_glu", "w_out", "w_down")
PACK_ROWS = 8 * LANES

def _pack(arrays):
    flat = jnp.concatenate([a.reshape(-1).astype(F32) for a in arrays])
    flat = jnp.pad(flat, (0, (-flat.shape[0]) % PACK_ROWS))
    return flat.reshape(-1, LANES)


def _unpack(packed, like):
    flat, out, at = packed.reshape(-1), [], 0
    for a in like:
        out.append(flat[at:at + a.size].reshape(a.shape))
        at += a.size
    return out


def _pad_lanes(a):
    return jnp.pad(a, ((0, 0), (0, LANES - a.shape[1])))


def kernel(x, g_mix, w_in, b_fgate, b_gates, q_norm, k_norm, s5_lambda_re, s5_lambda_im, s5_log_step, s5_b_re, s5_b_im,
           s5_c_re, s5_c_im, s5_d, w_glu, b_glu, w_proj_fox, w_proj_s5, w_out, g_ffn, w_gate_up, w_down,
           loss_target, m_g_mix, m_w_in, m_b_fgate, m_b_gates, m_q_norm, m_k_norm, m_s5_lambda_re,
           m_s5_lambda_im, m_s5_log_step, m_s5_b_re, m_s5_b_im, m_s5_c_re, m_s5_c_im, m_s5_d, m_w_glu,
           m_b_glu, m_w_proj_fox, m_w_proj_s5, m_w_out, m_g_ffn, m_w_gate_up, m_w_down, v_g_mix, v_w_in,
           v_b_fgate, v_b_gates, v_q_norm, v_k_norm, v_s5_lambda_re, v_s5_lambda_im, v_s5_log_step, v_s5_b_re,
           v_s5_b_im, v_s5_c_re, v_s5_c_im, v_s5_d, v_w_glu, v_b_glu, v_w_proj_fox, v_w_proj_s5, v_w_out,
           v_g_ffn, v_w_gate_up, v_w_down):
    given = dict(locals())
    weights = {n: given[n] for n in WEIGHTS}
    mom_m = {n: given["m_" + n] for n in WEIGHTS}
    mom_v = {n: given["v_" + n] for n in WEIGHTS}

    pos_x, pos_y, pos_c = _position()
    core = jnp.reshape(pos_c, (1,)).astype(jnp.int32)
    chip = jnp.reshape(2 * pos_x + pos_y, (1,)).astype(jnp.int32)

    xs, target = x[0], loss_target[0]
    t, d = xs.shape
    heads, dh = b_fgate.shape[-1], q_norm.shape[-1]
    fw = heads * dh
    groups, states, gwidth = s5_b_re.shape[1:]
    sw = groups * gwidth
    gp = groups * states
    assert dh == LANES and sw % LANES == 0 and LANES % gwidth == 0
    col_v, col_f, col_s5 = 3 * fw, 3 * fw + heads, 3 * fw + heads + sw

    shard = {n: weights[n][0].astype(BF16) for n in COLUMN_SHARDED + ROW_SHARDED}

    def whole(n, ag):
        if n in COLUMN_SHARDED:
            return ag.transpose(1, 0, 2).reshape(ag.shape[1], N_DEV * ag.shape[2])
        return ag.reshape(N_DEV * ag.shape[1], ag.shape[2])

    full = {}
    c_gu, r_dn = w_gate_up.shape[2], w_down.shape[1]
    assert c_gu == 2 * r_dn
    cp_gu = -(-c_gu // LANES) * LANES
    shard["w_gate_up"] = jnp.pad(shard["w_gate_up"], ((0, 0), (0, cp_gu - c_gu)))

    def down_rows(ag):
        gap = [jnp.zeros((cp_gu - c_gu, d), ag.dtype)] if cp_gu > c_gu else []
        return jnp.concatenate([p for b in range(N_DEV // 2) for p in [ag[2 * b], ag[2 * b + 1]] + gap], axis=0)

    c_in = w_in.shape[2]
    in_cols = N_DEV * c_in

    def in_pieces(lo, hi, take):
        cuts = [(k, max(lo, k * c_in), min(hi, (k + 1) * c_in)) for k in range(N_DEV)]
        return [take(k, a - k * c_in, b - k * c_in) for k, a, b in cuts if a < b]

    ag_in = _all_gather(shard["w_in"], "ag_w_in")

    def from_gathered(k, a, b):
        return ag_in[k][:, a:b]

    w_main = jnp.concatenate(in_pieces(0, col_v, from_gathered) + in_pieces(col_f, in_cols, from_gathered), axis=1)
    w_forget = _pad_lanes(jnp.concatenate(in_pieces(col_v, col_f, from_gathered), axis=1))
    z_s5, z_gate = 3 * fw, 3 * fw + sw

    u, r_mix = _rms_fwd(xs, g_mix, "rms_mix")
    early = ("w_proj_fox", "w_proj_s5", "w_glu", "w_out")
    z, got = _mm(u, w_main, "nn", BF16, "mm_z", job=_GatherJob([shard[n] for n in early]))
    for n, ag in zip(early, got):
        full[n] = whole(n, ag)
    zf = _mm(u, w_forget, "nn", F32, "mm_zf")
    qn, kn, r_q, r_k = _qk_prep(z, heads, dh, q_norm, k_norm, "qk_prep")
    b_forget = _pad_lanes(b_fgate)
    cum = _forget_fwd(zf, b_forget, "forget_fwd")
    cum_t = cum[:, :heads].T
    fcol, frow = cum_t[:, :, None], cum_t[:, None, :]
    (attn, lse), got = _attn_fwd(qn, kn, z, 2 * heads, fcol, frow, heads, dh, "attn_fwd",
                                 job=_GatherJob([shard["w_gate_up"]]))
    full["w_gate_up"] = got[0]

    lam_re, lam_im = s5_lambda_re.reshape(gp, 1), s5_lambda_im.reshape(gp, 1)
    log_step = jnp.repeat(s5_log_step.reshape(groups, 1), states, axis=1).reshape(gp, 1)
    b_re, b_im = s5_b_re.reshape(gp, gwidth), s5_b_im.reshape(gp, gwidth)
    lb_re, lb_im, bb_re, bb_im = _s5_prep(lam_re, lam_im, log_step, b_re, b_im, "s5_prep")
    nb, per = sw // LANES, LANES // gwidth
    eye = jnp.eye(per, dtype=F32)

    def diag_b(bb):
        return jnp.einsum("napi,ab->naibp", bb.reshape(nb, per, states, gwidth), eye).reshape(nb, LANES, per * states)

    def diag_c(c):
        return jnp.einsum("naip,ab->nbpai", c.reshape(nb, per, gwidth, states), eye).reshape(nb, per * states, LANES)

    def undiag_b(g):
        return jnp.einsum("naibp,ab->napi", g.reshape(nb, per, gwidth, per, states), eye).reshape(gp, gwidth)

    def undiag_c(g):
        return jnp.einsum("nbpai,ab->naip", g.reshape(nb, per, states, per, gwidth), eye).reshape(1, groups, gwidth, states)

    bbr, bbi = diag_b(bb_re).astype(BF16), diag_b(bb_im).astype(BF16)
    ccr, cci = diag_c(s5_c_re[0]).astype(BF16), diag_c(s5_c_im[0]).astype(BF16)
    a_re, a_im = lb_re.reshape(1, gp), lb_im.reshape(1, gp)
    d_skip = s5_d.reshape(1, sw)
    u5 = z[:, z_s5:z_s5 + sw]
    y5 = _s5_fwd(u5, bbr, bbi, ccr, cci, d_skip, a_re, a_im, "s5_fwd")
    ssm = _glu_fwd(y5, full["w_glu"], b_glu, "glu_fwd")

    pf = _mm(attn, full["w_proj_fox"], "nn", BF16, "mm_pf")
    ps = _mm(ssm, full["w_proj_s5"], "nn", BF16, "mm_ps")
    merged = _gate_merge_fwd(z, z_gate, b_gates, pf, ps, "merge_fwd")
    mo = _mm(merged, full["w_out"], "nn", F32, "mm_out")
    h, hn, r_ffn = _resid_rms(xs, mo, g_ffn, "resid_rms")
    gu, got = _mm(hn, full["w_gate_up"], "nn", BF16, "mm_gu", job=_GatherJob([shard["w_down"]]), b_stacked=True)
    full["w_down"] = down_rows(got[0])
    act = _swiglu_fwd(gu, "swiglu_fwd")
    dn = _mm(act, full["w_down"], "nn", F32, "mm_down")
    loss_blk, dy, dy_b = _loss_head(h, dn, target, "loss_head")
    loss = lax.psum(loss_blk[0, 0], ("x", "y", "c"))

    grad, sums, from_chips = {}, {}, {}

    def pair_sums(n, parts=None):
        if parts is None:
            g_full = grad[n]
            if n in COLUMN_SHARDED:
                parts = g_full.reshape(g_full.shape[0], N_DEV, g_full.shape[1] // N_DEV).transpose(1, 0, 2)
            else:
                parts = g_full.reshape(N_DEV, g_full.shape[0] // N_DEV, g_full.shape[1])
        got = _swap_with_sibling(parts, "rs_sibling_" + n)
        return _add_sibling(parts, got, core, "rs_add_" + n)

    dact = _mm(dy_b, full["w_down"], "nt", BF16, "mm_dact")
    gw_down = _mm(act, dy_b, "tn", BF16, "mm_gw_down")
    sums["w_down"] = pair_sums("w_down", jnp.stack(
        [lax.slice_in_dim(gw_down, cp_gu * (k // 2) + r_dn * (k % 2), cp_gu * (k // 2) + r_dn * (k % 2) + r_dn)
         for k in range(N_DEV)]))
    dgate, dup = _swiglu_bwd(gu, dact, "swiglu_bwd")
    dgu = jnp.concatenate([dgate, dup], axis=1)
    dhn, (from_chips["w_down"],) = _mm(dgu, full["w_gate_up"], "nt", F32, "mm_dhn", job=_ChipSwapJob([sums["w_down"]]),
                                       b_stacked=True)
    sums["w_gate_up"] = pair_sums("w_gate_up", _mm(hn, dgu, "tn", BF16, "mm_gw_gu", out_stack=N_DEV))
    dh_, dh_b, grad["g_ffn"] = _rms_bwd([dhn], h, r_ffn, g_ffn, [dy], "rms_ffn_bwd")
    dmerged = _mm(dh_b, full["w_out"], "nt", BF16, "mm_dmerged")
    grad["w_out"] = _mm(merged, dh_b, "tn", BF16, "mm_gw_out")
    dpf, dps, dz_gf, dz_gs, db_gf, db_gs = _gate_merge_bwd(dmerged, z, z_gate, b_gates, pf, ps, "merge_bwd")
    grad["b_gates"] = jnp.concatenate([db_gf, db_gs], axis=1)
    dattn = _mm(dpf, full["w_proj_fox"], "nt", BF16, "mm_dattn")
    grad["w_proj_fox"] = _mm(attn, dpf, "tn", BF16, "mm_gw_pf")
    dssm = _mm(dps, full["w_proj_s5"], "nt", BF16, "mm_dssm")
    grad["w_proj_s5"] = _mm(ssm, dps, "tn", BF16, "mm_gw_ps")

    dy5, g5, da5, grad["b_glu"] = _glu_bwd(y5, dssm, full["w_glu"], b_glu, "glu_bwd")
    grad["w_glu"] = _mm(g5, da5, "tn", BF16, "mm_gw_glu")
    for n in early:
        sums[n] = pair_sums(n)
    du5, d_bbr, d_bbi, d_ccr, d_cci, d_are, d_aim, d_dskip = _s5_bwd(
        u5, dy5, bbr, bbi, ccr, cci, d_skip, a_re, a_im, "s5_bwd")
    d_lre, d_lim, d_lstep, d_bre, d_bim = _s5_prep_bwd(
        lam_re, lam_im, log_step, b_re, b_im, d_are.reshape(gp, 1), d_aim.reshape(gp, 1),
        undiag_b(d_bbr), undiag_b(d_bbi), groups, "s5_prep_bwd")
    grad["s5_lambda_re"], grad["s5_lambda_im"] = d_lre.reshape(1, groups, states), d_lim.reshape(1, groups, states)
    grad["s5_log_step"] = d_lstep
    grad["s5_b_re"], grad["s5_b_im"] = d_bre.reshape(s5_b_re.shape), d_bim.reshape(s5_b_im.shape)
    grad["s5_c_re"], grad["s5_c_im"] = undiag_c(d_ccr), undiag_c(d_cci)
    grad["s5_d"] = d_dskip.reshape(s5_d.shape)

    delta = _attn_delta(attn, dattn, heads, dh, "attn_delta")
    (dqn,), got = _attn_bwd_q(qn, kn, z, 2 * heads, dattn, fcol, frow, lse, delta, heads, dh, "attn_bwd_q",
                              job=_ChipSwapJob([sums[n] for n in early]))
    from_chips.update(zip(early, got))
    (dkn, dv, df_k), (from_chips["w_gate_up"],) = _attn_bwd_kv(
        qn, kn, z, 2 * heads, dattn, fcol, frow, lse, delta, heads, dh, "attn_bwd_kv",
        job=_ChipSwapJob([sums["w_gate_up"]]))
    dq, dk, grad["q_norm"], grad["k_norm"] = _qk_prep_bwd(dqn, dkn, z, heads, dh, q_norm, k_norm, r_q, r_k, "qk_prep_bwd")
    dzf, db_forget = _forget_bwd(_pad_lanes(df_k[:, 0, :].T), zf, b_forget, "forget_bwd")
    grad["b_fgate"] = db_forget[:, :heads]

    dz = jnp.concatenate([dq, dk, dv, du5.astype(BF16), dz_gf, dz_gs], axis=1)
    gw_main = _mm(u, dz, "tn", BF16, "mm_gw_main")
    gw_forget = _mm(u, dzf, "tn", BF16, "mm_gw_forget")

    def from_grads(k, a, b):
        lo, hi = k * c_in + a, k * c_in + b
        if hi <= col_v:
            return gw_main[:, lo:hi]
        if hi <= col_f:
            return gw_forget[:, lo - col_v:hi - col_v]
        return gw_main[:, lo - heads:hi - heads]

    def in_part(k):
        lo, hi = k * c_in, (k + 1) * c_in
        cuts = [(max(lo, a), min(hi, b)) for a, b in ((0, col_v), (col_v, col_f), (col_f, in_cols))]
        return jnp.concatenate([from_grads(k, a - lo, b - lo) for a, b in cuts if a < b], axis=1)

    sums["w_in"] = pair_sums("w_in", jnp.stack([in_part(k) for k in range(N_DEV)]))
    du, (from_chips["w_in"],) = _mm(dz, w_main, "nt", F32, "mm_du", job=_ChipSwapJob([sums["w_in"]]))
    du_f = _mm(dzf, w_forget, "nt", F32, "mm_du_f")
    dx, _, grad["g_mix"] = _rms_bwd([du, du_f], xs, r_mix, g_mix, [dh_], "rms_mix_bwd")

    out_g, out_d, out_m, out_v = {}, {}, {}, {}
    for n in COLUMN_SHARDED + ROW_SHARDED:
        res = _adamw_shard(weights[n][0], mom_m[n][0], mom_v[n][0], sums[n], from_chips[n], chip, "adamw_" + n)
        out_g[n], out_d[n], out_m[n], out_v[n] = (r[None] for r in res)

    small = [n for n in WEIGHTS if n not in COLUMN_SHARDED + ROW_SHARDED]
    g_small = _all_reduce_small(_pack([grad[n] for n in small]), "ar_small")
    like = [weights[n] for n in small]
    res = _adamw_packed(_pack(like), _pack([mom_m[n] for n in small]), _pack([mom_v[n] for n in small]), g_small,
                        "adamw_small")
    for store, packed in zip((out_g, out_d, out_m, out_v), (g_small, *res)):
        for n, a in zip(small, _unpack(packed, like)):
            store[n] = a

    return (loss, dx[None], *[out_g[n] for n in WEIGHTS], *[out_d[n] for n in WEIGHTS],
            *[out_m[n] for n in WEIGHTS], *[out_v[n] for n in WEIGHTS])
```

```python
import functools
import math

import jax
import jax.numpy as jnp
from jax import lax
from jax.experimental import pallas as pl
from jax.experimental.pallas import tpu as pltpu

F32 = jnp.float32
BF16 = jnp.bfloat16

V7X_VMEM_LIMIT = 56 * 2**20
LANES = 128
N_DEV = 8
MESH = pl.DeviceIdType.MESH

RMS_EPS = 1e-6
MASK_VALUE = -1e30
ADAM_LR, ADAM_B1, ADAM_B2, ADAM_EPS, ADAM_WD, ADAM_STEP = 0.001, 0.9, 0.999, 1e-08, 0.01, 10
GELU_C = math.sqrt(2.0 / math.pi)
GELU_A = 0.044715


def _cp(sem=None):
    return pltpu.CompilerParams(dimension_semantics=sem, vmem_limit_bytes=V7X_VMEM_LIMIT)


def _tile(n, pref, unit=LANES):
    if n <= pref:
        return n
    t = (pref // unit) * unit
    while t >= unit:
        if n % t == 0:
            return t
        t -= unit
    raise ValueError(f"no tile for {n}")


def _row_tile(rows, cols, bytes_per_row_elem=4, target=2 * 2**20, unit=16):
    best = None
    for t in range(unit, rows + 1, unit):
        if rows % t == 0 and t * cols * bytes_per_row_elem <= target:
            best = t
    if best is None:
        best = unit if rows % unit == 0 else rows
    return best


def _sigmoid(x):
    return 1.0 / (1.0 + jnp.exp(-x))


def _position():
    return lax.axis_index("x"), lax.axis_index("y"), lax.axis_index("c")


def _other_chips(x, y):
    return [(1 - x, y), (x, 1 - y), (1 - x, 1 - y)]


def _all_gather(shard, name):
    rows, cols = shard.shape

    def body(x_ref, out_ref, send_sems, recv_sems, local_sem):
        x, y, c = _position()
        me, sibling = (x, y, c), (x, y, 1 - c)
        chips = _other_chips(x, y)

        def slot(px, py, pc):
            return out_ref.at[4 * px + 2 * py + pc]

        def copy(k, block, to, src=None):
            return pltpu.make_async_remote_copy(
                src_ref=slot(*block) if src is None else src, dst_ref=slot(*block),
                send_sem=send_sems.at[k], recv_sem=recv_sems.at[k], device_id=to, device_id_type=MESH)

        mine = pltpu.make_async_copy(x_ref, slot(*me), local_sem)
        mine.start()
        first = [copy(0, me, sibling, src=x_ref)]
        first += [copy(1 + j, me, (*chip, c), src=x_ref) for j, chip in enumerate(chips)]
        for cp in first:
            cp.start()
        passed = [copy(4 + j, (*chip, c), sibling) for j, chip in enumerate(chips)]
        for j, chip in enumerate(chips):
            copy(1 + j, (*chip, c), me).wait_recv()
            passed[j].start()
        copy(0, sibling, me).wait_recv()
        for j, chip in enumerate(chips):
            copy(4 + j, (*chip, 1 - c), me).wait_recv()
        for cp in first + passed:
            cp.wait_send()
        mine.wait()

    return pl.pallas_call(
        body, name=name,
        out_shape=jax.ShapeDtypeStruct((N_DEV, rows, cols), shard.dtype),
        in_specs=[pl.BlockSpec(memory_space=pltpu.HBM)],
        out_specs=pl.BlockSpec(memory_space=pltpu.HBM),
        scratch_shapes=[pltpu.SemaphoreType.DMA((7,)), pltpu.SemaphoreType.DMA((7,)), pltpu.SemaphoreType.DMA],
    )(shard)


def _swap_with_sibling(parts, name):
    _, rows, cols = parts.shape

    def body(p_ref, out_ref, send_sems, recv_sems):
        x, y, c = _position()
        copies = []
        for j in range(4):
            copies.append(pltpu.make_async_remote_copy(
                src_ref=p_ref.at[2 * j + (1 - c)], dst_ref=out_ref.at[j],
                send_sem=send_sems.at[j], recv_sem=recv_sems.at[j], device_id=(x, y, 1 - c), device_id_type=MESH))
        for cp in copies:
            cp.start()
        for cp in copies:
            cp.wait()

    return pl.pallas_call(
        body, name=name,
        out_shape=jax.ShapeDtypeStruct((4, rows, cols), parts.dtype),
        in_specs=[pl.BlockSpec(memory_space=pltpu.HBM)],
        out_specs=pl.BlockSpec(memory_space=pltpu.HBM),
        scratch_shapes=[pltpu.SemaphoreType.DMA((4,)), pltpu.SemaphoreType.DMA((4,))],
    )(parts)


def _add_sibling(parts, got, core, name):
    _, rows, cols = parts.shape
    tr = _row_tile(rows, cols)

    def body(core_ref, a_ref, b_ref, o_ref):
        o_ref[...] = (a_ref[...].astype(F32) + b_ref[...].astype(F32)).astype(o_ref.dtype)

    return pl.pallas_call(
        body, name=name,
        grid_spec=pltpu.PrefetchScalarGridSpec(
            num_scalar_prefetch=1, grid=(4, rows // tr),
            in_specs=[pl.BlockSpec((None, tr, cols), lambda j, i, core_ref: (2 * j + core_ref[0], i, 0)),
                      pl.BlockSpec((None, tr, cols), lambda j, i, core_ref: (j, i, 0))],
            out_specs=pl.BlockSpec((None, tr, cols), lambda j, i, core_ref: (j, i, 0))),
        out_shape=jax.ShapeDtypeStruct((4, rows, cols), BF16),
        compiler_params=_cp(("parallel", "parallel")),
    )(core, parts, got)


def _all_reduce_small(packed, name):
    rows, cols = packed.shape

    def body(x_ref, out_ref, gathered, send_sems, recv_sems):
        x, y, c = _position()
        me, sibling = (x, y, c), (x, y, 1 - c)
        chips = _other_chips(x, y)

        def slot(px, py, pc):
            return gathered.at[4 * px + 2 * py + pc]

        def copy(k, block, to, src=None):
            return pltpu.make_async_remote_copy(
                src_ref=slot(*block) if src is None else src, dst_ref=slot(*block),
                send_sem=send_sems.at[k], recv_sem=recv_sems.at[k], device_id=to, device_id_type=MESH)

        first = [copy(0, me, sibling, src=x_ref)]
        first += [copy(1 + j, me, (*chip, c), src=x_ref) for j, chip in enumerate(chips)]
        for cp in first:
            cp.start()
        passed = [copy(4 + j, (*chip, c), sibling) for j, chip in enumerate(chips)]
        for j, chip in enumerate(chips):
            copy(1 + j, (*chip, c), me).wait_recv()
            passed[j].start()
        copy(0, sibling, me).wait_recv()
        for j, chip in enumerate(chips):
            copy(4 + j, (*chip, 1 - c), me).wait_recv()
        for cp in first + passed:
            cp.wait_send()
        gathered[4 * x + 2 * y + c] = x_ref[...]
        total = gathered[0]
        for k in range(1, N_DEV):
            total = total + gathered[k]
        out_ref[...] = total

    return pl.pallas_call(
        body, name=name,
        out_shape=jax.ShapeDtypeStruct((rows, cols), F32),
        in_specs=[pl.BlockSpec(memory_space=pltpu.VMEM)],
        out_specs=pl.BlockSpec(memory_space=pltpu.VMEM),
        scratch_shapes=[pltpu.VMEM((N_DEV, rows, cols), F32),
                        pltpu.SemaphoreType.DMA((7,)), pltpu.SemaphoreType.DMA((7,))],
        compiler_params=pltpu.CompilerParams(vmem_limit_bytes=V7X_VMEM_LIMIT),
    )(packed)


class _GatherJob:
    def __init__(self, shards, rows=None, into=None):
        self.n, self.rows = len(shards), rows
        self.inputs = list(shards) + list(into or [])
        self.aliases = {self.n + i: i for i in range(len(into or []))}
        self.out_shape = [jax.ShapeDtypeStruct((N_DEV,) + s.shape, s.dtype) for s in shards]
        self.scratch = [pltpu.SemaphoreType.DMA((7 * self.n,)), pltpu.SemaphoreType.DMA((7 * self.n,)),
                        pltpu.SemaphoreType.DMA((self.n,))]

    def _plan(self, ins, outs, scratch):
        send_sems, recv_sems, local_sems = scratch
        x, y, c = _position()
        me, sibling = (x, y, c), (x, y, 1 - c)
        chips = _other_chips(x, y)

        def slot(i, px, py, pc):
            return _rows_of(outs[i].at[4 * px + 2 * py + pc], self.rows)

        def copy(i, k, block, to, own=False):
            return pltpu.make_async_remote_copy(
                src_ref=_rows_of(ins[i], self.rows) if own else slot(i, *block), dst_ref=slot(i, *block),
                send_sem=send_sems.at[7 * i + k], recv_sem=recv_sems.at[7 * i + k], device_id=to, device_id_type=MESH)

        def mine(i):
            return pltpu.make_async_copy(_rows_of(ins[i], self.rows), slot(i, *me), local_sems.at[i])

        return c, me, sibling, chips, copy, mine

    def begin(self, ins, outs, scratch):
        c, me, sibling, chips, copy, mine = self._plan(ins, outs, scratch)
        for i in range(self.n):
            mine(i).start()
            copy(i, 0, me, sibling, own=True).start()
            for j, chip in enumerate(chips):
                copy(i, 1 + j, me, (*chip, c), own=True).start()

    def middle(self, ins, outs, scratch):
        c, me, sibling, chips, copy, mine = self._plan(ins, outs, scratch)
        for i in range(self.n):
            for j, chip in enumerate(chips):
                copy(i, 1 + j, (*chip, c), me).wait_recv()
                copy(i, 4 + j, (*chip, c), sibling).start()

    def end(self, ins, outs, scratch):
        c, me, sibling, chips, copy, mine = self._plan(ins, outs, scratch)
        for i in range(self.n):
            copy(i, 0, sibling, me).wait_recv()
            for j, chip in enumerate(chips):
                copy(i, 4 + j, (*chip, 1 - c), me).wait_recv()
            copy(i, 0, me, sibling, own=True).wait_send()
            for j, chip in enumerate(chips):
                copy(i, 1 + j, me, (*chip, c), own=True).wait_send()
                copy(i, 4 + j, (*chip, c), sibling).wait_send()
            mine(i).wait()


class _ChipSwapJob:
    def __init__(self, sums, rows=None, into=None):
        self.n, self.rows = len(sums), rows
        self.inputs = list(sums) + list(into or [])
        self.aliases = {self.n + i: i for i in range(len(into or []))}
        self.out_shape = [jax.ShapeDtypeStruct((3,) + s.shape[1:], s.dtype) for s in sums]
        self.scratch = [pltpu.SemaphoreType.DMA((3 * self.n,)), pltpu.SemaphoreType.DMA((3 * self.n,))]

    def _copies(self, ins, outs, scratch):
        send_sems, recv_sems = scratch
        x, y, c = _position()
        return [pltpu.make_async_remote_copy(
            src_ref=_rows_of(ins[i].at[2 * px + py], self.rows), dst_ref=_rows_of(outs[i].at[k], self.rows),
            send_sem=send_sems.at[3 * i + k], recv_sem=recv_sems.at[3 * i + k],
            device_id=(px, py, c), device_id_type=MESH)
            for i in range(self.n) for k, (px, py) in enumerate(_other_chips(x, y))]

    def begin(self, ins, outs, scratch):
        for cp in self._copies(ins, outs, scratch):
            cp.start()

    def middle(self, ins, outs, scratch):
        pass

    def end(self, ins, outs, scratch):
        for cp in self._copies(ins, outs, scratch):
            cp.wait()


class _SiblingSwapJob:
    aliases = {}

    def __init__(self, parts):
        self.inputs = list(parts)
        self.out_shape = [jax.ShapeDtypeStruct((4,) + p.shape[1:], p.dtype) for p in parts]
        n = len(parts)
        self.scratch = [pltpu.SemaphoreType.DMA((4 * n,)), pltpu.SemaphoreType.DMA((4 * n,))]

    def _copies(self, ins, outs, scratch):
        send_sems, recv_sems = scratch
        x, y, c = _position()
        return [pltpu.make_async_remote_copy(
            src_ref=ins[i].at[2 * j + (1 - c)], dst_ref=outs[i].at[j], send_sem=send_sems.at[4 * i + j],
            recv_sem=recv_sems.at[4 * i + j], device_id=(x, y, 1 - c), device_id_type=MESH)
            for i in range(len(ins)) for j in range(4)]

    def begin(self, ins, outs, scratch):
        for cp in self._copies(ins, outs, scratch):
            cp.start()

    def middle(self, ins, outs, scratch):
        pass

    def end(self, ins, outs, scratch):
        for cp in self._copies(ins, outs, scratch):
            cp.wait()


class _JobGroup:
    def __init__(self, jobs):
        self.jobs = list(jobs)
        self.inputs = [a for j in jobs for a in j.inputs]
        self.out_shape = [s for j in jobs for s in j.out_shape]
        self.scratch = [s for j in jobs for s in j.scratch]
        self.aliases, at_in, at_out = {}, 0, 0
        for j in jobs:
            self.aliases.update({at_in + i: at_out + o for i, o in j.aliases.items()})
            at_in, at_out = at_in + len(j.inputs), at_out + len(j.out_shape)

    def _each(self, phase, ins, outs, scratch):
        for j in self.jobs:
            n_in, n_out, n_scr = len(j.inputs), len(j.out_shape), len(j.scratch)
            getattr(j, phase)(ins[:n_in], outs[:n_out], scratch[:n_scr])
            ins, outs, scratch = ins[n_in:], outs[n_out:], scratch[n_scr:]

    def begin(self, ins, outs, scratch):
        self._each("begin", ins, outs, scratch)

    def middle(self, ins, outs, scratch):
        self._each("middle", ins, outs, scratch)

    def end(self, ins, outs, scratch):
        self._each("end", ins, outs, scratch)


def _rows_of(ref, rows):
    return ref if rows is None else ref.at[pl.ds(rows[0], rows[1])]


def _call(body, *, name, grid, in_specs, out_specs, out_shape, scratch_shapes, semantics, operands, job=None):
    if job is None:
        return pl.pallas_call(
            body, name=name, grid=grid, in_specs=in_specs, out_specs=out_specs, out_shape=out_shape,
            scratch_shapes=scratch_shapes, compiler_params=_cp(semantics))(*operands)
    n_in, n_out, n_scr = len(in_specs), len(out_specs), len(scratch_shapes)
    j_in, j_out = len(job.inputs), len(job.out_shape)
    n_steps = math.prod(grid)
    hbm = pl.BlockSpec(memory_space=pltpu.HBM)

    def carrier(*refs):
        ins, refs = refs[:n_in], refs[n_in:]
        job_ins, refs = refs[:j_in], refs[j_in:]
        outs, refs = refs[:n_out], refs[n_out:]
        job_outs, refs = refs[:j_out], refs[j_out:]
        scr, job_scr = refs[:n_scr], refs[n_scr:]
        step = pl.program_id(0)
        for axis in range(1, len(grid)):
            step = step * grid[axis] + pl.program_id(axis)

        @pl.when(step == 0)
        def _():
            job.begin(job_ins, job_outs, job_scr)

        body(*ins, *outs, *scr)

        @pl.when(step == (3 * n_steps) // 4)
        def _():
            job.middle(job_ins, job_outs, job_scr)

        @pl.when(step == n_steps - 1)
        def _():
            job.end(job_ins, job_outs, job_scr)

    res = pl.pallas_call(
        carrier, name=name, grid=grid,
        in_specs=list(in_specs) + [hbm] * j_in, out_specs=list(out_specs) + [hbm] * j_out,
        out_shape=list(out_shape) + job.out_shape, scratch_shapes=list(scratch_shapes) + job.scratch,
        input_output_aliases={n_in + i: n_out + o for i, o in job.aliases.items()},
        compiler_params=_cp(("arbitrary",) * len(grid)))(*operands, *job.inputs)
    return res[:n_out], res[n_out:]


MM_VMEM_BUDGET = 44 * 2**20
MM_TILE_CAP = 1536
MM_MIN_INTENSITY = 340


def _divisor_tiles(n, cap):
    return [t for t in range(LANES, min(n, cap) + 1, LANES) if n % t == 0] or [n]


def _mm_tiles(m, n_unit, k_unit, whole_k, a_bytes, b_bytes, o_bytes):
    best, best_key = None, None
    for tm in _divisor_tiles(m, 1024):
        for tn in _divisor_tiles(n_unit, MM_TILE_CAP):
            for tk in _divisor_tiles(k_unit, k_unit):
                one_block = whole_k and tk == k_unit
                need = (2 * (tm * tk * a_bytes + tk * tn * b_bytes) + 2 * tm * tn * o_bytes + tm * tn * 4
                        + (0 if one_block else tm * tn * 4))
                intensity = tm * tn / (tm + tn)
                key = (intensity >= MM_MIN_INTENSITY, one_block, intensity, tk)
                if need <= MM_VMEM_BUDGET and (best_key is None or key > best_key):
                    best, best_key = (tm, tn, tk), key
    if best is None:
        raise ValueError(f"no matmul tiles for {(m, n_unit, k_unit)}")
    return best


def _mm(a, b, mode, out_dtype, name, job=None, b_stacked=False, out_stack=None):
    b_rows, b_cols = (b.shape[1], b.shape[0] * b.shape[2]) if b_stacked else b.shape
    b_unit = b.shape[2] if b_stacked else b_cols
    if mode == "nn":
        (m, k), (k2, n) = a.shape, (b_rows, b_cols)
    elif mode == "nt":
        (m, k), (n, k2) = a.shape, (b_rows, b_cols)
    else:
        (k, m), (k2, n) = a.shape, (b_rows, b_cols)
    assert k == k2, (name, a.shape, b.shape)
    n_unit = n // out_stack if out_stack else (b_unit if b_stacked and mode != "nt" else n)
    k_unit = b_unit if b_stacked and mode == "nt" else k
    tm, tn, tk = _mm_tiles(m, n_unit, k_unit, k_unit == k, a.dtype.itemsize, b.dtype.itemsize,
                           jnp.dtype(out_dtype).itemsize)
    nk = k // tk
    per_n, per_k = n_unit // tn, k_unit // tk
    if mode == "tn":
        a_spec = pl.BlockSpec((tk, tm), lambda i, j, l: (l, i))
        dims = (((0,), (0,)), ((), ()))
    else:
        a_spec = pl.BlockSpec((tm, tk), lambda i, j, l: (i, l))
        dims = (((1,), (1,)), ((), ())) if mode == "nt" else (((1,), (0,)), ((), ()))
    if mode == "nt" and b_stacked:
        b_spec = pl.BlockSpec((None, tn, tk), lambda i, j, l: (l // per_k, j, l % per_k))
    elif mode == "nt":
        b_spec = pl.BlockSpec((tn, tk), lambda i, j, l: (j, l))
    elif b_stacked:
        b_spec = pl.BlockSpec((None, tk, tn), lambda i, j, l: (j // per_n, l, j % per_n))
    else:
        b_spec = pl.BlockSpec((tk, tn), lambda i, j, l: (l, j))
    if out_stack:
        o_spec = pl.BlockSpec((None, tm, tn), lambda i, j, l: (j // per_n, i, j % per_n))
        o_shape = jax.ShapeDtypeStruct((out_stack, m, n_unit), out_dtype)
    else:
        o_spec = pl.BlockSpec((tm, tn), lambda i, j, l: (i, j))
        o_shape = jax.ShapeDtypeStruct((m, n), out_dtype)

    def product(a_ref, b_ref):
        return lax.dot_general(a_ref[...].astype(BF16), b_ref[...].astype(BF16), dims, preferred_element_type=F32)

    def body_whole_k(a_ref, b_ref, o_ref):
        o_ref[...] = product(a_ref, b_ref).astype(o_ref.dtype)

    def body_split_k(a_ref, b_ref, o_ref, acc_ref):
        l = pl.program_id(2)

        @pl.when(l == 0)
        def _():
            acc_ref[...] = product(a_ref, b_ref)

        @pl.when(l > 0)
        def _():
            acc_ref[...] += product(a_ref, b_ref)

        @pl.when(l == nk - 1)
        def _():
            o_ref[...] = acc_ref[...].astype(o_ref.dtype)

    res = _call(
        body_whole_k if nk == 1 else body_split_k, name=name, grid=(m // tm, n // tn, nk),
        in_specs=[a_spec, b_spec],
        out_specs=[o_spec],
        out_shape=[o_shape],
        scratch_shapes=[] if nk == 1 else [pltpu.VMEM((tm, tn), F32)],
        semantics=("parallel", "parallel", "arbitrary"), operands=(a, b), job=job)
    return res[0] if job is None else (res[0][0], res[1])


def _rms_fwd(x, g, name):
    t, d = x.shape
    tm = _tile(t, 256, 16)

    def body(x_ref, g_ref, u_ref, r_ref):
        xv = x_ref[...]
        r = lax.rsqrt(jnp.mean(xv * xv, axis=-1, keepdims=True) + RMS_EPS)
        u_ref[...] = (xv * r * g_ref[...]).astype(u_ref.dtype)
        r_ref[...] = r

    return pl.pallas_call(
        body, name=name, grid=(t // tm,),
        in_specs=[pl.BlockSpec((tm, d), lambda i: (i, 0)), pl.BlockSpec((1, d), lambda i: (0, 0))],
        out_specs=[pl.BlockSpec((tm, d), lambda i: (i, 0)), pl.BlockSpec((tm, 1), lambda i: (i, 0))],
        out_shape=[jax.ShapeDtypeStruct((t, d), BF16), jax.ShapeDtypeStruct((t, 1), F32)],
        compiler_params=_cp(("parallel",)),
    )(x, g)


def _rms_bwd(dn_parts, x, r, g, extra, name):
    t, d = x.shape
    tm = _tile(t, 128, 16)
    n_dn, n_extra = len(dn_parts), len(extra)

    def body(*refs):
        dn_refs = refs[:n_dn]
        x_ref, r_ref, g_ref = refs[n_dn:n_dn + 3]
        extra_refs = refs[n_dn + 3:n_dn + 3 + n_extra]
        dx_ref, dxb_ref, dg_ref = refs[n_dn + 3 + n_extra:]
        xhat = x_ref[...] * r_ref[...]
        dnv = dn_refs[0][...].astype(F32)
        for p in dn_refs[1:]:
            dnv = dnv + p[...].astype(F32)
        gd = dnv * g_ref[...]
        dx = r_ref[...] * (gd - xhat * jnp.mean(gd * xhat, axis=-1, keepdims=True))
        for e in extra_refs:
            dx = dx + e[...].astype(F32)
        dx_ref[...] = dx
        dxb_ref[...] = dx.astype(dxb_ref.dtype)

        @pl.when(pl.program_id(0) == 0)
        def _():
            dg_ref[...] = jnp.zeros_like(dg_ref)

        dg_ref[...] += jnp.sum(dnv * xhat, axis=0, keepdims=True)

    row = pl.BlockSpec((tm, d), lambda i: (i, 0))
    return pl.pallas_call(
        body, name=name, grid=(t // tm,),
        in_specs=[row] * n_dn + [row, pl.BlockSpec((tm, 1), lambda i: (i, 0)), pl.BlockSpec((1, d), lambda i: (0, 0))]
        + [row] * n_extra,
        out_specs=[row, row, pl.BlockSpec((1, d), lambda i: (0, 0))],
        out_shape=[jax.ShapeDtypeStruct((t, d), F32), jax.ShapeDtypeStruct((t, d), BF16), jax.ShapeDtypeStruct((1, d), F32)],
        compiler_params=_cp(("arbitrary",)),
    )(*dn_parts, x, r, g, *extra)


def _gate_merge_fwd(z, gate_col, b_gates, pf, ps, name):
    t, d = pf.shape
    tm, tn = _tile(t, 512, 16), _tile(math.gcd(d, gate_col), 512)
    nj, off = d // tn, gate_col // tn
    assert gate_col % tn == 0

    def body(zf_ref, zs_ref, bf_ref, bs_ref, pf_ref, ps_ref, o_ref):
        gf = _sigmoid(zf_ref[...].astype(F32) + bf_ref[...])
        gs = _sigmoid(zs_ref[...].astype(F32) + bs_ref[...])
        o_ref[...] = (gf * pf_ref[...].astype(F32) + gs * ps_ref[...].astype(F32)).astype(o_ref.dtype)

    blk = pl.BlockSpec((tm, tn), lambda i, j: (i, j))
    return pl.pallas_call(
        body, name=name, grid=(t // tm, nj),
        in_specs=[pl.BlockSpec((tm, tn), lambda i, j: (i, off + j)), pl.BlockSpec((tm, tn), lambda i, j: (i, off + nj + j)),
                  pl.BlockSpec((1, tn), lambda i, j: (0, j)), pl.BlockSpec((1, tn), lambda i, j: (0, nj + j)), blk, blk],
        out_specs=blk,
        out_shape=jax.ShapeDtypeStruct((t, d), BF16),
        compiler_params=_cp(("parallel", "parallel")),
    )(z, z, b_gates, b_gates, pf, ps)


def _gate_merge_bwd(dm, z, gate_col, b_gates, pf, ps, name):
    t, d = pf.shape
    tm, tn = _tile(t, 512, 16), _tile(math.gcd(d, gate_col), 512)
    nj, off = d // tn, gate_col // tn

    def body(dm_ref, zf_ref, zs_ref, bf_ref, bs_ref, pf_ref, ps_ref, dpf_ref, dps_ref, dzf_ref, dzs_ref, dbf_ref, dbs_ref):
        gf = _sigmoid(zf_ref[...].astype(F32) + bf_ref[...])
        gs = _sigmoid(zs_ref[...].astype(F32) + bs_ref[...])
        dmv = dm_ref[...].astype(F32)
        dpf_ref[...] = (dmv * gf).astype(dpf_ref.dtype)
        dps_ref[...] = (dmv * gs).astype(dps_ref.dtype)
        dzf = dmv * pf_ref[...].astype(F32) * gf * (1.0 - gf)
        dzs = dmv * ps_ref[...].astype(F32) * gs * (1.0 - gs)
        dzf_ref[...] = dzf.astype(dzf_ref.dtype)
        dzs_ref[...] = dzs.astype(dzs_ref.dtype)

        @pl.when(pl.program_id(1) == 0)
        def _():
            dbf_ref[...] = jnp.zeros_like(dbf_ref)
            dbs_ref[...] = jnp.zeros_like(dbs_ref)

        dbf_ref[...] += jnp.sum(dzf, axis=0, keepdims=True)
        dbs_ref[...] += jnp.sum(dzs, axis=0, keepdims=True)

    blk = pl.BlockSpec((tm, tn), lambda j, i: (i, j))
    lo = pl.BlockSpec((1, tn), lambda j, i: (0, j))
    hi = pl.BlockSpec((1, tn), lambda j, i: (0, nj + j))
    return pl.pallas_call(
        body, name=name, grid=(nj, t // tm),
        in_specs=[blk, pl.BlockSpec((tm, tn), lambda j, i: (i, off + j)), pl.BlockSpec((tm, tn), lambda j, i: (i, off + nj + j)),
                  lo, hi, blk, blk],
        out_specs=[blk, blk, blk, blk, lo, lo],
        out_shape=[jax.ShapeDtypeStruct((t, d), BF16)] * 4 + [jax.ShapeDtypeStruct((1, d), F32)] * 2,
        compiler_params=_cp(("parallel", "arbitrary")),
    )(dm, z, z, b_gates, b_gates, pf, ps)


def _resid_rms(x, mo, g, name):
    t, d = x.shape
    tm = _tile(t, 256, 16)

    def body(x_ref, mo_ref, g_ref, h_ref, hn_ref, r_ref):
        h = x_ref[...] + mo_ref[...].astype(F32)
        r = lax.rsqrt(jnp.mean(h * h, axis=-1, keepdims=True) + RMS_EPS)
        h_ref[...] = h
        hn_ref[...] = (h * r * g_ref[...]).astype(hn_ref.dtype)
        r_ref[...] = r

    row = pl.BlockSpec((tm, d), lambda i: (i, 0))
    col = pl.BlockSpec((tm, 1), lambda i: (i, 0))
    return pl.pallas_call(
        body, name=name, grid=(t // tm,),
        in_specs=[row, row, pl.BlockSpec((1, d), lambda i: (0, 0))],
        out_specs=[row, row, col],
        out_shape=[jax.ShapeDtypeStruct((t, d), F32), jax.ShapeDtypeStruct((t, d), BF16), jax.ShapeDtypeStruct((t, 1), F32)],
        compiler_params=_cp(("parallel",)),
    )(x, mo, g)


def _swiglu_fwd(gu, name):
    t, f2 = gu.shape
    f = f2 // 2
    tm, tn = _tile(t, 512, 16), _tile(f, 1024)
    nj = f // tn

    def body(g_ref, u_ref, o_ref):
        gate = g_ref[...].astype(F32)
        o_ref[...] = (gate * _sigmoid(gate) * u_ref[...].astype(F32)).astype(o_ref.dtype)

    return pl.pallas_call(
        body, name=name, grid=(t // tm, nj),
        in_specs=[pl.BlockSpec((tm, tn), lambda i, j: (i, j)), pl.BlockSpec((tm, tn), lambda i, j: (i, nj + j))],
        out_specs=pl.BlockSpec((tm, tn), lambda i, j: (i, j)),
        out_shape=jax.ShapeDtypeStruct((t, f), BF16),
        compiler_params=_cp(("parallel", "parallel")),
    )(gu, gu)


def _swiglu_bwd(gu, dact, name, job=None):
    t, f2 = gu.shape
    f = f2 // 2
    tm, tn = _tile(t, 512, 16), _tile(f, 1024)
    nj = f // tn

    def body(g_ref, u_ref, da_ref, o_ref):
        gate = g_ref[...].astype(F32)
        s = _sigmoid(gate)
        da = da_ref[...].astype(F32)
        dgate = da * u_ref[...].astype(F32) * s * (1.0 + gate * (1.0 - s))
        dup = da * gate * s
        o_ref[...] = jnp.where(pl.program_id(2) == 0, dgate, dup).astype(o_ref.dtype)

    lo = pl.BlockSpec((tm, tn), lambda i, j, half: (i, j))
    return _call(
        body, name=name, grid=(t // tm, nj, 2),
        in_specs=[lo, pl.BlockSpec((tm, tn), lambda i, j, half: (i, nj + j)), lo],
        out_specs=[pl.BlockSpec((tm, tn), lambda i, j, half: (i, half * nj + j))],
        out_shape=[jax.ShapeDtypeStruct((t, f2), BF16)],
        scratch_shapes=[], semantics=("parallel", "parallel", "arbitrary"), operands=(gu, gu, dact), job=job)


def _loss_head(h, dn, target, name):
    t, d = h.shape
    tm = _tile(t, 256, 16)

    def body(h_ref, dn_ref, t_ref, loss_ref, dy_ref, dyb_ref):
        err = h_ref[...] + dn_ref[...].astype(F32) - t_ref[...]
        dy_ref[...] = err * (1.0 / d)
        dyb_ref[...] = (err * (1.0 / d)).astype(dyb_ref.dtype)

        @pl.when(pl.program_id(0) == 0)
        def _():
            loss_ref[...] = jnp.zeros_like(loss_ref)

        loss_ref[...] += 0.5 * jnp.sum(jnp.mean(err * err, axis=-1, keepdims=True))

    row = pl.BlockSpec((tm, d), lambda i: (i, 0))
    return pl.pallas_call(
        body, name=name, grid=(t // tm,),
        in_specs=[row, row, row],
        out_specs=[pl.BlockSpec((8, LANES), lambda i: (0, 0)), row, row],
        out_shape=[jax.ShapeDtypeStruct((8, LANES), F32), jax.ShapeDtypeStruct((t, d), F32), jax.ShapeDtypeStruct((t, d), BF16)],
        compiler_params=_cp(("arbitrary",)),
    )(h, dn, target)


def _qk_prep(z, heads, dh, q_norm, k_norm, name):
    t = z.shape[0]
    tq = _tile(t, 512, 16)
    scale = 1.0 / math.sqrt(dh)

    def body(q_ref, k_ref, gq_ref, gk_ref, qn_ref, kn_ref, rq_ref, rk_ref):
        q = q_ref[...].astype(F32)
        k = k_ref[...].astype(F32)
        rq = lax.rsqrt(jnp.mean(q * q, axis=-1, keepdims=True) + RMS_EPS)
        rk = lax.rsqrt(jnp.mean(k * k, axis=-1, keepdims=True) + RMS_EPS)
        qn_ref[...] = (q * rq * gq_ref[...] * scale).astype(qn_ref.dtype)
        kn_ref[...] = (k * rk * gk_ref[...]).astype(kn_ref.dtype)
        rq_ref[...] = rq
        rk_ref[...] = rk

    blk = pl.BlockSpec((tq, dh), lambda i, h: (i, h))
    vec = pl.BlockSpec((1, dh), lambda i, h: (0, 0))
    col = pl.BlockSpec((None, tq, 1), lambda i, h: (h, i, 0))
    return pl.pallas_call(
        body, name=name, grid=(t // tq, heads),
        in_specs=[blk, pl.BlockSpec((tq, dh), lambda i, h: (i, heads + h)), vec, vec],
        out_specs=[blk, blk, col, col],
        out_shape=[jax.ShapeDtypeStruct((t, heads * dh), BF16)] * 2 + [jax.ShapeDtypeStruct((heads, t, 1), F32)] * 2,
        compiler_params=_cp(("parallel", "parallel")),
    )(z, z, q_norm, k_norm)


def _qk_prep_bwd(dqn, dkn, z, heads, dh, q_norm, k_norm, rq, rk, name):
    t = z.shape[0]
    tq = _tile(t, 512, 16)
    scale = 1.0 / math.sqrt(dh)

    def norm_bwd(dy, xv, r, g):
        xhat = xv * r
        gd = dy * g
        return r * (gd - xhat * jnp.mean(gd * xhat, axis=-1, keepdims=True)), jnp.sum(dy * xhat, axis=0, keepdims=True)

    def body(dqn_ref, dkn_ref, q_ref, k_ref, gq_ref, gk_ref, rq_ref, rk_ref, dq_ref, dk_ref, dgq_ref, dgk_ref):
        dq, dgq = norm_bwd(dqn_ref[...].astype(F32) * scale, q_ref[...].astype(F32), rq_ref[...], gq_ref[...])
        dk, dgk = norm_bwd(dkn_ref[...].astype(F32), k_ref[...].astype(F32), rk_ref[...], gk_ref[...])
        dq_ref[...] = dq.astype(dq_ref.dtype)
        dk_ref[...] = dk.astype(dk_ref.dtype)

        @pl.when((pl.program_id(0) == 0) & (pl.program_id(1) == 0))
        def _():
            dgq_ref[...] = jnp.zeros_like(dgq_ref)
            dgk_ref[...] = jnp.zeros_like(dgk_ref)

        dgq_ref[...] += dgq
        dgk_ref[...] += dgk

    blk = pl.BlockSpec((tq, dh), lambda i, h: (i, h))
    vec = pl.BlockSpec((1, dh), lambda i, h: (0, 0))
    col = pl.BlockSpec((None, tq, 1), lambda i, h: (h, i, 0))
    return pl.pallas_call(
        body, name=name, grid=(t // tq, heads),
        in_specs=[blk, blk, blk, pl.BlockSpec((tq, dh), lambda i, h: (i, heads + h)), vec, vec, col, col],
        out_specs=[blk, blk, vec, vec],
        out_shape=[jax.ShapeDtypeStruct((t, heads * dh), BF16)] * 2 + [jax.ShapeDtypeStruct((1, dh), F32)] * 2,
        compiler_params=_cp(("arbitrary", "arbitrary")),
    )(dqn, dkn, z, z, q_norm, k_norm, rq, rk)


def _tri_ones(n, upper):
    row = lax.broadcasted_iota(jnp.int32, (n, n), 0)
    col = lax.broadcasted_iota(jnp.int32, (n, n), 1)
    return jnp.where((col >= row) if upper else (col <= row), 1.0, 0.0).astype(F32)


def _forget_fwd(f, b, name):
    t, w = f.shape
    blk = _tile(t, 256, 8)
    nb = t // blk

    def body(f_ref, b_ref, out_ref):
        tri = _tri_ones(blk, upper=False)

        def step(i, carry):
            rows = pl.ds(pl.multiple_of(i * blk, blk), blk)
            logf = jax.nn.log_sigmoid(f_ref[rows, :] + b_ref[...])
            acc = jnp.dot(tri, logf, precision=lax.Precision.HIGHEST, preferred_element_type=F32) + carry
            out_ref[rows, :] = acc
            return acc[blk - 1:blk, :]

        lax.fori_loop(0, nb, step, jnp.zeros((1, w), F32))

    return pl.pallas_call(
        body, name=name,
        in_specs=[pl.BlockSpec(memory_space=pltpu.VMEM)] * 2,
        out_specs=pl.BlockSpec(memory_space=pltpu.VMEM),
        out_shape=jax.ShapeDtypeStruct((t, w), F32),
        compiler_params=_cp(),
    )(f, b)


def _forget_bwd(d_key, f, b, name):
    t, w = f.shape
    blk = _tile(t, 256, 8)
    nb = t // blk

    def body(dk_ref, f_ref, b_ref, df_ref, db_ref):
        tri = _tri_ones(blk, upper=True)

        def step(i, carry):
            suffix, db = carry
            rows = pl.ds(pl.multiple_of((nb - 1 - i) * blk, blk), blk)
            dlog = suffix - jnp.dot(tri, dk_ref[rows, :], precision=lax.Precision.HIGHEST, preferred_element_type=F32)
            df = dlog * _sigmoid(-(f_ref[rows, :] + b_ref[...]))
            df_ref[rows, :] = df
            return dlog[0:1, :], db + jnp.sum(df, axis=0, keepdims=True)

        _, db = lax.fori_loop(0, nb, step, (jnp.zeros((1, w), F32), jnp.zeros((1, w), F32)))
        db_ref[...] = db

    return pl.pallas_call(
        body, name=name,
        in_specs=[pl.BlockSpec(memory_space=pltpu.VMEM)] * 3,
        out_specs=[pl.BlockSpec(memory_space=pltpu.VMEM)] * 2,
        out_shape=[jax.ShapeDtypeStruct((t, w), F32), jax.ShapeDtypeStruct((1, w), F32)],
        compiler_params=_cp(),
    )(d_key, f, b)


def _attn_logits(q_ref, k_ref, fc_ref, fr_ref, diagonal):
    s = lax.dot_general(q_ref[...], k_ref[...], (((1,), (1,)), ((), ())), preferred_element_type=F32)
    s = s - (fr_ref[...] - fc_ref[0:1, :])
    if diagonal:
        row = lax.broadcasted_iota(jnp.int32, s.shape, 0)
        col = lax.broadcasted_iota(jnp.int32, s.shape, 1)
        s = jnp.where(col <= row, s, MASK_VALUE)
    return s


def _on_causal_blocks(q_block, k_block, step):
    @pl.when(k_block < q_block)
    def _():
        step(False)

    @pl.when(k_block == q_block)
    def _():
        step(True)


def _attn_fwd(qn, kn, v_src, v_col, fcol, frow, heads, dh, name, job=None):
    t = qn.shape[0]
    tq = tk = _tile(t, 512)
    nq = t // tq

    def body(q_ref, k_ref, v_ref, fc_ref, fr_ref, o_ref, lse_ref, m_sc, l_sc, acc_sc):
        qi, ki = pl.program_id(1), pl.program_id(2)

        @pl.when(ki == 0)
        def _():
            m_sc[...] = jnp.full_like(m_sc, MASK_VALUE)
            l_sc[...] = jnp.zeros_like(l_sc)
            acc_sc[...] = jnp.zeros_like(acc_sc)

        def step(diagonal):
            s = _attn_logits(q_ref, k_ref, fc_ref, fr_ref, diagonal)
            m_new = jnp.maximum(m_sc[...], jnp.max(s, axis=-1, keepdims=True))
            alpha = jnp.exp(m_sc[...] - m_new)
            p = jnp.exp(s - m_new)
            l_sc[...] = alpha * l_sc[...] + jnp.sum(p, axis=-1, keepdims=True)
            acc_sc[...] = alpha * acc_sc[...] + jnp.dot(p.astype(BF16), v_ref[...].astype(BF16), preferred_element_type=F32)
            m_sc[...] = m_new

        _on_causal_blocks(qi, ki, step)

        @pl.when(ki == nq - 1)
        def _():
            o_ref[...] = (acc_sc[...] / l_sc[...]).astype(o_ref.dtype)
            lse_ref[...] = m_sc[...] + jnp.log(l_sc[...])

    qblk = pl.BlockSpec((tq, dh), lambda h, i, j: (i, h))
    qcol = pl.BlockSpec((None, tq, 1), lambda h, i, j: (h, i, 0))
    return _call(
        body, name=name, grid=(heads, nq, nq),
        in_specs=[qblk,
                  pl.BlockSpec((tk, dh), lambda h, i, j: (jnp.minimum(j, i), h)),
                  pl.BlockSpec((tk, dh), lambda h, i, j: (jnp.minimum(j, i), v_col + h)),
                  qcol,
                  pl.BlockSpec((None, 1, tk), lambda h, i, j: (h, 0, jnp.minimum(j, i)))],
        out_specs=[qblk, qcol],
        out_shape=[jax.ShapeDtypeStruct((t, heads * dh), BF16), jax.ShapeDtypeStruct((heads, t, 1), F32)],
        scratch_shapes=[pltpu.VMEM((tq, 1), F32), pltpu.VMEM((tq, 1), F32), pltpu.VMEM((tq, dh), F32)],
        semantics=("parallel", "parallel", "arbitrary"), operands=(qn, kn, v_src, fcol, frow), job=job)


def _attn_delta(o, do, heads, dh, name):
    t = o.shape[0]
    tq = _tile(t, 512, 16)

    def body(o_ref, do_ref, out_ref):
        out_ref[...] = jnp.sum(o_ref[...].astype(F32) * do_ref[...].astype(F32), axis=-1, keepdims=True)

    blk = pl.BlockSpec((tq, dh), lambda i, h: (i, h))
    return pl.pallas_call(
        body, name=name, grid=(t // tq, heads),
        in_specs=[blk, blk],
        out_specs=pl.BlockSpec((None, tq, 1), lambda i, h: (h, i, 0)),
        out_shape=jax.ShapeDtypeStruct((heads, t, 1), F32),
        compiler_params=_cp(("parallel", "parallel")),
    )(o, do)


def _attn_bwd_q(qn, kn, v_src, v_col, do, fcol, frow, lse, delta, heads, dh, name, job=None):
    t = qn.shape[0]
    tq = tk = _tile(t, 512)
    nq = t // tq

    def body(q_ref, k_ref, v_ref, do_ref, fc_ref, fr_ref, lse_ref, dl_ref, dq_ref, dq_sc):
        qi, ki = pl.program_id(1), pl.program_id(2)

        @pl.when(ki == 0)
        def _():
            dq_sc[...] = jnp.zeros_like(dq_sc)

        def step(diagonal):
            s = _attn_logits(q_ref, k_ref, fc_ref, fr_ref, diagonal)
            p = jnp.exp(s - lse_ref[...])
            dp = lax.dot_general(do_ref[...].astype(BF16), v_ref[...].astype(BF16), (((1,), (1,)), ((), ())),
                                 preferred_element_type=F32)
            ds = p * (dp - dl_ref[...])
            dq_sc[...] += jnp.dot(ds.astype(BF16), k_ref[...], preferred_element_type=F32)

        _on_causal_blocks(qi, ki, step)

        @pl.when(ki == nq - 1)
        def _():
            dq_ref[...] = dq_sc[...]

    qblk = pl.BlockSpec((tq, dh), lambda h, i, j: (i, h))
    qcol = pl.BlockSpec((None, tq, 1), lambda h, i, j: (h, i, 0))
    return _call(
        body, name=name, grid=(heads, nq, nq),
        in_specs=[qblk,
                  pl.BlockSpec((tk, dh), lambda h, i, j: (jnp.minimum(j, i), h)),
                  pl.BlockSpec((tk, dh), lambda h, i, j: (jnp.minimum(j, i), v_col + h)),
                  qblk, qcol,
                  pl.BlockSpec((None, 1, tk), lambda h, i, j: (h, 0, jnp.minimum(j, i))),
                  qcol, qcol],
        out_specs=[qblk],
        out_shape=[jax.ShapeDtypeStruct((t, heads * dh), F32)],
        scratch_shapes=[pltpu.VMEM((tq, dh), F32)],
        semantics=("parallel", "parallel", "arbitrary"),
        operands=(qn, kn, v_src, do, fcol, frow, lse, delta), job=job)


def _attn_bwd_kv(qn, kn, v_src, v_col, do, fcol, frow, lse, delta, heads, dh, name, job=None):
    t = qn.shape[0]
    tq = tk = _tile(t, 512)
    nq = t // tq

    def body(q_ref, k_ref, v_ref, do_ref, fc_ref, fr_ref, lse_ref, dl_ref, dk_ref, dv_ref, dfk_ref, dk_sc, dv_sc, dfk_sc):
        ki, qi = pl.program_id(1), pl.program_id(2)

        @pl.when(qi == 0)
        def _():
            dk_sc[...] = jnp.zeros_like(dk_sc)
            dv_sc[...] = jnp.zeros_like(dv_sc)
            dfk_sc[...] = jnp.zeros_like(dfk_sc)

        def step(diagonal):
            s = _attn_logits(q_ref, k_ref, fc_ref, fr_ref, diagonal)
            p = jnp.exp(s - lse_ref[...])
            dob = do_ref[...].astype(BF16)
            dp = lax.dot_general(dob, v_ref[...].astype(BF16), (((1,), (1,)), ((), ())), preferred_element_type=F32)
            ds = p * (dp - dl_ref[...])
            dv_sc[...] += lax.dot_general(p.astype(BF16), dob, (((0,), (0,)), ((), ())), preferred_element_type=F32)
            dk_sc[...] += lax.dot_general(ds.astype(BF16), q_ref[...], (((0,), (0,)), ((), ())), preferred_element_type=F32)
            dfk_sc[...] += jnp.sum(ds, axis=0, keepdims=True)

        _on_causal_blocks(qi, ki, step)

        @pl.when(qi == nq - 1)
        def _():
            dk_ref[...] = dk_sc[...]
            dv_ref[...] = dv_sc[...].astype(dv_ref.dtype)
            dfk_ref[...] = dfk_sc[...]

    qblk = pl.BlockSpec((tq, dh), lambda h, j, i: (jnp.maximum(i, j), h))
    qcol = pl.BlockSpec((None, tq, 1), lambda h, j, i: (h, jnp.maximum(i, j), 0))
    kblk = pl.BlockSpec((tk, dh), lambda h, j, i: (j, h))
    krow = pl.BlockSpec((None, 1, tk), lambda h, j, i: (h, 0, j))
    return _call(
        body, name=name, grid=(heads, nq, nq),
        in_specs=[qblk, kblk, pl.BlockSpec((tk, dh), lambda h, j, i: (j, v_col + h)), qblk, qcol, krow, qcol, qcol],
        out_specs=[kblk, kblk, krow],
        out_shape=[jax.ShapeDtypeStruct((t, heads * dh), F32), jax.ShapeDtypeStruct((t, heads * dh), BF16),
                   jax.ShapeDtypeStruct((heads, 1, t), F32)],
        scratch_shapes=[pltpu.VMEM((tk, dh), F32), pltpu.VMEM((tk, dh), F32), pltpu.VMEM((1, tk), F32)],
        semantics=("parallel", "parallel", "arbitrary"),
        operands=(qn, kn, v_src, do, fcol, frow, lse, delta), job=job)


TIME_TILE = 8


def _s5_discretize(lam_re, lam_im, log_step, b_re, b_im):
    dt = jnp.exp(log_step)
    mag = jnp.exp(lam_re * dt)
    lb_re = mag * jnp.cos(lam_im * dt)
    lb_im = mag * jnp.sin(lam_im * dt)
    denom = lam_re * lam_re + lam_im * lam_im
    num_re = lb_re - 1.0
    fac_re = (num_re * lam_re + lb_im * lam_im) / denom
    fac_im = (lb_im * lam_re - num_re * lam_im) / denom
    return lb_re, lb_im, fac_re * b_re - fac_im * b_im, fac_re * b_im + fac_im * b_re


def _s5_prep(lam_re, lam_im, log_step, b_re, b_im, name):
    gp, width = b_re.shape

    def body(lr, li, ls, br, bi, o_lr, o_li, o_br, o_bi):
        res = _s5_discretize(lr[...], li[...], ls[...], br[...], bi[...])
        for ref, val in zip((o_lr, o_li, o_br, o_bi), res):
            ref[...] = val

    vm = pl.BlockSpec(memory_space=pltpu.VMEM)
    return pl.pallas_call(
        body, name=name, in_specs=[vm] * 5, out_specs=[vm] * 4,
        out_shape=[jax.ShapeDtypeStruct((gp, 1), F32)] * 2 + [jax.ShapeDtypeStruct((gp, width), F32)] * 2,
        compiler_params=_cp(),
    )(lam_re, lam_im, log_step, b_re, b_im)


def _s5_prep_bwd(lam_re, lam_im, log_step, b_re, b_im, d_lb_re, d_lb_im, d_bb_re, d_bb_im, groups, name):
    gp, width = b_re.shape
    states = gp // groups
    tr = _tile(gp, 512, 8)

    def body(lr, li, ls, br, bi, g_lr, g_li, g_br, g_bi, o_lr, o_li, o_ls, o_br, o_bi):
        _, vjp = jax.vjp(_s5_discretize, lr[...], li[...], ls[...], br[...], bi[...])
        d_lr, d_li, d_ls, d_br, d_bi = vjp((g_lr[...], g_li[...], g_br[...], g_bi[...]))
        o_lr[...] = d_lr
        o_li[...] = d_li
        o_br[...] = d_br
        o_bi[...] = d_bi
        row_group = (pl.program_id(0) * tr + lax.broadcasted_iota(jnp.int32, (tr, groups), 0)) // states
        col_group = lax.broadcasted_iota(jnp.int32, (tr, groups), 1)

        @pl.when(pl.program_id(0) == 0)
        def _():
            o_ls[...] = jnp.zeros_like(o_ls)

        o_ls[...] += jnp.sum(jnp.where(row_group == col_group, d_ls, 0.0), axis=0, keepdims=True)

    col = pl.BlockSpec((tr, 1), lambda i: (i, 0))
    mat = pl.BlockSpec((tr, width), lambda i: (i, 0))
    return pl.pallas_call(
        body, name=name, grid=(gp // tr,),
        in_specs=[col, col, col, mat, mat, col, col, mat, mat],
        out_specs=[col, col, pl.BlockSpec((1, groups), lambda i: (0, 0)), mat, mat],
        out_shape=[jax.ShapeDtypeStruct((gp, 1), F32)] * 2 + [jax.ShapeDtypeStruct((1, groups), F32)]
        + [jax.ShapeDtypeStruct((gp, width), F32)] * 2,
        compiler_params=_cp(("arbitrary",)),
    )(lam_re, lam_im, log_step, b_re, b_im, d_lb_re, d_lb_im, d_bb_re, d_bb_im)


def _shift_time(v, s, reverse):
    row = lax.broadcasted_iota(jnp.int32, v.shape, 0)
    if reverse:
        return jnp.where(row < TIME_TILE - s, pltpu.roll(v, TIME_TILE - s, 0), 0.0)
    return jnp.where(row >= s, pltpu.roll(v, s, 0), 0.0)


def _cmul(ar, ai, br, bi):
    return ar * br - ai * bi, ar * bi + ai * br


def _scan_time(xr_ref, xi_ref, ar, ai, reverse):
    t = xr_ref.shape[0]
    n_tiles = t // TIME_TILE
    powers = [(ar, ai)]
    for _ in range(TIME_TILE - 1):
        powers.append(_cmul(*powers[-1], ar, ai))
    order = powers[::-1] if reverse else powers
    carry_r = jnp.concatenate([p[0] for p in order], axis=0)
    carry_i = jnp.concatenate([p[1] for p in order], axis=0)
    levels = [(1, powers[0]), (2, powers[1]), (4, powers[3])]
    last = 0 if reverse else TIME_TILE - 1

    def tile(i, carry):
        cr, ci = carry
        idx = (n_tiles - 1 - i) if reverse else i
        rows = pl.ds(pl.multiple_of(idx * TIME_TILE, TIME_TILE), TIME_TILE)
        br, bi = xr_ref[rows, :], xi_ref[rows, :]
        for s, (pr, pi) in levels:
            sr, si = _cmul(pr, pi, _shift_time(br, s, reverse), _shift_time(bi, s, reverse))
            br, bi = br + sr, bi + si
        kr, ki = _cmul(carry_r, carry_i, cr, ci)
        br, bi = br + kr, bi + ki
        xr_ref[rows, :] = br
        xi_ref[rows, :] = bi
        return br[last:last + 1, :], bi[last:last + 1, :]

    zero = jnp.zeros_like(ar)
    lax.fori_loop(0, n_tiles, tile, (zero, zero))


def _s5_states(u_ref, bbr_ref, bbi_ref, ar_ref, ai_ref, xr, xi, chunk):
    t = u_ref.shape[0]
    for r0 in range(0, t, chunk):
        rows = pl.ds(r0, chunk)
        xr[rows, :] = jnp.dot(u_ref[rows, :], bbr_ref[...], preferred_element_type=F32)
        xi[rows, :] = jnp.dot(u_ref[rows, :], bbi_ref[...], preferred_element_type=F32)
    _scan_time(xr, xi, ar_ref[...], ai_ref[...], reverse=False)


def _s5_specs(t, nb_lanes, state_lanes):
    tok = pl.BlockSpec((t, nb_lanes), lambda j: (0, j))
    bb = pl.BlockSpec((None, nb_lanes, state_lanes), lambda j: (j, 0, 0))
    cc = pl.BlockSpec((None, state_lanes, nb_lanes), lambda j: (j, 0, 0))
    dvec = pl.BlockSpec((1, nb_lanes), lambda j: (0, j))
    avec = pl.BlockSpec((1, state_lanes), lambda j: (0, j))
    return tok, bb, cc, dvec, avec


def _s5_fwd(u5, bbr, bbi, ccr, cci, dskip, ar, ai, name):
    t, w = u5.shape
    nb, nb_lanes, state_lanes = bbr.shape
    chunk = _tile(t, 512, 16)

    def body(u_ref, bbr_ref, bbi_ref, cr_ref, ci_ref, d_ref, ar_ref, ai_ref, y_ref, xr, xi):
        _s5_states(u_ref, bbr_ref, bbi_ref, ar_ref, ai_ref, xr, xi, chunk)
        for r0 in range(0, t, chunk):
            rows = pl.ds(r0, chunk)
            y = jnp.dot(xr[rows, :].astype(BF16), cr_ref[...], preferred_element_type=F32)
            y = y - jnp.dot(xi[rows, :].astype(BF16), ci_ref[...], preferred_element_type=F32)
            y_ref[rows, :] = y + d_ref[...] * u_ref[rows, :].astype(F32)

    tok, bb, cc, dvec, avec = _s5_specs(t, nb_lanes, state_lanes)
    return pl.pallas_call(
        body, name=name, grid=(nb,),
        in_specs=[tok, bb, bb, cc, cc, dvec, avec, avec],
        out_specs=tok,
        out_shape=jax.ShapeDtypeStruct((t, w), F32),
        scratch_shapes=[pltpu.VMEM((t, state_lanes), F32)] * 2,
        compiler_params=_cp(("parallel",)),
    )(u5, bbr, bbi, ccr, cci, dskip, ar, ai)


def _s5_bwd(u5, dy, bbr, bbi, ccr, cci, dskip, ar, ai, name, job=None):
    t, w = u5.shape
    nb, nb_lanes, state_lanes = bbr.shape
    chunk = _tile(t, 512, 16)
    nt_dims = (((1,), (1,)), ((), ()))
    tn_dims = (((0,), (0,)), ((), ()))

    def body(u_ref, dy_ref, bbr_ref, bbi_ref, cr_ref, ci_ref, d_ref, ar_ref, ai_ref,
             du_ref, dbbr_ref, dbbi_ref, dcr_ref, dci_ref, dar_ref, dai_ref, dd_ref, xr, xi, gr, gi):
        _s5_states(u_ref, bbr_ref, bbi_ref, ar_ref, ai_ref, xr, xi, chunk)
        for r0 in range(0, t, chunk):
            rows = pl.ds(r0, chunk)
            dyb = dy_ref[rows, :].astype(BF16)
            gr[rows, :] = lax.dot_general(dyb, cr_ref[...], nt_dims, preferred_element_type=F32)
            gi[rows, :] = -lax.dot_general(dyb, ci_ref[...], nt_dims, preferred_element_type=F32)
        _scan_time(gr, gi, ar_ref[...], -ai_ref[...], reverse=True)

        dcr = jnp.zeros((state_lanes, nb_lanes), F32)
        dci = jnp.zeros((state_lanes, nb_lanes), F32)
        dbr = jnp.zeros((nb_lanes, state_lanes), F32)
        dbi = jnp.zeros((nb_lanes, state_lanes), F32)
        dd = jnp.zeros((1, nb_lanes), F32)
        for r0 in range(0, t, chunk):
            rows = pl.ds(r0, chunk)
            u = u_ref[rows, :]
            dyv = dy_ref[rows, :]
            dyb = dyv.astype(BF16)
            lr, li = gr[rows, :].astype(BF16), gi[rows, :].astype(BF16)
            dcr = dcr + lax.dot_general(xr[rows, :].astype(BF16), dyb, tn_dims, preferred_element_type=F32)
            dci = dci - lax.dot_general(xi[rows, :].astype(BF16), dyb, tn_dims, preferred_element_type=F32)
            dbr = dbr + lax.dot_general(u, lr, tn_dims, preferred_element_type=F32)
            dbi = dbi + lax.dot_general(u, li, tn_dims, preferred_element_type=F32)
            du = lax.dot_general(lr, bbr_ref[...], nt_dims, preferred_element_type=F32)
            du = du + lax.dot_general(li, bbi_ref[...], nt_dims, preferred_element_type=F32)
            du_ref[rows, :] = du + d_ref[...] * dyv
            dd = dd + jnp.sum(dyv * u.astype(F32), axis=0, keepdims=True)
        dcr_ref[...] = dcr
        dci_ref[...] = dci
        dbbr_ref[...] = dbr
        dbbi_ref[...] = dbi
        dd_ref[...] = dd

        first_row = lax.broadcasted_iota(jnp.int32, (TIME_TILE, state_lanes), 0) == 0

        def tile(i, carry):
            pr, pi, acc_r, acc_i = carry
            rows = pl.ds(pl.multiple_of(i * TIME_TILE, TIME_TILE), TIME_TILE)
            x_r, x_i, l_r, l_i = xr[rows, :], xi[rows, :], gr[rows, :], gi[rows, :]
            prev_r = jnp.where(first_row, pr, pltpu.roll(x_r, 1, 0))
            prev_i = jnp.where(first_row, pi, pltpu.roll(x_i, 1, 0))
            acc_r = acc_r + l_r * prev_r + l_i * prev_i
            acc_i = acc_i + l_i * prev_r - l_r * prev_i
            return x_r[TIME_TILE - 1:, :], x_i[TIME_TILE - 1:, :], acc_r, acc_i

        zrow = jnp.zeros((1, state_lanes), F32)
        ztile = jnp.zeros((TIME_TILE, state_lanes), F32)
        _, _, acc_r, acc_i = lax.fori_loop(0, t // TIME_TILE, tile, (zrow, zrow, ztile, ztile))
        dar_ref[...] = jnp.sum(acc_r, axis=0, keepdims=True)
        dai_ref[...] = jnp.sum(acc_i, axis=0, keepdims=True)

    tok, bb, cc, dvec, avec = _s5_specs(t, nb_lanes, state_lanes)
    return _call(
        body, name=name, grid=(nb,),
        in_specs=[tok, tok, bb, bb, cc, cc, dvec, avec, avec],
        out_specs=[tok, bb, bb, cc, cc, avec, avec, dvec],
        out_shape=[jax.ShapeDtypeStruct((t, w), F32)]
        + [jax.ShapeDtypeStruct((nb, nb_lanes, state_lanes), F32)] * 2
        + [jax.ShapeDtypeStruct((nb, state_lanes, nb_lanes), F32)] * 2
        + [jax.ShapeDtypeStruct((1, nb * state_lanes), F32)] * 2
        + [jax.ShapeDtypeStruct((1, w), F32)],
        scratch_shapes=[pltpu.VMEM((t, state_lanes), F32)] * 4,
        semantics=("parallel",), operands=(u5, dy, bbr, bbi, ccr, cci, dskip, ar, ai), job=job)


def _gelu(x):
    return 0.5 * x * (1.0 + jnp.tanh(GELU_C * (x + GELU_A * x * x * x)))


def _gelu_grad(x):
    th = jnp.tanh(GELU_C * (x + GELU_A * x * x * x))
    return 0.5 * (1.0 + th) + 0.5 * x * (1.0 - th * th) * GELU_C * (1.0 + 3.0 * GELU_A * x * x)


def _glu_fwd(y5, w, b, name):
    t, width = y5.shape
    tm = _tile(t, 512, 16)

    def body(y_ref, w_ref, b_ref, o_ref):
        g = _gelu(y_ref[...])
        a = jnp.dot(g.astype(BF16), w_ref[...], preferred_element_type=F32) + b_ref[...]
        o_ref[...] = (g * _sigmoid(a)).astype(o_ref.dtype)

    row = pl.BlockSpec((tm, width), lambda i: (i, 0))
    return pl.pallas_call(
        body, name=name, grid=(t // tm,),
        in_specs=[row, pl.BlockSpec((width, width), lambda i: (0, 0)), pl.BlockSpec((1, width), lambda i: (0, 0))],
        out_specs=row,
        out_shape=jax.ShapeDtypeStruct((t, width), BF16),
        compiler_params=_cp(("parallel",)),
    )(y5, w, b)


def _glu_bwd(y5, dout, w, b, name):
    t, width = y5.shape
    tm = _tile(t, 512, 16)

    def body(y_ref, do_ref, w_ref, b_ref, dy_ref, g_ref, da_ref, db_ref):
        y = y_ref[...]
        g = _gelu(y)
        s = _sigmoid(jnp.dot(g.astype(BF16), w_ref[...], preferred_element_type=F32) + b_ref[...])
        dout_v = do_ref[...].astype(F32)
        da = dout_v * g * s * (1.0 - s)
        dg = dout_v * s + lax.dot_general(da.astype(BF16), w_ref[...], (((1,), (1,)), ((), ())),
                                          preferred_element_type=F32)
        dy_ref[...] = dg * _gelu_grad(y)
        g_ref[...] = g.astype(g_ref.dtype)
        da_ref[...] = da.astype(da_ref.dtype)

        @pl.when(pl.program_id(0) == 0)
        def _():
            db_ref[...] = jnp.zeros_like(db_ref)

        db_ref[...] += jnp.sum(da, axis=0, keepdims=True)

    row = pl.BlockSpec((tm, width), lambda i: (i, 0))
    vec = pl.BlockSpec((1, width), lambda i: (0, 0))
    return pl.pallas_call(
        body, name=name, grid=(t // tm,),
        in_specs=[row, row, pl.BlockSpec((width, width), lambda i: (0, 0)), vec],
        out_specs=[row, row, row, vec],
        out_shape=[jax.ShapeDtypeStruct((t, width), F32), jax.ShapeDtypeStruct((t, width), BF16),
                   jax.ShapeDtypeStruct((t, width), BF16), jax.ShapeDtypeStruct((1, width), F32)],
        compiler_params=_cp(("arbitrary",)),
    )(y5, dout, w, b)


def _adamw(w, g, m, v):
    m = ADAM_B1 * m + (1.0 - ADAM_B1) * g
    v = ADAM_B2 * v + (1.0 - ADAM_B2) * (g * g)
    m_hat = m / (1.0 - ADAM_B1 ** ADAM_STEP)
    v_hat = v / (1.0 - ADAM_B2 ** ADAM_STEP)
    return -ADAM_LR * (m_hat / (jnp.sqrt(v_hat) + ADAM_EPS) + ADAM_WD * w), m, v


def _adamw_shard(w, m, v, sums, got, chip, name):
    rows, cols = w.shape
    wide = sums.shape[2]
    tr = _row_tile(rows, wide, target=2**20)

    def body(chip_ref, w_ref, m_ref, v_ref, s_ref, g0_ref, g1_ref, g2_ref, g_out, d_out, m_out, v_out):
        g = s_ref[...].astype(F32) + g0_ref[...].astype(F32) + g1_ref[...].astype(F32) + g2_ref[...].astype(F32)
        g = g[:, :cols]
        delta, m_new, v_new = _adamw(w_ref[...], g, m_ref[...], v_ref[...])
        g_out[...] = g
        d_out[...] = delta
        m_out[...] = m_new
        v_out[...] = v_new

    blk = pl.BlockSpec((tr, cols), lambda i, chip_ref: (i, 0))

    def part(k):
        return pl.BlockSpec((None, tr, wide), lambda i, chip_ref: (k, i, 0))

    return pl.pallas_call(
        body, name=name,
        grid_spec=pltpu.PrefetchScalarGridSpec(
            num_scalar_prefetch=1, grid=(rows // tr,),
            in_specs=[blk, blk, blk, pl.BlockSpec((None, tr, wide), lambda i, chip_ref: (chip_ref[0], i, 0)),
                      part(0), part(1), part(2)],
            out_specs=[blk] * 4),
        out_shape=[jax.ShapeDtypeStruct((rows, cols), F32)] * 4,
        compiler_params=_cp(("parallel",)),
    )(chip, w, m, v, sums, got, got, got)


def _adamw_packed(w, m, v, g, name):
    def body(w_ref, m_ref, v_ref, g_ref, d_out, m_out, v_out):
        delta, m_new, v_new = _adamw(w_ref[...], g_ref[...], m_ref[...], v_ref[...])
        d_out[...] = delta
        m_out[...] = m_new
        v_out[...] = v_new

    vm = pl.BlockSpec(memory_space=pltpu.VMEM)
    return pl.pallas_call(
        body, name=name, in_specs=[vm] * 4, out_specs=[vm] * 3,
        out_shape=[jax.ShapeDtypeStruct(w.shape, F32)] * 3,
        compiler_params=_cp(),
    )(w, m, v, g)


WEIGHTS = ("g_mix", "w_in", "b_fgate", "b_gates", "q_norm", "k_norm", "s5_lambda_re", "s5_lambda_im", "s5_log_step",
           "s5_b_re", "s5_b_im", "s5_c_re", "s5_c_im", "s5_d", "w_glu", "b_glu", "w_proj_fox", "w_proj_s5", "w_out",
           "g_ffn", "w_gate_up", "w_down")
COLUMN_SHARDED = ("w_in", "w_proj_fox", "w_proj_s5", "w_gate_up")
ROW_SHARDED = ("w_glu", "w_out", "w_down")
PACK_ROWS = 8 * LANES

def _pack(arrays):
    flat = jnp.concatenate([a.reshape(-1).astype(F32) for a in arrays])
    flat = jnp.pad(flat, (0, (-flat.shape[0]) % PACK_ROWS))
    return flat.reshape(-1, LANES)


def _unpack(packed, like):
    flat, out, at = packed.reshape(-1), [], 0
    for a in like:
        out.append(flat[at:at + a.size].reshape(a.shape))
        at += a.size
    return out


def _pad_lanes(a):
    return jnp.pad(a, ((0, 0), (0, LANES - a.shape[1])))


def kernel(x, g_mix, w_in, b_fgate, b_gates, q_norm, k_norm, s5_lambda_re, s5_lambda_im, s5_log_step, s5_b_re, s5_b_im,
           s5_c_re, s5_c_im, s5_d, w_glu, b_glu, w_proj_fox, w_proj_s5, w_out, g_ffn, w_gate_up, w_down,
           loss_target, m_g_mix, m_w_in, m_b_fgate, m_b_gates, m_q_norm, m_k_norm, m_s5_lambda_re,
           m_s5_lambda_im, m_s5_log_step, m_s5_b_re, m_s5_b_im, m_s5_c_re, m_s5_c_im, m_s5_d, m_w_glu,
           m_b_glu, m_w_proj_fox, m_w_proj_s5, m_w_out, m_g_ffn, m_w_gate_up, m_w_down, v_g_mix, v_w_in,
           v_b_fgate, v_b_gates, v_q_norm, v_k_norm, v_s5_lambda_re, v_s5_lambda_im, v_s5_log_step, v_s5_b_re,
           v_s5_b_im, v_s5_c_re, v_s5_c_im, v_s5_d, v_w_glu, v_b_glu, v_w_proj_fox, v_w_proj_s5, v_w_out,
           v_g_ffn, v_w_gate_up, v_w_down):
    given = dict(locals())
    weights = {n: given[n] for n in WEIGHTS}
    mom_m = {n: given["m_" + n] for n in WEIGHTS}
    mom_v = {n: given["v_" + n] for n in WEIGHTS}

    pos_x, pos_y, pos_c = _position()
    core = jnp.reshape(pos_c, (1,)).astype(jnp.int32)
    chip = jnp.reshape(2 * pos_x + pos_y, (1,)).astype(jnp.int32)

    xs, target = x[0], loss_target[0]
    t, d = xs.shape
    heads, dh = b_fgate.shape[-1], q_norm.shape[-1]
    fw = heads * dh
    groups, states, gwidth = s5_b_re.shape[1:]
    sw = groups * gwidth
    gp = groups * states
    assert dh == LANES and sw % LANES == 0 and LANES % gwidth == 0
    col_v, col_f, col_s5 = 3 * fw, 3 * fw + heads, 3 * fw + heads + sw

    shard = {n: weights[n][0].astype(BF16) for n in COLUMN_SHARDED + ROW_SHARDED}

    def whole(n, ag):
        if n in COLUMN_SHARDED:
            return ag.transpose(1, 0, 2).reshape(ag.shape[1], N_DEV * ag.shape[2])
        return ag.reshape(N_DEV * ag.shape[1], ag.shape[2])

    full = {}
    c_gu, r_dn = w_gate_up.shape[2], w_down.shape[1]
    assert c_gu == 2 * r_dn
    cp_gu = -(-c_gu // LANES) * LANES
    shard["w_gate_up"] = jnp.pad(shard["w_gate_up"], ((0, 0), (0, cp_gu - c_gu)))

    def down_rows(ag):
        gap = [jnp.zeros((cp_gu - c_gu, d), ag.dtype)] if cp_gu > c_gu else []
        return jnp.concatenate([p for b in range(N_DEV // 2) for p in [ag[2 * b], ag[2 * b + 1]] + gap], axis=0)

    c_in = w_in.shape[2]
    in_cols = N_DEV * c_in

    def in_pieces(lo, hi, take):
        cuts = [(k, max(lo, k * c_in), min(hi, (k + 1) * c_in)) for k in range(N_DEV)]
        return [take(k, a - k * c_in, b - k * c_in) for k, a, b in cuts if a < b]

    ag_in = _all_gather(shard["w_in"], "ag_w_in")

    def from_gathered(k, a, b):
        return ag_in[k][:, a:b]

    w_main = jnp.concatenate(in_pieces(0, col_v, from_gathered) + in_pieces(col_f, in_cols, from_gathered), axis=1)
    w_forget = _pad_lanes(jnp.concatenate(in_pieces(col_v, col_f, from_gathered), axis=1))
    z_s5, z_gate = 3 * fw, 3 * fw + sw

    u, r_mix = _rms_fwd(xs, g_mix, "rms_mix")
    early = ("w_proj_fox", "w_proj_s5", "w_glu", "w_out")
    half = d // 2
    z, gate_up_rows = _mm(u, w_main, "nn", BF16, "mm_z", job=_GatherJob([shard["w_gate_up"]], rows=(0, half)))
    zf = _mm(u, w_forget, "nn", F32, "mm_zf")
    qn, kn, r_q, r_k = _qk_prep(z, heads, dh, q_norm, k_norm, "qk_prep")
    b_forget = _pad_lanes(b_fgate)
    cum = _forget_fwd(zf, b_forget, "forget_fwd")
    cum_t = cum[:, :heads].T
    fcol, frow = cum_t[:, :, None], cum_t[:, None, :]
    (attn, lse), got = _attn_fwd(
        qn, kn, z, 2 * heads, fcol, frow, heads, dh, "attn_fwd",
        job=_JobGroup([_GatherJob([shard["w_gate_up"]], rows=(half, d - half), into=gate_up_rows),
                       _GatherJob([shard[n] for n in early])]))
    full["w_gate_up"] = got[0]
    for n, ag in zip(early, got[1:]):
        full[n] = whole(n, ag)

    lam_re, lam_im = s5_lambda_re.reshape(gp, 1), s5_lambda_im.reshape(gp, 1)
    log_step = jnp.repeat(s5_log_step.reshape(groups, 1), states, axis=1).reshape(gp, 1)
    b_re, b_im = s5_b_re.reshape(gp, gwidth), s5_b_im.reshape(gp, gwidth)
    lb_re, lb_im, bb_re, bb_im = _s5_prep(lam_re, lam_im, log_step, b_re, b_im, "s5_prep")
    nb, per = sw // LANES, LANES // gwidth
    eye = jnp.eye(per, dtype=F32)

    def diag_b(bb):
        return jnp.einsum("napi,ab->naibp", bb.reshape(nb, per, states, gwidth), eye).reshape(nb, LANES, per * states)

    def diag_c(c):
        return jnp.einsum("naip,ab->nbpai", c.reshape(nb, per, gwidth, states), eye).reshape(nb, per * states, LANES)

    def undiag_b(g):
        return jnp.einsum("naibp,ab->napi", g.reshape(nb, per, gwidth, per, states), eye).reshape(gp, gwidth)

    def undiag_c(g):
        return jnp.einsum("nbpai,ab->naip", g.reshape(nb, per, states, per, gwidth), eye).reshape(1, groups, gwidth, states)

    bbr, bbi = diag_b(bb_re).astype(BF16), diag_b(bb_im).astype(BF16)
    ccr, cci = diag_c(s5_c_re[0]).astype(BF16), diag_c(s5_c_im[0]).astype(BF16)
    a_re, a_im = lb_re.reshape(1, gp), lb_im.reshape(1, gp)
    d_skip = s5_d.reshape(1, sw)
    u5 = z[:, z_s5:z_s5 + sw]
    y5 = _s5_fwd(u5, bbr, bbi, ccr, cci, d_skip, a_re, a_im, "s5_fwd")
    ssm = _glu_fwd(y5, full["w_glu"], b_glu, "glu_fwd")

    pf = _mm(attn, full["w_proj_fox"], "nn", BF16, "mm_pf")
    ps = _mm(ssm, full["w_proj_s5"], "nn", BF16, "mm_ps")
    merged = _gate_merge_fwd(z, z_gate, b_gates, pf, ps, "merge_fwd")
    mo = _mm(merged, full["w_out"], "nn", F32, "mm_out")
    h, hn, r_ffn = _resid_rms(xs, mo, g_ffn, "resid_rms")
    gu, got = _mm(hn, full["w_gate_up"], "nn", BF16, "mm_gu", job=_GatherJob([shard["w_down"]]), b_stacked=True)
    full["w_down"] = down_rows(got[0])
    act = _swiglu_fwd(gu, "swiglu_fwd")
    dn = _mm(act, full["w_down"], "nn", F32, "mm_down")
    loss_blk, dy, dy_b = _loss_head(h, dn, target, "loss_head")
    loss = lax.psum(loss_blk[0, 0], ("x", "y", "c"))

    grad, sums, from_chips = {}, {}, {}

    def pair_sums(n, parts=None):
        if parts is None:
            g_full = grad[n]
            if n in COLUMN_SHARDED:
                parts = g_full.reshape(g_full.shape[0], N_DEV, g_full.shape[1] // N_DEV).transpose(1, 0, 2)
            else:
                parts = g_full.reshape(N_DEV, g_full.shape[0] // N_DEV, g_full.shape[1])
        got = _swap_with_sibling(parts, "rs_sibling_" + n)
        return _add_sibling(parts, got, core, "rs_add_" + n)

    dact = _mm(dy_b, full["w_down"], "nt", BF16, "mm_dact")
    gw_down = _mm(act, dy_b, "tn", BF16, "mm_gw_down")
    parts = jnp.stack(
        [lax.slice_in_dim(gw_down, cp_gu * (k // 2) + r_dn * (k % 2), cp_gu * (k // 2) + r_dn * (k % 2) + r_dn)
         for k in range(N_DEV)])
    (dgu,), (got,) = _swiglu_bwd(gu, dact, "swiglu_bwd", job=_SiblingSwapJob([parts]))
    sums["w_down"] = _add_sibling(parts, got, core, "rs_add_w_down")
    dhn, (from_chips["w_down"],) = _mm(dgu, full["w_gate_up"], "nt", F32, "mm_dhn", job=_ChipSwapJob([sums["w_down"]]),
                                       b_stacked=True)
    parts = _mm(hn, dgu, "tn", BF16, "mm_gw_gu", out_stack=N_DEV)
    dh_, dh_b, grad["g_ffn"] = _rms_bwd([dhn], h, r_ffn, g_ffn, [dy], "rms_ffn_bwd")
    dmerged, (got,) = _mm(dh_b, full["w_out"], "nt", BF16, "mm_dmerged", job=_SiblingSwapJob([parts]))
    sums["w_gate_up"] = _add_sibling(parts, got, core, "rs_add_w_gate_up")
    grad["w_out"] = _mm(merged, dh_b, "tn", BF16, "mm_gw_out")
    dpf, dps, dz_gf, dz_gs, db_gf, db_gs = _gate_merge_bwd(dmerged, z, z_gate, b_gates, pf, ps, "merge_bwd")
    grad["b_gates"] = jnp.concatenate([db_gf, db_gs], axis=1)
    dattn = _mm(dpf, full["w_proj_fox"], "nt", BF16, "mm_dattn")
    grad["w_proj_fox"] = _mm(attn, dpf, "tn", BF16, "mm_gw_pf")
    dssm = _mm(dps, full["w_proj_s5"], "nt", BF16, "mm_dssm")
    grad["w_proj_s5"] = _mm(ssm, dps, "tn", BF16, "mm_gw_ps")

    dy5, g5, da5, grad["b_glu"] = _glu_bwd(y5, dssm, full["w_glu"], b_glu, "glu_bwd")
    grad["w_glu"] = _mm(g5, da5, "tn", BF16, "mm_gw_glu")
    for n in early:
        sums[n] = pair_sums(n)
    (du5, d_bbr, d_bbi, d_ccr, d_cci, d_are, d_aim, d_dskip), got = _s5_bwd(
        u5, dy5, bbr, bbi, ccr, cci, d_skip, a_re, a_im, "s5_bwd", job=_ChipSwapJob([sums[n] for n in early]))
    from_chips.update(zip(early, got))
    d_lre, d_lim, d_lstep, d_bre, d_bim = _s5_prep_bwd(
        lam_re, lam_im, log_step, b_re, b_im, d_are.reshape(gp, 1), d_aim.reshape(gp, 1),
        undiag_b(d_bbr), undiag_b(d_bbi), groups, "s5_prep_bwd")
    grad["s5_lambda_re"], grad["s5_lambda_im"] = d_lre.reshape(1, groups, states), d_lim.reshape(1, groups, states)
    grad["s5_log_step"] = d_lstep
    grad["s5_b_re"], grad["s5_b_im"] = d_bre.reshape(s5_b_re.shape), d_bim.reshape(s5_b_im.shape)
    grad["s5_c_re"], grad["s5_c_im"] = undiag_c(d_ccr), undiag_c(d_cci)
    grad["s5_d"] = d_dskip.reshape(s5_d.shape)

    delta = _attn_delta(attn, dattn, heads, dh, "attn_delta")
    (dqn,), got = _attn_bwd_q(qn, kn, z, 2 * heads, dattn, fcol, frow, lse, delta, heads, dh, "attn_bwd_q",
                              job=_ChipSwapJob([sums["w_gate_up"]], rows=(0, half)))
    (dkn, dv, df_k), (from_chips["w_gate_up"],) = _attn_bwd_kv(
        qn, kn, z, 2 * heads, dattn, fcol, frow, lse, delta, heads, dh, "attn_bwd_kv",
        job=_ChipSwapJob([sums["w_gate_up"]], rows=(half, d - half), into=got))
    dq, dk, grad["q_norm"], grad["k_norm"] = _qk_prep_bwd(dqn, dkn, z, heads, dh, q_norm, k_norm, r_q, r_k, "qk_prep_bwd")
    dzf, db_forget = _forget_bwd(_pad_lanes(df_k[:, 0, :].T), zf, b_forget, "forget_bwd")
    grad["b_fgate"] = db_forget[:, :heads]

    dz = jnp.concatenate([dq, dk, dv, du5.astype(BF16), dz_gf, dz_gs], axis=1)
    gw_main = _mm(u, dz, "tn", BF16, "mm_gw_main")
    gw_forget = _mm(u, dzf, "tn", BF16, "mm_gw_forget")

    def from_grads(k, a, b):
        lo, hi = k * c_in + a, k * c_in + b
        if hi <= col_v:
            return gw_main[:, lo:hi]
        if hi <= col_f:
            return gw_forget[:, lo - col_v:hi - col_v]
        return gw_main[:, lo - heads:hi - heads]

    def in_part(k):
        lo, hi = k * c_in, (k + 1) * c_in
        cuts = [(max(lo, a), min(hi, b)) for a, b in ((0, col_v), (col_v, col_f), (col_f, in_cols))]
        return jnp.concatenate([from_grads(k, a - lo, b - lo) for a, b in cuts if a < b], axis=1)

    sums["w_in"] = pair_sums("w_in", jnp.stack([in_part(k) for k in range(N_DEV)]))
    du, (from_chips["w_in"],) = _mm(dz, w_main, "nt", F32, "mm_du", job=_ChipSwapJob([sums["w_in"]]))
    du_f = _mm(dzf, w_forget, "nt", F32, "mm_du_f")
    dx, _, grad["g_mix"] = _rms_bwd([du, du_f], xs, r_mix, g_mix, [dh_], "rms_mix_bwd")

    out_g, out_d, out_m, out_v = {}, {}, {}, {}
    for n in COLUMN_SHARDED + ROW_SHARDED:
        res = _adamw_shard(weights[n][0], mom_m[n][0], mom_v[n][0], sums[n], from_chips[n], chip, "adamw_" + n)
        out_g[n], out_d[n], out_m[n], out_v[n] = (r[None] for r in res)

    small = [n for n in WEIGHTS if n not in COLUMN_SHARDED + ROW_SHARDED]
    g_small = _all_reduce_small(_pack([grad[n] for n in small]), "ar_small")
    like = [weights[n] for n in small]
    res = _adamw_packed(_pack(like), _pack([mom_m[n] for n in small]), _pack([mom_v[n] for n in small]), g_small,
                        "adamw_small")
    for store, packed in zip((out_g, out_d, out_m, out_v), (g_small, *res)):
        for n, a in zip(small, _unpack(packed, like)):
            store[n] = a

    return (loss, dx[None], *[out_g[n] for n in WEIGHTS], *[out_d[n] for n in WEIGHTS],
            *[out_m[n] for n in WEIGHTS], *[out_v[n] for n in WEIGHTS])
```

```python
import functools
import math

import jax
import jax.numpy as jnp
from jax import lax
from jax.experimental import pallas as pl
from jax.experimental.pallas import tpu as pltpu

F32 = jnp.float32
BF16 = jnp.bfloat16

V7X_VMEM_LIMIT = 56 * 2**20
LANES = 128
N_DEV = 8
MESH = pl.DeviceIdType.MESH

RMS_EPS = 1e-6
MASK_VALUE = -1e30
ADAM_LR, ADAM_B1, ADAM_B2, ADAM_EPS, ADAM_WD, ADAM_STEP = 0.001, 0.9, 0.999, 1e-08, 0.01, 10
GELU_C = math.sqrt(2.0 / math.pi)
GELU_A = 0.044715


def _cp(sem=None):
    return pltpu.CompilerParams(dimension_semantics=sem, vmem_limit_bytes=V7X_VMEM_LIMIT)


def _tile(n, pref, unit=LANES):
    if n <= pref:
        return n
    t = (pref // unit) * unit
    while t >= unit:
        if n % t == 0:
            return t
        t -= unit
    raise ValueError(f"no tile for {n}")


def _row_tile(rows, cols, bytes_per_row_elem=4, target=2 * 2**20, unit=16):
    best = None
    for t in range(unit, rows + 1, unit):
        if rows % t == 0 and t * cols * bytes_per_row_elem <= target:
            best = t
    if best is None:
        best = unit if rows % unit == 0 else rows
    return best


def _sigmoid(x):
    return 1.0 / (1.0 + jnp.exp(-x))


def _position():
    return lax.axis_index("x"), lax.axis_index("y"), lax.axis_index("c")


def _other_chips(x, y):
    return [(1 - x, y), (x, 1 - y), (1 - x, 1 - y)]


def _all_gather(shard, name):
    rows, cols = shard.shape

    def body(x_ref, out_ref, send_sems, recv_sems, local_sem):
        x, y, c = _position()
        me, sibling = (x, y, c), (x, y, 1 - c)
        chips = _other_chips(x, y)

        def slot(px, py, pc):
            return out_ref.at[4 * px + 2 * py + pc]

        def copy(k, block, to, src=None):
            return pltpu.make_async_remote_copy(
                src_ref=slot(*block) if src is None else src, dst_ref=slot(*block),
                send_sem=send_sems.at[k], recv_sem=recv_sems.at[k], device_id=to, device_id_type=MESH)

        mine = pltpu.make_async_copy(x_ref, slot(*me), local_sem)
        mine.start()
        first = [copy(0, me, sibling, src=x_ref)]
        first += [copy(1 + j, me, (*chip, c), src=x_ref) for j, chip in enumerate(chips)]
        for cp in first:
            cp.start()
        passed = [copy(4 + j, (*chip, c), sibling) for j, chip in enumerate(chips)]
        for j, chip in enumerate(chips):
            copy(1 + j, (*chip, c), me).wait_recv()
            passed[j].start()
        copy(0, sibling, me).wait_recv()
        for j, chip in enumerate(chips):
            copy(4 + j, (*chip, 1 - c), me).wait_recv()
        for cp in first + passed:
            cp.wait_send()
        mine.wait()

    return pl.pallas_call(
        body, name=name,
        out_shape=jax.ShapeDtypeStruct((N_DEV, rows, cols), shard.dtype),
        in_specs=[pl.BlockSpec(memory_space=pltpu.HBM)],
        out_specs=pl.BlockSpec(memory_space=pltpu.HBM),
        scratch_shapes=[pltpu.SemaphoreType.DMA((7,)), pltpu.SemaphoreType.DMA((7,)), pltpu.SemaphoreType.DMA],
    )(shard)


def _swap_with_sibling(parts, name):
    _, rows, cols = parts.shape

    def body(p_ref, out_ref, send_sems, recv_sems):
        x, y, c = _position()
        copies = []
        for j in range(4):
            copies.append(pltpu.make_async_remote_copy(
                src_ref=p_ref.at[2 * j + (1 - c)], dst_ref=out_ref.at[j],
                send_sem=send_sems.at[j], recv_sem=recv_sems.at[j], device_id=(x, y, 1 - c), device_id_type=MESH))
        for cp in copies:
            cp.start()
        for cp in copies:
            cp.wait()

    return pl.pallas_call(
        body, name=name,
        out_shape=jax.ShapeDtypeStruct((4, rows, cols), parts.dtype),
        in_specs=[pl.BlockSpec(memory_space=pltpu.HBM)],
        out_specs=pl.BlockSpec(memory_space=pltpu.HBM),
        scratch_shapes=[pltpu.SemaphoreType.DMA((4,)), pltpu.SemaphoreType.DMA((4,))],
    )(parts)


def _add_sibling(parts, got, core, name):
    _, rows, cols = parts.shape
    tr = _row_tile(rows, cols)

    def body(core_ref, a_ref, b_ref, o_ref):
        o_ref[...] = (a_ref[...].astype(F32) + b_ref[...].astype(F32)).astype(o_ref.dtype)

    return pl.pallas_call(
        body, name=name,
        grid_spec=pltpu.PrefetchScalarGridSpec(
            num_scalar_prefetch=1, grid=(4, rows // tr),
            in_specs=[pl.BlockSpec((None, tr, cols), lambda j, i, core_ref: (2 * j + core_ref[0], i, 0)),
                      pl.BlockSpec((None, tr, cols), lambda j, i, core_ref: (j, i, 0))],
            out_specs=pl.BlockSpec((None, tr, cols), lambda j, i, core_ref: (j, i, 0))),
        out_shape=jax.ShapeDtypeStruct((4, rows, cols), BF16),
        compiler_params=_cp(("parallel", "parallel")),
    )(core, parts, got)


def _all_reduce_small(packed, name):
    rows, cols = packed.shape

    def body(x_ref, out_ref, gathered, send_sems, recv_sems):
        x, y, c = _position()
        me, sibling = (x, y, c), (x, y, 1 - c)
        chips = _other_chips(x, y)

        def slot(px, py, pc):
            return gathered.at[4 * px + 2 * py + pc]

        def copy(k, block, to, src=None):
            return pltpu.make_async_remote_copy(
                src_ref=slot(*block) if src is None else src, dst_ref=slot(*block),
                send_sem=send_sems.at[k], recv_sem=recv_sems.at[k], device_id=to, device_id_type=MESH)

        first = [copy(0, me, sibling, src=x_ref)]
        first += [copy(1 + j, me, (*chip, c), src=x_ref) for j, chip in enumerate(chips)]
        for cp in first:
            cp.start()
        passed = [copy(4 + j, (*chip, c), sibling) for j, chip in enumerate(chips)]
        for j, chip in enumerate(chips):
            copy(1 + j, (*chip, c), me).wait_recv()
            passed[j].start()
        copy(0, sibling, me).wait_recv()
        for j, chip in enumerate(chips):
            copy(4 + j, (*chip, 1 - c), me).wait_recv()
        for cp in first + passed:
            cp.wait_send()
        gathered[4 * x + 2 * y + c] = x_ref[...]
        total = gathered[0]
        for k in range(1, N_DEV):
            total = total + gathered[k]
        out_ref[...] = total

    return pl.pallas_call(
        body, name=name,
        out_shape=jax.ShapeDtypeStruct((rows, cols), F32),
        in_specs=[pl.BlockSpec(memory_space=pltpu.VMEM)],
        out_specs=pl.BlockSpec(memory_space=pltpu.VMEM),
        scratch_shapes=[pltpu.VMEM((N_DEV, rows, cols), F32),
                        pltpu.SemaphoreType.DMA((7,)), pltpu.SemaphoreType.DMA((7,))],
        compiler_params=pltpu.CompilerParams(vmem_limit_bytes=V7X_VMEM_LIMIT),
    )(packed)


class _GatherJob:
    def __init__(self, shards, rows=None, into=None):
        self.n, self.rows = len(shards), rows
        self.inputs = list(shards) + list(into or [])
        self.aliases = {self.n + i: i for i in range(len(into or []))}
        self.out_shape = [jax.ShapeDtypeStruct((N_DEV,) + s.shape, s.dtype) for s in shards]
        self.scratch = [pltpu.SemaphoreType.DMA((7 * self.n,)), pltpu.SemaphoreType.DMA((7 * self.n,)),
                        pltpu.SemaphoreType.DMA((self.n,))]

    def _plan(self, ins, outs, scratch):
        send_sems, recv_sems, local_sems = scratch
        x, y, c = _position()
        me, sibling = (x, y, c), (x, y, 1 - c)
        chips = _other_chips(x, y)

        def slot(i, px, py, pc):
            return _rows_of(outs[i].at[4 * px + 2 * py + pc], self.rows)

        def copy(i, k, block, to, own=False):
            return pltpu.make_async_remote_copy(
                src_ref=_rows_of(ins[i], self.rows) if own else slot(i, *block), dst_ref=slot(i, *block),
                send_sem=send_sems.at[7 * i + k], recv_sem=recv_sems.at[7 * i + k], device_id=to, device_id_type=MESH)

        def mine(i):
            return pltpu.make_async_copy(_rows_of(ins[i], self.rows), slot(i, *me), local_sems.at[i])

        return c, me, sibling, chips, copy, mine

    def begin(self, ins, outs, scratch):
        c, me, sibling, chips, copy, mine = self._plan(ins, outs, scratch)
        for i in range(self.n):
            mine(i).start()
            copy(i, 0, me, sibling, own=True).start()
            for j, chip in enumerate(chips):
                copy(i, 1 + j, me, (*chip, c), own=True).start()

    def middle(self, ins, outs, scratch):
        c, me, sibling, chips, copy, mine = self._plan(ins, outs, scratch)
        for i in range(self.n):
            for j, chip in enumerate(chips):
                copy(i, 1 + j, (*chip, c), me).wait_recv()
                copy(i, 4 + j, (*chip, c), sibling).start()

    def end(self, ins, outs, scratch):
        c, me, sibling, chips, copy, mine = self._plan(ins, outs, scratch)
        for i in range(self.n):
            copy(i, 0, sibling, me).wait_recv()
            for j, chip in enumerate(chips):
                copy(i, 4 + j, (*chip, 1 - c), me).wait_recv()
            copy(i, 0, me, sibling, own=True).wait_send()
            for j, chip in enumerate(chips):
                copy(i, 1 + j, me, (*chip, c), own=True).wait_send()
                copy(i, 4 + j, (*chip, c), sibling).wait_send()
            mine(i).wait()


class _ChipSwapJob:
    def __init__(self, sums, rows=None, into=None):
        self.n, self.rows = len(sums), rows
        self.inputs = list(sums) + list(into or [])
        self.aliases = {self.n + i: i for i in range(len(into or []))}
        self.out_shape = [jax.ShapeDtypeStruct((3,) + s.shape[1:], s.dtype) for s in sums]
        self.scratch = [pltpu.SemaphoreType.DMA((3 * self.n,)), pltpu.SemaphoreType.DMA((3 * self.n,))]

    def _copies(self, ins, outs, scratch):
        send_sems, recv_sems = scratch
        x, y, c = _position()
        return [pltpu.make_async_remote_copy(
            src_ref=_rows_of(ins[i].at[2 * px + py], self.rows), dst_ref=_rows_of(outs[i].at[k], self.rows),
            send_sem=send_sems.at[3 * i + k], recv_sem=recv_sems.at[3 * i + k],
            device_id=(px, py, c), device_id_type=MESH)
            for i in range(self.n) for k, (px, py) in enumerate(_other_chips(x, y))]

    def begin(self, ins, outs, scratch):
        for cp in self._copies(ins, outs, scratch):
            cp.start()

    def middle(self, ins, outs, scratch):
        pass

    def end(self, ins, outs, scratch):
        for cp in self._copies(ins, outs, scratch):
            cp.wait()


class _SiblingSwapJob:
    aliases = {}

    def __init__(self, parts):
        self.inputs = list(parts)
        self.out_shape = [jax.ShapeDtypeStruct((4,) + p.shape[1:], p.dtype) for p in parts]
        n = len(parts)
        self.scratch = [pltpu.SemaphoreType.DMA((4 * n,)), pltpu.SemaphoreType.DMA((4 * n,))]

    def _copies(self, ins, outs, scratch):
        send_sems, recv_sems = scratch
        x, y, c = _position()
        return [pltpu.make_async_remote_copy(
            src_ref=ins[i].at[2 * j + (1 - c)], dst_ref=outs[i].at[j], send_sem=send_sems.at[4 * i + j],
            recv_sem=recv_sems.at[4 * i + j], device_id=(x, y, 1 - c), device_id_type=MESH)
            for i in range(len(ins)) for j in range(4)]

    def begin(self, ins, outs, scratch):
        for cp in self._copies(ins, outs, scratch):
            cp.start()

    def middle(self, ins, outs, scratch):
        pass

    def end(self, ins, outs, scratch):
        for cp in self._copies(ins, outs, scratch):
            cp.wait()


class _JobGroup:
    def __init__(self, jobs):
        self.jobs = list(jobs)
        self.inputs = [a for j in jobs for a in j.inputs]
        self.out_shape = [s for j in jobs for s in j.out_shape]
        self.scratch = [s for j in jobs for s in j.scratch]
        self.aliases, at_in, at_out = {}, 0, 0
        for j in jobs:
            self.aliases.update({at_in + i: at_out + o for i, o in j.aliases.items()})
            at_in, at_out = at_in + len(j.inputs), at_out + len(j.out_shape)

    def _each(self, phase, ins, outs, scratch):
        for j in self.jobs:
            n_in, n_out, n_scr = len(j.inputs), len(j.out_shape), len(j.scratch)
            getattr(j, phase)(ins[:n_in], outs[:n_out], scratch[:n_scr])
            ins, outs, scratch = ins[n_in:], outs[n_out:], scratch[n_scr:]

    def begin(self, ins, outs, scratch):
        self._each("begin", ins, outs, scratch)

    def middle(self, ins, outs, scratch):
        self._each("middle", ins, outs, scratch)

    def end(self, ins, outs, scratch):
        self._each("end", ins, outs, scratch)


def _rows_of(ref, rows):
    return ref if rows is None else ref.at[pl.ds(rows[0], rows[1])]


def _call(body, *, name, grid, in_specs, out_specs, out_shape, scratch_shapes, semantics, operands, job=None):
    if job is None:
        return pl.pallas_call(
            body, name=name, grid=grid, in_specs=in_specs, out_specs=out_specs, out_shape=out_shape,
            scratch_shapes=scratch_shapes, compiler_params=_cp(semantics))(*operands)
    n_in, n_out, n_scr = len(in_specs), len(out_specs), len(scratch_shapes)
    j_in, j_out = len(job.inputs), len(job.out_shape)
    n_steps = math.prod(grid)
    hbm = pl.BlockSpec(memory_space=pltpu.HBM)

    def carrier(*refs):
        ins, refs = refs[:n_in], refs[n_in:]
        job_ins, refs = refs[:j_in], refs[j_in:]
        outs, refs = refs[:n_out], refs[n_out:]
        job_outs, refs = refs[:j_out], refs[j_out:]
        scr, job_scr = refs[:n_scr], refs[n_scr:]
        step = pl.program_id(0)
        for axis in range(1, len(grid)):
            step = step * grid[axis] + pl.program_id(axis)

        @pl.when(step == 0)
        def _():
            job.begin(job_ins, job_outs, job_scr)

        body(*ins, *outs, *scr)

        @pl.when(step == (3 * n_steps) // 4)
        def _():
            job.middle(job_ins, job_outs, job_scr)

        @pl.when(step == n_steps - 1)
        def _():
            job.end(job_ins, job_outs, job_scr)

    res = pl.pallas_call(
        carrier, name=name, grid=grid,
        in_specs=list(in_specs) + [hbm] * j_in, out_specs=list(out_specs) + [hbm] * j_out,
        out_shape=list(out_shape) + job.out_shape, scratch_shapes=list(scratch_shapes) + job.scratch,
        input_output_aliases={n_in + i: n_out + o for i, o in job.aliases.items()},
        compiler_params=_cp(("arbitrary",) * len(grid)))(*operands, *job.inputs)
    return res[:n_out], res[n_out:]


MM_VMEM_BUDGET = 44 * 2**20
MM_TILE_CAP = 1536
MM_MIN_INTENSITY = 340


def _divisor_tiles(n, cap):
    return [t for t in range(LANES, min(n, cap) + 1, LANES) if n % t == 0] or [n]


def _mm_tiles(m, n_unit, k_unit, whole_k, a_bytes, b_bytes, o_bytes):
    best, best_key = None, None
    for tm in _divisor_tiles(m, 1024):
        for tn in _divisor_tiles(n_unit, MM_TILE_CAP):
            for tk in _divisor_tiles(k_unit, k_unit):
                one_block = whole_k and tk == k_unit
                need = (2 * (tm * tk * a_bytes + tk * tn * b_bytes) + 2 * tm * tn * o_bytes + tm * tn * 4
                        + (0 if one_block else tm * tn * 4))
                intensity = tm * tn / (tm + tn)
                key = (intensity >= MM_MIN_INTENSITY, one_block, intensity, tk)
                if need <= MM_VMEM_BUDGET and (best_key is None or key > best_key):
                    best, best_key = (tm, tn, tk), key
    if best is None:
        raise ValueError(f"no matmul tiles for {(m, n_unit, k_unit)}")
    return best


def _mm(a, b, mode, out_dtype, name, job=None, b_stacked=False, out_stack=None):
    b_rows, b_cols = (b.shape[1], b.shape[0] * b.shape[2]) if b_stacked else b.shape
    b_unit = b.shape[2] if b_stacked else b_cols
    if mode == "nn":
        (m, k), (k2, n) = a.shape, (b_rows, b_cols)
    elif mode == "nt":
        (m, k), (n, k2) = a.shape, (b_rows, b_cols)
    else:
        (k, m), (k2, n) = a.shape, (b_rows, b_cols)
    assert k == k2, (name, a.shape, b.shape)
    n_unit = n // out_stack if out_stack else (b_unit if b_stacked and mode != "nt" else n)
    k_unit = b_unit if b_stacked and mode == "nt" else k
    tm, tn, tk = _mm_tiles(m, n_unit, k_unit, k_unit == k, a.dtype.itemsize, b.dtype.itemsize,
                           jnp.dtype(out_dtype).itemsize)
    nk = k // tk
    per_n, per_k = n_unit // tn, k_unit // tk
    if mode == "tn":
        a_spec = pl.BlockSpec((tk, tm), lambda i, j, l: (l, i))
        dims = (((0,), (0,)), ((), ()))
    else:
        a_spec = pl.BlockSpec((tm, tk), lambda i, j, l: (i, l))
        dims = (((1,), (1,)), ((), ())) if mode == "nt" else (((1,), (0,)), ((), ()))
    if mode == "nt" and b_stacked:
        b_spec = pl.BlockSpec((None, tn, tk), lambda i, j, l: (l // per_k, j, l % per_k))
    elif mode == "nt":
        b_spec = pl.BlockSpec((tn, tk), lambda i, j, l: (j, l))
    elif b_stacked:
        b_spec = pl.BlockSpec((None, tk, tn), lambda i, j, l: (j // per_n, l, j % per_n))
    else:
        b_spec = pl.BlockSpec((tk, tn), lambda i, j, l: (l, j))
    if out_stack:
        o_spec = pl.BlockSpec((None, tm, tn), lambda i, j, l: (j // per_n, i, j % per_n))
        o_shape = jax.ShapeDtypeStruct((out_stack, m, n_unit), out_dtype)
    else:
        o_spec = pl.BlockSpec((tm, tn), lambda i, j, l: (i, j))
        o_shape = jax.ShapeDtypeStruct((m, n), out_dtype)

    def product(a_ref, b_ref):
        return lax.dot_general(a_ref[...].astype(BF16), b_ref[...].astype(BF16), dims, preferred_element_type=F32)

    def body_whole_k(a_ref, b_ref, o_ref):
        o_ref[...] = product(a_ref, b_ref).astype(o_ref.dtype)

    def body_split_k(a_ref, b_ref, o_ref, acc_ref):
        l = pl.program_id(2)

        @pl.when(l == 0)
        def _():
            acc_ref[...] = product(a_ref, b_ref)

        @pl.when(l > 0)
        def _():
            acc_ref[...] += product(a_ref, b_ref)

        @pl.when(l == nk - 1)
        def _():
            o_ref[...] = acc_ref[...].astype(o_ref.dtype)

    res = _call(
        body_whole_k if nk == 1 else body_split_k, name=name, grid=(m // tm, n // tn, nk),
        in_specs=[a_spec, b_spec],
        out_specs=[o_spec],
        out_shape=[o_shape],
        scratch_shapes=[] if nk == 1 else [pltpu.VMEM((tm, tn), F32)],
        semantics=("parallel", "parallel", "arbitrary"), operands=(a, b), job=job)
    return res[0] if job is None else (res[0][0], res[1])


def _rms_fwd(x, g, name):
    t, d = x.shape
    tm = _tile(t, 256, 16)

    def body(x_ref, g_ref, u_ref, r_ref):
        xv = x_ref[...]
        r = lax.rsqrt(jnp.mean(xv * xv, axis=-1, keepdims=True) + RMS_EPS)
        u_ref[...] = (xv * r * g_ref[...]).astype(u_ref.dtype)
        r_ref[...] = r

    return pl.pallas_call(
        body, name=name, grid=(t // tm,),
        in_specs=[pl.BlockSpec((tm, d), lambda i: (i, 0)), pl.BlockSpec((1, d), lambda i: (0, 0))],
        out_specs=[pl.BlockSpec((tm, d), lambda i: (i, 0)), pl.BlockSpec((tm, 1), lambda i: (i, 0))],
        out_shape=[jax.ShapeDtypeStruct((t, d), BF16), jax.ShapeDtypeStruct((t, 1), F32)],
        compiler_params=_cp(("parallel",)),
    )(x, g)


def _rms_bwd(dn_parts, x, r, g, extra, name):
    t, d = x.shape
    tm = _tile(t, 128, 16)
    n_dn, n_extra = len(dn_parts), len(extra)

    def body(*refs):
        dn_refs = refs[:n_dn]
        x_ref, r_ref, g_ref = refs[n_dn:n_dn + 3]
        extra_refs = refs[n_dn + 3:n_dn + 3 + n_extra]
        dx_ref, dxb_ref, dg_ref = refs[n_dn + 3 + n_extra:]
        xhat = x_ref[...] * r_ref[...]
        dnv = dn_refs[0][...].astype(F32)
        for p in dn_refs[1:]:
            dnv = dnv + p[...].astype(F32)
        gd = dnv * g_ref[...]
        dx = r_ref[...] * (gd - xhat * jnp.mean(gd * xhat, axis=-1, keepdims=True))
        for e in extra_refs:
            dx = dx + e[...].astype(F32)
        dx_ref[...] = dx
        dxb_ref[...] = dx.astype(dxb_ref.dtype)

        @pl.when(pl.program_id(0) == 0)
        def _():
            dg_ref[...] = jnp.zeros_like(dg_ref)

        dg_ref[...] += jnp.sum(dnv * xhat, axis=0, keepdims=True)

    row = pl.BlockSpec((tm, d), lambda i: (i, 0))
    return pl.pallas_call(
        body, name=name, grid=(t // tm,),
        in_specs=[row] * n_dn + [row, pl.BlockSpec((tm, 1), lambda i: (i, 0)), pl.BlockSpec((1, d), lambda i: (0, 0))]
        + [row] * n_extra,
        out_specs=[row, row, pl.BlockSpec((1, d), lambda i: (0, 0))],
        out_shape=[jax.ShapeDtypeStruct((t, d), F32), jax.ShapeDtypeStruct((t, d), BF16), jax.ShapeDtypeStruct((1, d), F32)],
        compiler_params=_cp(("arbitrary",)),
    )(*dn_parts, x, r, g, *extra)


def _gate_merge_fwd(z, gate_col, b_gates, pf, ps, name):
    t, d = pf.shape
    tm, tn = _tile(t, 512, 16), _tile(math.gcd(d, gate_col), 512)
    nj, off = d // tn, gate_col // tn
    assert gate_col % tn == 0

    def body(zf_ref, zs_ref, bf_ref, bs_ref, pf_ref, ps_ref, o_ref):
        gf = _sigmoid(zf_ref[...].astype(F32) + bf_ref[...])
        gs = _sigmoid(zs_ref[...].astype(F32) + bs_ref[...])
        o_ref[...] = (gf * pf_ref[...].astype(F32) + gs * ps_ref[...].astype(F32)).astype(o_ref.dtype)

    blk = pl.BlockSpec((tm, tn), lambda i, j: (i, j))
    return pl.pallas_call(
        body, name=name, grid=(t // tm, nj),
        in_specs=[pl.BlockSpec((tm, tn), lambda i, j: (i, off + j)), pl.BlockSpec((tm, tn), lambda i, j: (i, off + nj + j)),
                  pl.BlockSpec((1, tn), lambda i, j: (0, j)), pl.BlockSpec((1, tn), lambda i, j: (0, nj + j)), blk, blk],
        out_specs=blk,
        out_shape=jax.ShapeDtypeStruct((t, d), BF16),
        compiler_params=_cp(("parallel", "parallel")),
    )(z, z, b_gates, b_gates, pf, ps)


def _gate_merge_bwd(dm, z, gate_col, b_gates, pf, ps, name):
    t, d = pf.shape
    tm, tn = _tile(t, 512, 16), _tile(math.gcd(d, gate_col), 512)
    nj, off = d // tn, gate_col // tn

    def body(dm_ref, zf_ref, zs_ref, bf_ref, bs_ref, pf_ref, ps_ref, dpf_ref, dps_ref, dzf_ref, dzs_ref, dbf_ref, dbs_ref):
        gf = _sigmoid(zf_ref[...].astype(F32) + bf_ref[...])
        gs = _sigmoid(zs_ref[...].astype(F32) + bs_ref[...])
        dmv = dm_ref[...].astype(F32)
        dpf_ref[...] = (dmv * gf).astype(dpf_ref.dtype)
        dps_ref[...] = (dmv * gs).astype(dps_ref.dtype)
        dzf = dmv * pf_ref[...].astype(F32) * gf * (1.0 - gf)
        dzs = dmv * ps_ref[...].astype(F32) * gs * (1.0 - gs)
        dzf_ref[...] = dzf.astype(dzf_ref.dtype)
        dzs_ref[...] = dzs.astype(dzs_ref.dtype)

        @pl.when(pl.program_id(1) == 0)
        def _():
            dbf_ref[...] = jnp.zeros_like(dbf_ref)
            dbs_ref[...] = jnp.zeros_like(dbs_ref)

        dbf_ref[...] += jnp.sum(dzf, axis=0, keepdims=True)
        dbs_ref[...] += jnp.sum(dzs, axis=0, keepdims=True)

    blk = pl.BlockSpec((tm, tn), lambda j, i: (i, j))
    lo = pl.BlockSpec((1, tn), lambda j, i: (0, j))
    hi = pl.BlockSpec((1, tn), lambda j, i: (0, nj + j))
    return pl.pallas_call(
        body, name=name, grid=(nj, t // tm),
        in_specs=[blk, pl.BlockSpec((tm, tn), lambda j, i: (i, off + j)), pl.BlockSpec((tm, tn), lambda j, i: (i, off + nj + j)),
                  lo, hi, blk, blk],
        out_specs=[blk, blk, blk, blk, lo, lo],
        out_shape=[jax.ShapeDtypeStruct((t, d), BF16)] * 4 + [jax.ShapeDtypeStruct((1, d), F32)] * 2,
        compiler_params=_cp(("parallel", "arbitrary")),
    )(dm, z, z, b_gates, b_gates, pf, ps)


def _resid_rms(x, mo, g, name):
    t, d = x.shape
    tm = _tile(t, 256, 16)

    def body(x_ref, mo_ref, g_ref, h_ref, hn_ref, r_ref):
        h = x_ref[...] + mo_ref[...].astype(F32)
        r = lax.rsqrt(jnp.mean(h * h, axis=-1, keepdims=True) + RMS_EPS)
        h_ref[...] = h
        hn_ref[...] = (h * r * g_ref[...]).astype(hn_ref.dtype)
        r_ref[...] = r

    row = pl.BlockSpec((tm, d), lambda i: (i, 0))
    col = pl.BlockSpec((tm, 1), lambda i: (i, 0))
    return pl.pallas_call(
        body, name=name, grid=(t // tm,),
        in_specs=[row, row, pl.BlockSpec((1, d), lambda i: (0, 0))],
        out_specs=[row, row, col],
        out_shape=[jax.ShapeDtypeStruct((t, d), F32), jax.ShapeDtypeStruct((t, d), BF16), jax.ShapeDtypeStruct((t, 1), F32)],
        compiler_params=_cp(("parallel",)),
    )(x, mo, g)


def _swiglu_fwd(gu, name):
    t, f2 = gu.shape
    f = f2 // 2
    tm, tn = _tile(t, 512, 16), _tile(f, 1024)
    nj = f // tn

    def body(g_ref, u_ref, o_ref):
        gate = g_ref[...].astype(F32)
        o_ref[...] = (gate * _sigmoid(gate) * u_ref[...].astype(F32)).astype(o_ref.dtype)

    return pl.pallas_call(
        body, name=name, grid=(t // tm, nj),
        in_specs=[pl.BlockSpec((tm, tn), lambda i, j: (i, j)), pl.BlockSpec((tm, tn), lambda i, j: (i, nj + j))],
        out_specs=pl.BlockSpec((tm, tn), lambda i, j: (i, j)),
        out_shape=jax.ShapeDtypeStruct((t, f), BF16),
        compiler_params=_cp(("parallel", "parallel")),
    )(gu, gu)


def _swiglu_bwd(gu, dact, name, job=None):
    t, f2 = gu.shape
    f = f2 // 2
    tm, tn = _tile(t, 512, 16), _tile(f, 1024)
    nj = f // tn

    def body(g_ref, u_ref, da_ref, dg_ref, du_ref):
        gate = g_ref[...].astype(F32)
        s = _sigmoid(gate)
        da = da_ref[...].astype(F32)
        dg_ref[...] = (da * u_ref[...].astype(F32) * s * (1.0 + gate * (1.0 - s))).astype(dg_ref.dtype)
        du_ref[...] = (da * gate * s).astype(du_ref.dtype)

    lo = pl.BlockSpec((tm, tn), lambda i, j: (i, j))
    return _call(
        body, name=name, grid=(t // tm, nj),
        in_specs=[lo, pl.BlockSpec((tm, tn), lambda i, j: (i, nj + j)), lo],
        out_specs=[lo, lo],
        out_shape=[jax.ShapeDtypeStruct((t, f), BF16)] * 2,
        scratch_shapes=[], semantics=("parallel", "parallel"), operands=(gu, gu, dact), job=job)


def _loss_head(h, dn, target, name):
    t, d = h.shape
    tm = _tile(t, 256, 16)

    def body(h_ref, dn_ref, t_ref, loss_ref, dy_ref, dyb_ref):
        err = h_ref[...] + dn_ref[...].astype(F32) - t_ref[...]
        dy_ref[...] = err * (1.0 / d)
        dyb_ref[...] = (err * (1.0 / d)).astype(dyb_ref.dtype)

        @pl.when(pl.program_id(0) == 0)
        def _():
            loss_ref[...] = jnp.zeros_like(loss_ref)

        loss_ref[...] += 0.5 * jnp.sum(jnp.mean(err * err, axis=-1, keepdims=True))

    row = pl.BlockSpec((tm, d), lambda i: (i, 0))
    return pl.pallas_call(
        body, name=name, grid=(t // tm,),
        in_specs=[row, row, row],
        out_specs=[pl.BlockSpec((8, LANES), lambda i: (0, 0)), row, row],
        out_shape=[jax.ShapeDtypeStruct((8, LANES), F32), jax.ShapeDtypeStruct((t, d), F32), jax.ShapeDtypeStruct((t, d), BF16)],
        compiler_params=_cp(("arbitrary",)),
    )(h, dn, target)


def _qk_prep(z, heads, dh, q_norm, k_norm, name):
    t = z.shape[0]
    tq = _tile(t, 512, 16)
    scale = 1.0 / math.sqrt(dh)

    def body(q_ref, k_ref, gq_ref, gk_ref, qn_ref, kn_ref, rq_ref, rk_ref):
        q = q_ref[...].astype(F32)
        k = k_ref[...].astype(F32)
        rq = lax.rsqrt(jnp.mean(q * q, axis=-1, keepdims=True) + RMS_EPS)
        rk = lax.rsqrt(jnp.mean(k * k, axis=-1, keepdims=True) + RMS_EPS)
        qn_ref[...] = (q * rq * gq_ref[...] * scale).astype(qn_ref.dtype)
        kn_ref[...] = (k * rk * gk_ref[...]).astype(kn_ref.dtype)
        rq_ref[...] = rq
        rk_ref[...] = rk

    blk = pl.BlockSpec((tq, dh), lambda i, h: (i, h))
    vec = pl.BlockSpec((1, dh), lambda i, h: (0, 0))
    col = pl.BlockSpec((None, tq, 1), lambda i, h: (h, i, 0))
    return pl.pallas_call(
        body, name=name, grid=(t // tq, heads),
        in_specs=[blk, pl.BlockSpec((tq, dh), lambda i, h: (i, heads + h)), vec, vec],
        out_specs=[blk, blk, col, col],
        out_shape=[jax.ShapeDtypeStruct((t, heads * dh), BF16)] * 2 + [jax.ShapeDtypeStruct((heads, t, 1), F32)] * 2,
        compiler_params=_cp(("parallel", "parallel")),
    )(z, z, q_norm, k_norm)


def _qk_prep_bwd(dqn, dkn, z, heads, dh, q_norm, k_norm, rq, rk, name):
    t = z.shape[0]
    tq = _tile(t, 512, 16)
    scale = 1.0 / math.sqrt(dh)

    def norm_bwd(dy, xv, r, g):
        xhat = xv * r
        gd = dy * g
        return r * (gd - xhat * jnp.mean(gd * xhat, axis=-1, keepdims=True)), jnp.sum(dy * xhat, axis=0, keepdims=True)

    def body(dqn_ref, dkn_ref, q_ref, k_ref, gq_ref, gk_ref, rq_ref, rk_ref, dq_ref, dk_ref, dgq_ref, dgk_ref):
        dq, dgq = norm_bwd(dqn_ref[...].astype(F32) * scale, q_ref[...].astype(F32), rq_ref[...], gq_ref[...])
        dk, dgk = norm_bwd(dkn_ref[...].astype(F32), k_ref[...].astype(F32), rk_ref[...], gk_ref[...])
        dq_ref[...] = dq.astype(dq_ref.dtype)
        dk_ref[...] = dk.astype(dk_ref.dtype)

        @pl.when((pl.program_id(0) == 0) & (pl.program_id(1) == 0))
        def _():
            dgq_ref[...] = jnp.zeros_like(dgq_ref)
            dgk_ref[...] = jnp.zeros_like(dgk_ref)

        dgq_ref[...] += dgq
        dgk_ref[...] += dgk

    blk = pl.BlockSpec((tq, dh), lambda i, h: (i, h))
    vec = pl.BlockSpec((1, dh), lambda i, h: (0, 0))
    col = pl.BlockSpec((None, tq, 1), lambda i, h: (h, i, 0))
    return pl.pallas_call(
        body, name=name, grid=(t // tq, heads),
        in_specs=[blk, blk, blk, pl.BlockSpec((tq, dh), lambda i, h: (i, heads + h)), vec, vec, col, col],
        out_specs=[blk, blk, vec, vec],
        out_shape=[jax.ShapeDtypeStruct((t, heads * dh), BF16)] * 2 + [jax.ShapeDtypeStruct((1, dh), F32)] * 2,
        compiler_params=_cp(("arbitrary", "arbitrary")),
    )(dqn, dkn, z, z, q_norm, k_norm, rq, rk)


def _tri_ones(n, upper):
    row = lax.broadcasted_iota(jnp.int32, (n, n), 0)
    col = lax.broadcasted_iota(jnp.int32, (n, n), 1)
    return jnp.where((col >= row) if upper else (col <= row), 1.0, 0.0).astype(F32)


def _forget_fwd(f, b, name):
    t, w = f.shape
    blk = _tile(t, 256, 8)
    nb = t // blk

    def body(f_ref, b_ref, out_ref):
        tri = _tri_ones(blk, upper=False)

        def step(i, carry):
            rows = pl.ds(pl.multiple_of(i * blk, blk), blk)
            logf = jax.nn.log_sigmoid(f_ref[rows, :] + b_ref[...])
            acc = jnp.dot(tri, logf, precision=lax.Precision.HIGHEST, preferred_element_type=F32) + carry
            out_ref[rows, :] = acc
            return acc[blk - 1:blk, :]

        lax.fori_loop(0, nb, step, jnp.zeros((1, w), F32))

    return pl.pallas_call(
        body, name=name,
        in_specs=[pl.BlockSpec(memory_space=pltpu.VMEM)] * 2,
        out_specs=pl.BlockSpec(memory_space=pltpu.VMEM),
        out_shape=jax.ShapeDtypeStruct((t, w), F32),
        compiler_params=_cp(),
    )(f, b)


def _forget_bwd(d_key, f, b, name):
    t, w = f.shape
    blk = _tile(t, 256, 8)
    nb = t // blk

    def body(dk_ref, f_ref, b_ref, df_ref, db_ref):
        tri = _tri_ones(blk, upper=True)

        def step(i, carry):
            suffix, db = carry
            rows = pl.ds(pl.multiple_of((nb - 1 - i) * blk, blk), blk)
            dlog = suffix - jnp.dot(tri, dk_ref[rows, :], precision=lax.Precision.HIGHEST, preferred_element_type=F32)
            df = dlog * _sigmoid(-(f_ref[rows, :] + b_ref[...]))
            df_ref[rows, :] = df
            return dlog[0:1, :], db + jnp.sum(df, axis=0, keepdims=True)

        _, db = lax.fori_loop(0, nb, step, (jnp.zeros((1, w), F32), jnp.zeros((1, w), F32)))
        db_ref[...] = db

    return pl.pallas_call(
        body, name=name,
        in_specs=[pl.BlockSpec(memory_space=pltpu.VMEM)] * 3,
        out_specs=[pl.BlockSpec(memory_space=pltpu.VMEM)] * 2,
        out_shape=[jax.ShapeDtypeStruct((t, w), F32), jax.ShapeDtypeStruct((1, w), F32)],
        compiler_params=_cp(),
    )(d_key, f, b)


def _attn_logits(q, k, f_keys, f_first, diagonal):
    s = lax.dot_general(q, k, (((1,), (1,)), ((), ())), preferred_element_type=F32)
    s = s - (f_keys - f_first)
    if diagonal:
        row = lax.broadcasted_iota(jnp.int32, s.shape, 0)
        col = lax.broadcasted_iota(jnp.int32, s.shape, 1)
        s = jnp.where(col <= row, s, MASK_VALUE)
    return s


def _block_at(i, blk):
    return pl.ds(pl.multiple_of(i * blk, blk), blk)


ATTN_BLOCK = 512


def _attn_fwd(qn, kn, v_src, v_col, fcol, frow, heads, dh, name, job=None):
    t = qn.shape[0]
    blk = _tile(t, ATTN_BLOCK)

    def body(q_ref, k_ref, v_ref, fc_ref, fr_ref, o_ref, lse_ref):
        qi = pl.program_id(1)
        q = q_ref[...]
        f_first = fc_ref[0:1, :]

        def block(ki, carry, diagonal):
            m, l, acc = carry
            keys = _block_at(ki, blk)
            s = _attn_logits(q, k_ref[keys, :], fr_ref[:, keys], f_first, diagonal)
            m_new = jnp.maximum(m, jnp.max(s, axis=-1, keepdims=True))
            alpha = jnp.exp(m - m_new)
            p = jnp.exp(s - m_new)
            l = alpha * l + jnp.sum(p, axis=-1, keepdims=True)
            acc = alpha * acc + jnp.dot(p.astype(BF16), v_ref[keys, :].astype(BF16), preferred_element_type=F32)
            return m_new, l, acc

        start = (jnp.full((blk, 1), MASK_VALUE, F32), jnp.zeros((blk, 1), F32), jnp.zeros((blk, dh), F32))
        below = lax.fori_loop(0, qi, lambda ki, carry: block(ki, carry, False), start)
        m, l, acc = block(qi, below, True)
        o_ref[...] = (acc / l).astype(o_ref.dtype)
        lse_ref[...] = m + jnp.log(l)

    qblk = pl.BlockSpec((blk, dh), lambda h, i: (i, h))
    qcol = pl.BlockSpec((None, blk, 1), lambda h, i: (h, i, 0))
    return _call(
        body, name=name, grid=(heads, t // blk),
        in_specs=[qblk,
                  pl.BlockSpec((t, dh), lambda h, i: (0, h)),
                  pl.BlockSpec((t, dh), lambda h, i: (0, v_col + h)),
                  qcol,
                  pl.BlockSpec((None, 1, t), lambda h, i: (h, 0, 0))],
        out_specs=[qblk, qcol],
        out_shape=[jax.ShapeDtypeStruct((t, heads * dh), BF16), jax.ShapeDtypeStruct((heads, t, 1), F32)],
        scratch_shapes=[], semantics=("parallel", "arbitrary"), operands=(qn, kn, v_src, fcol, frow), job=job)


def _attn_delta(o, do, heads, dh, name):
    t = o.shape[0]
    tq = _tile(t, 512, 16)

    def body(o_ref, do_ref, out_ref):
        out_ref[...] = jnp.sum(o_ref[...].astype(F32) * do_ref[...].astype(F32), axis=-1, keepdims=True)

    blk = pl.BlockSpec((tq, dh), lambda i, h: (i, h))
    return pl.pallas_call(
        body, name=name, grid=(t // tq, heads),
        in_specs=[blk, blk],
        out_specs=pl.BlockSpec((None, tq, 1), lambda i, h: (h, i, 0)),
        out_shape=jax.ShapeDtypeStruct((heads, t, 1), F32),
        compiler_params=_cp(("parallel", "parallel")),
    )(o, do)


def _attn_bwd_q(qn, kn, v_src, v_col, do, fcol, frow, lse, delta, heads, dh, name, job=None):
    t = qn.shape[0]
    blk = _tile(t, ATTN_BLOCK)

    def body(q_ref, k_ref, v_ref, do_ref, fc_ref, fr_ref, lse_ref, dl_ref, dq_ref):
        qi = pl.program_id(1)
        q, dob = q_ref[...], do_ref[...].astype(BF16)
        f_first, lse, dl = fc_ref[0:1, :], lse_ref[...], dl_ref[...]

        def block(ki, dq, diagonal):
            keys = _block_at(ki, blk)
            k = k_ref[keys, :]
            p = jnp.exp(_attn_logits(q, k, fr_ref[:, keys], f_first, diagonal) - lse)
            dp = lax.dot_general(dob, v_ref[keys, :].astype(BF16), (((1,), (1,)), ((), ())), preferred_element_type=F32)
            ds = p * (dp - dl)
            return dq + jnp.dot(ds.astype(BF16), k, preferred_element_type=F32)

        below = lax.fori_loop(0, qi, lambda ki, dq: block(ki, dq, False), jnp.zeros((blk, dh), F32))
        dq_ref[...] = block(qi, below, True)

    qblk = pl.BlockSpec((blk, dh), lambda h, i: (i, h))
    qcol = pl.BlockSpec((None, blk, 1), lambda h, i: (h, i, 0))
    return _call(
        body, name=name, grid=(heads, t // blk),
        in_specs=[qblk,
                  pl.BlockSpec((t, dh), lambda h, i: (0, h)),
                  pl.BlockSpec((t, dh), lambda h, i: (0, v_col + h)),
                  qblk, qcol,
                  pl.BlockSpec((None, 1, t), lambda h, i: (h, 0, 0)),
                  qcol, qcol],
        out_specs=[qblk],
        out_shape=[jax.ShapeDtypeStruct((t, heads * dh), F32)],
        scratch_shapes=[], semantics=("parallel", "arbitrary"),
        operands=(qn, kn, v_src, do, fcol, frow, lse, delta), job=job)


def _attn_bwd_kv(qn, kn, v_src, v_col, do, fcol, frow, lse, delta, heads, dh, name, job=None):
    t = qn.shape[0]
    blk = _tile(t, ATTN_BLOCK)
    nq = t // blk
    tn_dims = (((0,), (0,)), ((), ()))

    def body(q_ref, k_ref, v_ref, do_ref, fc_ref, fr_ref, lse_ref, dl_ref, dk_ref, dv_ref, dfk_ref):
        ki = pl.program_id(1)
        k, v, f_keys = k_ref[...], v_ref[...].astype(BF16), fr_ref[...]

        def block(qi, carry, diagonal):
            dk, dv, dfk = carry
            rows = _block_at(qi, blk)
            q, dob = q_ref[rows, :], do_ref[rows, :].astype(BF16)
            f_first = fc_ref[pl.ds(pl.multiple_of(qi * blk, blk), 1), :]
            p = jnp.exp(_attn_logits(q, k, f_keys, f_first, diagonal) - lse_ref[rows, :])
            dp = lax.dot_general(dob, v, (((1,), (1,)), ((), ())), preferred_element_type=F32)
            ds = p * (dp - dl_ref[rows, :])
            dv = dv + lax.dot_general(p.astype(BF16), dob, tn_dims, preferred_element_type=F32)
            dk = dk + lax.dot_general(ds.astype(BF16), q, tn_dims, preferred_element_type=F32)
            return dk, dv, dfk + jnp.sum(ds, axis=0, keepdims=True)

        start = (jnp.zeros((blk, dh), F32), jnp.zeros((blk, dh), F32), jnp.zeros((1, blk), F32))
        dk, dv, dfk = lax.fori_loop(ki + 1, nq, lambda qi, carry: block(qi, carry, False), block(ki, start, True))
        dk_ref[...] = dk
        dv_ref[...] = dv.astype(dv_ref.dtype)
        dfk_ref[...] = dfk

    whole = pl.BlockSpec((t, dh), lambda h, j: (0, h))
    wcol = pl.BlockSpec((None, t, 1), lambda h, j: (h, 0, 0))
    kblk = pl.BlockSpec((blk, dh), lambda h, j: (j, h))
    krow = pl.BlockSpec((None, 1, blk), lambda h, j: (h, 0, j))
    return _call(
        body, name=name, grid=(heads, nq),
        in_specs=[whole, kblk, pl.BlockSpec((blk, dh), lambda h, j: (j, v_col + h)), whole, wcol, krow, wcol, wcol],
        out_specs=[kblk, kblk, krow],
        out_shape=[jax.ShapeDtypeStruct((t, heads * dh), F32), jax.ShapeDtypeStruct((t, heads * dh), BF16),
                   jax.ShapeDtypeStruct((heads, 1, t), F32)],
        scratch_shapes=[], semantics=("parallel", "arbitrary"),
        operands=(qn, kn, v_src, do, fcol, frow, lse, delta), job=job)


TIME_TILE = 8


def _s5_discretize(lam_re, lam_im, log_step, b_re, b_im):
    dt = jnp.exp(log_step)
    mag = jnp.exp(lam_re * dt)
    lb_re = mag * jnp.cos(lam_im * dt)
    lb_im = mag * jnp.sin(lam_im * dt)
    denom = lam_re * lam_re + lam_im * lam_im
    num_re = lb_re - 1.0
    fac_re = (num_re * lam_re + lb_im * lam_im) / denom
    fac_im = (lb_im * lam_re - num_re * lam_im) / denom
    return lb_re, lb_im, fac_re * b_re - fac_im * b_im, fac_re * b_im + fac_im * b_re


def _s5_prep(lam_re, lam_im, log_step, b_re, b_im, name):
    gp, width = b_re.shape

    def body(lr, li, ls, br, bi, o_lr, o_li, o_br, o_bi):
        res = _s5_discretize(lr[...], li[...], ls[...], br[...], bi[...])
        for ref, val in zip((o_lr, o_li, o_br, o_bi), res):
            ref[...] = val

    vm = pl.BlockSpec(memory_space=pltpu.VMEM)
    return pl.pallas_call(
        body, name=name, in_specs=[vm] * 5, out_specs=[vm] * 4,
        out_shape=[jax.ShapeDtypeStruct((gp, 1), F32)] * 2 + [jax.ShapeDtypeStruct((gp, width), F32)] * 2,
        compiler_params=_cp(),
    )(lam_re, lam_im, log_step, b_re, b_im)


def _s5_prep_bwd(lam_re, lam_im, log_step, b_re, b_im, d_lb_re, d_lb_im, d_bb_re, d_bb_im, groups, name):
    gp, width = b_re.shape
    states = gp // groups
    tr = _tile(gp, 512, 8)

    def body(lr, li, ls, br, bi, g_lr, g_li, g_br, g_bi, o_lr, o_li, o_ls, o_br, o_bi):
        _, vjp = jax.vjp(_s5_discretize, lr[...], li[...], ls[...], br[...], bi[...])
        d_lr, d_li, d_ls, d_br, d_bi = vjp((g_lr[...], g_li[...], g_br[...], g_bi[...]))
        o_lr[...] = d_lr
        o_li[...] = d_li
        o_br[...] = d_br
        o_bi[...] = d_bi
        row_group = (pl.program_id(0) * tr + lax.broadcasted_iota(jnp.int32, (tr, groups), 0)) // states
        col_group = lax.broadcasted_iota(jnp.int32, (tr, groups), 1)

        @pl.when(pl.program_id(0) == 0)
        def _():
            o_ls[...] = jnp.zeros_like(o_ls)

        o_ls[...] += jnp.sum(jnp.where(row_group == col_group, d_ls, 0.0), axis=0, keepdims=True)

    col = pl.BlockSpec((tr, 1), lambda i: (i, 0))
    mat = pl.BlockSpec((tr, width), lambda i: (i, 0))
    return pl.pallas_call(
        body, name=name, grid=(gp // tr,),
        in_specs=[col, col, col, mat, mat, col, col, mat, mat],
        out_specs=[col, col, pl.BlockSpec((1, groups), lambda i: (0, 0)), mat, mat],
        out_shape=[jax.ShapeDtypeStruct((gp, 1), F32)] * 2 + [jax.ShapeDtypeStruct((1, groups), F32)]
        + [jax.ShapeDtypeStruct((gp, width), F32)] * 2,
        compiler_params=_cp(("arbitrary",)),
    )(lam_re, lam_im, log_step, b_re, b_im, d_lb_re, d_lb_im, d_bb_re, d_bb_im)


def _shift_time(v, s, reverse):
    row = lax.broadcasted_iota(jnp.int32, v.shape, 0)
    if reverse:
        return jnp.where(row < TIME_TILE - s, pltpu.roll(v, TIME_TILE - s, 0), 0.0)
    return jnp.where(row >= s, pltpu.roll(v, s, 0), 0.0)


def _cmul(ar, ai, br, bi):
    return ar * br - ai * bi, ar * bi + ai * br


def _scan_time(xr_ref, xi_ref, ar, ai, reverse):
    t = xr_ref.shape[0]
    n_tiles = t // TIME_TILE
    powers = [(ar, ai)]
    for _ in range(TIME_TILE - 1):
        powers.append(_cmul(*powers[-1], ar, ai))
    order = powers[::-1] if reverse else powers
    carry_r = jnp.concatenate([p[0] for p in order], axis=0)
    carry_i = jnp.concatenate([p[1] for p in order], axis=0)
    levels = [(1, powers[0]), (2, powers[1]), (4, powers[3])]
    last = 0 if reverse else TIME_TILE - 1

    def tile(i, carry):
        cr, ci = carry
        idx = (n_tiles - 1 - i) if reverse else i
        rows = pl.ds(pl.multiple_of(idx * TIME_TILE, TIME_TILE), TIME_TILE)
        br, bi = xr_ref[rows, :], xi_ref[rows, :]
        for s, (pr, pi) in levels:
            sr, si = _cmul(pr, pi, _shift_time(br, s, reverse), _shift_time(bi, s, reverse))
            br, bi = br + sr, bi + si
        kr, ki = _cmul(carry_r, carry_i, cr, ci)
        br, bi = br + kr, bi + ki
        xr_ref[rows, :] = br
        xi_ref[rows, :] = bi
        return br[last:last + 1, :], bi[last:last + 1, :]

    zero = jnp.zeros_like(ar)
    lax.fori_loop(0, n_tiles, tile, (zero, zero))


def _s5_states(u_ref, bbr_ref, bbi_ref, ar_ref, ai_ref, xr, xi, chunk):
    t = u_ref.shape[0]
    for r0 in range(0, t, chunk):
        rows = pl.ds(r0, chunk)
        xr[rows, :] = jnp.dot(u_ref[rows, :], bbr_ref[...], preferred_element_type=F32)
        xi[rows, :] = jnp.dot(u_ref[rows, :], bbi_ref[...], preferred_element_type=F32)
    _scan_time(xr, xi, ar_ref[...], ai_ref[...], reverse=False)


def _s5_specs(t, nb_lanes, state_lanes):
    tok = pl.BlockSpec((t, nb_lanes), lambda j: (0, j))
    bb = pl.BlockSpec((None, nb_lanes, state_lanes), lambda j: (j, 0, 0))
    cc = pl.BlockSpec((None, state_lanes, nb_lanes), lambda j: (j, 0, 0))
    dvec = pl.BlockSpec((1, nb_lanes), lambda j: (0, j))
    avec = pl.BlockSpec((1, state_lanes), lambda j: (0, j))
    return tok, bb, cc, dvec, avec


def _s5_fwd(u5, bbr, bbi, ccr, cci, dskip, ar, ai, name, job=None):
    t, w = u5.shape
    nb, nb_lanes, state_lanes = bbr.shape
    chunk = _tile(t, 512, 16)

    def body(u_ref, bbr_ref, bbi_ref, cr_ref, ci_ref, d_ref, ar_ref, ai_ref, y_ref, xr, xi):
        _s5_states(u_ref, bbr_ref, bbi_ref, ar_ref, ai_ref, xr, xi, chunk)
        for r0 in range(0, t, chunk):
            rows = pl.ds(r0, chunk)
            y = jnp.dot(xr[rows, :].astype(BF16), cr_ref[...], preferred_element_type=F32)
            y = y - jnp.dot(xi[rows, :].astype(BF16), ci_ref[...], preferred_element_type=F32)
            y_ref[rows, :] = y + d_ref[...] * u_ref[rows, :].astype(F32)

    tok, bb, cc, dvec, avec = _s5_specs(t, nb_lanes, state_lanes)
    return _call(
        body, name=name, grid=(nb,),
        in_specs=[tok, bb, bb, cc, cc, dvec, avec, avec],
        out_specs=[tok],
        out_shape=[jax.ShapeDtypeStruct((t, w), F32)],
        scratch_shapes=[pltpu.VMEM((t, state_lanes), F32)] * 2,
        semantics=("parallel",), operands=(u5, bbr, bbi, ccr, cci, dskip, ar, ai), job=job)


def _s5_bwd(u5, dy, bbr, bbi, ccr, cci, dskip, ar, ai, name, job=None):
    t, w = u5.shape
    nb, nb_lanes, state_lanes = bbr.shape
    chunk = _tile(t, 512, 16)
    nt_dims = (((1,), (1,)), ((), ()))
    tn_dims = (((0,), (0,)), ((), ()))

    def body(u_ref, dy_ref, bbr_ref, bbi_ref, cr_ref, ci_ref, d_ref, ar_ref, ai_ref,
             du_ref, dbbr_ref, dbbi_ref, dcr_ref, dci_ref, dar_ref, dai_ref, dd_ref, xr, xi, gr, gi):
        _s5_states(u_ref, bbr_ref, bbi_ref, ar_ref, ai_ref, xr, xi, chunk)
        for r0 in range(0, t, chunk):
            rows = pl.ds(r0, chunk)
            dyb = dy_ref[rows, :].astype(BF16)
            gr[rows, :] = lax.dot_general(dyb, cr_ref[...], nt_dims, preferred_element_type=F32)
            gi[rows, :] = -lax.dot_general(dyb, ci_ref[...], nt_dims, preferred_element_type=F32)
        _scan_time(gr, gi, ar_ref[...], -ai_ref[...], reverse=True)

        dcr = jnp.zeros((state_lanes, nb_lanes), F32)
        dci = jnp.zeros((state_lanes, nb_lanes), F32)
        dbr = jnp.zeros((nb_lanes, state_lanes), F32)
        dbi = jnp.zeros((nb_lanes, state_lanes), F32)
        dd = jnp.zeros((1, nb_lanes), F32)
        for r0 in range(0, t, chunk):
            rows = pl.ds(r0, chunk)
            u = u_ref[rows, :]
            dyv = dy_ref[rows, :]
            dyb = dyv.astype(BF16)
            lr, li = gr[rows, :].astype(BF16), gi[rows, :].astype(BF16)
            dcr = dcr + lax.dot_general(xr[rows, :].astype(BF16), dyb, tn_dims, preferred_element_type=F32)
            dci = dci - lax.dot_general(xi[rows, :].astype(BF16), dyb, tn_dims, preferred_element_type=F32)
            dbr = dbr + lax.dot_general(u, lr, tn_dims, preferred_element_type=F32)
            dbi = dbi + lax.dot_general(u, li, tn_dims, preferred_element_type=F32)
            du = lax.dot_general(lr, bbr_ref[...], nt_dims, preferred_element_type=F32)
            du = du + lax.dot_general(li, bbi_ref[...], nt_dims, preferred_element_type=F32)
            du_ref[rows, :] = du + d_ref[...] * dyv
            dd = dd + jnp.sum(dyv * u.astype(F32), axis=0, keepdims=True)
        dcr_ref[...] = dcr
        dci_ref[...] = dci
        dbbr_ref[...] = dbr
        dbbi_ref[...] = dbi
        dd_ref[...] = dd

        first_row = lax.broadcasted_iota(jnp.int32, (TIME_TILE, state_lanes), 0) == 0

        def tile(i, carry):
            pr, pi, acc_r, acc_i = carry
            rows = pl.ds(pl.multiple_of(i * TIME_TILE, TIME_TILE), TIME_TILE)
            x_r, x_i, l_r, l_i = xr[rows, :], xi[rows, :], gr[rows, :], gi[rows, :]
            prev_r = jnp.where(first_row, pr, pltpu.roll(x_r, 1, 0))
            prev_i = jnp.where(first_row, pi, pltpu.roll(x_i, 1, 0))
            acc_r = acc_r + l_r * prev_r + l_i * prev_i
            acc_i = acc_i + l_i * prev_r - l_r * prev_i
            return x_r[TIME_TILE - 1:, :], x_i[TIME_TILE - 1:, :], acc_r, acc_i

        zrow = jnp.zeros((1, state_lanes), F32)
        ztile = jnp.zeros((TIME_TILE, state_lanes), F32)
        _, _, acc_r, acc_i = lax.fori_loop(0, t // TIME_TILE, tile, (zrow, zrow, ztile, ztile))
        dar_ref[...] = jnp.sum(acc_r, axis=0, keepdims=True)
        dai_ref[...] = jnp.sum(acc_i, axis=0, keepdims=True)

    tok, bb, cc, dvec, avec = _s5_specs(t, nb_lanes, state_lanes)
    return _call(
        body, name=name, grid=(nb,),
        in_specs=[tok, tok, bb, bb, cc, cc, dvec, avec, avec],
        out_specs=[tok, bb, bb, cc, cc, avec, avec, dvec],
        out_shape=[jax.ShapeDtypeStruct((t, w), F32)]
        + [jax.ShapeDtypeStruct((nb, nb_lanes, state_lanes), F32)] * 2
        + [jax.ShapeDtypeStruct((nb, state_lanes, nb_lanes), F32)] * 2
        + [jax.ShapeDtypeStruct((1, nb * state_lanes), F32)] * 2
        + [jax.ShapeDtypeStruct((1, w), F32)],
        scratch_shapes=[pltpu.VMEM((t, state_lanes), F32)] * 4,
        semantics=("parallel",), operands=(u5, dy, bbr, bbi, ccr, cci, dskip, ar, ai), job=job)


def _gelu(x):
    return 0.5 * x * (1.0 + jnp.tanh(GELU_C * (x + GELU_A * x * x * x)))


def _gelu_grad(x):
    th = jnp.tanh(GELU_C * (x + GELU_A * x * x * x))
    return 0.5 * (1.0 + th) + 0.5 * x * (1.0 - th * th) * GELU_C * (1.0 + 3.0 * GELU_A * x * x)


def _glu_fwd(y5, w, b, name):
    t, width = y5.shape
    tm = _tile(t, 512, 16)

    def body(y_ref, w_ref, b_ref, o_ref):
        g = _gelu(y_ref[...])
        a = jnp.dot(g.astype(BF16), w_ref[...], preferred_element_type=F32) + b_ref[...]
        o_ref[...] = (g * _sigmoid(a)).astype(o_ref.dtype)

    row = pl.BlockSpec((tm, width), lambda i: (i, 0))
    return pl.pallas_call(
        body, name=name, grid=(t // tm,),
        in_specs=[row, pl.BlockSpec((width, width), lambda i: (0, 0)), pl.BlockSpec((1, width), lambda i: (0, 0))],
        out_specs=row,
        out_shape=jax.ShapeDtypeStruct((t, width), BF16),
        compiler_params=_cp(("parallel",)),
    )(y5, w, b)


def _glu_bwd(y5, dout, w, b, name):
    t, width = y5.shape
    tm = _tile(t, 512, 16)

    def body(y_ref, do_ref, w_ref, b_ref, dy_ref, g_ref, da_ref, db_ref):
        y = y_ref[...]
        g = _gelu(y)
        s = _sigmoid(jnp.dot(g.astype(BF16), w_ref[...], preferred_element_type=F32) + b_ref[...])
        dout_v = do_ref[...].astype(F32)
        da = dout_v * g * s * (1.0 - s)
        dg = dout_v * s + lax.dot_general(da.astype(BF16), w_ref[...], (((1,), (1,)), ((), ())),
                                          preferred_element_type=F32)
        dy_ref[...] = dg * _gelu_grad(y)
        g_ref[...] = g.astype(g_ref.dtype)
        da_ref[...] = da.astype(da_ref.dtype)

        @pl.when(pl.program_id(0) == 0)
        def _():
            db_ref[...] = jnp.zeros_like(db_ref)

        db_ref[...] += jnp.sum(da, axis=0, keepdims=True)

    row = pl.BlockSpec((tm, width), lambda i: (i, 0))
    vec = pl.BlockSpec((1, width), lambda i: (0, 0))
    return pl.pallas_call(
        body, name=name, grid=(t // tm,),
        in_specs=[row, row, pl.BlockSpec((width, width), lambda i: (0, 0)), vec],
        out_specs=[row, row, row, vec],
        out_shape=[jax.ShapeDtypeStruct((t, width), F32), jax.ShapeDtypeStruct((t, width), BF16),
                   jax.ShapeDtypeStruct((t, width), BF16), jax.ShapeDtypeStruct((1, width), F32)],
        compiler_params=_cp(("arbitrary",)),
    )(y5, dout, w, b)


def _adamw(w, g, m, v):
    m = ADAM_B1 * m + (1.0 - ADAM_B1) * g
    v = ADAM_B2 * v + (1.0 - ADAM_B2) * (g * g)
    m_hat = m / (1.0 - ADAM_B1 ** ADAM_STEP)
    v_hat = v / (1.0 - ADAM_B2 ** ADAM_STEP)
    return -ADAM_LR * (m_hat / (jnp.sqrt(v_hat) + ADAM_EPS) + ADAM_WD * w), m, v


def _adamw_shard(w, m, v, sums, got, chip, name):
    rows, cols = w.shape
    wide = sums.shape[2]
    tr = _row_tile(rows, wide, target=2**20)

    def body(chip_ref, w_ref, m_ref, v_ref, s_ref, g0_ref, g1_ref, g2_ref, g_out, d_out, m_out, v_out):
        g = s_ref[...].astype(F32) + g0_ref[...].astype(F32) + g1_ref[...].astype(F32) + g2_ref[...].astype(F32)
        g = g[:, :cols]
        delta, m_new, v_new = _adamw(w_ref[...], g, m_ref[...], v_ref[...])
        g_out[...] = g
        d_out[...] = delta
        m_out[...] = m_new
        v_out[...] = v_new

    blk = pl.BlockSpec((tr, cols), lambda i, chip_ref: (i, 0))

    def part(k):
        return pl.BlockSpec((None, tr, wide), lambda i, chip_ref: (k, i, 0))

    return pl.pallas_call(
        body, name=name,
        grid_spec=pltpu.PrefetchScalarGridSpec(
            num_scalar_prefetch=1, grid=(rows // tr,),
            in_specs=[blk, blk, blk, pl.BlockSpec((None, tr, wide), lambda i, chip_ref: (chip_ref[0], i, 0)),
                      part(0), part(1), part(2)],
            out_specs=[blk] * 4),
        out_shape=[jax.ShapeDtypeStruct((rows, cols), F32)] * 4,
        compiler_params=_cp(("parallel",)),
    )(chip, w, m, v, sums, got, got, got)


def _adamw_packed(w, m, v, g, name):
    def body(w_ref, m_ref, v_ref, g_ref, d_out, m_out, v_out):
        delta, m_new, v_new = _adamw(w_ref[...], g_ref[...], m_ref[...], v_ref[...])
        d_out[...] = delta
        m_out[...] = m_new
        v_out[...] = v_new

    vm = pl.BlockSpec(memory_space=pltpu.VMEM)
    return pl.pallas_call(
        body, name=name, in_specs=[vm] * 4, out_specs=[vm] * 3,
        out_shape=[jax.ShapeDtypeStruct(w.shape, F32)] * 3,
        compiler_params=_cp(),
    )(w, m, v, g)


WEIGHTS = ("g_mix", "w_in", "b_fgate", "b_gates", "q_norm", "k_norm", "s5_lambda_re", "s5_lambda_im", "s5_log_step",
           "s5_b_re", "s5_b_im", "s5_c_re", "s5_c_im", "s5_d", "w_glu", "b_glu", "w_proj_fox", "w_proj_s5", "w_out",
           "g_ffn", "w_gate_up", "w_down")
COLUMN_SHARDED = ("w_in", "w_proj_fox", "w_proj_s5", "w_gate_up")
ROW_SHARDED = ("w_glu", "w_out", "w_down")
PACK_ROWS = 8 * LANES

def _pack(arrays):
    flat = jnp.concatenate([a.reshape(-1).astype(F32) for a in arrays])
    flat = jnp.pad(flat, (0, (-flat.shape[0]) % PACK_ROWS))
    return flat.reshape(-1, LANES)


def _unpack(packed, like):
    flat, out, at = packed.reshape(-1), [], 0
    for a in like:
        out.append(flat[at:at + a.size].reshape(a.shape))
        at += a.size
    return out


def _split_rows(rows, shares, unit=16):
    out, first = [], 0
    for k, share in enumerate(shares):
        count = rows - first if k == len(shares) - 1 else max(unit, int(rows * share) // unit * unit)
        out.append((first, count))
        first += count
    assert first == rows and all(c > 0 for _, c in out), (rows, out)
    return out


def _pad_lanes(a):
    return jnp.pad(a, ((0, 0), (0, LANES - a.shape[1])))


def kernel(x, g_mix, w_in, b_fgate, b_gates, q_norm, k_norm, s5_lambda_re, s5_lambda_im, s5_log_step, s5_b_re, s5_b_im,
           s5_c_re, s5_c_im, s5_d, w_glu, b_glu, w_proj_fox, w_proj_s5, w_out, g_ffn, w_gate_up, w_down,
           loss_target, m_g_mix, m_w_in, m_b_fgate, m_b_gates, m_q_norm, m_k_norm, m_s5_lambda_re,
           m_s5_lambda_im, m_s5_log_step, m_s5_b_re, m_s5_b_im, m_s5_c_re, m_s5_c_im, m_s5_d, m_w_glu,
           m_b_glu, m_w_proj_fox, m_w_proj_s5, m_w_out, m_g_ffn, m_w_gate_up, m_w_down, v_g_mix, v_w_in,
           v_b_fgate, v_b_gates, v_q_norm, v_k_norm, v_s5_lambda_re, v_s5_lambda_im, v_s5_log_step, v_s5_b_re,
           v_s5_b_im, v_s5_c_re, v_s5_c_im, v_s5_d, v_w_glu, v_b_glu, v_w_proj_fox, v_w_proj_s5, v_w_out,
           v_g_ffn, v_w_gate_up, v_w_down):
    given = dict(locals())
    weights = {n: given[n] for n in WEIGHTS}
    mom_m = {n: given["m_" + n] for n in WEIGHTS}
    mom_v = {n: given["v_" + n] for n in WEIGHTS}

    pos_x, pos_y, pos_c = _position()
    core = jnp.reshape(pos_c, (1,)).astype(jnp.int32)
    chip = jnp.reshape(2 * pos_x + pos_y, (1,)).astype(jnp.int32)

    xs, target = x[0], loss_target[0]
    t, d = xs.shape
    heads, dh = b_fgate.shape[-1], q_norm.shape[-1]
    fw = heads * dh
    groups, states, gwidth = s5_b_re.shape[1:]
    sw = groups * gwidth
    gp = groups * states
    assert dh == LANES and sw % LANES == 0 and LANES % gwidth == 0
    col_v, col_f, col_s5 = 3 * fw, 3 * fw + heads, 3 * fw + heads + sw

    shard = {n: weights[n][0].astype(BF16) for n in COLUMN_SHARDED + ROW_SHARDED}

    def whole(n, ag):
        if n in COLUMN_SHARDED:
            return ag.transpose(1, 0, 2).reshape(ag.shape[1], N_DEV * ag.shape[2])
        return ag.reshape(N_DEV * ag.shape[1], ag.shape[2])

    full = {}
    c_gu, r_dn = w_gate_up.shape[2], w_down.shape[1]
    assert c_gu == 2 * r_dn
    cp_gu = -(-c_gu // LANES) * LANES
    shard["w_gate_up"] = jnp.pad(shard["w_gate_up"], ((0, 0), (0, cp_gu - c_gu)))

    def down_rows(ag):
        gap = [jnp.zeros((cp_gu - c_gu, d), ag.dtype)] if cp_gu > c_gu else []
        return jnp.concatenate([p for b in range(N_DEV // 2) for p in [ag[2 * b], ag[2 * b + 1]] + gap], axis=0)

    c_in = w_in.shape[2]
    in_cols = N_DEV * c_in

    def in_pieces(lo, hi, take):
        cuts = [(k, max(lo, k * c_in), min(hi, (k + 1) * c_in)) for k in range(N_DEV)]
        return [take(k, a - k * c_in, b - k * c_in) for k, a, b in cuts if a < b]

    ag_in = _all_gather(shard["w_in"], "ag_w_in")

    def from_gathered(k, a, b):
        return ag_in[k][:, a:b]

    w_main = jnp.concatenate(in_pieces(0, col_v, from_gathered) + in_pieces(col_f, in_cols, from_gathered), axis=1)
    w_forget = _pad_lanes(jnp.concatenate(in_pieces(col_v, col_f, from_gathered), axis=1))
    z_s5, z_gate = 3 * fw, 3 * fw + sw

    u, r_mix = _rms_fwd(xs, g_mix, "rms_mix")
    early = ("w_proj_fox", "w_proj_s5", "w_glu", "w_out")
    gu_rows = _split_rows(d, (0.5625, 0.15625, 0.078125, 0.203125))

    def gate_up_piece(k, so_far):
        return _GatherJob([shard["w_gate_up"]], rows=gu_rows[k], into=so_far)

    z, gu_buf = _mm(u, w_main, "nn", BF16, "mm_z", job=gate_up_piece(0, None))
    zf = _mm(u, w_forget, "nn", F32, "mm_zf")
    qn, kn, r_q, r_k = _qk_prep(z, heads, dh, q_norm, k_norm, "qk_prep")
    b_forget = _pad_lanes(b_fgate)
    cum = _forget_fwd(zf, b_forget, "forget_fwd")
    cum_t = cum[:, :heads].T
    fcol, frow = cum_t[:, :, None], cum_t[:, None, :]
    (attn, lse), got = _attn_fwd(qn, kn, z, 2 * heads, fcol, frow, heads, dh, "attn_fwd",
                                 job=_GatherJob([shard[n] for n in early]))
    for n, ag in zip(early, got):
        full[n] = whole(n, ag)

    lam_re, lam_im = s5_lambda_re.reshape(gp, 1), s5_lambda_im.reshape(gp, 1)
    log_step = jnp.repeat(s5_log_step.reshape(groups, 1), states, axis=1).reshape(gp, 1)
    b_re, b_im = s5_b_re.reshape(gp, gwidth), s5_b_im.reshape(gp, gwidth)
    lb_re, lb_im, bb_re, bb_im = _s5_prep(lam_re, lam_im, log_step, b_re, b_im, "s5_prep")
    nb, per = sw // LANES, LANES // gwidth
    eye = jnp.eye(per, dtype=F32)

    def diag_b(bb):
        return jnp.einsum("napi,ab->naibp", bb.reshape(nb, per, states, gwidth), eye).reshape(nb, LANES, per * states)

    def diag_c(c):
        return jnp.einsum("naip,ab->nbpai", c.reshape(nb, per, gwidth, states), eye).reshape(nb, per * states, LANES)

    def undiag_b(g):
        return jnp.einsum("naibp,ab->napi", g.reshape(nb, per, gwidth, per, states), eye).reshape(gp, gwidth)

    def undiag_c(g):
        return jnp.einsum("nbpai,ab->naip", g.reshape(nb, per, states, per, gwidth), eye).reshape(1, groups, gwidth, states)

    bbr, bbi = diag_b(bb_re).astype(BF16), diag_b(bb_im).astype(BF16)
    ccr, cci = diag_c(s5_c_re[0]).astype(BF16), diag_c(s5_c_im[0]).astype(BF16)
    a_re, a_im = lb_re.reshape(1, gp), lb_im.reshape(1, gp)
    d_skip = s5_d.reshape(1, sw)
    u5 = z[:, z_s5:z_s5 + sw]
    (y5,), gu_buf = _s5_fwd(u5, bbr, bbi, ccr, cci, d_skip, a_re, a_im, "s5_fwd", job=gate_up_piece(1, gu_buf))
    ssm = _glu_fwd(y5, full["w_glu"], b_glu, "glu_fwd")

    pf, gu_buf = _mm(attn, full["w_proj_fox"], "nn", BF16, "mm_pf", job=gate_up_piece(2, gu_buf))
    ps = _mm(ssm, full["w_proj_s5"], "nn", BF16, "mm_ps")
    merged = _gate_merge_fwd(z, z_gate, b_gates, pf, ps, "merge_fwd")
    mo, gu_buf = _mm(merged, full["w_out"], "nn", F32, "mm_out", job=gate_up_piece(3, gu_buf))
    full["w_gate_up"] = gu_buf[0]
    h, hn, r_ffn = _resid_rms(xs, mo, g_ffn, "resid_rms")
    gu, got = _mm(hn, full["w_gate_up"], "nn", BF16, "mm_gu", job=_GatherJob([shard["w_down"]]), b_stacked=True)
    full["w_down"] = down_rows(got[0])
    act = _swiglu_fwd(gu, "swiglu_fwd")
    dn = _mm(act, full["w_down"], "nn", F32, "mm_down")
    loss_blk, dy, dy_b = _loss_head(h, dn, target, "loss_head")
    loss = lax.psum(loss_blk[0, 0], ("x", "y", "c"))

    grad, sums, from_chips = {}, {}, {}

    def pair_sums(n, parts=None):
        if parts is None:
            g_full = grad[n]
            if n in COLUMN_SHARDED:
                parts = g_full.reshape(g_full.shape[0], N_DEV, g_full.shape[1] // N_DEV).transpose(1, 0, 2)
            else:
                parts = g_full.reshape(N_DEV, g_full.shape[0] // N_DEV, g_full.shape[1])
        got = _swap_with_sibling(parts, "rs_sibling_" + n)
        return _add_sibling(parts, got, core, "rs_add_" + n)

    dact = _mm(dy_b, full["w_down"], "nt", BF16, "mm_dact")
    gw_down = _mm(act, dy_b, "tn", BF16, "mm_gw_down")
    parts = jnp.stack(
        [lax.slice_in_dim(gw_down, cp_gu * (k // 2) + r_dn * (k % 2), cp_gu * (k // 2) + r_dn * (k % 2) + r_dn)
         for k in range(N_DEV)])
    (dgate, dup), (got,) = _swiglu_bwd(gu, dact, "swiglu_bwd", job=_SiblingSwapJob([parts]))
    dgu = jnp.concatenate([dgate, dup], axis=1)
    sums["w_down"] = _add_sibling(parts, got, core, "rs_add_w_down")
    dhn, (from_chips["w_down"],) = _mm(dgu, full["w_gate_up"], "nt", F32, "mm_dhn", job=_ChipSwapJob([sums["w_down"]]),
                                       b_stacked=True)
    parts = _mm(hn, dgu, "tn", BF16, "mm_gw_gu", out_stack=N_DEV)
    dh_, dh_b, grad["g_ffn"] = _rms_bwd([dhn], h, r_ffn, g_ffn, [dy], "rms_ffn_bwd")
    dmerged, (got,) = _mm(dh_b, full["w_out"], "nt", BF16, "mm_dmerged", job=_SiblingSwapJob([parts]))
    sums["w_gate_up"] = _add_sibling(parts, got, core, "rs_add_w_gate_up")
    grad["w_out"] = _mm(merged, dh_b, "tn", BF16, "mm_gw_out")
    dpf, dps, dz_gf, dz_gs, db_gf, db_gs = _gate_merge_bwd(dmerged, z, z_gate, b_gates, pf, ps, "merge_bwd")
    grad["b_gates"] = jnp.concatenate([db_gf, db_gs], axis=1)
    dattn = _mm(dpf, full["w_proj_fox"], "nt", BF16, "mm_dattn")
    grad["w_proj_fox"] = _mm(attn, dpf, "tn", BF16, "mm_gw_pf")
    dssm = _mm(dps, full["w_proj_s5"], "nt", BF16, "mm_dssm")
    grad["w_proj_s5"] = _mm(ssm, dps, "tn", BF16, "mm_gw_ps")

    dy5, g5, da5, grad["b_glu"] = _glu_bwd(y5, dssm, full["w_glu"], b_glu, "glu_bwd")
    grad["w_glu"] = _mm(g5, da5, "tn", BF16, "mm_gw_glu")
    for n in early:
        sums[n] = pair_sums(n)
    (du5, d_bbr, d_bbi, d_ccr, d_cci, d_are, d_aim, d_dskip), got = _s5_bwd(
        u5, dy5, bbr, bbi, ccr, cci, d_skip, a_re, a_im, "s5_bwd", job=_ChipSwapJob([sums[n] for n in early]))
    from_chips.update(zip(early, got))
    d_lre, d_lim, d_lstep, d_bre, d_bim = _s5_prep_bwd(
        lam_re, lam_im, log_step, b_re, b_im, d_are.reshape(gp, 1), d_aim.reshape(gp, 1),
        undiag_b(d_bbr), undiag_b(d_bbi), groups, "s5_prep_bwd")
    grad["s5_lambda_re"], grad["s5_lambda_im"] = d_lre.reshape(1, groups, states), d_lim.reshape(1, groups, states)
    grad["s5_log_step"] = d_lstep
    grad["s5_b_re"], grad["s5_b_im"] = d_bre.reshape(s5_b_re.shape), d_bim.reshape(s5_b_im.shape)
    grad["s5_c_re"], grad["s5_c_im"] = undiag_c(d_ccr), undiag_c(d_cci)
    grad["s5_d"] = d_dskip.reshape(s5_d.shape)

    delta = _attn_delta(attn, dattn, heads, dh, "attn_delta")
    swap_rows = _split_rows(d, (0.36, 0.5, 0.14))

    def gate_up_swap(k, so_far):
        return _ChipSwapJob([sums["w_gate_up"]], rows=swap_rows[k], into=so_far)

    (dqn,), got = _attn_bwd_q(qn, kn, z, 2 * heads, dattn, fcol, frow, lse, delta, heads, dh, "attn_bwd_q",
                              job=gate_up_swap(0, None))
    (dkn, dv, df_k), got = _attn_bwd_kv(qn, kn, z, 2 * heads, dattn, fcol, frow, lse, delta, heads, dh, "attn_bwd_kv",
                                        job=gate_up_swap(1, got))
    dq, dk, grad["q_norm"], grad["k_norm"] = _qk_prep_bwd(dqn, dkn, z, heads, dh, q_norm, k_norm, r_q, r_k, "qk_prep_bwd")
    dzf, db_forget = _forget_bwd(_pad_lanes(df_k[:, 0, :].T), zf, b_forget, "forget_bwd")
    grad["b_fgate"] = db_forget[:, :heads]

    dz = jnp.concatenate([dq, dk, dv, du5.astype(BF16), dz_gf, dz_gs], axis=1)
    gw_main, (from_chips["w_gate_up"],) = _mm(u, dz, "tn", BF16, "mm_gw_main", job=gate_up_swap(2, got))
    gw_forget = _mm(u, dzf, "tn", BF16, "mm_gw_forget")

    def from_grads(k, a, b):
        lo, hi = k * c_in + a, k * c_in + b
        if hi <= col_v:
            return gw_main[:, lo:hi]
        if hi <= col_f:
            return gw_forget[:, lo - col_v:hi - col_v]
        return gw_main[:, lo - heads:hi - heads]

    def in_part(k):
        lo, hi = k * c_in, (k + 1) * c_in
        cuts = [(max(lo, a), min(hi, b)) for a, b in ((0, col_v), (col_v, col_f), (col_f, in_cols))]
        return jnp.concatenate([from_grads(k, a - lo, b - lo) for a, b in cuts if a < b], axis=1)

    sums["w_in"] = pair_sums("w_in", jnp.stack([in_part(k) for k in range(N_DEV)]))
    du, (from_chips["w_in"],) = _mm(dz, w_main, "nt", F32, "mm_du", job=_ChipSwapJob([sums["w_in"]]))
    du_f = _mm(dzf, w_forget, "nt", F32, "mm_du_f")
    dx, _, grad["g_mix"] = _rms_bwd([du, du_f], xs, r_mix, g_mix, [dh_], "rms_mix_bwd")

    out_g, out_d, out_m, out_v = {}, {}, {}, {}
    for n in COLUMN_SHARDED + ROW_SHARDED:
        res = _adamw_shard(weights[n][0], mom_m[n][0], mom_v[n][0], sums[n], from_chips[n], chip, "adamw_" + n)
        out_g[n], out_d[n], out_m[n], out_v[n] = (r[None] for r in res)

    small = [n for n in WEIGHTS if n not in COLUMN_SHARDED + ROW_SHARDED]
    g_small = _all_reduce_small(_pack([grad[n] for n in small]), "ar_small")
    like = [weights[n] for n in small]
    res = _adamw_packed(_pack(like), _pack([mom_m[n] for n in small]), _pack([mom_v[n] for n in small]), g_small,
                        "adamw_small")
    for store, packed in zip((out_g, out_d, out_m, out_v), (g_small, *res)):
        for n, a in zip(small, _unpack(packed, like)):
            store[n] = a

    return (loss, dx[None], *[out_g[n] for n in WEIGHTS], *[out_d[n] for n in WEIGHTS],
            *[out_m[n] for n in WEIGHTS], *[out_v[n] for n in WEIGHTS])
```

```python
import functools
import math

import jax
import jax.numpy as jnp
from jax import lax
from jax.experimental import pallas as pl
from jax.experimental.pallas import tpu as pltpu

F32 = jnp.float32
BF16 = jnp.bfloat16

V7X_VMEM_LIMIT = 56 * 2**20
LANES = 128
N_DEV = 8
MESH = pl.DeviceIdType.MESH

RMS_EPS = 1e-6
MASK_VALUE = -1e30
ADAM_LR, ADAM_B1, ADAM_B2, ADAM_EPS, ADAM_WD, ADAM_STEP = 0.001, 0.9, 0.999, 1e-08, 0.01, 10
GELU_C = math.sqrt(2.0 / math.pi)
GELU_A = 0.044715


def _cp(sem=None):
    return pltpu.CompilerParams(dimension_semantics=sem, vmem_limit_bytes=V7X_VMEM_LIMIT)


def _tile(n, pref, unit=LANES):
    if n <= pref:
        return n
    t = (pref // unit) * unit
    while t >= unit:
        if n % t == 0:
            return t
        t -= unit
    raise ValueError(f"no tile for {n}")


def _row_tile(rows, cols, bytes_per_row_elem=4, target=2 * 2**20, unit=16):
    best = None
    for t in range(unit, rows + 1, unit):
        if rows % t == 0 and t * cols * bytes_per_row_elem <= target:
            best = t
    if best is None:
        best = unit if rows % unit == 0 else rows
    return best


def _sigmoid(x):
    return 1.0 / (1.0 + jnp.exp(-x))


def _position():
    return lax.axis_index("x"), lax.axis_index("y"), lax.axis_index("c")


def _other_chips(x, y):
    return [(1 - x, y), (x, 1 - y), (1 - x, 1 - y)]


def _all_gather(shard, name):
    job = _GatherJob([shard])

    def body(x_ref, out_ref, *scratch):
        job.begin([x_ref], [out_ref], scratch)
        job.middle([x_ref], [out_ref], scratch)
        job.end([x_ref], [out_ref], scratch)

    return pl.pallas_call(
        body, name=name,
        out_shape=job.out_shape[0],
        in_specs=[pl.BlockSpec(memory_space=pltpu.HBM)],
        out_specs=pl.BlockSpec(memory_space=pltpu.HBM),
        scratch_shapes=job.scratch,
    )(shard)


def _swap_with_sibling(parts, name):
    _, rows, cols = parts.shape

    def body(p_ref, out_ref, send_sems, recv_sems):
        x, y, c = _position()
        copies = []
        for j in range(4):
            copies.append(pltpu.make_async_remote_copy(
                src_ref=p_ref.at[2 * j + (1 - c)], dst_ref=out_ref.at[j],
                send_sem=send_sems.at[j], recv_sem=recv_sems.at[j], device_id=(x, y, 1 - c), device_id_type=MESH))
        for cp in copies:
            cp.start()
        for cp in copies:
            cp.wait()

    return pl.pallas_call(
        body, name=name,
        out_shape=jax.ShapeDtypeStruct((4, rows, cols), parts.dtype),
        in_specs=[pl.BlockSpec(memory_space=pltpu.HBM)],
        out_specs=pl.BlockSpec(memory_space=pltpu.HBM),
        scratch_shapes=[pltpu.SemaphoreType.DMA((4,)), pltpu.SemaphoreType.DMA((4,))],
    )(parts)


def _add_sibling(parts, got, core, name):
    _, rows, cols = parts.shape
    tr = _row_tile(rows, cols)

    def body(core_ref, a_ref, b_ref, o_ref):
        o_ref[...] = (a_ref[...].astype(F32) + b_ref[...].astype(F32)).astype(o_ref.dtype)

    return pl.pallas_call(
        body, name=name,
        grid_spec=pltpu.PrefetchScalarGridSpec(
            num_scalar_prefetch=1, grid=(4, rows // tr),
            in_specs=[pl.BlockSpec((None, tr, cols), lambda j, i, core_ref: (2 * j + core_ref[0], i, 0)),
                      pl.BlockSpec((None, tr, cols), lambda j, i, core_ref: (j, i, 0))],
            out_specs=pl.BlockSpec((None, tr, cols), lambda j, i, core_ref: (j, i, 0))),
        out_shape=jax.ShapeDtypeStruct((4, rows, cols), BF16),
        compiler_params=_cp(("parallel", "parallel")),
    )(core, parts, got)


def _all_reduce_small(packed, name):
    rows, cols = packed.shape

    def body(x_ref, out_ref, gathered, send_sems, recv_sems):
        x, y, c = _position()
        me, sibling = (x, y, c), (x, y, 1 - c)
        chips = _other_chips(x, y)

        def slot(px, py, pc):
            return gathered.at[4 * px + 2 * py + pc]

        def copy(k, block, to, src=None):
            return pltpu.make_async_remote_copy(
                src_ref=slot(*block) if src is None else src, dst_ref=slot(*block),
                send_sem=send_sems.at[k], recv_sem=recv_sems.at[k], device_id=to, device_id_type=MESH)

        first = [copy(0, me, sibling, src=x_ref)]
        first += [copy(1 + j, me, (*chip, c), src=x_ref) for j, chip in enumerate(chips)]
        for cp in first:
            cp.start()
        passed = [copy(4 + j, (*chip, c), sibling) for j, chip in enumerate(chips)]
        for j, chip in enumerate(chips):
            copy(1 + j, (*chip, c), me).wait_recv()
            passed[j].start()
        copy(0, sibling, me).wait_recv()
        for j, chip in enumerate(chips):
            copy(4 + j, (*chip, 1 - c), me).wait_recv()
        for cp in first + passed:
            cp.wait_send()
        gathered[4 * x + 2 * y + c] = x_ref[...]
        total = gathered[0]
        for k in range(1, N_DEV):
            total = total + gathered[k]
        out_ref[...] = total

    return pl.pallas_call(
        body, name=name,
        out_shape=jax.ShapeDtypeStruct((rows, cols), F32),
        in_specs=[pl.BlockSpec(memory_space=pltpu.VMEM)],
        out_specs=pl.BlockSpec(memory_space=pltpu.VMEM),
        scratch_shapes=[pltpu.VMEM((N_DEV, rows, cols), F32),
                        pltpu.SemaphoreType.DMA((7,)), pltpu.SemaphoreType.DMA((7,))],
        compiler_params=pltpu.CompilerParams(vmem_limit_bytes=V7X_VMEM_LIMIT),
    )(packed)


class _GatherJob:
    def __init__(self, shards, rows=None, into=None):
        self.n, self.rows = len(shards), rows
        self.inputs = list(shards) + list(into or [])
        self.aliases = {self.n + i: i for i in range(len(into or []))}
        self.out_shape = [jax.ShapeDtypeStruct((N_DEV,) + s.shape, s.dtype) for s in shards]
        self.scratch = [pltpu.SemaphoreType.DMA((self.COPIES * self.n,)),
                        pltpu.SemaphoreType.DMA((self.COPIES * self.n,)), pltpu.SemaphoreType.DMA((self.n,))]
        self.whole = [rows or (0, s.shape[0]) for s in shards]
        self.first = [(lo, max(16, n // 32 * 16)) if n > 16 else (lo, n) for lo, n in self.whole]
        self.second = [(lo + h, n - h) for (lo, n), (_, h) in zip(self.whole, self.first)]

    COPIES = 8
    TO_SIBLING, TO_X, TO_Y, RELAY_X_BLOCK, RELAY_Y_BLOCK, PASS_X, PASS_Y, PASS_DIAGONAL = range(8)

    def _plan(self, ins, outs, scratch):
        send_sems, recv_sems, local_sems = scratch
        x, y, c = _position()

        def slot(i, block, rows):
            px, py, pc = block
            return _rows_of(outs[i].at[4 * px + 2 * py + pc], rows)

        def copy(i, k, block, to, rows, own=False):
            return pltpu.make_async_remote_copy(
                src_ref=_rows_of(ins[i], rows) if own else slot(i, block, rows), dst_ref=slot(i, block, rows),
                send_sem=send_sems.at[self.COPIES * i + k], recv_sem=recv_sems.at[self.COPIES * i + k],
                device_id=to, device_id_type=MESH)

        def mine(i):
            return pltpu.make_async_copy(_rows_of(ins[i], self.whole[i]), slot(i, (x, y, c), self.whole[i]),
                                         local_sems.at[i])

        return x, y, c, copy, mine

    def begin(self, ins, outs, scratch):
        x, y, c, copy, mine = self._plan(ins, outs, scratch)
        me = (x, y, c)
        for i in range(self.n):
            mine(i).start()
            copy(i, self.TO_SIBLING, me, (x, y, 1 - c), self.whole[i], own=True).start()
            copy(i, self.TO_X, me, (1 - x, y, c), self.whole[i], own=True).start()
            copy(i, self.TO_Y, me, (x, 1 - y, c), self.whole[i], own=True).start()

    def middle(self, ins, outs, scratch):
        x, y, c, copy, mine = self._plan(ins, outs, scratch)
        me, sibling, x_chip, y_chip = (x, y, c), (x, y, 1 - c), (1 - x, y, c), (x, 1 - y, c)
        for i in range(self.n):
            copy(i, self.TO_X, x_chip, me, self.whole[i]).wait_recv()
            copy(i, self.RELAY_X_BLOCK, x_chip, y_chip, self.first[i]).start()
            copy(i, self.PASS_X, x_chip, sibling, self.whole[i]).start()
            copy(i, self.TO_Y, y_chip, me, self.whole[i]).wait_recv()
            if self.second[i][1]:
                copy(i, self.RELAY_Y_BLOCK, y_chip, x_chip, self.second[i]).start()
            copy(i, self.PASS_Y, y_chip, sibling, self.whole[i]).start()

    def end(self, ins, outs, scratch):
        x, y, c, copy, mine = self._plan(ins, outs, scratch)
        me, sibling, x_chip, y_chip, diagonal = (x, y, c), (x, y, 1 - c), (1 - x, y, c), (x, 1 - y, c), (1 - x, 1 - y, c)
        for i in range(self.n):
            copy(i, self.RELAY_X_BLOCK, diagonal, me, self.first[i]).wait_recv()
            if self.second[i][1]:
                copy(i, self.RELAY_Y_BLOCK, diagonal, me, self.second[i]).wait_recv()
            copy(i, self.PASS_DIAGONAL, diagonal, sibling, self.whole[i]).start()
        for i in range(self.n):
            copy(i, self.TO_SIBLING, sibling, me, self.whole[i]).wait_recv()
            copy(i, self.PASS_X, (1 - x, y, 1 - c), me, self.whole[i]).wait_recv()
            copy(i, self.PASS_Y, (x, 1 - y, 1 - c), me, self.whole[i]).wait_recv()
            copy(i, self.PASS_DIAGONAL, (1 - x, 1 - y, 1 - c), me, self.whole[i]).wait_recv()
            copy(i, self.TO_SIBLING, me, sibling, self.whole[i], own=True).wait_send()
            copy(i, self.TO_X, me, x_chip, self.whole[i], own=True).wait_send()
            copy(i, self.TO_Y, me, y_chip, self.whole[i], own=True).wait_send()
            copy(i, self.RELAY_X_BLOCK, x_chip, y_chip, self.first[i]).wait_send()
            if self.second[i][1]:
                copy(i, self.RELAY_Y_BLOCK, y_chip, x_chip, self.second[i]).wait_send()
            copy(i, self.PASS_X, x_chip, sibling, self.whole[i]).wait_send()
            copy(i, self.PASS_Y, y_chip, sibling, self.whole[i]).wait_send()
            copy(i, self.PASS_DIAGONAL, diagonal, sibling, self.whole[i]).wait_send()
            mine(i).wait()


class _ChipSwapJob:
    def __init__(self, sums, rows=None, into=None):
        self.n, self.rows = len(sums), rows
        self.inputs = list(sums) + list(into or [])
        self.aliases = {self.n + i: i for i in range(len(into or []))}
        self.out_shape = [jax.ShapeDtypeStruct((3,) + s.shape[1:], s.dtype) for s in sums]
        self.scratch = [pltpu.SemaphoreType.DMA((3 * self.n,)), pltpu.SemaphoreType.DMA((3 * self.n,))]

    def _copies(self, ins, outs, scratch):
        send_sems, recv_sems = scratch
        x, y, c = _position()
        return [pltpu.make_async_remote_copy(
            src_ref=_rows_of(ins[i].at[2 * px + py], self.rows), dst_ref=_rows_of(outs[i].at[k], self.rows),
            send_sem=send_sems.at[3 * i + k], recv_sem=recv_sems.at[3 * i + k],
            device_id=(px, py, c), device_id_type=MESH)
            for i in range(self.n) for k, (px, py) in enumerate(_other_chips(x, y))]

    def begin(self, ins, outs, scratch):
        for cp in self._copies(ins, outs, scratch):
            cp.start()

    def middle(self, ins, outs, scratch):
        pass

    def end(self, ins, outs, scratch):
        for cp in self._copies(ins, outs, scratch):
            cp.wait()


class _SiblingSwapJob:
    aliases = {}

    def __init__(self, parts):
        self.inputs = list(parts)
        self.out_shape = [jax.ShapeDtypeStruct((4,) + p.shape[1:], p.dtype) for p in parts]
        n = len(parts)
        self.scratch = [pltpu.SemaphoreType.DMA((4 * n,)), pltpu.SemaphoreType.DMA((4 * n,))]

    def _copies(self, ins, outs, scratch):
        send_sems, recv_sems = scratch
        x, y, c = _position()
        return [pltpu.make_async_remote_copy(
            src_ref=ins[i].at[2 * j + (1 - c)], dst_ref=outs[i].at[j], send_sem=send_sems.at[4 * i + j],
            recv_sem=recv_sems.at[4 * i + j], device_id=(x, y, 1 - c), device_id_type=MESH)
            for i in range(len(ins)) for j in range(4)]

    def begin(self, ins, outs, scratch):
        for cp in self._copies(ins, outs, scratch):
            cp.start()

    def middle(self, ins, outs, scratch):
        pass

    def end(self, ins, outs, scratch):
        for cp in self._copies(ins, outs, scratch):
            cp.wait()


class _JobGroup:
    def __init__(self, jobs):
        self.jobs = list(jobs)
        self.inputs = [a for j in jobs for a in j.inputs]
        self.out_shape = [s for j in jobs for s in j.out_shape]
        self.scratch = [s for j in jobs for s in j.scratch]
        self.aliases, at_in, at_out = {}, 0, 0
        for j in jobs:
            self.aliases.update({at_in + i: at_out + o for i, o in j.aliases.items()})
            at_in, at_out = at_in + len(j.inputs), at_out + len(j.out_shape)

    def _each(self, phase, ins, outs, scratch):
        for j in self.jobs:
            n_in, n_out, n_scr = len(j.inputs), len(j.out_shape), len(j.scratch)
            getattr(j, phase)(ins[:n_in], outs[:n_out], scratch[:n_scr])
            ins, outs, scratch = ins[n_in:], outs[n_out:], scratch[n_scr:]

    def begin(self, ins, outs, scratch):
        self._each("begin", ins, outs, scratch)

    def middle(self, ins, outs, scratch):
        self._each("middle", ins, outs, scratch)

    def end(self, ins, outs, scratch):
        self._each("end", ins, outs, scratch)


def _rows_of(ref, rows):
    return ref if rows is None else ref.at[pl.ds(rows[0], rows[1])]


def _call(body, *, name, grid, in_specs, out_specs, out_shape, scratch_shapes, semantics, operands, job=None):
    if job is None:
        return pl.pallas_call(
            body, name=name, grid=grid, in_specs=in_specs, out_specs=out_specs, out_shape=out_shape,
            scratch_shapes=scratch_shapes, compiler_params=_cp(semantics))(*operands)
    n_in, n_out, n_scr = len(in_specs), len(out_specs), len(scratch_shapes)
    j_in, j_out = len(job.inputs), len(job.out_shape)
    n_steps = math.prod(grid)
    hbm = pl.BlockSpec(memory_space=pltpu.HBM)

    def carrier(*refs):
        ins, refs = refs[:n_in], refs[n_in:]
        job_ins, refs = refs[:j_in], refs[j_in:]
        outs, refs = refs[:n_out], refs[n_out:]
        job_outs, refs = refs[:j_out], refs[j_out:]
        scr, job_scr = refs[:n_scr], refs[n_scr:]
        step = pl.program_id(0)
        for axis in range(1, len(grid)):
            step = step * grid[axis] + pl.program_id(axis)

        @pl.when(step == 0)
        def _():
            job.begin(job_ins, job_outs, job_scr)

        body(*ins, *outs, *scr)

        @pl.when(step == (3 * n_steps) // 4)
        def _():
            job.middle(job_ins, job_outs, job_scr)

        @pl.when(step == n_steps - 1)
        def _():
            job.end(job_ins, job_outs, job_scr)

    res = pl.pallas_call(
        carrier, name=name, grid=grid,
        in_specs=list(in_specs) + [hbm] * j_in, out_specs=list(out_specs) + [hbm] * j_out,
        out_shape=list(out_shape) + job.out_shape, scratch_shapes=list(scratch_shapes) + job.scratch,
        input_output_aliases={n_in + i: n_out + o for i, o in job.aliases.items()},
        compiler_params=_cp(("arbitrary",) * len(grid)))(*operands, *job.inputs)
    return res[:n_out], res[n_out:]


MM_VMEM_BUDGET = 44 * 2**20
MM_TILE_CAP = 1536
MM_MIN_INTENSITY = 340


def _divisor_tiles(n, cap):
    return [t for t in range(LANES, min(n, cap) + 1, LANES) if n % t == 0] or [n]


def _mm_tiles(m, n_unit, k_unit, whole_k, a_bytes, b_bytes, o_bytes):
    best, best_key = None, None
    for tm in _divisor_tiles(m, 1024):
        for tn in _divisor_tiles(n_unit, MM_TILE_CAP):
            for tk in _divisor_tiles(k_unit, k_unit):
                one_block = whole_k and tk == k_unit
                need = (2 * (tm * tk * a_bytes + tk * tn * b_bytes) + 2 * tm * tn * o_bytes + tm * tn * 4
                        + (0 if one_block else tm * tn * 4))
                intensity = tm * tn / (tm + tn)
                key = (intensity >= MM_MIN_INTENSITY, one_block, intensity, tk)
                if need <= MM_VMEM_BUDGET and (best_key is None or key > best_key):
                    best, best_key = (tm, tn, tk), key
    if best is None:
        raise ValueError(f"no matmul tiles for {(m, n_unit, k_unit)}")
    return best


def _mm(a, b, mode, out_dtype, name, job=None, b_stacked=False, out_stack=None):
    b_rows, b_cols = (b.shape[1], b.shape[0] * b.shape[2]) if b_stacked else b.shape
    b_unit = b.shape[2] if b_stacked else b_cols
    if mode == "nn":
        (m, k), (k2, n) = a.shape, (b_rows, b_cols)
    elif mode == "nt":
        (m, k), (n, k2) = a.shape, (b_rows, b_cols)
    else:
        (k, m), (k2, n) = a.shape, (b_rows, b_cols)
    assert k == k2, (name, a.shape, b.shape)
    n_unit = n // out_stack if out_stack else (b_unit if b_stacked and mode != "nt" else n)
    k_unit = b_unit if b_stacked and mode == "nt" else k
    tm, tn, tk = _mm_tiles(m, n_unit, k_unit, k_unit == k, a.dtype.itemsize, b.dtype.itemsize,
                           jnp.dtype(out_dtype).itemsize)
    nk = k // tk
    per_n, per_k = n_unit // tn, k_unit // tk
    if mode == "tn":
        a_spec = pl.BlockSpec((tk, tm), lambda i, j, l: (l, i))
        dims = (((0,), (0,)), ((), ()))
    else:
        a_spec = pl.BlockSpec((tm, tk), lambda i, j, l: (i, l))
        dims = (((1,), (1,)), ((), ())) if mode == "nt" else (((1,), (0,)), ((), ()))
    if mode == "nt" and b_stacked:
        b_spec = pl.BlockSpec((None, tn, tk), lambda i, j, l: (l // per_k, j, l % per_k))
    elif mode == "nt":
        b_spec = pl.BlockSpec((tn, tk), lambda i, j, l: (j, l))
    elif b_stacked:
        b_spec = pl.BlockSpec((None, tk, tn), lambda i, j, l: (j // per_n, l, j % per_n))
    else:
        b_spec = pl.BlockSpec((tk, tn), lambda i, j, l: (l, j))
    if out_stack:
        o_spec = pl.BlockSpec((None, tm, tn), lambda i, j, l: (j // per_n, i, j % per_n))
        o_shape = jax.ShapeDtypeStruct((out_stack, m, n_unit), out_dtype)
    else:
        o_spec = pl.BlockSpec((tm, tn), lambda i, j, l: (i, j))
        o_shape = jax.ShapeDtypeStruct((m, n), out_dtype)

    def product(a_ref, b_ref):
        return lax.dot_general(a_ref[...].astype(BF16), b_ref[...].astype(BF16), dims, preferred_element_type=F32)

    def body_whole_k(a_ref, b_ref, o_ref):
        o_ref[...] = product(a_ref, b_ref).astype(o_ref.dtype)

    def body_split_k(a_ref, b_ref, o_ref, acc_ref):
        l = pl.program_id(2)

        @pl.when(l == 0)
        def _():
            acc_ref[...] = product(a_ref, b_ref)

        @pl.when(l > 0)
        def _():
            acc_ref[...] += product(a_ref, b_ref)

        @pl.when(l == nk - 1)
        def _():
            o_ref[...] = acc_ref[...].astype(o_ref.dtype)

    res = _call(
        body_whole_k if nk == 1 else body_split_k, name=name, grid=(m // tm, n // tn, nk),
        in_specs=[a_spec, b_spec],
        out_specs=[o_spec],
        out_shape=[o_shape],
        scratch_shapes=[] if nk == 1 else [pltpu.VMEM((tm, tn), F32)],
        semantics=("parallel", "parallel", "arbitrary"), operands=(a, b), job=job)
    return res[0] if job is None else (res[0][0], res[1])


def _rms_fwd(x, g, name):
    t, d = x.shape
    tm = _tile(t, 256, 16)

    def body(x_ref, g_ref, u_ref, r_ref):
        xv = x_ref[...]
        r = lax.rsqrt(jnp.mean(xv * xv, axis=-1, keepdims=True) + RMS_EPS)
        u_ref[...] = (xv * r * g_ref[...]).astype(u_ref.dtype)
        r_ref[...] = r

    return pl.pallas_call(
        body, name=name, grid=(t // tm,),
        in_specs=[pl.BlockSpec((tm, d), lambda i: (i, 0)), pl.BlockSpec((1, d), lambda i: (0, 0))],
        out_specs=[pl.BlockSpec((tm, d), lambda i: (i, 0)), pl.BlockSpec((tm, 1), lambda i: (i, 0))],
        out_shape=[jax.ShapeDtypeStruct((t, d), BF16), jax.ShapeDtypeStruct((t, 1), F32)],
        compiler_params=_cp(("parallel",)),
    )(x, g)


def _rms_bwd(dn_parts, x, r, g, extra, name, job=None):
    t, d = x.shape
    tm = _tile(t, 128, 16)
    n_dn, n_extra = len(dn_parts), len(extra)

    def body(*refs):
        dn_refs = refs[:n_dn]
        x_ref, r_ref, g_ref = refs[n_dn:n_dn + 3]
        extra_refs = refs[n_dn + 3:n_dn + 3 + n_extra]
        dx_ref, dxb_ref, dg_ref = refs[n_dn + 3 + n_extra:]
        xhat = x_ref[...] * r_ref[...]
        dnv = dn_refs[0][...].astype(F32)
        for p in dn_refs[1:]:
            dnv = dnv + p[...].astype(F32)
        gd = dnv * g_ref[...]
        dx = r_ref[...] * (gd - xhat * jnp.mean(gd * xhat, axis=-1, keepdims=True))
        for e in extra_refs:
            dx = dx + e[...].astype(F32)
        dx_ref[...] = dx
        dxb_ref[...] = dx.astype(dxb_ref.dtype)

        @pl.when(pl.program_id(0) == 0)
        def _():
            dg_ref[...] = jnp.zeros_like(dg_ref)

        dg_ref[...] += jnp.sum(dnv * xhat, axis=0, keepdims=True)

    row = pl.BlockSpec((tm, d), lambda i: (i, 0))
    return _call(
        body, name=name, grid=(t // tm,),
        in_specs=[row] * n_dn + [row, pl.BlockSpec((tm, 1), lambda i: (i, 0)), pl.BlockSpec((1, d), lambda i: (0, 0))]
        + [row] * n_extra,
        out_specs=[row, row, pl.BlockSpec((1, d), lambda i: (0, 0))],
        out_shape=[jax.ShapeDtypeStruct((t, d), F32), jax.ShapeDtypeStruct((t, d), BF16), jax.ShapeDtypeStruct((1, d), F32)],
        scratch_shapes=[], semantics=("arbitrary",), operands=(*dn_parts, x, r, g, *extra), job=job)


def _gate_merge_fwd(z, gate_col, b_gates, pf, ps, name):
    t, d = pf.shape
    tm, tn = _tile(t, 512, 16), _tile(math.gcd(d, gate_col), 512)
    nj, off = d // tn, gate_col // tn
    assert gate_col % tn == 0

    def body(zf_ref, zs_ref, bf_ref, bs_ref, pf_ref, ps_ref, o_ref):
        gf = _sigmoid(zf_ref[...].astype(F32) + bf_ref[...])
        gs = _sigmoid(zs_ref[...].astype(F32) + bs_ref[...])
        o_ref[...] = (gf * pf_ref[...].astype(F32) + gs * ps_ref[...].astype(F32)).astype(o_ref.dtype)

    blk = pl.BlockSpec((tm, tn), lambda i, j: (i, j))
    return pl.pallas_call(
        body, name=name, grid=(t // tm, nj),
        in_specs=[pl.BlockSpec((tm, tn), lambda i, j: (i, off + j)), pl.BlockSpec((tm, tn), lambda i, j: (i, off + nj + j)),
                  pl.BlockSpec((1, tn), lambda i, j: (0, j)), pl.BlockSpec((1, tn), lambda i, j: (0, nj + j)), blk, blk],
        out_specs=blk,
        out_shape=jax.ShapeDtypeStruct((t, d), BF16),
        compiler_params=_cp(("parallel", "parallel")),
    )(z, z, b_gates, b_gates, pf, ps)


def _gate_merge_bwd(dm, z, gate_col, b_gates, pf, ps, name):
    t, d = pf.shape
    tm, tn = _tile(t, 512, 16), _tile(math.gcd(d, gate_col), 512)
    nj, off = d // tn, gate_col // tn

    def body(dm_ref, zf_ref, zs_ref, bf_ref, bs_ref, pf_ref, ps_ref, dpf_ref, dps_ref, dzf_ref, dzs_ref, dbf_ref, dbs_ref):
        gf = _sigmoid(zf_ref[...].astype(F32) + bf_ref[...])
        gs = _sigmoid(zs_ref[...].astype(F32) + bs_ref[...])
        dmv = dm_ref[...].astype(F32)
        dpf_ref[...] = (dmv * gf).astype(dpf_ref.dtype)
        dps_ref[...] = (dmv * gs).astype(dps_ref.dtype)
        dzf = dmv * pf_ref[...].astype(F32) * gf * (1.0 - gf)
        dzs = dmv * ps_ref[...].astype(F32) * gs * (1.0 - gs)
        dzf_ref[...] = dzf.astype(dzf_ref.dtype)
        dzs_ref[...] = dzs.astype(dzs_ref.dtype)

        @pl.when(pl.program_id(1) == 0)
        def _():
            dbf_ref[...] = jnp.zeros_like(dbf_ref)
            dbs_ref[...] = jnp.zeros_like(dbs_ref)

        dbf_ref[...] += jnp.sum(dzf, axis=0, keepdims=True)
        dbs_ref[...] += jnp.sum(dzs, axis=0, keepdims=True)

    blk = pl.BlockSpec((tm, tn), lambda j, i: (i, j))
    lo = pl.BlockSpec((1, tn), lambda j, i: (0, j))
    hi = pl.BlockSpec((1, tn), lambda j, i: (0, nj + j))
    return pl.pallas_call(
        body, name=name, grid=(nj, t // tm),
        in_specs=[blk, pl.BlockSpec((tm, tn), lambda j, i: (i, off + j)), pl.BlockSpec((tm, tn), lambda j, i: (i, off + nj + j)),
                  lo, hi, blk, blk],
        out_specs=[blk, blk, blk, blk, lo, lo],
        out_shape=[jax.ShapeDtypeStruct((t, d), BF16)] * 4 + [jax.ShapeDtypeStruct((1, d), F32)] * 2,
        compiler_params=_cp(("parallel", "arbitrary")),
    )(dm, z, z, b_gates, b_gates, pf, ps)


def _resid_rms(x, mo, g, name):
    t, d = x.shape
    tm = _tile(t, 256, 16)

    def body(x_ref, mo_ref, g_ref, h_ref, hn_ref, r_ref):
        h = x_ref[...] + mo_ref[...].astype(F32)
        r = lax.rsqrt(jnp.mean(h * h, axis=-1, keepdims=True) + RMS_EPS)
        h_ref[...] = h
        hn_ref[...] = (h * r * g_ref[...]).astype(hn_ref.dtype)
        r_ref[...] = r

    row = pl.BlockSpec((tm, d), lambda i: (i, 0))
    col = pl.BlockSpec((tm, 1), lambda i: (i, 0))
    return pl.pallas_call(
        body, name=name, grid=(t // tm,),
        in_specs=[row, row, pl.BlockSpec((1, d), lambda i: (0, 0))],
        out_specs=[row, row, col],
        out_shape=[jax.ShapeDtypeStruct((t, d), F32), jax.ShapeDtypeStruct((t, d), BF16), jax.ShapeDtypeStruct((t, 1), F32)],
        compiler_params=_cp(("parallel",)),
    )(x, mo, g)


def _swiglu_fwd(gu, name):
    t, f2 = gu.shape
    f = f2 // 2
    tm, tn = _tile(t, 512, 16), _tile(f, 1024)
    nj = f // tn

    def body(g_ref, u_ref, o_ref):
        gate = g_ref[...].astype(F32)
        o_ref[...] = (gate * _sigmoid(gate) * u_ref[...].astype(F32)).astype(o_ref.dtype)

    return pl.pallas_call(
        body, name=name, grid=(t // tm, nj),
        in_specs=[pl.BlockSpec((tm, tn), lambda i, j: (i, j)), pl.BlockSpec((tm, tn), lambda i, j: (i, nj + j))],
        out_specs=pl.BlockSpec((tm, tn), lambda i, j: (i, j)),
        out_shape=jax.ShapeDtypeStruct((t, f), BF16),
        compiler_params=_cp(("parallel", "parallel")),
    )(gu, gu)


def _swiglu_bwd(gu, dact, name, job=None):
    t, f2 = gu.shape
    f = f2 // 2
    tm, tn = _tile(t, 512, 16), _tile(f, 1024)
    nj = f // tn

    def body(g_ref, u_ref, da_ref, dg_ref, du_ref):
        gate = g_ref[...].astype(F32)
        s = _sigmoid(gate)
        da = da_ref[...].astype(F32)
        dg_ref[...] = (da * u_ref[...].astype(F32) * s * (1.0 + gate * (1.0 - s))).astype(dg_ref.dtype)
        du_ref[...] = (da * gate * s).astype(du_ref.dtype)

    lo = pl.BlockSpec((tm, tn), lambda i, j: (i, j))
    return _call(
        body, name=name, grid=(t // tm, nj),
        in_specs=[lo, pl.BlockSpec((tm, tn), lambda i, j: (i, nj + j)), lo],
        out_specs=[lo, lo],
        out_shape=[jax.ShapeDtypeStruct((t, f), BF16)] * 2,
        scratch_shapes=[], semantics=("parallel", "parallel"), operands=(gu, gu, dact), job=job)


def _loss_head(h, dn, target, name):
    t, d = h.shape
    tm = _tile(t, 256, 16)

    def body(h_ref, dn_ref, t_ref, loss_ref, dy_ref, dyb_ref):
        err = h_ref[...] + dn_ref[...].astype(F32) - t_ref[...]
        dy_ref[...] = err * (1.0 / d)
        dyb_ref[...] = (err * (1.0 / d)).astype(dyb_ref.dtype)

        @pl.when(pl.program_id(0) == 0)
        def _():
            loss_ref[...] = jnp.zeros_like(loss_ref)

        loss_ref[...] += 0.5 * jnp.sum(jnp.mean(err * err, axis=-1, keepdims=True))

    row = pl.BlockSpec((tm, d), lambda i: (i, 0))
    return pl.pallas_call(
        body, name=name, grid=(t // tm,),
        in_specs=[row, row, row],
        out_specs=[pl.BlockSpec((8, LANES), lambda i: (0, 0)), row, row],
        out_shape=[jax.ShapeDtypeStruct((8, LANES), F32), jax.ShapeDtypeStruct((t, d), F32), jax.ShapeDtypeStruct((t, d), BF16)],
        compiler_params=_cp(("arbitrary",)),
    )(h, dn, target)


def _qk_prep(z, heads, dh, q_norm, k_norm, name):
    t = z.shape[0]
    tq = _tile(t, 512, 16)
    scale = 1.0 / math.sqrt(dh)

    def body(q_ref, k_ref, gq_ref, gk_ref, qn_ref, kn_ref, rq_ref, rk_ref):
        q = q_ref[...].astype(F32)
        k = k_ref[...].astype(F32)
        rq = lax.rsqrt(jnp.mean(q * q, axis=-1, keepdims=True) + RMS_EPS)
        rk = lax.rsqrt(jnp.mean(k * k, axis=-1, keepdims=True) + RMS_EPS)
        qn_ref[...] = (q * rq * gq_ref[...] * scale).astype(qn_ref.dtype)
        kn_ref[...] = (k * rk * gk_ref[...]).astype(kn_ref.dtype)
        rq_ref[...] = rq
        rk_ref[...] = rk

    blk = pl.BlockSpec((tq, dh), lambda i, h: (i, h))
    vec = pl.BlockSpec((1, dh), lambda i, h: (0, 0))
    col = pl.BlockSpec((None, tq, 1), lambda i, h: (h, i, 0))
    return pl.pallas_call(
        body, name=name, grid=(t // tq, heads),
        in_specs=[blk, pl.BlockSpec((tq, dh), lambda i, h: (i, heads + h)), vec, vec],
        out_specs=[blk, blk, col, col],
        out_shape=[jax.ShapeDtypeStruct((t, heads * dh), BF16)] * 2 + [jax.ShapeDtypeStruct((heads, t, 1), F32)] * 2,
        compiler_params=_cp(("parallel", "parallel")),
    )(z, z, q_norm, k_norm)


def _qk_prep_bwd(dqn, dkn, z, heads, dh, q_norm, k_norm, rq, rk, name):
    t = z.shape[0]
    tq = _tile(t, 512, 16)
    scale = 1.0 / math.sqrt(dh)

    def norm_bwd(dy, xv, r, g):
        xhat = xv * r
        gd = dy * g
        return r * (gd - xhat * jnp.mean(gd * xhat, axis=-1, keepdims=True)), jnp.sum(dy * xhat, axis=0, keepdims=True)

    def body(dqn_ref, dkn_ref, q_ref, k_ref, gq_ref, gk_ref, rq_ref, rk_ref, dq_ref, dk_ref, dgq_ref, dgk_ref):
        dq, dgq = norm_bwd(dqn_ref[...].astype(F32) * scale, q_ref[...].astype(F32), rq_ref[...], gq_ref[...])
        dk, dgk = norm_bwd(dkn_ref[...].astype(F32), k_ref[...].astype(F32), rk_ref[...], gk_ref[...])
        dq_ref[...] = dq.astype(dq_ref.dtype)
        dk_ref[...] = dk.astype(dk_ref.dtype)

        @pl.when((pl.program_id(0) == 0) & (pl.program_id(1) == 0))
        def _():
            dgq_ref[...] = jnp.zeros_like(dgq_ref)
            dgk_ref[...] = jnp.zeros_like(dgk_ref)

        dgq_ref[...] += dgq
        dgk_ref[...] += dgk

    blk = pl.BlockSpec((tq, dh), lambda i, h: (i, h))
    vec = pl.BlockSpec((1, dh), lambda i, h: (0, 0))
    col = pl.BlockSpec((None, tq, 1), lambda i, h: (h, i, 0))
    return pl.pallas_call(
        body, name=name, grid=(t // tq, heads),
        in_specs=[blk, blk, blk, pl.BlockSpec((tq, dh), lambda i, h: (i, heads + h)), vec, vec, col, col],
        out_specs=[blk, blk, vec, vec],
        out_shape=[jax.ShapeDtypeStruct((t, heads * dh), BF16)] * 2 + [jax.ShapeDtypeStruct((1, dh), F32)] * 2,
        compiler_params=_cp(("arbitrary", "arbitrary")),
    )(dqn, dkn, z, z, q_norm, k_norm, rq, rk)


def _tri_ones(n, upper):
    row = lax.broadcasted_iota(jnp.int32, (n, n), 0)
    col = lax.broadcasted_iota(jnp.int32, (n, n), 1)
    return jnp.where((col >= row) if upper else (col <= row), 1.0, 0.0).astype(F32)


def _forget_fwd(f, b, name):
    t, w = f.shape
    blk = _tile(t, 256, 8)
    nb = t // blk

    def body(f_ref, b_ref, out_ref):
        tri = _tri_ones(blk, upper=False)

        def step(i, carry):
            rows = pl.ds(pl.multiple_of(i * blk, blk), blk)
            logf = jax.nn.log_sigmoid(f_ref[rows, :] + b_ref[...])
            acc = jnp.dot(tri, logf, precision=lax.Precision.HIGHEST, preferred_element_type=F32) + carry
            out_ref[rows, :] = acc
            return acc[blk - 1:blk, :]

        lax.fori_loop(0, nb, step, jnp.zeros((1, w), F32))

    return pl.pallas_call(
        body, name=name,
        in_specs=[pl.BlockSpec(memory_space=pltpu.VMEM)] * 2,
        out_specs=pl.BlockSpec(memory_space=pltpu.VMEM),
        out_shape=jax.ShapeDtypeStruct((t, w), F32),
        compiler_params=_cp(),
    )(f, b)


def _forget_bwd(d_key, f, b, name):
    t, w = f.shape
    blk = _tile(t, 256, 8)
    nb = t // blk

    def body(dk_ref, f_ref, b_ref, df_ref, db_ref):
        tri = _tri_ones(blk, upper=True)

        def step(i, carry):
            suffix, db = carry
            rows = pl.ds(pl.multiple_of((nb - 1 - i) * blk, blk), blk)
            dlog = suffix - jnp.dot(tri, dk_ref[rows, :], precision=lax.Precision.HIGHEST, preferred_element_type=F32)
            df = dlog * _sigmoid(-(f_ref[rows, :] + b_ref[...]))
            df_ref[rows, :] = df
            return dlog[0:1, :], db + jnp.sum(df, axis=0, keepdims=True)

        _, db = lax.fori_loop(0, nb, step, (jnp.zeros((1, w), F32), jnp.zeros((1, w), F32)))
        db_ref[...] = db

    return pl.pallas_call(
        body, name=name,
        in_specs=[pl.BlockSpec(memory_space=pltpu.VMEM)] * 3,
        out_specs=[pl.BlockSpec(memory_space=pltpu.VMEM)] * 2,
        out_shape=[jax.ShapeDtypeStruct((t, w), F32), jax.ShapeDtypeStruct((1, w), F32)],
        compiler_params=_cp(),
    )(d_key, f, b)


def _attn_logits(q, k, f_keys, f_first, diagonal):
    s = lax.dot_general(q, k, (((1,), (1,)), ((), ())), preferred_element_type=F32)
    s = s - (f_keys - f_first)
    if diagonal:
        row = lax.broadcasted_iota(jnp.int32, s.shape, 0)
        col = lax.broadcasted_iota(jnp.int32, s.shape, 1)
        s = jnp.where(col <= row, s, MASK_VALUE)
    return s


def _block_at(i, blk):
    return pl.ds(pl.multiple_of(i * blk, blk), blk)


ATTN_BLOCK = 512


def _attn_fwd(qn, kn, v_src, v_col, fcol, frow, heads, dh, name, job=None):
    t = qn.shape[0]
    blk = _tile(t, ATTN_BLOCK)

    def body(q_ref, k_ref, v_ref, fc_ref, fr_ref, o_ref, lse_ref):
        qi = pl.program_id(1)
        q = q_ref[...]
        f_first = fc_ref[0:1, :]

        def block(ki, carry, diagonal):
            m, l, acc = carry
            keys = _block_at(ki, blk)
            s = _attn_logits(q, k_ref[keys, :], fr_ref[:, keys], f_first, diagonal)
            m_new = jnp.maximum(m, jnp.max(s, axis=-1, keepdims=True))
            alpha = jnp.exp(m - m_new)
            p = jnp.exp(s - m_new)
            l = alpha * l + jnp.sum(p, axis=-1, keepdims=True)
            acc = alpha * acc + jnp.dot(p.astype(BF16), v_ref[keys, :].astype(BF16), preferred_element_type=F32)
            return m_new, l, acc

        start = (jnp.full((blk, 1), MASK_VALUE, F32), jnp.zeros((blk, 1), F32), jnp.zeros((blk, dh), F32))
        below = lax.fori_loop(0, qi, lambda ki, carry: block(ki, carry, False), start)
        m, l, acc = block(qi, below, True)
        o_ref[...] = (acc / l).astype(o_ref.dtype)
        lse_ref[...] = m + jnp.log(l)

    qblk = pl.BlockSpec((blk, dh), lambda h, i: (i, h))
    qcol = pl.BlockSpec((None, blk, 1), lambda h, i: (h, i, 0))
    return _call(
        body, name=name, grid=(heads, t // blk),
        in_specs=[qblk,
                  pl.BlockSpec((t, dh), lambda h, i: (0, h)),
                  pl.BlockSpec((t, dh), lambda h, i: (0, v_col + h)),
                  qcol,
                  pl.BlockSpec((None, 1, t), lambda h, i: (h, 0, 0))],
        out_specs=[qblk, qcol],
        out_shape=[jax.ShapeDtypeStruct((t, heads * dh), BF16), jax.ShapeDtypeStruct((heads, t, 1), F32)],
        scratch_shapes=[], semantics=("parallel", "arbitrary"), operands=(qn, kn, v_src, fcol, frow), job=job)


def _attn_delta(o, do, heads, dh, name):
    t = o.shape[0]
    tq = _tile(t, 512, 16)

    def body(o_ref, do_ref, out_ref):
        out_ref[...] = jnp.sum(o_ref[...].astype(F32) * do_ref[...].astype(F32), axis=-1, keepdims=True)

    blk = pl.BlockSpec((tq, dh), lambda i, h: (i, h))
    return pl.pallas_call(
        body, name=name, grid=(t // tq, heads),
        in_specs=[blk, blk],
        out_specs=pl.BlockSpec((None, tq, 1), lambda i, h: (h, i, 0)),
        out_shape=jax.ShapeDtypeStruct((heads, t, 1), F32),
        compiler_params=_cp(("parallel", "parallel")),
    )(o, do)


def _attn_bwd_q(qn, kn, v_src, v_col, do, fcol, frow, lse, delta, heads, dh, name, job=None):
    t = qn.shape[0]
    blk = _tile(t, ATTN_BLOCK)

    def body(q_ref, k_ref, v_ref, do_ref, fc_ref, fr_ref, lse_ref, dl_ref, dq_ref):
        qi = pl.program_id(1)
        q, dob = q_ref[...], do_ref[...].astype(BF16)
        f_first, lse, dl = fc_ref[0:1, :], lse_ref[...], dl_ref[...]

        def block(ki, dq, diagonal):
            keys = _block_at(ki, blk)
            k = k_ref[keys, :]
            p = jnp.exp(_attn_logits(q, k, fr_ref[:, keys], f_first, diagonal) - lse)
            dp = lax.dot_general(dob, v_ref[keys, :].astype(BF16), (((1,), (1,)), ((), ())), preferred_element_type=F32)
            ds = p * (dp - dl)
            return dq + jnp.dot(ds.astype(BF16), k, preferred_element_type=F32)

        below = lax.fori_loop(0, qi, lambda ki, dq: block(ki, dq, False), jnp.zeros((blk, dh), F32))
        dq_ref[...] = block(qi, below, True)

    qblk = pl.BlockSpec((blk, dh), lambda h, i: (i, h))
    qcol = pl.BlockSpec((None, blk, 1), lambda h, i: (h, i, 0))
    return _call(
        body, name=name, grid=(heads, t // blk),
        in_specs=[qblk,
                  pl.BlockSpec((t, dh), lambda h, i: (0, h)),
                  pl.BlockSpec((t, dh), lambda h, i: (0, v_col + h)),
                  qblk, qcol,
                  pl.BlockSpec((None, 1, t), lambda h, i: (h, 0, 0)),
                  qcol, qcol],
        out_specs=[qblk],
        out_shape=[jax.ShapeDtypeStruct((t, heads * dh), F32)],
        scratch_shapes=[], semantics=("parallel", "arbitrary"),
        operands=(qn, kn, v_src, do, fcol, frow, lse, delta), job=job)


def _attn_bwd_kv(qn, kn, v_src, v_col, do, fcol, frow, lse, delta, heads, dh, name, job=None):
    t = qn.shape[0]
    blk = _tile(t, ATTN_BLOCK)
    nq = t // blk
    tn_dims = (((0,), (0,)), ((), ()))

    def body(q_ref, k_ref, v_ref, do_ref, fc_ref, fr_ref, lse_ref, dl_ref, dk_ref, dv_ref, dfk_ref):
        ki = pl.program_id(1)
        k, v, f_keys = k_ref[...], v_ref[...].astype(BF16), fr_ref[...]

        def block(qi, carry, diagonal):
            dk, dv, dfk = carry
            rows = _block_at(qi, blk)
            q, dob = q_ref[rows, :], do_ref[rows, :].astype(BF16)
            f_first = fc_ref[pl.ds(pl.multiple_of(qi * blk, blk), 1), :]
            p = jnp.exp(_attn_logits(q, k, f_keys, f_first, diagonal) - lse_ref[rows, :])
            dp = lax.dot_general(dob, v, (((1,), (1,)), ((), ())), preferred_element_type=F32)
            ds = p * (dp - dl_ref[rows, :])
            dv = dv + lax.dot_general(p.astype(BF16), dob, tn_dims, preferred_element_type=F32)
            dk = dk + lax.dot_general(ds.astype(BF16), q, tn_dims, preferred_element_type=F32)
            return dk, dv, dfk + jnp.sum(ds, axis=0, keepdims=True)

        start = (jnp.zeros((blk, dh), F32), jnp.zeros((blk, dh), F32), jnp.zeros((1, blk), F32))
        dk, dv, dfk = lax.fori_loop(ki + 1, nq, lambda qi, carry: block(qi, carry, False), block(ki, start, True))
        dk_ref[...] = dk
        dv_ref[...] = dv.astype(dv_ref.dtype)
        dfk_ref[...] = dfk

    whole = pl.BlockSpec((t, dh), lambda h, j: (0, h))
    wcol = pl.BlockSpec((None, t, 1), lambda h, j: (h, 0, 0))
    kblk = pl.BlockSpec((blk, dh), lambda h, j: (j, h))
    krow = pl.BlockSpec((None, 1, blk), lambda h, j: (h, 0, j))
    return _call(
        body, name=name, grid=(heads, nq),
        in_specs=[whole, kblk, pl.BlockSpec((blk, dh), lambda h, j: (j, v_col + h)), whole, wcol, krow, wcol, wcol],
        out_specs=[kblk, kblk, krow],
        out_shape=[jax.ShapeDtypeStruct((t, heads * dh), F32), jax.ShapeDtypeStruct((t, heads * dh), BF16),
                   jax.ShapeDtypeStruct((heads, 1, t), F32)],
        scratch_shapes=[], semantics=("parallel", "arbitrary"),
        operands=(qn, kn, v_src, do, fcol, frow, lse, delta), job=job)


TIME_TILE = 8


def _s5_discretize(lam_re, lam_im, log_step, b_re, b_im):
    dt = jnp.exp(log_step)
    mag = jnp.exp(lam_re * dt)
    lb_re = mag * jnp.cos(lam_im * dt)
    lb_im = mag * jnp.sin(lam_im * dt)
    denom = lam_re * lam_re + lam_im * lam_im
    num_re = lb_re - 1.0
    fac_re = (num_re * lam_re + lb_im * lam_im) / denom
    fac_im = (lb_im * lam_re - num_re * lam_im) / denom
    return lb_re, lb_im, fac_re * b_re - fac_im * b_im, fac_re * b_im + fac_im * b_re


def _s5_prep(lam_re, lam_im, log_step, b_re, b_im, name):
    gp, width = b_re.shape

    def body(lr, li, ls, br, bi, o_lr, o_li, o_br, o_bi):
        res = _s5_discretize(lr[...], li[...], ls[...], br[...], bi[...])
        for ref, val in zip((o_lr, o_li, o_br, o_bi), res):
            ref[...] = val

    vm = pl.BlockSpec(memory_space=pltpu.VMEM)
    return pl.pallas_call(
        body, name=name, in_specs=[vm] * 5, out_specs=[vm] * 4,
        out_shape=[jax.ShapeDtypeStruct((gp, 1), F32)] * 2 + [jax.ShapeDtypeStruct((gp, width), F32)] * 2,
        compiler_params=_cp(),
    )(lam_re, lam_im, log_step, b_re, b_im)


def _s5_prep_bwd(lam_re, lam_im, log_step, b_re, b_im, d_lb_re, d_lb_im, d_bb_re, d_bb_im, groups, name):
    gp, width = b_re.shape
    states = gp // groups
    tr = _tile(gp, 512, 8)

    def body(lr, li, ls, br, bi, g_lr, g_li, g_br, g_bi, o_lr, o_li, o_ls, o_br, o_bi):
        _, vjp = jax.vjp(_s5_discretize, lr[...], li[...], ls[...], br[...], bi[...])
        d_lr, d_li, d_ls, d_br, d_bi = vjp((g_lr[...], g_li[...], g_br[...], g_bi[...]))
        o_lr[...] = d_lr
        o_li[...] = d_li
        o_br[...] = d_br
        o_bi[...] = d_bi
        row_group = (pl.program_id(0) * tr + lax.broadcasted_iota(jnp.int32, (tr, groups), 0)) // states
        col_group = lax.broadcasted_iota(jnp.int32, (tr, groups), 1)

        @pl.when(pl.program_id(0) == 0)
        def _():
            o_ls[...] = jnp.zeros_like(o_ls)

        o_ls[...] += jnp.sum(jnp.where(row_group == col_group, d_ls, 0.0), axis=0, keepdims=True)

    col = pl.BlockSpec((tr, 1), lambda i: (i, 0))
    mat = pl.BlockSpec((tr, width), lambda i: (i, 0))
    return pl.pallas_call(
        body, name=name, grid=(gp // tr,),
        in_specs=[col, col, col, mat, mat, col, col, mat, mat],
        out_specs=[col, col, pl.BlockSpec((1, groups), lambda i: (0, 0)), mat, mat],
        out_shape=[jax.ShapeDtypeStruct((gp, 1), F32)] * 2 + [jax.ShapeDtypeStruct((1, groups), F32)]
        + [jax.ShapeDtypeStruct((gp, width), F32)] * 2,
        compiler_params=_cp(("arbitrary",)),
    )(lam_re, lam_im, log_step, b_re, b_im, d_lb_re, d_lb_im, d_bb_re, d_bb_im)


def _shift_time(v, s, reverse):
    row = lax.broadcasted_iota(jnp.int32, v.shape, 0)
    if reverse:
        return jnp.where(row < TIME_TILE - s, pltpu.roll(v, TIME_TILE - s, 0), 0.0)
    return jnp.where(row >= s, pltpu.roll(v, s, 0), 0.0)


def _cmul(ar, ai, br, bi):
    return ar * br - ai * bi, ar * bi + ai * br


def _scan_time(xr_ref, xi_ref, ar, ai, reverse):
    t = xr_ref.shape[0]
    n_tiles = t // TIME_TILE
    powers = [(ar, ai)]
    for _ in range(TIME_TILE - 1):
        powers.append(_cmul(*powers[-1], ar, ai))
    order = powers[::-1] if reverse else powers
    carry_r = jnp.concatenate([p[0] for p in order], axis=0)
    carry_i = jnp.concatenate([p[1] for p in order], axis=0)
    levels = [(1, powers[0]), (2, powers[1]), (4, powers[3])]
    last = 0 if reverse else TIME_TILE - 1

    def tile(i, carry):
        cr, ci = carry
        idx = (n_tiles - 1 - i) if reverse else i
        rows = pl.ds(pl.multiple_of(idx * TIME_TILE, TIME_TILE), TIME_TILE)
        br, bi = xr_ref[rows, :], xi_ref[rows, :]
        for s, (pr, pi) in levels:
            sr, si = _cmul(pr, pi, _shift_time(br, s, reverse), _shift_time(bi, s, reverse))
            br, bi = br + sr, bi + si
        kr, ki = _cmul(carry_r, carry_i, cr, ci)
        br, bi = br + kr, bi + ki
        xr_ref[rows, :] = br
        xi_ref[rows, :] = bi
        return br[last:last + 1, :], bi[last:last + 1, :]

    zero = jnp.zeros_like(ar)
    lax.fori_loop(0, n_tiles, tile, (zero, zero))


def _s5_states(u_ref, bbr_ref, bbi_ref, ar_ref, ai_ref, xr, xi, chunk):
    t = u_ref.shape[0]
    for r0 in range(0, t, chunk):
        rows = pl.ds(r0, chunk)
        xr[rows, :] = jnp.dot(u_ref[rows, :], bbr_ref[...], preferred_element_type=F32)
        xi[rows, :] = jnp.dot(u_ref[rows, :], bbi_ref[...], preferred_element_type=F32)
    _scan_time(xr, xi, ar_ref[...], ai_ref[...], reverse=False)


def _s5_specs(t, nb_lanes, state_lanes):
    tok = pl.BlockSpec((t, nb_lanes), lambda j: (0, j))
    bb = pl.BlockSpec((None, nb_lanes, state_lanes), lambda j: (j, 0, 0))
    cc = pl.BlockSpec((None, state_lanes, nb_lanes), lambda j: (j, 0, 0))
    dvec = pl.BlockSpec((1, nb_lanes), lambda j: (0, j))
    avec = pl.BlockSpec((1, state_lanes), lambda j: (0, j))
    return tok, bb, cc, dvec, avec


def _s5_fwd(u5, bbr, bbi, ccr, cci, dskip, ar, ai, name, job=None):
    t, w = u5.shape
    nb, nb_lanes, state_lanes = bbr.shape
    chunk = _tile(t, 512, 16)

    def body(u_ref, bbr_ref, bbi_ref, cr_ref, ci_ref, d_ref, ar_ref, ai_ref, y_ref, xr, xi):
        _s5_states(u_ref, bbr_ref, bbi_ref, ar_ref, ai_ref, xr, xi, chunk)
        for r0 in range(0, t, chunk):
            rows = pl.ds(r0, chunk)
            y = jnp.dot(xr[rows, :].astype(BF16), cr_ref[...], preferred_element_type=F32)
            y = y - jnp.dot(xi[rows, :].astype(BF16), ci_ref[...], preferred_element_type=F32)
            y_ref[rows, :] = y + d_ref[...] * u_ref[rows, :].astype(F32)

    tok, bb, cc, dvec, avec = _s5_specs(t, nb_lanes, state_lanes)
    return _call(
        body, name=name, grid=(nb,),
        in_specs=[tok, bb, bb, cc, cc, dvec, avec, avec],
        out_specs=[tok],
        out_shape=[jax.ShapeDtypeStruct((t, w), F32)],
        scratch_shapes=[pltpu.VMEM((t, state_lanes), F32)] * 2,
        semantics=("parallel",), operands=(u5, bbr, bbi, ccr, cci, dskip, ar, ai), job=job)


def _s5_bwd(u5, dy, bbr, bbi, ccr, cci, dskip, ar, ai, name, job=None):
    t, w = u5.shape
    nb, nb_lanes, state_lanes = bbr.shape
    chunk = _tile(t, 512, 16)
    nt_dims = (((1,), (1,)), ((), ()))
    tn_dims = (((0,), (0,)), ((), ()))

    def body(u_ref, dy_ref, bbr_ref, bbi_ref, cr_ref, ci_ref, d_ref, ar_ref, ai_ref,
             du_ref, dbbr_ref, dbbi_ref, dcr_ref, dci_ref, dar_ref, dai_ref, dd_ref, xr, xi, gr, gi):
        _s5_states(u_ref, bbr_ref, bbi_ref, ar_ref, ai_ref, xr, xi, chunk)
        for r0 in range(0, t, chunk):
            rows = pl.ds(r0, chunk)
            dyb = dy_ref[rows, :].astype(BF16)
            gr[rows, :] = lax.dot_general(dyb, cr_ref[...], nt_dims, preferred_element_type=F32)
            gi[rows, :] = -lax.dot_general(dyb, ci_ref[...], nt_dims, preferred_element_type=F32)
        _scan_time(gr, gi, ar_ref[...], -ai_ref[...], reverse=True)

        dcr = jnp.zeros((state_lanes, nb_lanes), F32)
        dci = jnp.zeros((state_lanes, nb_lanes), F32)
        dbr = jnp.zeros((nb_lanes, state_lanes), F32)
        dbi = jnp.zeros((nb_lanes, state_lanes), F32)
        dd = jnp.zeros((1, nb_lanes), F32)
        for r0 in range(0, t, chunk):
            rows = pl.ds(r0, chunk)
            u = u_ref[rows, :]
            dyv = dy_ref[rows, :]
            dyb = dyv.astype(BF16)
            lr, li = gr[rows, :].astype(BF16), gi[rows, :].astype(BF16)
            dcr = dcr + lax.dot_general(xr[rows, :].astype(BF16), dyb, tn_dims, preferred_element_type=F32)
            dci = dci - lax.dot_general(xi[rows, :].astype(BF16), dyb, tn_dims, preferred_element_type=F32)
            dbr = dbr + lax.dot_general(u, lr, tn_dims, preferred_element_type=F32)
            dbi = dbi + lax.dot_general(u, li, tn_dims, preferred_element_type=F32)
            du = lax.dot_general(lr, bbr_ref[...], nt_dims, preferred_element_type=F32)
            du = du + lax.dot_general(li, bbi_ref[...], nt_dims, preferred_element_type=F32)
            du_ref[rows, :] = du + d_ref[...] * dyv
            dd = dd + jnp.sum(dyv * u.astype(F32), axis=0, keepdims=True)
        dcr_ref[...] = dcr
        dci_ref[...] = dci
        dbbr_ref[...] = dbr
        dbbi_ref[...] = dbi
        dd_ref[...] = dd

        first_row = lax.broadcasted_iota(jnp.int32, (TIME_TILE, state_lanes), 0) == 0

        def tile(i, carry):
            pr, pi, acc_r, acc_i = carry
            rows = pl.ds(pl.multiple_of(i * TIME_TILE, TIME_TILE), TIME_TILE)
            x_r, x_i, l_r, l_i = xr[rows, :], xi[rows, :], gr[rows, :], gi[rows, :]
            prev_r = jnp.where(first_row, pr, pltpu.roll(x_r, 1, 0))
            prev_i = jnp.where(first_row, pi, pltpu.roll(x_i, 1, 0))
            acc_r = acc_r + l_r * prev_r + l_i * prev_i
            acc_i = acc_i + l_i * prev_r - l_r * prev_i
            return x_r[TIME_TILE - 1:, :], x_i[TIME_TILE - 1:, :], acc_r, acc_i

        zrow = jnp.zeros((1, state_lanes), F32)
        ztile = jnp.zeros((TIME_TILE, state_lanes), F32)
        _, _, acc_r, acc_i = lax.fori_loop(0, t // TIME_TILE, tile, (zrow, zrow, ztile, ztile))
        dar_ref[...] = jnp.sum(acc_r, axis=0, keepdims=True)
        dai_ref[...] = jnp.sum(acc_i, axis=0, keepdims=True)

    tok, bb, cc, dvec, avec = _s5_specs(t, nb_lanes, state_lanes)
    return _call(
        body, name=name, grid=(nb,),
        in_specs=[tok, tok, bb, bb, cc, cc, dvec, avec, avec],
        out_specs=[tok, bb, bb, cc, cc, avec, avec, dvec],
        out_shape=[jax.ShapeDtypeStruct((t, w), F32)]
        + [jax.ShapeDtypeStruct((nb, nb_lanes, state_lanes), F32)] * 2
        + [jax.ShapeDtypeStruct((nb, state_lanes, nb_lanes), F32)] * 2
        + [jax.ShapeDtypeStruct((1, nb * state_lanes), F32)] * 2
        + [jax.ShapeDtypeStruct((1, w), F32)],
        scratch_shapes=[pltpu.VMEM((t, state_lanes), F32)] * 4,
        semantics=("parallel",), operands=(u5, dy, bbr, bbi, ccr, cci, dskip, ar, ai), job=job)


def _gelu(x):
    return 0.5 * x * (1.0 + jnp.tanh(GELU_C * (x + GELU_A * x * x * x)))


def _gelu_grad(x):
    th = jnp.tanh(GELU_C * (x + GELU_A * x * x * x))
    return 0.5 * (1.0 + th) + 0.5 * x * (1.0 - th * th) * GELU_C * (1.0 + 3.0 * GELU_A * x * x)


def _glu_fwd(y5, w, b, name):
    t, width = y5.shape
    tm = _tile(t, 512, 16)

    def body(y_ref, w_ref, b_ref, o_ref):
        g = _gelu(y_ref[...])
        a = jnp.dot(g.astype(BF16), w_ref[...], preferred_element_type=F32) + b_ref[...]
        o_ref[...] = (g * _sigmoid(a)).astype(o_ref.dtype)

    row = pl.BlockSpec((tm, width), lambda i: (i, 0))
    return pl.pallas_call(
        body, name=name, grid=(t // tm,),
        in_specs=[row, pl.BlockSpec((width, width), lambda i: (0, 0)), pl.BlockSpec((1, width), lambda i: (0, 0))],
        out_specs=row,
        out_shape=jax.ShapeDtypeStruct((t, width), BF16),
        compiler_params=_cp(("parallel",)),
    )(y5, w, b)


def _glu_bwd(y5, dout, w, b, name):
    t, width = y5.shape
    tm = _tile(t, 512, 16)

    def body(y_ref, do_ref, w_ref, b_ref, dy_ref, g_ref, da_ref, db_ref):
        y = y_ref[...]
        g = _gelu(y)
        s = _sigmoid(jnp.dot(g.astype(BF16), w_ref[...], preferred_element_type=F32) + b_ref[...])
        dout_v = do_ref[...].astype(F32)
        da = dout_v * g * s * (1.0 - s)
        dg = dout_v * s + lax.dot_general(da.astype(BF16), w_ref[...], (((1,), (1,)), ((), ())),
                                          preferred_element_type=F32)
        dy_ref[...] = dg * _gelu_grad(y)
        g_ref[...] = g.astype(g_ref.dtype)
        da_ref[...] = da.astype(da_ref.dtype)

        @pl.when(pl.program_id(0) == 0)
        def _():
            db_ref[...] = jnp.zeros_like(db_ref)

        db_ref[...] += jnp.sum(da, axis=0, keepdims=True)

    row = pl.BlockSpec((tm, width), lambda i: (i, 0))
    vec = pl.BlockSpec((1, width), lambda i: (0, 0))
    return pl.pallas_call(
        body, name=name, grid=(t // tm,),
        in_specs=[row, row, pl.BlockSpec((width, width), lambda i: (0, 0)), vec],
        out_specs=[row, row, row, vec],
        out_shape=[jax.ShapeDtypeStruct((t, width), F32), jax.ShapeDtypeStruct((t, width), BF16),
                   jax.ShapeDtypeStruct((t, width), BF16), jax.ShapeDtypeStruct((1, width), F32)],
        compiler_params=_cp(("arbitrary",)),
    )(y5, dout, w, b)


def _adamw(w, g, m, v):
    m = ADAM_B1 * m + (1.0 - ADAM_B1) * g
    v = ADAM_B2 * v + (1.0 - ADAM_B2) * (g * g)
    m_hat = m / (1.0 - ADAM_B1 ** ADAM_STEP)
    v_hat = v / (1.0 - ADAM_B2 ** ADAM_STEP)
    return -ADAM_LR * (m_hat / (jnp.sqrt(v_hat) + ADAM_EPS) + ADAM_WD * w), m, v


def _adamw_shard(w, m, v, sums, got, chip, name):
    rows, cols = w.shape
    wide = sums.shape[2]
    tr = _row_tile(rows, wide, target=2**20)

    def body(chip_ref, w_ref, m_ref, v_ref, s_ref, g0_ref, g1_ref, g2_ref, g_out, d_out, m_out, v_out):
        g = s_ref[...].astype(F32) + g0_ref[...].astype(F32) + g1_ref[...].astype(F32) + g2_ref[...].astype(F32)
        g = g[:, :cols]
        delta, m_new, v_new = _adamw(w_ref[...], g, m_ref[...], v_ref[...])
        g_out[...] = g
        d_out[...] = delta
        m_out[...] = m_new
        v_out[...] = v_new

    blk = pl.BlockSpec((tr, cols), lambda i, chip_ref: (i, 0))

    def part(k):
        return pl.BlockSpec((None, tr, wide), lambda i, chip_ref: (k, i, 0))

    return pl.pallas_call(
        body, name=name,
        grid_spec=pltpu.PrefetchScalarGridSpec(
            num_scalar_prefetch=1, grid=(rows // tr,),
            in_specs=[blk, blk, blk, pl.BlockSpec((None, tr, wide), lambda i, chip_ref: (chip_ref[0], i, 0)),
                      part(0), part(1), part(2)],
            out_specs=[blk] * 4),
        out_shape=[jax.ShapeDtypeStruct((rows, cols), F32)] * 4,
        compiler_params=_cp(("parallel",)),
    )(chip, w, m, v, sums, got, got, got)


def _adamw_packed(w, m, v, g, name):
    def body(w_ref, m_ref, v_ref, g_ref, d_out, m_out, v_out):
        delta, m_new, v_new = _adamw(w_ref[...], g_ref[...], m_ref[...], v_ref[...])
        d_out[...] = delta
        m_out[...] = m_new
        v_out[...] = v_new

    vm = pl.BlockSpec(memory_space=pltpu.VMEM)
    return pl.pallas_call(
        body, name=name, in_specs=[vm] * 4, out_specs=[vm] * 3,
        out_shape=[jax.ShapeDtypeStruct(w.shape, F32)] * 3,
        compiler_params=_cp(),
    )(w, m, v, g)


WEIGHTS = ("g_mix", "w_in", "b_fgate", "b_gates", "q_norm", "k_norm", "s5_lambda_re", "s5_lambda_im", "s5_log_step",
           "s5_b_re", "s5_b_im", "s5_c_re", "s5_c_im", "s5_d", "w_glu", "b_glu", "w_proj_fox", "w_proj_s5", "w_out",
           "g_ffn", "w_gate_up", "w_down")
COLUMN_SHARDED = ("w_in", "w_proj_fox", "w_proj_s5", "w_gate_up")
ROW_SHARDED = ("w_glu", "w_out", "w_down")
PACK_ROWS = 8 * LANES

def _pack(arrays):
    flat = jnp.concatenate([a.reshape(-1).astype(F32) for a in arrays])
    flat = jnp.pad(flat, (0, (-flat.shape[0]) % PACK_ROWS))
    return flat.reshape(-1, LANES)


def _unpack(packed, like):
    flat, out, at = packed.reshape(-1), [], 0
    for a in like:
        out.append(flat[at:at + a.size].reshape(a.shape))
        at += a.size
    return out


def _split_rows(rows, shares, unit=16):
    out, first = [], 0
    for k, share in enumerate(shares):
        count = rows - first if k == len(shares) - 1 else max(unit, int(rows * share) // unit * unit)
        out.append((first, count))
        first += count
    assert first == rows and all(c > 0 for _, c in out), (rows, out)
    return out


def _pad_lanes(a):
    return jnp.pad(a, ((0, 0), (0, LANES - a.shape[1])))


def kernel(x, g_mix, w_in, b_fgate, b_gates, q_norm, k_norm, s5_lambda_re, s5_lambda_im, s5_log_step, s5_b_re, s5_b_im,
           s5_c_re, s5_c_im, s5_d, w_glu, b_glu, w_proj_fox, w_proj_s5, w_out, g_ffn, w_gate_up, w_down,
           loss_target, m_g_mix, m_w_in, m_b_fgate, m_b_gates, m_q_norm, m_k_norm, m_s5_lambda_re,
           m_s5_lambda_im, m_s5_log_step, m_s5_b_re, m_s5_b_im, m_s5_c_re, m_s5_c_im, m_s5_d, m_w_glu,
           m_b_glu, m_w_proj_fox, m_w_proj_s5, m_w_out, m_g_ffn, m_w_gate_up, m_w_down, v_g_mix, v_w_in,
           v_b_fgate, v_b_gates, v_q_norm, v_k_norm, v_s5_lambda_re, v_s5_lambda_im, v_s5_log_step, v_s5_b_re,
           v_s5_b_im, v_s5_c_re, v_s5_c_im, v_s5_d, v_w_glu, v_b_glu, v_w_proj_fox, v_w_proj_s5, v_w_out,
           v_g_ffn, v_w_gate_up, v_w_down):
    given = dict(locals())
    weights = {n: given[n] for n in WEIGHTS}
    mom_m = {n: given["m_" + n] for n in WEIGHTS}
    mom_v = {n: given["v_" + n] for n in WEIGHTS}

    pos_x, pos_y, pos_c = _position()
    core = jnp.reshape(pos_c, (1,)).astype(jnp.int32)
    chip = jnp.reshape(2 * pos_x + pos_y, (1,)).astype(jnp.int32)

    xs, target = x[0], loss_target[0]
    t, d = xs.shape
    heads, dh = b_fgate.shape[-1], q_norm.shape[-1]
    fw = heads * dh
    groups, states, gwidth = s5_b_re.shape[1:]
    sw = groups * gwidth
    gp = groups * states
    assert dh == LANES and sw % LANES == 0 and LANES % gwidth == 0
    col_v, col_f, col_s5 = 3 * fw, 3 * fw + heads, 3 * fw + heads + sw

    shard = {n: weights[n][0].astype(BF16) for n in COLUMN_SHARDED + ROW_SHARDED}

    def whole(n, ag):
        if n in COLUMN_SHARDED:
            return ag.transpose(1, 0, 2).reshape(ag.shape[1], N_DEV * ag.shape[2])
        return ag.reshape(N_DEV * ag.shape[1], ag.shape[2])

    full = {}
    c_gu, r_dn = w_gate_up.shape[2], w_down.shape[1]
    assert c_gu == 2 * r_dn
    cp_gu = -(-c_gu // LANES) * LANES
    shard["w_gate_up"] = jnp.pad(shard["w_gate_up"], ((0, 0), (0, cp_gu - c_gu)))

    def down_rows(ag):
        gap = [jnp.zeros((cp_gu - c_gu, d), ag.dtype)] if cp_gu > c_gu else []
        return jnp.concatenate([p for b in range(N_DEV // 2) for p in [ag[2 * b], ag[2 * b + 1]] + gap], axis=0)

    c_in = w_in.shape[2]
    in_cols = N_DEV * c_in

    def in_pieces(lo, hi, take):
        cuts = [(k, max(lo, k * c_in), min(hi, (k + 1) * c_in)) for k in range(N_DEV)]
        return [take(k, a - k * c_in, b - k * c_in) for k, a, b in cuts if a < b]

    ag_in = _all_gather(shard["w_in"], "ag_w_in")

    def from_gathered(k, a, b):
        return ag_in[k][:, a:b]

    w_main = jnp.concatenate(in_pieces(0, col_v, from_gathered) + in_pieces(col_f, in_cols, from_gathered), axis=1)
    w_forget = _pad_lanes(jnp.concatenate(in_pieces(col_v, col_f, from_gathered), axis=1))
    z_s5, z_gate = 3 * fw, 3 * fw + sw

    u, r_mix = _rms_fwd(xs, g_mix, "rms_mix")
    early = ("w_proj_fox", "w_proj_s5", "w_glu", "w_out")
    gu_rows = _split_rows(d, (0.62, 0.17, 0.06, 0.15))

    def gate_up_piece(k, so_far):
        return _GatherJob([shard["w_gate_up"]], rows=gu_rows[k], into=so_far)

    z, gu_buf = _mm(u, w_main, "nn", BF16, "mm_z", job=gate_up_piece(0, None))
    zf = _mm(u, w_forget, "nn", F32, "mm_zf")
    qn, kn, r_q, r_k = _qk_prep(z, heads, dh, q_norm, k_norm, "qk_prep")
    b_forget = _pad_lanes(b_fgate)
    cum = _forget_fwd(zf, b_forget, "forget_fwd")
    cum_t = cum[:, :heads].T
    fcol, frow = cum_t[:, :, None], cum_t[:, None, :]
    (attn, lse), got = _attn_fwd(qn, kn, z, 2 * heads, fcol, frow, heads, dh, "attn_fwd",
                                 job=_GatherJob([shard[n] for n in early]))
    for n, ag in zip(early, got):
        full[n] = whole(n, ag)

    lam_re, lam_im = s5_lambda_re.reshape(gp, 1), s5_lambda_im.reshape(gp, 1)
    log_step = jnp.repeat(s5_log_step.reshape(groups, 1), states, axis=1).reshape(gp, 1)
    b_re, b_im = s5_b_re.reshape(gp, gwidth), s5_b_im.reshape(gp, gwidth)
    lb_re, lb_im, bb_re, bb_im = _s5_prep(lam_re, lam_im, log_step, b_re, b_im, "s5_prep")
    nb, per = sw // LANES, LANES // gwidth
    eye = jnp.eye(per, dtype=F32)

    def diag_b(bb):
        return jnp.einsum("napi,ab->naibp", bb.reshape(nb, per, states, gwidth), eye).reshape(nb, LANES, per * states)

    def diag_c(c):
        return jnp.einsum("naip,ab->nbpai", c.reshape(nb, per, gwidth, states), eye).reshape(nb, per * states, LANES)

    def undiag_b(g):
        return jnp.einsum("naibp,ab->napi", g.reshape(nb, per, gwidth, per, states), eye).reshape(gp, gwidth)

    def undiag_c(g):
        return jnp.einsum("nbpai,ab->naip", g.reshape(nb, per, states, per, gwidth), eye).reshape(1, groups, gwidth, states)

    bbr, bbi = diag_b(bb_re).astype(BF16), diag_b(bb_im).astype(BF16)
    ccr, cci = diag_c(s5_c_re[0]).astype(BF16), diag_c(s5_c_im[0]).astype(BF16)
    a_re, a_im = lb_re.reshape(1, gp), lb_im.reshape(1, gp)
    d_skip = s5_d.reshape(1, sw)
    u5 = z[:, z_s5:z_s5 + sw]
    (y5,), gu_buf = _s5_fwd(u5, bbr, bbi, ccr, cci, d_skip, a_re, a_im, "s5_fwd", job=gate_up_piece(1, gu_buf))
    ssm = _glu_fwd(y5, full["w_glu"], b_glu, "glu_fwd")

    pf, gu_buf = _mm(attn, full["w_proj_fox"], "nn", BF16, "mm_pf", job=gate_up_piece(2, gu_buf))
    ps = _mm(ssm, full["w_proj_s5"], "nn", BF16, "mm_ps")
    merged = _gate_merge_fwd(z, z_gate, b_gates, pf, ps, "merge_fwd")
    mo, gu_buf = _mm(merged, full["w_out"], "nn", F32, "mm_out", job=gate_up_piece(3, gu_buf))
    full["w_gate_up"] = gu_buf[0]
    h, hn, r_ffn = _resid_rms(xs, mo, g_ffn, "resid_rms")
    gu, got = _mm(hn, full["w_gate_up"], "nn", BF16, "mm_gu", job=_GatherJob([shard["w_down"]]), b_stacked=True)
    full["w_down"] = down_rows(got[0])
    act = _swiglu_fwd(gu, "swiglu_fwd")
    dn = _mm(act, full["w_down"], "nn", F32, "mm_down")
    loss_blk, dy, dy_b = _loss_head(h, dn, target, "loss_head")
    loss = lax.psum(loss_blk[0, 0], ("x", "y", "c"))

    grad, sums, from_chips = {}, {}, {}

    def pair_sums(n, parts=None):
        if parts is None:
            g_full = grad[n]
            if n in COLUMN_SHARDED:
                parts = g_full.reshape(g_full.shape[0], N_DEV, g_full.shape[1] // N_DEV).transpose(1, 0, 2)
            else:
                parts = g_full.reshape(N_DEV, g_full.shape[0] // N_DEV, g_full.shape[1])
        got = _swap_with_sibling(parts, "rs_sibling_" + n)
        return _add_sibling(parts, got, core, "rs_add_" + n)

    dact = _mm(dy_b, full["w_down"], "nt", BF16, "mm_dact")
    gw_down = _mm(act, dy_b, "tn", BF16, "mm_gw_down")
    parts = jnp.stack(
        [lax.slice_in_dim(gw_down, cp_gu * (k // 2) + r_dn * (k % 2), cp_gu * (k // 2) + r_dn * (k % 2) + r_dn)
         for k in range(N_DEV)])
    (dgate, dup), (got,) = _swiglu_bwd(gu, dact, "swiglu_bwd", job=_SiblingSwapJob([parts]))
    dgu = jnp.concatenate([dgate, dup], axis=1)
    sums["w_down"] = _add_sibling(parts, got, core, "rs_add_w_down")
    dhn, (from_chips["w_down"],) = _mm(dgu, full["w_gate_up"], "nt", F32, "mm_dhn", job=_ChipSwapJob([sums["w_down"]]),
                                       b_stacked=True)
    parts = _mm(hn, dgu, "tn", BF16, "mm_gw_gu", out_stack=N_DEV)
    dh_, dh_b, grad["g_ffn"] = _rms_bwd([dhn], h, r_ffn, g_ffn, [dy], "rms_ffn_bwd")
    dmerged, (got,) = _mm(dh_b, full["w_out"], "nt", BF16, "mm_dmerged", job=_SiblingSwapJob([parts]))
    sums["w_gate_up"] = _add_sibling(parts, got, core, "rs_add_w_gate_up")
    grad["w_out"] = _mm(merged, dh_b, "tn", BF16, "mm_gw_out")
    dpf, dps, dz_gf, dz_gs, db_gf, db_gs = _gate_merge_bwd(dmerged, z, z_gate, b_gates, pf, ps, "merge_bwd")
    grad["b_gates"] = jnp.concatenate([db_gf, db_gs], axis=1)
    dattn = _mm(dpf, full["w_proj_fox"], "nt", BF16, "mm_dattn")
    grad["w_proj_fox"] = _mm(attn, dpf, "tn", BF16, "mm_gw_pf")
    dssm = _mm(dps, full["w_proj_s5"], "nt", BF16, "mm_dssm")
    grad["w_proj_s5"] = _mm(ssm, dps, "tn", BF16, "mm_gw_ps")

    dy5, g5, da5, grad["b_glu"] = _glu_bwd(y5, dssm, full["w_glu"], b_glu, "glu_bwd")
    grad["w_glu"] = _mm(g5, da5, "tn", BF16, "mm_gw_glu")
    for n in early:
        sums[n] = pair_sums(n)
    (du5, d_bbr, d_bbi, d_ccr, d_cci, d_are, d_aim, d_dskip), got = _s5_bwd(
        u5, dy5, bbr, bbi, ccr, cci, d_skip, a_re, a_im, "s5_bwd", job=_ChipSwapJob([sums[n] for n in early]))
    from_chips.update(zip(early, got))
    d_lre, d_lim, d_lstep, d_bre, d_bim = _s5_prep_bwd(
        lam_re, lam_im, log_step, b_re, b_im, d_are.reshape(gp, 1), d_aim.reshape(gp, 1),
        undiag_b(d_bbr), undiag_b(d_bbi), groups, "s5_prep_bwd")
    grad["s5_lambda_re"], grad["s5_lambda_im"] = d_lre.reshape(1, groups, states), d_lim.reshape(1, groups, states)
    grad["s5_log_step"] = d_lstep
    grad["s5_b_re"], grad["s5_b_im"] = d_bre.reshape(s5_b_re.shape), d_bim.reshape(s5_b_im.shape)
    grad["s5_c_re"], grad["s5_c_im"] = undiag_c(d_ccr), undiag_c(d_cci)
    grad["s5_d"] = d_dskip.reshape(s5_d.shape)

    delta = _attn_delta(attn, dattn, heads, dh, "attn_delta")
    swap_rows = _split_rows(d, (0.36, 0.5, 0.14))

    def gate_up_swap(k, so_far):
        return _ChipSwapJob([sums["w_gate_up"]], rows=swap_rows[k], into=so_far)

    (dqn,), got = _attn_bwd_q(qn, kn, z, 2 * heads, dattn, fcol, frow, lse, delta, heads, dh, "attn_bwd_q",
                              job=gate_up_swap(0, None))
    (dkn, dv, df_k), got = _attn_bwd_kv(qn, kn, z, 2 * heads, dattn, fcol, frow, lse, delta, heads, dh, "attn_bwd_kv",
                                        job=gate_up_swap(1, got))
    dq, dk, grad["q_norm"], grad["k_norm"] = _qk_prep_bwd(dqn, dkn, z, heads, dh, q_norm, k_norm, r_q, r_k, "qk_prep_bwd")
    dzf, db_forget = _forget_bwd(_pad_lanes(df_k[:, 0, :].T), zf, b_forget, "forget_bwd")
    grad["b_fgate"] = db_forget[:, :heads]

    dz = jnp.concatenate([dq, dk, dv, du5.astype(BF16), dz_gf, dz_gs], axis=1)
    gw_main, (from_chips["w_gate_up"],) = _mm(u, dz, "tn", BF16, "mm_gw_main", job=gate_up_swap(2, got))
    gw_forget = _mm(u, dzf, "tn", BF16, "mm_gw_forget")

    def from_grads(k, a, b):
        lo, hi = k * c_in + a, k * c_in + b
        if hi <= col_v:
            return gw_main[:, lo:hi]
        if hi <= col_f:
            return gw_forget[:, lo - col_v:hi - col_v]
        return gw_main[:, lo - heads:hi - heads]

    def in_part(k):
        lo, hi = k * c_in, (k + 1) * c_in
        cuts = [(max(lo, a), min(hi, b)) for a, b in ((0, col_v), (col_v, col_f), (col_f, in_cols))]
        return jnp.concatenate([from_grads(k, a - lo, b - lo) for a, b in cuts if a < b], axis=1)

    sums["w_in"] = pair_sums("w_in", jnp.stack([in_part(k) for k in range(N_DEV)]))
    in_rows = _split_rows(d, (0.8, 0.2))
    du, got = _mm(dz, w_main, "nt", F32, "mm_du", job=_ChipSwapJob([sums["w_in"]], rows=in_rows[0]))
    du_f = _mm(dzf, w_forget, "nt", F32, "mm_du_f")
    (dx, _, grad["g_mix"]), (from_chips["w_in"],) = _rms_bwd(
        [du, du_f], xs, r_mix, g_mix, [dh_], "rms_mix_bwd", job=_ChipSwapJob([sums["w_in"]], rows=in_rows[1], into=got))

    out_g, out_d, out_m, out_v = {}, {}, {}, {}
    for n in COLUMN_SHARDED + ROW_SHARDED:
        res = _adamw_shard(weights[n][0], mom_m[n][0], mom_v[n][0], sums[n], from_chips[n], chip, "adamw_" + n)
        out_g[n], out_d[n], out_m[n], out_v[n] = (r[None] for r in res)

    small = [n for n in WEIGHTS if n not in COLUMN_SHARDED + ROW_SHARDED]
    g_small = _all_reduce_small(_pack([grad[n] for n in small]), "ar_small")
    like = [weights[n] for n in small]
    res = _adamw_packed(_pack(like), _pack([mom_m[n] for n in small]), _pack([mom_v[n] for n in small]), g_small,
                        "adamw_small")
    for store, packed in zip((out_g, out_d, out_m, out_v), (g_small, *res)):
        for n, a in zip(small, _unpack(packed, like)):
            store[n] = a

    return (loss, dx[None], *[out_g[n] for n in WEIGHTS], *[out_d[n] for n in WEIGHTS],
            *[out_m[n] for n in WEIGHTS], *[out_v[n] for n in WEIGHTS])
```

```python
import functools
import math

import jax
import jax.numpy as jnp
from jax import lax
from jax.experimental import pallas as pl
from jax.experimental.pallas import tpu as pltpu

F32 = jnp.float32
BF16 = jnp.bfloat16

V7X_VMEM_LIMIT = 56 * 2**20
LANES = 128
N_DEV = 8
MESH = pl.DeviceIdType.MESH

RMS_EPS = 1e-6
MASK_VALUE = -1e30
ADAM_LR, ADAM_B1, ADAM_B2, ADAM_EPS, ADAM_WD, ADAM_STEP = 0.001, 0.9, 0.999, 1e-08, 0.01, 10
GELU_C = math.sqrt(2.0 / math.pi)
GELU_A = 0.044715


def _cp(sem=None):
    return pltpu.CompilerParams(dimension_semantics=sem, vmem_limit_bytes=V7X_VMEM_LIMIT)


def _tile(n, pref, unit=LANES):
    if n <= pref:
        return n
    t = (pref // unit) * unit
    while t >= unit:
        if n % t == 0:
            return t
        t -= unit
    raise ValueError(f"no tile for {n}")


def _row_tile(rows, cols, bytes_per_row_elem=4, target=2 * 2**20, unit=16):
    best = None
    for t in range(unit, rows + 1, unit):
        if rows % t == 0 and t * cols * bytes_per_row_elem <= target:
            best = t
    if best is None:
        best = unit if rows % unit == 0 else rows
    return best


def _sigmoid(x):
    return 1.0 / (1.0 + jnp.exp(-x))


def _position():
    return lax.axis_index("x"), lax.axis_index("y"), lax.axis_index("c")


def _other_chips(x, y):
    return [(1 - x, y), (x, 1 - y), (1 - x, 1 - y)]


def _all_gather(shard, name):
    job = _GatherJob([shard])

    def body(x_ref, out_ref, *scratch):
        job.begin([x_ref], [out_ref], scratch)
        job.middle([x_ref], [out_ref], scratch)
        job.end([x_ref], [out_ref], scratch)

    return pl.pallas_call(
        body, name=name,
        out_shape=job.out_shape[0],
        in_specs=[pl.BlockSpec(memory_space=pltpu.HBM)],
        out_specs=pl.BlockSpec(memory_space=pltpu.HBM),
        scratch_shapes=job.scratch,
    )(shard)


def _swap_with_sibling(parts, name):
    _, rows, cols = parts.shape

    def body(p_ref, out_ref, send_sems, recv_sems):
        x, y, c = _position()
        copies = []
        for j in range(4):
            copies.append(pltpu.make_async_remote_copy(
                src_ref=p_ref.at[2 * j + (1 - c)], dst_ref=out_ref.at[j],
                send_sem=send_sems.at[j], recv_sem=recv_sems.at[j], device_id=(x, y, 1 - c), device_id_type=MESH))
        for cp in copies:
            cp.start()
        for cp in copies:
            cp.wait()

    return pl.pallas_call(
        body, name=name,
        out_shape=jax.ShapeDtypeStruct((4, rows, cols), parts.dtype),
        in_specs=[pl.BlockSpec(memory_space=pltpu.HBM)],
        out_specs=pl.BlockSpec(memory_space=pltpu.HBM),
        scratch_shapes=[pltpu.SemaphoreType.DMA((4,)), pltpu.SemaphoreType.DMA((4,))],
    )(parts)


def _add_sibling(parts, got, core, name):
    _, rows, cols = parts.shape
    tr = _row_tile(rows, cols)

    def body(core_ref, a_ref, b_ref, o_ref):
        o_ref[...] = (a_ref[...].astype(F32) + b_ref[...].astype(F32)).astype(o_ref.dtype)

    return pl.pallas_call(
        body, name=name,
        grid_spec=pltpu.PrefetchScalarGridSpec(
            num_scalar_prefetch=1, grid=(4, rows // tr),
            in_specs=[pl.BlockSpec((None, tr, cols), lambda j, i, core_ref: (2 * j + core_ref[0], i, 0)),
                      pl.BlockSpec((None, tr, cols), lambda j, i, core_ref: (j, i, 0))],
            out_specs=pl.BlockSpec((None, tr, cols), lambda j, i, core_ref: (j, i, 0))),
        out_shape=jax.ShapeDtypeStruct((4, rows, cols), BF16),
        compiler_params=_cp(("parallel", "parallel")),
    )(core, parts, got)


def _all_reduce_small(packed, name):
    rows, cols = packed.shape

    def body(x_ref, out_ref, gathered, send_sems, recv_sems):
        x, y, c = _position()
        me, sibling = (x, y, c), (x, y, 1 - c)
        chips = _other_chips(x, y)

        def slot(px, py, pc):
            return gathered.at[4 * px + 2 * py + pc]

        def copy(k, block, to, src=None):
            return pltpu.make_async_remote_copy(
                src_ref=slot(*block) if src is None else src, dst_ref=slot(*block),
                send_sem=send_sems.at[k], recv_sem=recv_sems.at[k], device_id=to, device_id_type=MESH)

        first = [copy(0, me, sibling, src=x_ref)]
        first += [copy(1 + j, me, (*chip, c), src=x_ref) for j, chip in enumerate(chips)]
        for cp in first:
            cp.start()
        passed = [copy(4 + j, (*chip, c), sibling) for j, chip in enumerate(chips)]
        for j, chip in enumerate(chips):
            copy(1 + j, (*chip, c), me).wait_recv()
            passed[j].start()
        copy(0, sibling, me).wait_recv()
        for j, chip in enumerate(chips):
            copy(4 + j, (*chip, 1 - c), me).wait_recv()
        for cp in first + passed:
            cp.wait_send()
        gathered[4 * x + 2 * y + c] = x_ref[...]
        total = gathered[0]
        for k in range(1, N_DEV):
            total = total + gathered[k]
        out_ref[...] = total

    return pl.pallas_call(
        body, name=name,
        out_shape=jax.ShapeDtypeStruct((rows, cols), F32),
        in_specs=[pl.BlockSpec(memory_space=pltpu.VMEM)],
        out_specs=pl.BlockSpec(memory_space=pltpu.VMEM),
        scratch_shapes=[pltpu.VMEM((N_DEV, rows, cols), F32),
                        pltpu.SemaphoreType.DMA((7,)), pltpu.SemaphoreType.DMA((7,))],
        compiler_params=pltpu.CompilerParams(vmem_limit_bytes=V7X_VMEM_LIMIT),
    )(packed)


class _GatherJob:
    def __init__(self, shards, rows=None, into=None):
        self.n, self.rows = len(shards), rows
        self.inputs = list(shards) + list(into or [])
        self.aliases = {self.n + i: i for i in range(len(into or []))}
        self.out_shape = [jax.ShapeDtypeStruct((N_DEV,) + s.shape, s.dtype) for s in shards]
        self.scratch = [pltpu.SemaphoreType.DMA((self.COPIES * self.n,)),
                        pltpu.SemaphoreType.DMA((self.COPIES * self.n,)), pltpu.SemaphoreType.DMA((self.n,))]
        self.whole = [rows or (0, s.shape[0]) for s in shards]
        self.first = [(lo, max(16, n // 32 * 16)) if n > 16 else (lo, n) for lo, n in self.whole]
        self.second = [(lo + h, n - h) for (lo, n), (_, h) in zip(self.whole, self.first)]

    COPIES = 8
    TO_SIBLING, TO_X, TO_Y, RELAY_X_BLOCK, RELAY_Y_BLOCK, PASS_X, PASS_Y, PASS_DIAGONAL = range(8)

    def _plan(self, ins, outs, scratch):
        send_sems, recv_sems, local_sems = scratch
        x, y, c = _position()

        def slot(i, block, rows):
            px, py, pc = block
            return _rows_of(outs[i].at[4 * px + 2 * py + pc], rows)

        def copy(i, k, block, to, rows, own=False):
            return pltpu.make_async_remote_copy(
                src_ref=_rows_of(ins[i], rows) if own else slot(i, block, rows), dst_ref=slot(i, block, rows),
                send_sem=send_sems.at[self.COPIES * i + k], recv_sem=recv_sems.at[self.COPIES * i + k],
                device_id=to, device_id_type=MESH)

        def mine(i):
            return pltpu.make_async_copy(_rows_of(ins[i], self.whole[i]), slot(i, (x, y, c), self.whole[i]),
                                         local_sems.at[i])

        return x, y, c, copy, mine

    def begin(self, ins, outs, scratch):
        x, y, c, copy, mine = self._plan(ins, outs, scratch)
        me = (x, y, c)
        for i in range(self.n):
            mine(i).start()
            copy(i, self.TO_SIBLING, me, (x, y, 1 - c), self.whole[i], own=True).start()
            copy(i, self.TO_X, me, (1 - x, y, c), self.whole[i], own=True).start()
            copy(i, self.TO_Y, me, (x, 1 - y, c), self.whole[i], own=True).start()

    def middle(self, ins, outs, scratch):
        x, y, c, copy, mine = self._plan(ins, outs, scratch)
        me, sibling, x_chip, y_chip = (x, y, c), (x, y, 1 - c), (1 - x, y, c), (x, 1 - y, c)
        for i in range(self.n):
            copy(i, self.TO_X, x_chip, me, self.whole[i]).wait_recv()
            copy(i, self.RELAY_X_BLOCK, x_chip, y_chip, self.first[i]).start()
            copy(i, self.PASS_X, x_chip, sibling, self.whole[i]).start()
            copy(i, self.TO_Y, y_chip, me, self.whole[i]).wait_recv()
            if self.second[i][1]:
                copy(i, self.RELAY_Y_BLOCK, y_chip, x_chip, self.second[i]).start()
            copy(i, self.PASS_Y, y_chip, sibling, self.whole[i]).start()

    def end(self, ins, outs, scratch):
        x, y, c, copy, mine = self._plan(ins, outs, scratch)
        me, sibling, x_chip, y_chip, diagonal = (x, y, c), (x, y, 1 - c), (1 - x, y, c), (x, 1 - y, c), (1 - x, 1 - y, c)
        for i in range(self.n):
            copy(i, self.RELAY_X_BLOCK, diagonal, me, self.first[i]).wait_recv()
            if self.second[i][1]:
                copy(i, self.RELAY_Y_BLOCK, diagonal, me, self.second[i]).wait_recv()
            copy(i, self.PASS_DIAGONAL, diagonal, sibling, self.whole[i]).start()
        for i in range(self.n):
            copy(i, self.TO_SIBLING, sibling, me, self.whole[i]).wait_recv()
            copy(i, self.PASS_X, (1 - x, y, 1 - c), me, self.whole[i]).wait_recv()
            copy(i, self.PASS_Y, (x, 1 - y, 1 - c), me, self.whole[i]).wait_recv()
            copy(i, self.PASS_DIAGONAL, (1 - x, 1 - y, 1 - c), me, self.whole[i]).wait_recv()
            copy(i, self.TO_SIBLING, me, sibling, self.whole[i], own=True).wait_send()
            copy(i, self.TO_X, me, x_chip, self.whole[i], own=True).wait_send()
            copy(i, self.TO_Y, me, y_chip, self.whole[i], own=True).wait_send()
            copy(i, self.RELAY_X_BLOCK, x_chip, y_chip, self.first[i]).wait_send()
            if self.second[i][1]:
                copy(i, self.RELAY_Y_BLOCK, y_chip, x_chip, self.second[i]).wait_send()
            copy(i, self.PASS_X, x_chip, sibling, self.whole[i]).wait_send()
            copy(i, self.PASS_Y, y_chip, sibling, self.whole[i]).wait_send()
            copy(i, self.PASS_DIAGONAL, diagonal, sibling, self.whole[i]).wait_send()
            mine(i).wait()


class _ChipSwapJob:
    def __init__(self, sums, rows=None, into=None):
        self.n, self.rows = len(sums), rows
        self.inputs = list(sums) + list(into or [])
        self.aliases = {self.n + i: i for i in range(len(into or []))}
        self.out_shape = [jax.ShapeDtypeStruct((3,) + s.shape[1:], s.dtype) for s in sums]
        self.scratch = [pltpu.SemaphoreType.DMA((3 * self.n,)), pltpu.SemaphoreType.DMA((3 * self.n,))]

    def _copies(self, ins, outs, scratch):
        send_sems, recv_sems = scratch
        x, y, c = _position()
        return [pltpu.make_async_remote_copy(
            src_ref=_rows_of(ins[i].at[2 * px + py], self.rows), dst_ref=_rows_of(outs[i].at[k], self.rows),
            send_sem=send_sems.at[3 * i + k], recv_sem=recv_sems.at[3 * i + k],
            device_id=(px, py, c), device_id_type=MESH)
            for i in range(self.n) for k, (px, py) in enumerate(_other_chips(x, y))]

    def begin(self, ins, outs, scratch):
        for cp in self._copies(ins, outs, scratch):
            cp.start()

    def middle(self, ins, outs, scratch):
        pass

    def end(self, ins, outs, scratch):
        for cp in self._copies(ins, outs, scratch):
            cp.wait()


class _SiblingSwapJob:
    aliases = {}

    def __init__(self, parts):
        self.inputs = list(parts)
        self.out_shape = [jax.ShapeDtypeStruct((4,) + p.shape[1:], p.dtype) for p in parts]
        n = len(parts)
        self.scratch = [pltpu.SemaphoreType.DMA((4 * n,)), pltpu.SemaphoreType.DMA((4 * n,))]

    def _copies(self, ins, outs, scratch):
        send_sems, recv_sems = scratch
        x, y, c = _position()
        return [pltpu.make_async_remote_copy(
            src_ref=ins[i].at[2 * j + (1 - c)], dst_ref=outs[i].at[j], send_sem=send_sems.at[4 * i + j],
            recv_sem=recv_sems.at[4 * i + j], device_id=(x, y, 1 - c), device_id_type=MESH)
            for i in range(len(ins)) for j in range(4)]

    def begin(self, ins, outs, scratch):
        for cp in self._copies(ins, outs, scratch):
            cp.start()

    def middle(self, ins, outs, scratch):
        pass

    def end(self, ins, outs, scratch):
        for cp in self._copies(ins, outs, scratch):
            cp.wait()


class _JobGroup:
    def __init__(self, jobs):
        self.jobs = list(jobs)
        self.inputs = [a for j in jobs for a in j.inputs]
        self.out_shape = [s for j in jobs for s in j.out_shape]
        self.scratch = [s for j in jobs for s in j.scratch]
        self.aliases, at_in, at_out = {}, 0, 0
        for j in jobs:
            self.aliases.update({at_in + i: at_out + o for i, o in j.aliases.items()})
            at_in, at_out = at_in + len(j.inputs), at_out + len(j.out_shape)

    def _each(self, phase, ins, outs, scratch):
        for j in self.jobs:
            n_in, n_out, n_scr = len(j.inputs), len(j.out_shape), len(j.scratch)
            getattr(j, phase)(ins[:n_in], outs[:n_out], scratch[:n_scr])
            ins, outs, scratch = ins[n_in:], outs[n_out:], scratch[n_scr:]

    def begin(self, ins, outs, scratch):
        self._each("begin", ins, outs, scratch)

    def middle(self, ins, outs, scratch):
        self._each("middle", ins, outs, scratch)

    def end(self, ins, outs, scratch):
        self._each("end", ins, outs, scratch)


def _rows_of(ref, rows):
    return ref if rows is None else ref.at[pl.ds(rows[0], rows[1])]


def _call(body, *, name, grid, in_specs, out_specs, out_shape, scratch_shapes, semantics, operands, job=None):
    if job is None:
        return pl.pallas_call(
            body, name=name, grid=grid, in_specs=in_specs, out_specs=out_specs, out_shape=out_shape,
            scratch_shapes=scratch_shapes, compiler_params=_cp(semantics))(*operands)
    n_in, n_out, n_scr = len(in_specs), len(out_specs), len(scratch_shapes)
    j_in, j_out = len(job.inputs), len(job.out_shape)
    n_steps = math.prod(grid)
    hbm = pl.BlockSpec(memory_space=pltpu.HBM)

    def carrier(*refs):
        ins, refs = refs[:n_in], refs[n_in:]
        job_ins, refs = refs[:j_in], refs[j_in:]
        outs, refs = refs[:n_out], refs[n_out:]
        job_outs, refs = refs[:j_out], refs[j_out:]
        scr, job_scr = refs[:n_scr], refs[n_scr:]
        step = pl.program_id(0)
        for axis in range(1, len(grid)):
            step = step * grid[axis] + pl.program_id(axis)

        @pl.when(step == 0)
        def _():
            job.begin(job_ins, job_outs, job_scr)

        body(*ins, *outs, *scr)

        @pl.when(step == (3 * n_steps) // 4)
        def _():
            job.middle(job_ins, job_outs, job_scr)

        @pl.when(step == n_steps - 1)
        def _():
            job.end(job_ins, job_outs, job_scr)

    res = pl.pallas_call(
        carrier, name=name, grid=grid,
        in_specs=list(in_specs) + [hbm] * j_in, out_specs=list(out_specs) + [hbm] * j_out,
        out_shape=list(out_shape) + job.out_shape, scratch_shapes=list(scratch_shapes) + job.scratch,
        input_output_aliases={n_in + i: n_out + o for i, o in job.aliases.items()},
        compiler_params=_cp(("arbitrary",) * len(grid)))(*operands, *job.inputs)
    return res[:n_out], res[n_out:]


MM_VMEM_BUDGET = 44 * 2**20
MM_TILE_CAP = 1536
MM_MIN_INTENSITY = 340


def _divisor_tiles(n, cap):
    return [t for t in range(LANES, min(n, cap) + 1, LANES) if n % t == 0] or [n]


def _mm_tiles(m, n_unit, k_unit, whole_k, a_bytes, b_bytes, o_bytes):
    best, best_key = None, None
    for tm in _divisor_tiles(m, 1024):
        for tn in _divisor_tiles(n_unit, MM_TILE_CAP):
            for tk in _divisor_tiles(k_unit, k_unit):
                one_block = whole_k and tk == k_unit
                need = (2 * (tm * tk * a_bytes + tk * tn * b_bytes) + 2 * tm * tn * o_bytes + tm * tn * 4
                        + (0 if one_block else tm * tn * 4))
                intensity = tm * tn / (tm + tn)
                key = (intensity >= MM_MIN_INTENSITY, one_block, intensity, tk)
                if need <= MM_VMEM_BUDGET and (best_key is None or key > best_key):
                    best, best_key = (tm, tn, tk), key
    if best is None:
        raise ValueError(f"no matmul tiles for {(m, n_unit, k_unit)}")
    return best


def _mm(a, b, mode, out_dtype, name, job=None, b_stacked=False, out_stack=None):
    b_rows, b_cols = (b.shape[1], b.shape[0] * b.shape[2]) if b_stacked else b.shape
    b_unit = b.shape[2] if b_stacked else b_cols
    if mode == "nn":
        (m, k), (k2, n) = a.shape, (b_rows, b_cols)
    elif mode == "nt":
        (m, k), (n, k2) = a.shape, (b_rows, b_cols)
    else:
        (k, m), (k2, n) = a.shape, (b_rows, b_cols)
    assert k == k2, (name, a.shape, b.shape)
    n_unit = n // out_stack if out_stack else (b_unit if b_stacked and mode != "nt" else n)
    k_unit = b_unit if b_stacked and mode == "nt" else k
    tm, tn, tk = _mm_tiles(m, n_unit, k_unit, k_unit == k, a.dtype.itemsize, b.dtype.itemsize,
                           jnp.dtype(out_dtype).itemsize)
    nk = k // tk
    per_n, per_k = n_unit // tn, k_unit // tk
    if mode == "tn":
        a_spec = pl.BlockSpec((tk, tm), lambda i, j, l: (l, i))
        dims = (((0,), (0,)), ((), ()))
    else:
        a_spec = pl.BlockSpec((tm, tk), lambda i, j, l: (i, l))
        dims = (((1,), (1,)), ((), ())) if mode == "nt" else (((1,), (0,)), ((), ()))
    if mode == "nt" and b_stacked:
        b_spec = pl.BlockSpec((None, tn, tk), lambda i, j, l: (l // per_k, j, l % per_k))
    elif mode == "nt":
        b_spec = pl.BlockSpec((tn, tk), lambda i, j, l: (j, l))
    elif b_stacked:
        b_spec = pl.BlockSpec((None, tk, tn), lambda i, j, l: (j // per_n, l, j % per_n))
    else:
        b_spec = pl.BlockSpec((tk, tn), lambda i, j, l: (l, j))
    if out_stack:
        o_spec = pl.BlockSpec((None, tm, tn), lambda i, j, l: (j // per_n, i, j % per_n))
        o_shape = jax.ShapeDtypeStruct((out_stack, m, n_unit), out_dtype)
    else:
        o_spec = pl.BlockSpec((tm, tn), lambda i, j, l: (i, j))
        o_shape = jax.ShapeDtypeStruct((m, n), out_dtype)

    def product(a_ref, b_ref):
        return lax.dot_general(a_ref[...].astype(BF16), b_ref[...].astype(BF16), dims, preferred_element_type=F32)

    def body_whole_k(a_ref, b_ref, o_ref):
        o_ref[...] = product(a_ref, b_ref).astype(o_ref.dtype)

    def body_split_k(a_ref, b_ref, o_ref, acc_ref):
        l = pl.program_id(2)

        @pl.when(l == 0)
        def _():
            acc_ref[...] = product(a_ref, b_ref)

        @pl.when(l > 0)
        def _():
            acc_ref[...] += product(a_ref, b_ref)

        @pl.when(l == nk - 1)
        def _():
            o_ref[...] = acc_ref[...].astype(o_ref.dtype)

    res = _call(
        body_whole_k if nk == 1 else body_split_k, name=name, grid=(m // tm, n // tn, nk),
        in_specs=[a_spec, b_spec],
        out_specs=[o_spec],
        out_shape=[o_shape],
        scratch_shapes=[] if nk == 1 else [pltpu.VMEM((tm, tn), F32)],
        semantics=("parallel", "parallel", "arbitrary"), operands=(a, b), job=job)
    return res[0] if job is None else (res[0][0], res[1])


def _rms_fwd(x, g, name):
    t, d = x.shape
    tm = _tile(t, 256, 16)

    def body(x_ref, g_ref, u_ref, r_ref):
        xv = x_ref[...]
        r = lax.rsqrt(jnp.mean(xv * xv, axis=-1, keepdims=True) + RMS_EPS)
        u_ref[...] = (xv * r * g_ref[...]).astype(u_ref.dtype)
        r_ref[...] = r

    return pl.pallas_call(
        body, name=name, grid=(t // tm,),
        in_specs=[pl.BlockSpec((tm, d), lambda i: (i, 0)), pl.BlockSpec((1, d), lambda i: (0, 0))],
        out_specs=[pl.BlockSpec((tm, d), lambda i: (i, 0)), pl.BlockSpec((tm, 1), lambda i: (i, 0))],
        out_shape=[jax.ShapeDtypeStruct((t, d), BF16), jax.ShapeDtypeStruct((t, 1), F32)],
        compiler_params=_cp(("parallel",)),
    )(x, g)


def _rms_bwd(dn_parts, x, r, g, extra, name, job=None):
    t, d = x.shape
    tm = _tile(t, 128, 16)
    n_dn, n_extra = len(dn_parts), len(extra)

    def body(*refs):
        dn_refs = refs[:n_dn]
        x_ref, r_ref, g_ref = refs[n_dn:n_dn + 3]
        extra_refs = refs[n_dn + 3:n_dn + 3 + n_extra]
        dx_ref, dxb_ref, dg_ref = refs[n_dn + 3 + n_extra:]
        xhat = x_ref[...] * r_ref[...]
        dnv = dn_refs[0][...].astype(F32)
        for p in dn_refs[1:]:
            dnv = dnv + p[...].astype(F32)
        gd = dnv * g_ref[...]
        dx = r_ref[...] * (gd - xhat * jnp.mean(gd * xhat, axis=-1, keepdims=True))
        for e in extra_refs:
            dx = dx + e[...].astype(F32)
        dx_ref[...] = dx
        dxb_ref[...] = dx.astype(dxb_ref.dtype)

        @pl.when(pl.program_id(0) == 0)
        def _():
            dg_ref[...] = jnp.zeros_like(dg_ref)

        dg_ref[...] += jnp.sum(dnv * xhat, axis=0, keepdims=True)

    row = pl.BlockSpec((tm, d), lambda i: (i, 0))
    return _call(
        body, name=name, grid=(t // tm,),
        in_specs=[row] * n_dn + [row, pl.BlockSpec((tm, 1), lambda i: (i, 0)), pl.BlockSpec((1, d), lambda i: (0, 0))]
        + [row] * n_extra,
        out_specs=[row, row, pl.BlockSpec((1, d), lambda i: (0, 0))],
        out_shape=[jax.ShapeDtypeStruct((t, d), F32), jax.ShapeDtypeStruct((t, d), BF16), jax.ShapeDtypeStruct((1, d), F32)],
        scratch_shapes=[], semantics=("arbitrary",), operands=(*dn_parts, x, r, g, *extra), job=job)


def _gate_merge_fwd(z, gate_col, b_gates, pf, ps, name):
    t, d = pf.shape
    tm, tn = _tile(t, 512, 16), _tile(math.gcd(d, gate_col), 512)
    nj, off = d // tn, gate_col // tn
    assert gate_col % tn == 0

    def body(zf_ref, zs_ref, bf_ref, bs_ref, pf_ref, ps_ref, o_ref):
        gf = _sigmoid(zf_ref[...].astype(F32) + bf_ref[...])
        gs = _sigmoid(zs_ref[...].astype(F32) + bs_ref[...])
        o_ref[...] = (gf * pf_ref[...].astype(F32) + gs * ps_ref[...].astype(F32)).astype(o_ref.dtype)

    blk = pl.BlockSpec((tm, tn), lambda i, j: (i, j))
    return pl.pallas_call(
        body, name=name, grid=(t // tm, nj),
        in_specs=[pl.BlockSpec((tm, tn), lambda i, j: (i, off + j)), pl.BlockSpec((tm, tn), lambda i, j: (i, off + nj + j)),
                  pl.BlockSpec((1, tn), lambda i, j: (0, j)), pl.BlockSpec((1, tn), lambda i, j: (0, nj + j)), blk, blk],
        out_specs=blk,
        out_shape=jax.ShapeDtypeStruct((t, d), BF16),
        compiler_params=_cp(("parallel", "parallel")),
    )(z, z, b_gates, b_gates, pf, ps)


def _gate_merge_bwd(dm, z, gate_col, b_gates, pf, ps, name):
    t, d = pf.shape
    tm, tn = _tile(t, 512, 16), _tile(math.gcd(d, gate_col), 512)
    nj, off = d // tn, gate_col // tn

    def body(dm_ref, zf_ref, zs_ref, bf_ref, bs_ref, pf_ref, ps_ref, dpf_ref, dps_ref, dzf_ref, dzs_ref, dbf_ref, dbs_ref):
        gf = _sigmoid(zf_ref[...].astype(F32) + bf_ref[...])
        gs = _sigmoid(zs_ref[...].astype(F32) + bs_ref[...])
        dmv = dm_ref[...].astype(F32)
        dpf_ref[...] = (dmv * gf).astype(dpf_ref.dtype)
        dps_ref[...] = (dmv * gs).astype(dps_ref.dtype)
        dzf = dmv * pf_ref[...].astype(F32) * gf * (1.0 - gf)
        dzs = dmv * ps_ref[...].astype(F32) * gs * (1.0 - gs)
        dzf_ref[...] = dzf.astype(dzf_ref.dtype)
        dzs_ref[...] = dzs.astype(dzs_ref.dtype)

        @pl.when(pl.program_id(1) == 0)
        def _():
            dbf_ref[...] = jnp.zeros_like(dbf_ref)
            dbs_ref[...] = jnp.zeros_like(dbs_ref)

        dbf_ref[...] += jnp.sum(dzf, axis=0, keepdims=True)
        dbs_ref[...] += jnp.sum(dzs, axis=0, keepdims=True)

    blk = pl.BlockSpec((tm, tn), lambda j, i: (i, j))
    lo = pl.BlockSpec((1, tn), lambda j, i: (0, j))
    hi = pl.BlockSpec((1, tn), lambda j, i: (0, nj + j))
    return pl.pallas_call(
        body, name=name, grid=(nj, t // tm),
        in_specs=[blk, pl.BlockSpec((tm, tn), lambda j, i: (i, off + j)), pl.BlockSpec((tm, tn), lambda j, i: (i, off + nj + j)),
                  lo, hi, blk, blk],
        out_specs=[blk, blk, blk, blk, lo, lo],
        out_shape=[jax.ShapeDtypeStruct((t, d), BF16)] * 4 + [jax.ShapeDtypeStruct((1, d), F32)] * 2,
        compiler_params=_cp(("parallel", "arbitrary")),
    )(dm, z, z, b_gates, b_gates, pf, ps)


def _resid_rms(x, mo, g, name):
    t, d = x.shape
    tm = _tile(t, 256, 16)

    def body(x_ref, mo_ref, g_ref, h_ref, hn_ref, r_ref):
        h = x_ref[...] + mo_ref[...].astype(F32)
        r = lax.rsqrt(jnp.mean(h * h, axis=-1, keepdims=True) + RMS_EPS)
        h_ref[...] = h
        hn_ref[...] = (h * r * g_ref[...]).astype(hn_ref.dtype)
        r_ref[...] = r

    row = pl.BlockSpec((tm, d), lambda i: (i, 0))
    col = pl.BlockSpec((tm, 1), lambda i: (i, 0))
    return pl.pallas_call(
        body, name=name, grid=(t // tm,),
        in_specs=[row, row, pl.BlockSpec((1, d), lambda i: (0, 0))],
        out_specs=[row, row, col],
        out_shape=[jax.ShapeDtypeStruct((t, d), F32), jax.ShapeDtypeStruct((t, d), BF16), jax.ShapeDtypeStruct((t, 1), F32)],
        compiler_params=_cp(("parallel",)),
    )(x, mo, g)


def _swiglu_fwd(gu, name):
    t, f2 = gu.shape
    f = f2 // 2
    tm, tn = _tile(t, 512, 16), _tile(f, 1024)
    nj = f // tn

    def body(g_ref, u_ref, o_ref):
        gate = g_ref[...].astype(F32)
        o_ref[...] = (gate * _sigmoid(gate) * u_ref[...].astype(F32)).astype(o_ref.dtype)

    return pl.pallas_call(
        body, name=name, grid=(t // tm, nj),
        in_specs=[pl.BlockSpec((tm, tn), lambda i, j: (i, j)), pl.BlockSpec((tm, tn), lambda i, j: (i, nj + j))],
        out_specs=pl.BlockSpec((tm, tn), lambda i, j: (i, j)),
        out_shape=jax.ShapeDtypeStruct((t, f), BF16),
        compiler_params=_cp(("parallel", "parallel")),
    )(gu, gu)


def _swiglu_bwd(gu, dact, name, job=None):
    t, f2 = gu.shape
    f = f2 // 2
    tm, tn = _tile(t, 512, 16), _tile(f, 1024)
    nj = f // tn

    def body(g_ref, u_ref, da_ref, dg_ref, du_ref):
        gate = g_ref[...].astype(F32)
        s = _sigmoid(gate)
        da = da_ref[...].astype(F32)
        dg_ref[...] = (da * u_ref[...].astype(F32) * s * (1.0 + gate * (1.0 - s))).astype(dg_ref.dtype)
        du_ref[...] = (da * gate * s).astype(du_ref.dtype)

    lo = pl.BlockSpec((tm, tn), lambda i, j: (i, j))
    return _call(
        body, name=name, grid=(t // tm, nj),
        in_specs=[lo, pl.BlockSpec((tm, tn), lambda i, j: (i, nj + j)), lo],
        out_specs=[lo, lo],
        out_shape=[jax.ShapeDtypeStruct((t, f), BF16)] * 2,
        scratch_shapes=[], semantics=("parallel", "parallel"), operands=(gu, gu, dact), job=job)


def _loss_head(h, dn, target, name):
    t, d = h.shape
    tm = _tile(t, 256, 16)

    def body(h_ref, dn_ref, t_ref, loss_ref, dy_ref, dyb_ref):
        err = h_ref[...] + dn_ref[...].astype(F32) - t_ref[...]
        dy_ref[...] = err * (1.0 / d)
        dyb_ref[...] = (err * (1.0 / d)).astype(dyb_ref.dtype)

        @pl.when(pl.program_id(0) == 0)
        def _():
            loss_ref[...] = jnp.zeros_like(loss_ref)

        loss_ref[...] += 0.5 * jnp.sum(jnp.mean(err * err, axis=-1, keepdims=True))

    row = pl.BlockSpec((tm, d), lambda i: (i, 0))
    return pl.pallas_call(
        body, name=name, grid=(t // tm,),
        in_specs=[row, row, row],
        out_specs=[pl.BlockSpec((8, LANES), lambda i: (0, 0)), row, row],
        out_shape=[jax.ShapeDtypeStruct((8, LANES), F32), jax.ShapeDtypeStruct((t, d), F32), jax.ShapeDtypeStruct((t, d), BF16)],
        compiler_params=_cp(("arbitrary",)),
    )(h, dn, target)


def _qk_prep(z, heads, dh, q_norm, k_norm, name):
    t = z.shape[0]
    tq = _tile(t, 512, 16)
    scale = 1.0 / math.sqrt(dh)

    def body(q_ref, k_ref, gq_ref, gk_ref, qn_ref, kn_ref, rq_ref, rk_ref):
        q = q_ref[...].astype(F32)
        k = k_ref[...].astype(F32)
        rq = lax.rsqrt(jnp.mean(q * q, axis=-1, keepdims=True) + RMS_EPS)
        rk = lax.rsqrt(jnp.mean(k * k, axis=-1, keepdims=True) + RMS_EPS)
        qn_ref[...] = (q * rq * gq_ref[...] * scale).astype(qn_ref.dtype)
        kn_ref[...] = (k * rk * gk_ref[...]).astype(kn_ref.dtype)
        rq_ref[...] = rq
        rk_ref[...] = rk

    blk = pl.BlockSpec((tq, dh), lambda i, h: (i, h))
    vec = pl.BlockSpec((1, dh), lambda i, h: (0, 0))
    col = pl.BlockSpec((None, tq, 1), lambda i, h: (h, i, 0))
    return pl.pallas_call(
        body, name=name, grid=(t // tq, heads),
        in_specs=[blk, pl.BlockSpec((tq, dh), lambda i, h: (i, heads + h)), vec, vec],
        out_specs=[blk, blk, col, col],
        out_shape=[jax.ShapeDtypeStruct((t, heads * dh), BF16)] * 2 + [jax.ShapeDtypeStruct((heads, t, 1), F32)] * 2,
        compiler_params=_cp(("parallel", "parallel")),
    )(z, z, q_norm, k_norm)


def _qk_prep_bwd(dqn, dkn, z, heads, dh, q_norm, k_norm, rq, rk, name):
    t = z.shape[0]
    tq = _tile(t, 512, 16)
    scale = 1.0 / math.sqrt(dh)

    def norm_bwd(dy, xv, r, g):
        xhat = xv * r
        gd = dy * g
        return r * (gd - xhat * jnp.mean(gd * xhat, axis=-1, keepdims=True)), jnp.sum(dy * xhat, axis=0, keepdims=True)

    def body(dqn_ref, dkn_ref, q_ref, k_ref, gq_ref, gk_ref, rq_ref, rk_ref, dq_ref, dk_ref, dgq_ref, dgk_ref):
        dq, dgq = norm_bwd(dqn_ref[...].astype(F32) * scale, q_ref[...].astype(F32), rq_ref[...], gq_ref[...])
        dk, dgk = norm_bwd(dkn_ref[...].astype(F32), k_ref[...].astype(F32), rk_ref[...], gk_ref[...])
        dq_ref[...] = dq.astype(dq_ref.dtype)
        dk_ref[...] = dk.astype(dk_ref.dtype)

        @pl.when((pl.program_id(0) == 0) & (pl.program_id(1) == 0))
        def _():
            dgq_ref[...] = jnp.zeros_like(dgq_ref)
            dgk_ref[...] = jnp.zeros_like(dgk_ref)

        dgq_ref[...] += dgq
        dgk_ref[...] += dgk

    blk = pl.BlockSpec((tq, dh), lambda i, h: (i, h))
    vec = pl.BlockSpec((1, dh), lambda i, h: (0, 0))
    col = pl.BlockSpec((None, tq, 1), lambda i, h: (h, i, 0))
    return pl.pallas_call(
        body, name=name, grid=(t // tq, heads),
        in_specs=[blk, blk, blk, pl.BlockSpec((tq, dh), lambda i, h: (i, heads + h)), vec, vec, col, col],
        out_specs=[blk, blk, vec, vec],
        out_shape=[jax.ShapeDtypeStruct((t, heads * dh), BF16)] * 2 + [jax.ShapeDtypeStruct((1, dh), F32)] * 2,
        compiler_params=_cp(("arbitrary", "arbitrary")),
    )(dqn, dkn, z, z, q_norm, k_norm, rq, rk)


def _tri_ones(n, upper):
    row = lax.broadcasted_iota(jnp.int32, (n, n), 0)
    col = lax.broadcasted_iota(jnp.int32, (n, n), 1)
    return jnp.where((col >= row) if upper else (col <= row), 1.0, 0.0).astype(F32)


def _forget_fwd(f, b, name):
    t, w = f.shape
    blk = _tile(t, 256, 8)
    nb = t // blk

    def body(f_ref, b_ref, out_ref):
        tri = _tri_ones(blk, upper=False)

        def step(i, carry):
            rows = pl.ds(pl.multiple_of(i * blk, blk), blk)
            logf = jax.nn.log_sigmoid(f_ref[rows, :] + b_ref[...])
            acc = jnp.dot(tri, logf, precision=lax.Precision.HIGHEST, preferred_element_type=F32) + carry
            out_ref[rows, :] = acc
            return acc[blk - 1:blk, :]

        lax.fori_loop(0, nb, step, jnp.zeros((1, w), F32))

    return pl.pallas_call(
        body, name=name,
        in_specs=[pl.BlockSpec(memory_space=pltpu.VMEM)] * 2,
        out_specs=pl.BlockSpec(memory_space=pltpu.VMEM),
        out_shape=jax.ShapeDtypeStruct((t, w), F32),
        compiler_params=_cp(),
    )(f, b)


def _forget_bwd(d_query, d_key, f, b, name):
    t, w = f.shape
    blk = _tile(t, 256, 8)
    nb = t // blk

    def body(dq_ref, dk_ref, f_ref, b_ref, df_ref, db_ref):
        tri = _tri_ones(blk, upper=True)

        def step(i, carry):
            suffix, db = carry
            rows = pl.ds(pl.multiple_of((nb - 1 - i) * blk, blk), blk)
            dcum = dq_ref[rows, :] - dk_ref[rows, :]
            dlog = suffix + jnp.dot(tri, dcum, precision=lax.Precision.HIGHEST, preferred_element_type=F32)
            df = dlog * _sigmoid(-(f_ref[rows, :] + b_ref[...]))
            df_ref[rows, :] = df
            return dlog[0:1, :], db + jnp.sum(df, axis=0, keepdims=True)

        _, db = lax.fori_loop(0, nb, step, (jnp.zeros((1, w), F32), jnp.zeros((1, w), F32)))
        db_ref[...] = db

    return pl.pallas_call(
        body, name=name,
        in_specs=[pl.BlockSpec(memory_space=pltpu.VMEM)] * 4,
        out_specs=[pl.BlockSpec(memory_space=pltpu.VMEM)] * 2,
        out_shape=[jax.ShapeDtypeStruct((t, w), F32), jax.ShapeDtypeStruct((1, w), F32)],
        compiler_params=_cp(),
    )(d_query, d_key, f, b)


def _attn_logits(q, k, f_keys, f_first, diagonal):
    s = lax.dot_general(q, k, (((1,), (1,)), ((), ())), preferred_element_type=F32)
    s = s - (f_keys - f_first)
    if diagonal:
        row = lax.broadcasted_iota(jnp.int32, s.shape, 0)
        col = lax.broadcasted_iota(jnp.int32, s.shape, 1)
        s = jnp.where(col <= row, s, MASK_VALUE)
    return s


def _block_at(i, blk):
    return pl.ds(pl.multiple_of(i * blk, blk), blk)


ATTN_BLOCK = 512


def _attn_fwd(qn, kn, v_src, v_col, fcol, frow, heads, dh, name, job=None):
    t = qn.shape[0]
    blk = _tile(t, ATTN_BLOCK)

    def body(q_ref, k_ref, v_ref, fc_ref, fr_ref, o_ref, lse_ref):
        qi = pl.program_id(1)
        q = q_ref[...]
        f_first = fc_ref[0:1, :]

        def block(ki, carry, diagonal):
            m, l, acc = carry
            keys = _block_at(ki, blk)
            s = _attn_logits(q, k_ref[keys, :], fr_ref[:, keys], f_first, diagonal)
            m_new = jnp.maximum(m, jnp.max(s, axis=-1, keepdims=True))
            alpha = jnp.exp(m - m_new)
            p = jnp.exp(s - m_new)
            l = alpha * l + jnp.sum(p, axis=-1, keepdims=True)
            acc = alpha * acc + jnp.dot(p.astype(BF16), v_ref[keys, :].astype(BF16), preferred_element_type=F32)
            return m_new, l, acc

        start = (jnp.full((blk, 1), MASK_VALUE, F32), jnp.zeros((blk, 1), F32), jnp.zeros((blk, dh), F32))
        below = lax.fori_loop(0, qi, lambda ki, carry: block(ki, carry, False), start)
        m, l, acc = block(qi, below, True)
        o_ref[...] = (acc / l).astype(o_ref.dtype)
        lse_ref[...] = m + jnp.log(l)

    qblk = pl.BlockSpec((blk, dh), lambda h, i: (i, h))
    qcol = pl.BlockSpec((None, blk, 1), lambda h, i: (h, i, 0))
    return _call(
        body, name=name, grid=(heads, t // blk),
        in_specs=[qblk,
                  pl.BlockSpec((t, dh), lambda h, i: (0, h)),
                  pl.BlockSpec((t, dh), lambda h, i: (0, v_col + h)),
                  qcol,
                  pl.BlockSpec((None, 1, t), lambda h, i: (h, 0, 0))],
        out_specs=[qblk, qcol],
        out_shape=[jax.ShapeDtypeStruct((t, heads * dh), BF16), jax.ShapeDtypeStruct((heads, t, 1), F32)],
        scratch_shapes=[], semantics=("parallel", "arbitrary"), operands=(qn, kn, v_src, fcol, frow), job=job)


def _attn_delta(o, do, heads, dh, name):
    t = o.shape[0]
    tq = _tile(t, 512, 16)

    def body(o_ref, do_ref, out_ref):
        out_ref[...] = jnp.sum(o_ref[...].astype(F32) * do_ref[...].astype(F32), axis=-1, keepdims=True)

    blk = pl.BlockSpec((tq, dh), lambda i, h: (i, h))
    return pl.pallas_call(
        body, name=name, grid=(t // tq, heads),
        in_specs=[blk, blk],
        out_specs=pl.BlockSpec((None, tq, 1), lambda i, h: (h, i, 0)),
        out_shape=jax.ShapeDtypeStruct((heads, t, 1), F32),
        compiler_params=_cp(("parallel", "parallel")),
    )(o, do)


def _attn_bwd_q(qn, kn, v_src, v_col, do, fcol, frow, lse, delta, heads, dh, name, job=None):
    t = qn.shape[0]
    blk = _tile(t, ATTN_BLOCK)

    def body(q_ref, k_ref, v_ref, do_ref, fc_ref, fr_ref, lse_ref, dl_ref, dq_ref, dfq_ref):
        qi = pl.program_id(1)
        q, dob = q_ref[...], do_ref[...].astype(BF16)
        f_first, lse, dl = fc_ref[0:1, :], lse_ref[...], dl_ref[...]

        def block(ki, carry, diagonal):
            dq, dfq = carry
            keys = _block_at(ki, blk)
            k = k_ref[keys, :]
            p = jnp.exp(_attn_logits(q, k, fr_ref[:, keys], f_first, diagonal) - lse)
            dp = lax.dot_general(dob, v_ref[keys, :].astype(BF16), (((1,), (1,)), ((), ())), preferred_element_type=F32)
            ds = p * (dp - dl)
            return dq + jnp.dot(ds.astype(BF16), k, preferred_element_type=F32), dfq + jnp.sum(ds, axis=-1, keepdims=True)

        start = (jnp.zeros((blk, dh), F32), jnp.zeros((blk, 1), F32))
        below = lax.fori_loop(0, qi, lambda ki, carry: block(ki, carry, False), start)
        dq_ref[...], dfq_ref[...] = block(qi, below, True)

    qblk = pl.BlockSpec((blk, dh), lambda h, i: (i, h))
    qcol = pl.BlockSpec((None, blk, 1), lambda h, i: (h, i, 0))
    return _call(
        body, name=name, grid=(heads, t // blk),
        in_specs=[qblk,
                  pl.BlockSpec((t, dh), lambda h, i: (0, h)),
                  pl.BlockSpec((t, dh), lambda h, i: (0, v_col + h)),
                  qblk, qcol,
                  pl.BlockSpec((None, 1, t), lambda h, i: (h, 0, 0)),
                  qcol, qcol],
        out_specs=[qblk, qcol],
        out_shape=[jax.ShapeDtypeStruct((t, heads * dh), F32), jax.ShapeDtypeStruct((heads, t, 1), F32)],
        scratch_shapes=[], semantics=("parallel", "arbitrary"),
        operands=(qn, kn, v_src, do, fcol, frow, lse, delta), job=job)


def _attn_bwd_kv(qn, kn, v_src, v_col, do, fcol, frow, lse, delta, heads, dh, name, job=None):
    t = qn.shape[0]
    blk = _tile(t, ATTN_BLOCK)
    nq = t // blk
    tn_dims = (((0,), (0,)), ((), ()))

    def body(q_ref, k_ref, v_ref, do_ref, fc_ref, fr_ref, lse_ref, dl_ref, dk_ref, dv_ref, dfk_ref):
        ki = pl.program_id(1)
        k, v, f_keys = k_ref[...], v_ref[...].astype(BF16), fr_ref[...]

        def block(qi, carry, diagonal):
            dk, dv, dfk = carry
            rows = _block_at(qi, blk)
            q, dob = q_ref[rows, :], do_ref[rows, :].astype(BF16)
            f_first = fc_ref[pl.ds(pl.multiple_of(qi * blk, blk), 1), :]
            p = jnp.exp(_attn_logits(q, k, f_keys, f_first, diagonal) - lse_ref[rows, :])
            dp = lax.dot_general(dob, v, (((1,), (1,)), ((), ())), preferred_element_type=F32)
            ds = p * (dp - dl_ref[rows, :])
            dv = dv + lax.dot_general(p.astype(BF16), dob, tn_dims, preferred_element_type=F32)
            dk = dk + lax.dot_general(ds.astype(BF16), q, tn_dims, preferred_element_type=F32)
            return dk, dv, dfk + jnp.sum(ds, axis=0, keepdims=True)

        start = (jnp.zeros((blk, dh), F32), jnp.zeros((blk, dh), F32), jnp.zeros((1, blk), F32))
        dk, dv, dfk = lax.fori_loop(ki + 1, nq, lambda qi, carry: block(qi, carry, False), block(ki, start, True))
        dk_ref[...] = dk
        dv_ref[...] = dv.astype(dv_ref.dtype)
        dfk_ref[...] = dfk

    whole = pl.BlockSpec((t, dh), lambda h, j: (0, h))
    wcol = pl.BlockSpec((None, t, 1), lambda h, j: (h, 0, 0))
    kblk = pl.BlockSpec((blk, dh), lambda h, j: (j, h))
    krow = pl.BlockSpec((None, 1, blk), lambda h, j: (h, 0, j))
    return _call(
        body, name=name, grid=(heads, nq),
        in_specs=[whole, kblk, pl.BlockSpec((blk, dh), lambda h, j: (j, v_col + h)), whole, wcol, krow, wcol, wcol],
        out_specs=[kblk, kblk, krow],
        out_shape=[jax.ShapeDtypeStruct((t, heads * dh), F32), jax.ShapeDtypeStruct((t, heads * dh), BF16),
                   jax.ShapeDtypeStruct((heads, 1, t), F32)],
        scratch_shapes=[], semantics=("parallel", "arbitrary"),
        operands=(qn, kn, v_src, do, fcol, frow, lse, delta), job=job)


TIME_TILE = 8


def _s5_discretize(lam_re, lam_im, log_step, b_re, b_im):
    dt = jnp.exp(log_step)
    mag = jnp.exp(lam_re * dt)
    lb_re = mag * jnp.cos(lam_im * dt)
    lb_im = mag * jnp.sin(lam_im * dt)
    denom = lam_re * lam_re + lam_im * lam_im
    num_re = lb_re - 1.0
    fac_re = (num_re * lam_re + lb_im * lam_im) / denom
    fac_im = (lb_im * lam_re - num_re * lam_im) / denom
    return lb_re, lb_im, fac_re * b_re - fac_im * b_im, fac_re * b_im + fac_im * b_re


def _s5_prep(lam_re, lam_im, log_step, b_re, b_im, name):
    gp, width = b_re.shape

    def body(lr, li, ls, br, bi, o_lr, o_li, o_br, o_bi):
        res = _s5_discretize(lr[...], li[...], ls[...], br[...], bi[...])
        for ref, val in zip((o_lr, o_li, o_br, o_bi), res):
            ref[...] = val

    vm = pl.BlockSpec(memory_space=pltpu.VMEM)
    return pl.pallas_call(
        body, name=name, in_specs=[vm] * 5, out_specs=[vm] * 4,
        out_shape=[jax.ShapeDtypeStruct((gp, 1), F32)] * 2 + [jax.ShapeDtypeStruct((gp, width), F32)] * 2,
        compiler_params=_cp(),
    )(lam_re, lam_im, log_step, b_re, b_im)


def _s5_prep_bwd(lam_re, lam_im, log_step, b_re, b_im, d_lb_re, d_lb_im, d_bb_re, d_bb_im, groups, name):
    gp, width = b_re.shape
    states = gp // groups
    tr = _tile(gp, 512, 8)

    def body(lr, li, ls, br, bi, g_lr, g_li, g_br, g_bi, o_lr, o_li, o_ls, o_br, o_bi):
        _, vjp = jax.vjp(_s5_discretize, lr[...], li[...], ls[...], br[...], bi[...])
        d_lr, d_li, d_ls, d_br, d_bi = vjp((g_lr[...], g_li[...], g_br[...], g_bi[...]))
        o_lr[...] = d_lr
        o_li[...] = d_li
        o_br[...] = d_br
        o_bi[...] = d_bi
        row_group = (pl.program_id(0) * tr + lax.broadcasted_iota(jnp.int32, (tr, groups), 0)) // states
        col_group = lax.broadcasted_iota(jnp.int32, (tr, groups), 1)

        @pl.when(pl.program_id(0) == 0)
        def _():
            o_ls[...] = jnp.zeros_like(o_ls)

        o_ls[...] += jnp.sum(jnp.where(row_group == col_group, d_ls, 0.0), axis=0, keepdims=True)

    col = pl.BlockSpec((tr, 1), lambda i: (i, 0))
    mat = pl.BlockSpec((tr, width), lambda i: (i, 0))
    return pl.pallas_call(
        body, name=name, grid=(gp // tr,),
        in_specs=[col, col, col, mat, mat, col, col, mat, mat],
        out_specs=[col, col, pl.BlockSpec((1, groups), lambda i: (0, 0)), mat, mat],
        out_shape=[jax.ShapeDtypeStruct((gp, 1), F32)] * 2 + [jax.ShapeDtypeStruct((1, groups), F32)]
        + [jax.ShapeDtypeStruct((gp, width), F32)] * 2,
        compiler_params=_cp(("arbitrary",)),
    )(lam_re, lam_im, log_step, b_re, b_im, d_lb_re, d_lb_im, d_bb_re, d_bb_im)


def _shift_time(v, s, reverse):
    row = lax.broadcasted_iota(jnp.int32, v.shape, 0)
    if reverse:
        return jnp.where(row < TIME_TILE - s, pltpu.roll(v, TIME_TILE - s, 0), 0.0)
    return jnp.where(row >= s, pltpu.roll(v, s, 0), 0.0)


def _cmul(ar, ai, br, bi):
    return ar * br - ai * bi, ar * bi + ai * br


def _scan_time(xr_ref, xi_ref, ar, ai, reverse):
    t = xr_ref.shape[0]
    n_tiles = t // TIME_TILE
    powers = [(ar, ai)]
    for _ in range(TIME_TILE - 1):
        powers.append(_cmul(*powers[-1], ar, ai))
    order = powers[::-1] if reverse else powers
    carry_r = jnp.concatenate([p[0] for p in order], axis=0)
    carry_i = jnp.concatenate([p[1] for p in order], axis=0)
    levels = [(1, powers[0]), (2, powers[1]), (4, powers[3])]
    last = 0 if reverse else TIME_TILE - 1

    def tile(i, carry):
        cr, ci = carry
        idx = (n_tiles - 1 - i) if reverse else i
        rows = pl.ds(pl.multiple_of(idx * TIME_TILE, TIME_TILE), TIME_TILE)
        br, bi = xr_ref[rows, :], xi_ref[rows, :]
        for s, (pr, pi) in levels:
            sr, si = _cmul(pr, pi, _shift_time(br, s, reverse), _shift_time(bi, s, reverse))
            br, bi = br + sr, bi + si
        kr, ki = _cmul(carry_r, carry_i, cr, ci)
        br, bi = br + kr, bi + ki
        xr_ref[rows, :] = br
        xi_ref[rows, :] = bi
        return br[last:last + 1, :], bi[last:last + 1, :]

    zero = jnp.zeros_like(ar)
    lax.fori_loop(0, n_tiles, tile, (zero, zero))


def _s5_states(u_ref, bbr_ref, bbi_ref, ar_ref, ai_ref, xr, xi, chunk):
    t = u_ref.shape[0]
    for r0 in range(0, t, chunk):
        rows = pl.ds(r0, chunk)
        xr[rows, :] = jnp.dot(u_ref[rows, :], bbr_ref[...], preferred_element_type=F32)
        xi[rows, :] = jnp.dot(u_ref[rows, :], bbi_ref[...], preferred_element_type=F32)
    _scan_time(xr, xi, ar_ref[...], ai_ref[...], reverse=False)


def _s5_specs(t, nb_lanes, state_lanes):
    tok = pl.BlockSpec((t, nb_lanes), lambda j: (0, j))
    bb = pl.BlockSpec((None, nb_lanes, state_lanes), lambda j: (j, 0, 0))
    cc = pl.BlockSpec((None, state_lanes, nb_lanes), lambda j: (j, 0, 0))
    dvec = pl.BlockSpec((1, nb_lanes), lambda j: (0, j))
    avec = pl.BlockSpec((1, state_lanes), lambda j: (0, j))
    return tok, bb, cc, dvec, avec


def _s5_fwd(u5, bbr, bbi, ccr, cci, dskip, ar, ai, name, job=None):
    t, w = u5.shape
    nb, nb_lanes, state_lanes = bbr.shape
    chunk = _tile(t, 512, 16)

    def body(u_ref, bbr_ref, bbi_ref, cr_ref, ci_ref, d_ref, ar_ref, ai_ref, y_ref, xr, xi):
        _s5_states(u_ref, bbr_ref, bbi_ref, ar_ref, ai_ref, xr, xi, chunk)
        for r0 in range(0, t, chunk):
            rows = pl.ds(r0, chunk)
            y = jnp.dot(xr[rows, :].astype(BF16), cr_ref[...], preferred_element_type=F32)
            y = y - jnp.dot(xi[rows, :].astype(BF16), ci_ref[...], preferred_element_type=F32)
            y_ref[rows, :] = y + d_ref[...] * u_ref[rows, :].astype(F32)

    tok, bb, cc, dvec, avec = _s5_specs(t, nb_lanes, state_lanes)
    return _call(
        body, name=name, grid=(nb,),
        in_specs=[tok, bb, bb, cc, cc, dvec, avec, avec],
        out_specs=[tok],
        out_shape=[jax.ShapeDtypeStruct((t, w), F32)],
        scratch_shapes=[pltpu.VMEM((t, state_lanes), F32)] * 2,
        semantics=("parallel",), operands=(u5, bbr, bbi, ccr, cci, dskip, ar, ai), job=job)


def _s5_bwd(u5, dy, bbr, bbi, ccr, cci, dskip, ar, ai, name, job=None):
    t, w = u5.shape
    nb, nb_lanes, state_lanes = bbr.shape
    chunk = _tile(t, 512, 16)
    nt_dims = (((1,), (1,)), ((), ()))
    tn_dims = (((0,), (0,)), ((), ()))

    def body(u_ref, dy_ref, bbr_ref, bbi_ref, cr_ref, ci_ref, d_ref, ar_ref, ai_ref,
             du_ref, dbbr_ref, dbbi_ref, dcr_ref, dci_ref, dar_ref, dai_ref, dd_ref, xr, xi, gr, gi):
        _s5_states(u_ref, bbr_ref, bbi_ref, ar_ref, ai_ref, xr, xi, chunk)
        for r0 in range(0, t, chunk):
            rows = pl.ds(r0, chunk)
            dyb = dy_ref[rows, :].astype(BF16)
            gr[rows, :] = lax.dot_general(dyb, cr_ref[...], nt_dims, preferred_element_type=F32)
            gi[rows, :] = -lax.dot_general(dyb, ci_ref[...], nt_dims, preferred_element_type=F32)
        _scan_time(gr, gi, ar_ref[...], -ai_ref[...], reverse=True)

        dcr = jnp.zeros((state_lanes, nb_lanes), F32)
        dci = jnp.zeros((state_lanes, nb_lanes), F32)
        dbr = jnp.zeros((nb_lanes, state_lanes), F32)
        dbi = jnp.zeros((nb_lanes, state_lanes), F32)
        dd = jnp.zeros((1, nb_lanes), F32)
        for r0 in range(0, t, chunk):
            rows = pl.ds(r0, chunk)
            u = u_ref[rows, :]
            dyv = dy_ref[rows, :]
            dyb = dyv.astype(BF16)
            lr, li = gr[rows, :].astype(BF16), gi[rows, :].astype(BF16)
            dcr = dcr + lax.dot_general(xr[rows, :].astype(BF16), dyb, tn_dims, preferred_element_type=F32)
            dci = dci - lax.dot_general(xi[rows, :].astype(BF16), dyb, tn_dims, preferred_element_type=F32)
            dbr = dbr + lax.dot_general(u, lr, tn_dims, preferred_element_type=F32)
            dbi = dbi + lax.dot_general(u, li, tn_dims, preferred_element_type=F32)
            du = lax.dot_general(lr, bbr_ref[...], nt_dims, preferred_element_type=F32)
            du = du + lax.dot_general(li, bbi_ref[...], nt_dims, preferred_element_type=F32)
            du_ref[rows, :] = du + d_ref[...] * dyv
            dd = dd + jnp.sum(dyv * u.astype(F32), axis=0, keepdims=True)
        dcr_ref[...] = dcr
        dci_ref[...] = dci
        dbbr_ref[...] = dbr
        dbbi_ref[...] = dbi
        dd_ref[...] = dd

        first_row = lax.broadcasted_iota(jnp.int32, (TIME_TILE, state_lanes), 0) == 0

        def tile(i, carry):
            pr, pi, acc_r, acc_i = carry
            rows = pl.ds(pl.multiple_of(i * TIME_TILE, TIME_TILE), TIME_TILE)
            x_r, x_i, l_r, l_i = xr[rows, :], xi[rows, :], gr[rows, :], gi[rows, :]
            prev_r = jnp.where(first_row, pr, pltpu.roll(x_r, 1, 0))
            prev_i = jnp.where(first_row, pi, pltpu.roll(x_i, 1, 0))
            acc_r = acc_r + l_r * prev_r + l_i * prev_i
            acc_i = acc_i + l_i * prev_r - l_r * prev_i
            return x_r[TIME_TILE - 1:, :], x_i[TIME_TILE - 1:, :], acc_r, acc_i

        zrow = jnp.zeros((1, state_lanes), F32)
        ztile = jnp.zeros((TIME_TILE, state_lanes), F32)
        _, _, acc_r, acc_i = lax.fori_loop(0, t // TIME_TILE, tile, (zrow, zrow, ztile, ztile))
        dar_ref[...] = jnp.sum(acc_r, axis=0, keepdims=True)
        dai_ref[...] = jnp.sum(acc_i, axis=0, keepdims=True)

    tok, bb, cc, dvec, avec = _s5_specs(t, nb_lanes, state_lanes)
    return _call(
        body, name=name, grid=(nb,),
        in_specs=[tok, tok, bb, bb, cc, cc, dvec, avec, avec],
        out_specs=[tok, bb, bb, cc, cc, avec, avec, dvec],
        out_shape=[jax.ShapeDtypeStruct((t, w), F32)]
        + [jax.ShapeDtypeStruct((nb, nb_lanes, state_lanes), F32)] * 2
        + [jax.ShapeDtypeStruct((nb, state_lanes, nb_lanes), F32)] * 2
        + [jax.ShapeDtypeStruct((1, nb * state_lanes), F32)] * 2
        + [jax.ShapeDtypeStruct((1, w), F32)],
        scratch_shapes=[pltpu.VMEM((t, state_lanes), F32)] * 4,
        semantics=("parallel",), operands=(u5, dy, bbr, bbi, ccr, cci, dskip, ar, ai), job=job)


def _gelu(x):
    return 0.5 * x * (1.0 + jnp.tanh(GELU_C * (x + GELU_A * x * x * x)))


def _gelu_grad(x):
    th = jnp.tanh(GELU_C * (x + GELU_A * x * x * x))
    return 0.5 * (1.0 + th) + 0.5 * x * (1.0 - th * th) * GELU_C * (1.0 + 3.0 * GELU_A * x * x)


def _glu_fwd(y5, w, b, name):
    t, width = y5.shape
    tm = _tile(t, 512, 16)

    def body(y_ref, w_ref, b_ref, o_ref):
        g = _gelu(y_ref[...])
        a = jnp.dot(g.astype(BF16), w_ref[...], preferred_element_type=F32) + b_ref[...]
        o_ref[...] = (g * _sigmoid(a)).astype(o_ref.dtype)

    row = pl.BlockSpec((tm, width), lambda i: (i, 0))
    return pl.pallas_call(
        body, name=name, grid=(t // tm,),
        in_specs=[row, pl.BlockSpec((width, width), lambda i: (0, 0)), pl.BlockSpec((1, width), lambda i: (0, 0))],
        out_specs=row,
        out_shape=jax.ShapeDtypeStruct((t, width), BF16),
        compiler_params=_cp(("parallel",)),
    )(y5, w, b)


def _glu_bwd(y5, dout, w, b, name):
    t, width = y5.shape
    tm = _tile(t, 512, 16)

    def body(y_ref, do_ref, w_ref, b_ref, dy_ref, g_ref, da_ref, db_ref):
        y = y_ref[...]
        g = _gelu(y)
        s = _sigmoid(jnp.dot(g.astype(BF16), w_ref[...], preferred_element_type=F32) + b_ref[...])
        dout_v = do_ref[...].astype(F32)
        da = dout_v * g * s * (1.0 - s)
        dg = dout_v * s + lax.dot_general(da.astype(BF16), w_ref[...], (((1,), (1,)), ((), ())),
                                          preferred_element_type=F32)
        dy_ref[...] = dg * _gelu_grad(y)
        g_ref[...] = g.astype(g_ref.dtype)
        da_ref[...] = da.astype(da_ref.dtype)

        @pl.when(pl.program_id(0) == 0)
        def _():
            db_ref[...] = jnp.zeros_like(db_ref)

        db_ref[...] += jnp.sum(da, axis=0, keepdims=True)

    row = pl.BlockSpec((tm, width), lambda i: (i, 0))
    vec = pl.BlockSpec((1, width), lambda i: (0, 0))
    return pl.pallas_call(
        body, name=name, grid=(t // tm,),
        in_specs=[row, row, pl.BlockSpec((width, width), lambda i: (0, 0)), vec],
        out_specs=[row, row, row, vec],
        out_shape=[jax.ShapeDtypeStruct((t, width), F32), jax.ShapeDtypeStruct((t, width), BF16),
                   jax.ShapeDtypeStruct((t, width), BF16), jax.ShapeDtypeStruct((1, width), F32)],
        compiler_params=_cp(("arbitrary",)),
    )(y5, dout, w, b)


def _adamw(w, g, m, v):
    m = ADAM_B1 * m + (1.0 - ADAM_B1) * g
    v = ADAM_B2 * v + (1.0 - ADAM_B2) * (g * g)
    m_hat = m / (1.0 - ADAM_B1 ** ADAM_STEP)
    v_hat = v / (1.0 - ADAM_B2 ** ADAM_STEP)
    return -ADAM_LR * (m_hat / (jnp.sqrt(v_hat) + ADAM_EPS) + ADAM_WD * w), m, v


def _adamw_shard(w, m, v, sums, got, chip, name):
    rows, cols = w.shape
    wide = sums.shape[2]
    tr = _row_tile(rows, wide, target=2**20)

    def body(chip_ref, w_ref, m_ref, v_ref, s_ref, g0_ref, g1_ref, g2_ref, g_out, d_out, m_out, v_out):
        g = s_ref[...].astype(F32) + g0_ref[...].astype(F32) + g1_ref[...].astype(F32) + g2_ref[...].astype(F32)
        g = g[:, :cols]
        delta, m_new, v_new = _adamw(w_ref[...], g, m_ref[...], v_ref[...])
        g_out[...] = g
        d_out[...] = delta
        m_out[...] = m_new
        v_out[...] = v_new

    blk = pl.BlockSpec((tr, cols), lambda i, chip_ref: (i, 0))

    def part(k):
        return pl.BlockSpec((None, tr, wide), lambda i, chip_ref: (k, i, 0))

    return pl.pallas_call(
        body, name=name,
        grid_spec=pltpu.PrefetchScalarGridSpec(
            num_scalar_prefetch=1, grid=(rows // tr,),
            in_specs=[blk, blk, blk, pl.BlockSpec((None, tr, wide), lambda i, chip_ref: (chip_ref[0], i, 0)),
                      part(0), part(1), part(2)],
            out_specs=[blk] * 4),
        out_shape=[jax.ShapeDtypeStruct((rows, cols), F32)] * 4,
        compiler_params=_cp(("parallel",)),
    )(chip, w, m, v, sums, got, got, got)


def _adamw_packed(w, m, v, g, name):
    def body(w_ref, m_ref, v_ref, g_ref, d_out, m_out, v_out):
        delta, m_new, v_new = _adamw(w_ref[...], g_ref[...], m_ref[...], v_ref[...])
        d_out[...] = delta
        m_out[...] = m_new
        v_out[...] = v_new

    vm = pl.BlockSpec(memory_space=pltpu.VMEM)
    return pl.pallas_call(
        body, name=name, in_specs=[vm] * 4, out_specs=[vm] * 3,
        out_shape=[jax.ShapeDtypeStruct(w.shape, F32)] * 3,
        compiler_params=_cp(),
    )(w, m, v, g)


WEIGHTS = ("g_mix", "w_in", "b_fgate", "b_gates", "q_norm", "k_norm", "s5_lambda_re", "s5_lambda_im", "s5_log_step",
           "s5_b_re", "s5_b_im", "s5_c_re", "s5_c_im", "s5_d", "w_glu", "b_glu", "w_proj_fox", "w_proj_s5", "w_out",
           "g_ffn", "w_gate_up", "w_down")
COLUMN_SHARDED = ("w_in", "w_proj_fox", "w_proj_s5", "w_gate_up")
ROW_SHARDED = ("w_glu", "w_out", "w_down")
PACK_ROWS = 8 * LANES

def _pack(arrays):
    flat = jnp.concatenate([a.reshape(-1).astype(F32) for a in arrays])
    flat = jnp.pad(flat, (0, (-flat.shape[0]) % PACK_ROWS))
    return flat.reshape(-1, LANES)


def _unpack(packed, like):
    flat, out, at = packed.reshape(-1), [], 0
    for a in like:
        out.append(flat[at:at + a.size].reshape(a.shape))
        at += a.size
    return out


def _split_rows(rows, shares, unit=16):
    out, first = [], 0
    for k, share in enumerate(shares):
        count = rows - first if k == len(shares) - 1 else max(unit, int(rows * share) // unit * unit)
        out.append((first, count))
        first += count
    assert first == rows and all(c > 0 for _, c in out), (rows, out)
    return out


def _pad_lanes(a):
    return jnp.pad(a, ((0, 0), (0, LANES - a.shape[1])))


def kernel(x, g_mix, w_in, b_fgate, b_gates, q_norm, k_norm, s5_lambda_re, s5_lambda_im, s5_log_step, s5_b_re, s5_b_im,
           s5_c_re, s5_c_im, s5_d, w_glu, b_glu, w_proj_fox, w_proj_s5, w_out, g_ffn, w_gate_up, w_down,
           loss_target, m_g_mix, m_w_in, m_b_fgate, m_b_gates, m_q_norm, m_k_norm, m_s5_lambda_re,
           m_s5_lambda_im, m_s5_log_step, m_s5_b_re, m_s5_b_im, m_s5_c_re, m_s5_c_im, m_s5_d, m_w_glu,
           m_b_glu, m_w_proj_fox, m_w_proj_s5, m_w_out, m_g_ffn, m_w_gate_up, m_w_down, v_g_mix, v_w_in,
           v_b_fgate, v_b_gates, v_q_norm, v_k_norm, v_s5_lambda_re, v_s5_lambda_im, v_s5_log_step, v_s5_b_re,
           v_s5_b_im, v_s5_c_re, v_s5_c_im, v_s5_d, v_w_glu, v_b_glu, v_w_proj_fox, v_w_proj_s5, v_w_out,
           v_g_ffn, v_w_gate_up, v_w_down):
    given = dict(locals())
    weights = {n: given[n] for n in WEIGHTS}
    mom_m = {n: given["m_" + n] for n in WEIGHTS}
    mom_v = {n: given["v_" + n] for n in WEIGHTS}

    pos_x, pos_y, pos_c = _position()
    core = jnp.reshape(pos_c, (1,)).astype(jnp.int32)
    chip = jnp.reshape(2 * pos_x + pos_y, (1,)).astype(jnp.int32)

    xs, target = x[0], loss_target[0]
    t, d = xs.shape
    heads, dh = b_fgate.shape[-1], q_norm.shape[-1]
    fw = heads * dh
    groups, states, gwidth = s5_b_re.shape[1:]
    sw = groups * gwidth
    gp = groups * states
    assert dh == LANES and sw % LANES == 0 and LANES % gwidth == 0
    col_v, col_f, col_s5 = 3 * fw, 3 * fw + heads, 3 * fw + heads + sw

    shard = {n: weights[n][0].astype(BF16) for n in COLUMN_SHARDED + ROW_SHARDED}

    def whole(n, ag):
        if n in COLUMN_SHARDED:
            return ag.transpose(1, 0, 2).reshape(ag.shape[1], N_DEV * ag.shape[2])
        return ag.reshape(N_DEV * ag.shape[1], ag.shape[2])

    full = {}
    c_gu, r_dn = w_gate_up.shape[2], w_down.shape[1]
    assert c_gu == 2 * r_dn
    cp_gu = -(-c_gu // LANES) * LANES
    shard["w_gate_up"] = jnp.pad(shard["w_gate_up"], ((0, 0), (0, cp_gu - c_gu)))

    def down_rows(ag):
        gap = [jnp.zeros((cp_gu - c_gu, d), ag.dtype)] if cp_gu > c_gu else []
        return jnp.concatenate([p for b in range(N_DEV // 2) for p in [ag[2 * b], ag[2 * b + 1]] + gap], axis=0)

    c_in = w_in.shape[2]
    in_cols = N_DEV * c_in

    def in_pieces(lo, hi, take):
        cuts = [(k, max(lo, k * c_in), min(hi, (k + 1) * c_in)) for k in range(N_DEV)]
        return [take(k, a - k * c_in, b - k * c_in) for k, a, b in cuts if a < b]

    ag_in = _all_gather(shard["w_in"], "ag_w_in")

    def from_gathered(k, a, b):
        return ag_in[k][:, a:b]

    w_main = jnp.concatenate(in_pieces(0, col_v, from_gathered) + in_pieces(col_f, in_cols, from_gathered), axis=1)
    w_forget = _pad_lanes(jnp.concatenate(in_pieces(col_v, col_f, from_gathered), axis=1))
    z_s5, z_gate = 3 * fw, 3 * fw + sw

    u, r_mix = _rms_fwd(xs, g_mix, "rms_mix")
    early = ("w_proj_fox", "w_proj_s5", "w_glu", "w_out")
    gu_rows = _split_rows(d, (0.62, 0.17, 0.06, 0.15))

    def gate_up_piece(k, so_far):
        return _GatherJob([shard["w_gate_up"]], rows=gu_rows[k], into=so_far)

    z, gu_buf = _mm(u, w_main, "nn", BF16, "mm_z", job=gate_up_piece(0, None))
    zf = _mm(u, w_forget, "nn", F32, "mm_zf")
    qn, kn, r_q, r_k = _qk_prep(z, heads, dh, q_norm, k_norm, "qk_prep")
    b_forget = _pad_lanes(b_fgate)
    cum = _forget_fwd(zf, b_forget, "forget_fwd")
    cum_t = cum[:, :heads].T
    fcol, frow = cum_t[:, :, None], cum_t[:, None, :]
    (attn, lse), got = _attn_fwd(qn, kn, z, 2 * heads, fcol, frow, heads, dh, "attn_fwd",
                                 job=_GatherJob([shard[n] for n in early]))
    for n, ag in zip(early, got):
        full[n] = whole(n, ag)

    lam_re, lam_im = s5_lambda_re.reshape(gp, 1), s5_lambda_im.reshape(gp, 1)
    log_step = jnp.repeat(s5_log_step.reshape(groups, 1), states, axis=1).reshape(gp, 1)
    b_re, b_im = s5_b_re.reshape(gp, gwidth), s5_b_im.reshape(gp, gwidth)
    lb_re, lb_im, bb_re, bb_im = _s5_prep(lam_re, lam_im, log_step, b_re, b_im, "s5_prep")
    nb, per = sw // LANES, LANES // gwidth
    eye = jnp.eye(per, dtype=F32)

    def diag_b(bb):
        return jnp.einsum("napi,ab->naibp", bb.reshape(nb, per, states, gwidth), eye).reshape(nb, LANES, per * states)

    def diag_c(c):
        return jnp.einsum("naip,ab->nbpai", c.reshape(nb, per, gwidth, states), eye).reshape(nb, per * states, LANES)

    def undiag_b(g):
        return jnp.einsum("naibp,ab->napi", g.reshape(nb, per, gwidth, per, states), eye).reshape(gp, gwidth)

    def undiag_c(g):
        return jnp.einsum("nbpai,ab->naip", g.reshape(nb, per, states, per, gwidth), eye).reshape(1, groups, gwidth, states)

    bbr, bbi = diag_b(bb_re).astype(BF16), diag_b(bb_im).astype(BF16)
    ccr, cci = diag_c(s5_c_re[0]).astype(BF16), diag_c(s5_c_im[0]).astype(BF16)
    a_re, a_im = lb_re.reshape(1, gp), lb_im.reshape(1, gp)
    d_skip = s5_d.reshape(1, sw)
    u5 = z[:, z_s5:z_s5 + sw]
    (y5,), gu_buf = _s5_fwd(u5, bbr, bbi, ccr, cci, d_skip, a_re, a_im, "s5_fwd", job=gate_up_piece(1, gu_buf))
    ssm = _glu_fwd(y5, full["w_glu"], b_glu, "glu_fwd")

    pf, gu_buf = _mm(attn, full["w_proj_fox"], "nn", BF16, "mm_pf", job=gate_up_piece(2, gu_buf))
    ps = _mm(ssm, full["w_proj_s5"], "nn", BF16, "mm_ps")
    merged = _gate_merge_fwd(z, z_gate, b_gates, pf, ps, "merge_fwd")
    mo, gu_buf = _mm(merged, full["w_out"], "nn", F32, "mm_out", job=gate_up_piece(3, gu_buf))
    full["w_gate_up"] = gu_buf[0]
    h, hn, r_ffn = _resid_rms(xs, mo, g_ffn, "resid_rms")
    gu, got = _mm(hn, full["w_gate_up"], "nn", BF16, "mm_gu", job=_GatherJob([shard["w_down"]]), b_stacked=True)
    full["w_down"] = down_rows(got[0])
    act = _swiglu_fwd(gu, "swiglu_fwd")
    dn = _mm(act, full["w_down"], "nn", F32, "mm_down")
    loss_blk, dy, dy_b = _loss_head(h, dn, target, "loss_head")
    loss = lax.psum(loss_blk[0, 0], ("x", "y", "c"))

    grad, sums, from_chips = {}, {}, {}

    def pair_sums(n, parts=None):
        if parts is None:
            g_full = grad[n]
            if n in COLUMN_SHARDED:
                parts = g_full.reshape(g_full.shape[0], N_DEV, g_full.shape[1] // N_DEV).transpose(1, 0, 2)
            else:
                parts = g_full.reshape(N_DEV, g_full.shape[0] // N_DEV, g_full.shape[1])
        got = _swap_with_sibling(parts, "rs_sibling_" + n)
        return _add_sibling(parts, got, core, "rs_add_" + n)

    dact = _mm(dy_b, full["w_down"], "nt", BF16, "mm_dact")
    gw_down = _mm(act, dy_b, "tn", BF16, "mm_gw_down")
    parts = jnp.stack(
        [lax.slice_in_dim(gw_down, cp_gu * (k // 2) + r_dn * (k % 2), cp_gu * (k // 2) + r_dn * (k % 2) + r_dn)
         for k in range(N_DEV)])
    (dgate, dup), (got,) = _swiglu_bwd(gu, dact, "swiglu_bwd", job=_SiblingSwapJob([parts]))
    dgu = jnp.concatenate([dgate, dup], axis=1)
    sums["w_down"] = _add_sibling(parts, got, core, "rs_add_w_down")
    dhn, (from_chips["w_down"],) = _mm(dgu, full["w_gate_up"], "nt", F32, "mm_dhn", job=_ChipSwapJob([sums["w_down"]]),
                                       b_stacked=True)
    parts = _mm(hn, dgu, "tn", BF16, "mm_gw_gu", out_stack=N_DEV)
    dh_, dh_b, grad["g_ffn"] = _rms_bwd([dhn], h, r_ffn, g_ffn, [dy], "rms_ffn_bwd")
    dmerged, (got,) = _mm(dh_b, full["w_out"], "nt", BF16, "mm_dmerged", job=_SiblingSwapJob([parts]))
    sums["w_gate_up"] = _add_sibling(parts, got, core, "rs_add_w_gate_up")
    grad["w_out"] = _mm(merged, dh_b, "tn", BF16, "mm_gw_out")
    dpf, dps, dz_gf, dz_gs, db_gf, db_gs = _gate_merge_bwd(dmerged, z, z_gate, b_gates, pf, ps, "merge_bwd")
    grad["b_gates"] = jnp.concatenate([db_gf, db_gs], axis=1)
    dattn = _mm(dpf, full["w_proj_fox"], "nt", BF16, "mm_dattn")
    grad["w_proj_fox"] = _mm(attn, dpf, "tn", BF16, "mm_gw_pf")
    dssm = _mm(dps, full["w_proj_s5"], "nt", BF16, "mm_dssm")
    grad["w_proj_s5"] = _mm(ssm, dps, "tn", BF16, "mm_gw_ps")

    dy5, g5, da5, grad["b_glu"] = _glu_bwd(y5, dssm, full["w_glu"], b_glu, "glu_bwd")
    grad["w_glu"] = _mm(g5, da5, "tn", BF16, "mm_gw_glu")
    for n in early:
        sums[n] = pair_sums(n)
    (du5, d_bbr, d_bbi, d_ccr, d_cci, d_are, d_aim, d_dskip), got = _s5_bwd(
        u5, dy5, bbr, bbi, ccr, cci, d_skip, a_re, a_im, "s5_bwd", job=_ChipSwapJob([sums[n] for n in early]))
    from_chips.update(zip(early, got))
    d_lre, d_lim, d_lstep, d_bre, d_bim = _s5_prep_bwd(
        lam_re, lam_im, log_step, b_re, b_im, d_are.reshape(gp, 1), d_aim.reshape(gp, 1),
        undiag_b(d_bbr), undiag_b(d_bbi), groups, "s5_prep_bwd")
    grad["s5_lambda_re"], grad["s5_lambda_im"] = d_lre.reshape(1, groups, states), d_lim.reshape(1, groups, states)
    grad["s5_log_step"] = d_lstep
    grad["s5_b_re"], grad["s5_b_im"] = d_bre.reshape(s5_b_re.shape), d_bim.reshape(s5_b_im.shape)
    grad["s5_c_re"], grad["s5_c_im"] = undiag_c(d_ccr), undiag_c(d_cci)
    grad["s5_d"] = d_dskip.reshape(s5_d.shape)

    delta = _attn_delta(attn, dattn, heads, dh, "attn_delta")
    swap_rows = _split_rows(d, (0.36, 0.5, 0.14))

    def gate_up_swap(k, so_far):
        return _ChipSwapJob([sums["w_gate_up"]], rows=swap_rows[k], into=so_far)

    (dqn, df_q), got = _attn_bwd_q(qn, kn, z, 2 * heads, dattn, fcol, frow, lse, delta, heads, dh, "attn_bwd_q",
                              job=gate_up_swap(0, None))
    (dkn, dv, df_k), got = _attn_bwd_kv(qn, kn, z, 2 * heads, dattn, fcol, frow, lse, delta, heads, dh, "attn_bwd_kv",
                                        job=gate_up_swap(1, got))
    dq, dk, grad["q_norm"], grad["k_norm"] = _qk_prep_bwd(dqn, dkn, z, heads, dh, q_norm, k_norm, r_q, r_k, "qk_prep_bwd")
    dzf, db_forget = _forget_bwd(_pad_lanes(df_q[:, :, 0].T), _pad_lanes(df_k[:, 0, :].T), zf, b_forget, "forget_bwd")
    grad["b_fgate"] = db_forget[:, :heads]

    dz = jnp.concatenate([dq, dk, dv, du5.astype(BF16), dz_gf, dz_gs], axis=1)
    gw_main, (from_chips["w_gate_up"],) = _mm(u, dz, "tn", BF16, "mm_gw_main", job=gate_up_swap(2, got))
    gw_forget = _mm(u, dzf, "tn", BF16, "mm_gw_forget")

    def from_grads(k, a, b):
        lo, hi = k * c_in + a, k * c_in + b
        if hi <= col_v:
            return gw_main[:, lo:hi]
        if hi <= col_f:
            return gw_forget[:, lo - col_v:hi - col_v]
        return gw_main[:, lo - heads:hi - heads]

    def in_part(k):
        lo, hi = k * c_in, (k + 1) * c_in
        cuts = [(max(lo, a), min(hi, b)) for a, b in ((0, col_v), (col_v, col_f), (col_f, in_cols))]
        return jnp.concatenate([from_grads(k, a - lo, b - lo) for a, b in cuts if a < b], axis=1)

    sums["w_in"] = pair_sums("w_in", jnp.stack([in_part(k) for k in range(N_DEV)]))
    in_rows = _split_rows(d, (0.8, 0.2))
    du, got = _mm(dz, w_main, "nt", F32, "mm_du", job=_ChipSwapJob([sums["w_in"]], rows=in_rows[0]))
    du_f = _mm(dzf, w_forget, "nt", F32, "mm_du_f")
    (dx, _, grad["g_mix"]), (from_chips["w_in"],) = _rms_bwd(
        [du, du_f], xs, r_mix, g_mix, [dh_], "rms_mix_bwd", job=_ChipSwapJob([sums["w_in"]], rows=in_rows[1], into=got))

    out_g, out_d, out_m, out_v = {}, {}, {}, {}
    for n in COLUMN_SHARDED + ROW_SHARDED:
        res = _adamw_shard(weights[n][0], mom_m[n][0], mom_v[n][0], sums[n], from_chips[n], chip, "adamw_" + n)
        out_g[n], out_d[n], out_m[n], out_v[n] = (r[None] for r in res)

    small = [n for n in WEIGHTS if n not in COLUMN_SHARDED + ROW_SHARDED]
    g_small = _all_reduce_small(_pack([grad[n] for n in small]), "ar_small")
    like = [weights[n] for n in small]
    res = _adamw_packed(_pack(like), _pack([mom_m[n] for n in small]), _pack([mom_v[n] for n in small]), g_small,
                        "adamw_small")
    for store, packed in zip((out_g, out_d, out_m, out_v), (g_small, *res)):
        for n, a in zip(small, _unpack(packed, like)):
            store[n] = a

    return (loss, dx[None], *[out_g[n] for n in WEIGHTS], *[out_d[n] for n in WEIGHTS],
            *[out_m[n] for n in WEIGHTS], *[out_v[n] for n in WEIGHTS])
```

```python
import functools
import math

import jax
import jax.numpy as jnp
from jax import lax
from jax.experimental import pallas as pl
from jax.experimental.pallas import tpu as pltpu

F32 = jnp.float32
BF16 = jnp.bfloat16

V7X_VMEM_LIMIT = 56 * 2**20
LANES = 128
N_DEV = 8
MESH = pl.DeviceIdType.MESH

RMS_EPS = 1e-6
MASK_VALUE = -1e30
ADAM_LR, ADAM_B1, ADAM_B2, ADAM_EPS, ADAM_WD, ADAM_STEP = 0.001, 0.9, 0.999, 1e-08, 0.01, 10
GELU_C = math.sqrt(2.0 / math.pi)
GELU_A = 0.044715


def _cp(sem=None):
    return pltpu.CompilerParams(dimension_semantics=sem, vmem_limit_bytes=V7X_VMEM_LIMIT)


def _tile(n, pref, unit=LANES):
    if n <= pref:
        return n
    t = (pref // unit) * unit
    while t >= unit:
        if n % t == 0:
            return t
        t -= unit
    raise ValueError(f"no tile for {n}")


def _row_tile(rows, cols, bytes_per_row_elem=4, target=2 * 2**20, unit=16):
    best = None
    for t in range(unit, rows + 1, unit):
        if rows % t == 0 and t * cols * bytes_per_row_elem <= target:
            best = t
    if best is None:
        best = unit if rows % unit == 0 else rows
    return best


def _sigmoid(x):
    return 1.0 / (1.0 + jnp.exp(-x))


def _position():
    return lax.axis_index("x"), lax.axis_index("y"), lax.axis_index("c")


def _other_chips(x, y):
    return [(1 - x, y), (x, 1 - y), (1 - x, 1 - y)]


def _all_gather(shard, name):
    job = _GatherJob([shard])

    def body(x_ref, out_ref, *scratch):
        job.begin([x_ref], [out_ref], scratch)
        job.middle([x_ref], [out_ref], scratch)
        job.end([x_ref], [out_ref], scratch)

    return pl.pallas_call(
        body, name=name,
        out_shape=job.out_shape[0],
        in_specs=[pl.BlockSpec(memory_space=pltpu.HBM)],
        out_specs=pl.BlockSpec(memory_space=pltpu.HBM),
        scratch_shapes=job.scratch,
    )(shard)


def _swap_with_sibling(parts, name):
    _, rows, cols = parts.shape

    def body(p_ref, out_ref, send_sems, recv_sems):
        x, y, c = _position()
        copies = []
        for j in range(4):
            copies.append(pltpu.make_async_remote_copy(
                src_ref=p_ref.at[2 * j + (1 - c)], dst_ref=out_ref.at[j],
                send_sem=send_sems.at[j], recv_sem=recv_sems.at[j], device_id=(x, y, 1 - c), device_id_type=MESH))
        for cp in copies:
            cp.start()
        for cp in copies:
            cp.wait()

    return pl.pallas_call(
        body, name=name,
        out_shape=jax.ShapeDtypeStruct((4, rows, cols), parts.dtype),
        in_specs=[pl.BlockSpec(memory_space=pltpu.HBM)],
        out_specs=pl.BlockSpec(memory_space=pltpu.HBM),
        scratch_shapes=[pltpu.SemaphoreType.DMA((4,)), pltpu.SemaphoreType.DMA((4,))],
    )(parts)


def _add_sibling(parts, got, core, name):
    _, rows, cols = parts.shape
    tr = _row_tile(rows, cols)

    def body(core_ref, a_ref, b_ref, o_ref):
        o_ref[...] = (a_ref[...].astype(F32) + b_ref[...].astype(F32)).astype(o_ref.dtype)

    return pl.pallas_call(
        body, name=name,
        grid_spec=pltpu.PrefetchScalarGridSpec(
            num_scalar_prefetch=1, grid=(4, rows // tr),
            in_specs=[pl.BlockSpec((None, tr, cols), lambda j, i, core_ref: (2 * j + core_ref[0], i, 0)),
                      pl.BlockSpec((None, tr, cols), lambda j, i, core_ref: (j, i, 0))],
            out_specs=pl.BlockSpec((None, tr, cols), lambda j, i, core_ref: (j, i, 0))),
        out_shape=jax.ShapeDtypeStruct((4, rows, cols), BF16),
        compiler_params=_cp(("parallel", "parallel")),
    )(core, parts, got)


def _all_reduce_small(packed, name):
    rows, cols = packed.shape

    def body(x_ref, out_ref, gathered, send_sems, recv_sems):
        x, y, c = _position()
        me, sibling = (x, y, c), (x, y, 1 - c)
        chips = _other_chips(x, y)

        def slot(px, py, pc):
            return gathered.at[4 * px + 2 * py + pc]

        def copy(k, block, to, src=None):
            return pltpu.make_async_remote_copy(
                src_ref=slot(*block) if src is None else src, dst_ref=slot(*block),
                send_sem=send_sems.at[k], recv_sem=recv_sems.at[k], device_id=to, device_id_type=MESH)

        first = [copy(0, me, sibling, src=x_ref)]
        first += [copy(1 + j, me, (*chip, c), src=x_ref) for j, chip in enumerate(chips)]
        for cp in first:
            cp.start()
        passed = [copy(4 + j, (*chip, c), sibling) for j, chip in enumerate(chips)]
        for j, chip in enumerate(chips):
            copy(1 + j, (*chip, c), me).wait_recv()
            passed[j].start()
        copy(0, sibling, me).wait_recv()
        for j, chip in enumerate(chips):
            copy(4 + j, (*chip, 1 - c), me).wait_recv()
        for cp in first + passed:
            cp.wait_send()
        gathered[4 * x + 2 * y + c] = x_ref[...]
        total = gathered[0]
        for k in range(1, N_DEV):
            total = total + gathered[k]
        out_ref[...] = total

    return pl.pallas_call(
        body, name=name,
        out_shape=jax.ShapeDtypeStruct((rows, cols), F32),
        in_specs=[pl.BlockSpec(memory_space=pltpu.VMEM)],
        out_specs=pl.BlockSpec(memory_space=pltpu.VMEM),
        scratch_shapes=[pltpu.VMEM((N_DEV, rows, cols), F32),
                        pltpu.SemaphoreType.DMA((7,)), pltpu.SemaphoreType.DMA((7,))],
        compiler_params=pltpu.CompilerParams(vmem_limit_bytes=V7X_VMEM_LIMIT),
    )(packed)


class _GatherJob:
    def __init__(self, shards, rows=None, into=None):
        self.n, self.rows = len(shards), rows
        self.inputs = list(shards) + list(into or [])
        self.aliases = {self.n + i: i for i in range(len(into or []))}
        self.out_shape = [jax.ShapeDtypeStruct((N_DEV,) + s.shape, s.dtype) for s in shards]
        self.scratch = [pltpu.SemaphoreType.DMA((self.COPIES * self.n,)),
                        pltpu.SemaphoreType.DMA((self.COPIES * self.n,)), pltpu.SemaphoreType.DMA((self.n,))]
        self.whole = [rows or (0, s.shape[0]) for s in shards]
        self.first = [(lo, max(16, n // 32 * 16)) if n > 16 else (lo, n) for lo, n in self.whole]
        self.second = [(lo + h, n - h) for (lo, n), (_, h) in zip(self.whole, self.first)]

    COPIES = 8
    TO_SIBLING, TO_X, TO_Y, RELAY_X_BLOCK, RELAY_Y_BLOCK, PASS_X, PASS_Y, PASS_DIAGONAL = range(8)

    def _plan(self, ins, outs, scratch):
        send_sems, recv_sems, local_sems = scratch
        x, y, c = _position()

        def slot(i, block, rows):
            px, py, pc = block
            return _rows_of(outs[i].at[4 * px + 2 * py + pc], rows)

        def copy(i, k, block, to, rows, own=False):
            return pltpu.make_async_remote_copy(
                src_ref=_rows_of(ins[i], rows) if own else slot(i, block, rows), dst_ref=slot(i, block, rows),
                send_sem=send_sems.at[self.COPIES * i + k], recv_sem=recv_sems.at[self.COPIES * i + k],
                device_id=to, device_id_type=MESH)

        def mine(i):
            return pltpu.make_async_copy(_rows_of(ins[i], self.whole[i]), slot(i, (x, y, c), self.whole[i]),
                                         local_sems.at[i])

        return x, y, c, copy, mine

    def begin(self, ins, outs, scratch):
        x, y, c, copy, mine = self._plan(ins, outs, scratch)
        me = (x, y, c)
        for i in range(self.n):
            mine(i).start()
            copy(i, self.TO_SIBLING, me, (x, y, 1 - c), self.whole[i], own=True).start()
            copy(i, self.TO_X, me, (1 - x, y, c), self.whole[i], own=True).start()
            copy(i, self.TO_Y, me, (x, 1 - y, c), self.whole[i], own=True).start()

    def middle(self, ins, outs, scratch):
        x, y, c, copy, mine = self._plan(ins, outs, scratch)
        me, sibling, x_chip, y_chip = (x, y, c), (x, y, 1 - c), (1 - x, y, c), (x, 1 - y, c)
        for i in range(self.n):
            copy(i, self.TO_X, x_chip, me, self.whole[i]).wait_recv()
            copy(i, self.RELAY_X_BLOCK, x_chip, y_chip, self.first[i]).start()
            copy(i, self.PASS_X, x_chip, sibling, self.whole[i]).start()
            copy(i, self.TO_Y, y_chip, me, self.whole[i]).wait_recv()
            if self.second[i][1]:
                copy(i, self.RELAY_Y_BLOCK, y_chip, x_chip, self.second[i]).start()
            copy(i, self.PASS_Y, y_chip, sibling, self.whole[i]).start()

    def end(self, ins, outs, scratch):
        x, y, c, copy, mine = self._plan(ins, outs, scratch)
        me, sibling, x_chip, y_chip, diagonal = (x, y, c), (x, y, 1 - c), (1 - x, y, c), (x, 1 - y, c), (1 - x, 1 - y, c)
        for i in range(self.n):
            copy(i, self.RELAY_X_BLOCK, diagonal, me, self.first[i]).wait_recv()
            if self.second[i][1]:
                copy(i, self.RELAY_Y_BLOCK, diagonal, me, self.second[i]).wait_recv()
            copy(i, self.PASS_DIAGONAL, diagonal, sibling, self.whole[i]).start()
        for i in range(self.n):
            copy(i, self.TO_SIBLING, sibling, me, self.whole[i]).wait_recv()
            copy(i, self.PASS_X, (1 - x, y, 1 - c), me, self.whole[i]).wait_recv()
            copy(i, self.PASS_Y, (x, 1 - y, 1 - c), me, self.whole[i]).wait_recv()
            copy(i, self.PASS_DIAGONAL, (1 - x, 1 - y, 1 - c), me, self.whole[i]).wait_recv()
            copy(i, self.TO_SIBLING, me, sibling, self.whole[i], own=True).wait_send()
            copy(i, self.TO_X, me, x_chip, self.whole[i], own=True).wait_send()
            copy(i, self.TO_Y, me, y_chip, self.whole[i], own=True).wait_send()
            copy(i, self.RELAY_X_BLOCK, x_chip, y_chip, self.first[i]).wait_send()
            if self.second[i][1]:
                copy(i, self.RELAY_Y_BLOCK, y_chip, x_chip, self.second[i]).wait_send()
            copy(i, self.PASS_X, x_chip, sibling, self.whole[i]).wait_send()
            copy(i, self.PASS_Y, y_chip, sibling, self.whole[i]).wait_send()
            copy(i, self.PASS_DIAGONAL, diagonal, sibling, self.whole[i]).wait_send()
            mine(i).wait()


class _ChipSwapJob:
    def __init__(self, sums, rows=None, into=None):
        self.n, self.rows = len(sums), rows
        self.inputs = list(sums) + list(into or [])
        self.aliases = {self.n + i: i for i in range(len(into or []))}
        self.out_shape = [jax.ShapeDtypeStruct((3,) + s.shape[1:], s.dtype) for s in sums]
        self.scratch = [pltpu.SemaphoreType.DMA((3 * self.n,)), pltpu.SemaphoreType.DMA((3 * self.n,))]

    def _copies(self, ins, outs, scratch):
        send_sems, recv_sems = scratch
        x, y, c = _position()
        return [pltpu.make_async_remote_copy(
            src_ref=_rows_of(ins[i].at[2 * px + py], self.rows), dst_ref=_rows_of(outs[i].at[k], self.rows),
            send_sem=send_sems.at[3 * i + k], recv_sem=recv_sems.at[3 * i + k],
            device_id=(px, py, c), device_id_type=MESH)
            for i in range(self.n) for k, (px, py) in enumerate(_other_chips(x, y))]

    def begin(self, ins, outs, scratch):
        for cp in self._copies(ins, outs, scratch):
            cp.start()

    def middle(self, ins, outs, scratch):
        pass

    def end(self, ins, outs, scratch):
        for cp in self._copies(ins, outs, scratch):
            cp.wait()


class _SiblingSwapJob:
    aliases = {}

    def __init__(self, parts):
        self.inputs = list(parts)
        self.out_shape = [jax.ShapeDtypeStruct((4,) + p.shape[1:], p.dtype) for p in parts]
        n = len(parts)
        self.scratch = [pltpu.SemaphoreType.DMA((4 * n,)), pltpu.SemaphoreType.DMA((4 * n,))]

    def _copies(self, ins, outs, scratch):
        send_sems, recv_sems = scratch
        x, y, c = _position()
        return [pltpu.make_async_remote_copy(
            src_ref=ins[i].at[2 * j + (1 - c)], dst_ref=outs[i].at[j], send_sem=send_sems.at[4 * i + j],
            recv_sem=recv_sems.at[4 * i + j], device_id=(x, y, 1 - c), device_id_type=MESH)
            for i in range(len(ins)) for j in range(4)]

    def begin(self, ins, outs, scratch):
        for cp in self._copies(ins, outs, scratch):
            cp.start()

    def middle(self, ins, outs, scratch):
        pass

    def end(self, ins, outs, scratch):
        for cp in self._copies(ins, outs, scratch):
            cp.wait()


class _JobGroup:
    def __init__(self, jobs):
        self.jobs = list(jobs)
        self.inputs = [a for j in jobs for a in j.inputs]
        self.out_shape = [s for j in jobs for s in j.out_shape]
        self.scratch = [s for j in jobs for s in j.scratch]
        self.aliases, at_in, at_out = {}, 0, 0
        for j in jobs:
            self.aliases.update({at_in + i: at_out + o for i, o in j.aliases.items()})
            at_in, at_out = at_in + len(j.inputs), at_out + len(j.out_shape)

    def _each(self, phase, ins, outs, scratch):
        for j in self.jobs:
            n_in, n_out, n_scr = len(j.inputs), len(j.out_shape), len(j.scratch)
            getattr(j, phase)(ins[:n_in], outs[:n_out], scratch[:n_scr])
            ins, outs, scratch = ins[n_in:], outs[n_out:], scratch[n_scr:]

    def begin(self, ins, outs, scratch):
        self._each("begin", ins, outs, scratch)

    def middle(self, ins, outs, scratch):
        self._each("middle", ins, outs, scratch)

    def end(self, ins, outs, scratch):
        self._each("end", ins, outs, scratch)


def _rows_of(ref, rows):
    return ref if rows is None else ref.at[pl.ds(rows[0], rows[1])]


def _call(body, *, name, grid, in_specs, out_specs, out_shape, scratch_shapes, semantics, operands, job=None):
    if job is None:
        return pl.pallas_call(
            body, name=name, grid=grid, in_specs=in_specs, out_specs=out_specs, out_shape=out_shape,
            scratch_shapes=scratch_shapes, compiler_params=_cp(semantics))(*operands)
    n_in, n_out, n_scr = len(in_specs), len(out_specs), len(scratch_shapes)
    j_in, j_out = len(job.inputs), len(job.out_shape)
    n_steps = math.prod(grid)
    hbm = pl.BlockSpec(memory_space=pltpu.HBM)

    def carrier(*refs):
        ins, refs = refs[:n_in], refs[n_in:]
        job_ins, refs = refs[:j_in], refs[j_in:]
        outs, refs = refs[:n_out], refs[n_out:]
        job_outs, refs = refs[:j_out], refs[j_out:]
        scr, job_scr = refs[:n_scr], refs[n_scr:]
        step = pl.program_id(0)
        for axis in range(1, len(grid)):
            step = step * grid[axis] + pl.program_id(axis)

        @pl.when(step == 0)
        def _():
            job.begin(job_ins, job_outs, job_scr)

        body(*ins, *outs, *scr)

        @pl.when(step == (2 * n_steps) // 3)
        def _():
            job.middle(job_ins, job_outs, job_scr)

        @pl.when(step == n_steps - 1)
        def _():
            job.end(job_ins, job_outs, job_scr)

    res = pl.pallas_call(
        carrier, name=name, grid=grid,
        in_specs=list(in_specs) + [hbm] * j_in, out_specs=list(out_specs) + [hbm] * j_out,
        out_shape=list(out_shape) + job.out_shape, scratch_shapes=list(scratch_shapes) + job.scratch,
        input_output_aliases={n_in + i: n_out + o for i, o in job.aliases.items()},
        compiler_params=_cp(("arbitrary",) * len(grid)))(*operands, *job.inputs)
    return res[:n_out], res[n_out:]


MM_VMEM_BUDGET = 44 * 2**20
MM_TILE_CAP = 1536
MM_MIN_INTENSITY = 340


def _divisor_tiles(n, cap):
    return [t for t in range(LANES, min(n, cap) + 1, LANES) if n % t == 0] or [n]


def _mm_tiles(m, n_unit, k_unit, whole_k, a_bytes, b_bytes, o_bytes):
    best, best_key = None, None
    for tm in _divisor_tiles(m, 1024):
        for tn in _divisor_tiles(n_unit, MM_TILE_CAP):
            for tk in _divisor_tiles(k_unit, k_unit):
                one_block = whole_k and tk == k_unit
                need = (2 * (tm * tk * a_bytes + tk * tn * b_bytes) + 2 * tm * tn * o_bytes + tm * tn * 4
                        + (0 if one_block else tm * tn * 4))
                intensity = tm * tn / (tm + tn)
                key = (intensity >= MM_MIN_INTENSITY, one_block, intensity, tk)
                if need <= MM_VMEM_BUDGET and (best_key is None or key > best_key):
                    best, best_key = (tm, tn, tk), key
    if best is None:
        raise ValueError(f"no matmul tiles for {(m, n_unit, k_unit)}")
    return best


def _mm(a, b, mode, out_dtype, name, job=None, b_stacked=False, out_stack=None):
    b_rows, b_cols = (b.shape[1], b.shape[0] * b.shape[2]) if b_stacked else b.shape
    b_unit = b.shape[2] if b_stacked else b_cols
    if mode == "nn":
        (m, k), (k2, n) = a.shape, (b_rows, b_cols)
    elif mode == "nt":
        (m, k), (n, k2) = a.shape, (b_rows, b_cols)
    else:
        (k, m), (k2, n) = a.shape, (b_rows, b_cols)
    assert k == k2, (name, a.shape, b.shape)
    n_unit = n // out_stack if out_stack else (b_unit if b_stacked and mode != "nt" else n)
    k_unit = b_unit if b_stacked and mode == "nt" else k
    tm, tn, tk = _mm_tiles(m, n_unit, k_unit, k_unit == k, a.dtype.itemsize, b.dtype.itemsize,
                           jnp.dtype(out_dtype).itemsize)
    nk = k // tk
    per_n, per_k = n_unit // tn, k_unit // tk
    if mode == "tn":
        a_spec = pl.BlockSpec((tk, tm), lambda i, j, l: (l, i))
        dims = (((0,), (0,)), ((), ()))
    else:
        a_spec = pl.BlockSpec((tm, tk), lambda i, j, l: (i, l))
        dims = (((1,), (1,)), ((), ())) if mode == "nt" else (((1,), (0,)), ((), ()))
    if mode == "nt" and b_stacked:
        b_spec = pl.BlockSpec((None, tn, tk), lambda i, j, l: (l // per_k, j, l % per_k))
    elif mode == "nt":
        b_spec = pl.BlockSpec((tn, tk), lambda i, j, l: (j, l))
    elif b_stacked:
        b_spec = pl.BlockSpec((None, tk, tn), lambda i, j, l: (j // per_n, l, j % per_n))
    else:
        b_spec = pl.BlockSpec((tk, tn), lambda i, j, l: (l, j))
    if out_stack:
        o_spec = pl.BlockSpec((None, tm, tn), lambda i, j, l: (j // per_n, i, j % per_n))
        o_shape = jax.ShapeDtypeStruct((out_stack, m, n_unit), out_dtype)
    else:
        o_spec = pl.BlockSpec((tm, tn), lambda i, j, l: (i, j))
        o_shape = jax.ShapeDtypeStruct((m, n), out_dtype)

    def product(a_ref, b_ref):
        return lax.dot_general(a_ref[...].astype(BF16), b_ref[...].astype(BF16), dims, preferred_element_type=F32)

    def body_whole_k(a_ref, b_ref, o_ref):
        o_ref[...] = product(a_ref, b_ref).astype(o_ref.dtype)

    def body_split_k(a_ref, b_ref, o_ref, acc_ref):
        l = pl.program_id(2)

        @pl.when(l == 0)
        def _():
            acc_ref[...] = product(a_ref, b_ref)

        @pl.when(l > 0)
        def _():
            acc_ref[...] += product(a_ref, b_ref)

        @pl.when(l == nk - 1)
        def _():
            o_ref[...] = acc_ref[...].astype(o_ref.dtype)

    res = _call(
        body_whole_k if nk == 1 else body_split_k, name=name, grid=(m // tm, n // tn, nk),
        in_specs=[a_spec, b_spec],
        out_specs=[o_spec],
        out_shape=[o_shape],
        scratch_shapes=[] if nk == 1 else [pltpu.VMEM((tm, tn), F32)],
        semantics=("parallel", "parallel", "arbitrary"), operands=(a, b), job=job)
    return res[0] if job is None else (res[0][0], res[1])


def _rms_fwd(x, g, name):
    t, d = x.shape
    tm = _tile(t, 256, 16)

    def body(x_ref, g_ref, u_ref, r_ref):
        xv = x_ref[...]
        r = lax.rsqrt(jnp.mean(xv * xv, axis=-1, keepdims=True) + RMS_EPS)
        u_ref[...] = (xv * r * g_ref[...]).astype(u_ref.dtype)
        r_ref[...] = r

    return pl.pallas_call(
        body, name=name, grid=(t // tm,),
        in_specs=[pl.BlockSpec((tm, d), lambda i: (i, 0)), pl.BlockSpec((1, d), lambda i: (0, 0))],
        out_specs=[pl.BlockSpec((tm, d), lambda i: (i, 0)), pl.BlockSpec((tm, 1), lambda i: (i, 0))],
        out_shape=[jax.ShapeDtypeStruct((t, d), BF16), jax.ShapeDtypeStruct((t, 1), F32)],
        compiler_params=_cp(("parallel",)),
    )(x, g)


def _rms_bwd(dn_parts, x, r, g, extra, name, job=None):
    t, d = x.shape
    tm = _tile(t, 128, 16)
    n_dn, n_extra = len(dn_parts), len(extra)

    def body(*refs):
        dn_refs = refs[:n_dn]
        x_ref, r_ref, g_ref = refs[n_dn:n_dn + 3]
        extra_refs = refs[n_dn + 3:n_dn + 3 + n_extra]
        dx_ref, dxb_ref, dg_ref = refs[n_dn + 3 + n_extra:]
        xhat = x_ref[...] * r_ref[...]
        dnv = dn_refs[0][...].astype(F32)
        for p in dn_refs[1:]:
            dnv = dnv + p[...].astype(F32)
        gd = dnv * g_ref[...]
        dx = r_ref[...] * (gd - xhat * jnp.mean(gd * xhat, axis=-1, keepdims=True))
        for e in extra_refs:
            dx = dx + e[...].astype(F32)
        dx_ref[...] = dx
        dxb_ref[...] = dx.astype(dxb_ref.dtype)

        @pl.when(pl.program_id(0) == 0)
        def _():
            dg_ref[...] = jnp.zeros_like(dg_ref)

        dg_ref[...] += jnp.sum(dnv * xhat, axis=0, keepdims=True)

    row = pl.BlockSpec((tm, d), lambda i: (i, 0))
    return _call(
        body, name=name, grid=(t // tm,),
        in_specs=[row] * n_dn + [row, pl.BlockSpec((tm, 1), lambda i: (i, 0)), pl.BlockSpec((1, d), lambda i: (0, 0))]
        + [row] * n_extra,
        out_specs=[row, row, pl.BlockSpec((1, d), lambda i: (0, 0))],
        out_shape=[jax.ShapeDtypeStruct((t, d), F32), jax.ShapeDtypeStruct((t, d), BF16), jax.ShapeDtypeStruct((1, d), F32)],
        scratch_shapes=[], semantics=("arbitrary",), operands=(*dn_parts, x, r, g, *extra), job=job)


def _gate_merge_fwd(z, gate_col, b_gates, pf, ps, name):
    t, d = pf.shape
    tm, tn = _tile(t, 512, 16), _tile(math.gcd(d, gate_col), 512)
    nj, off = d // tn, gate_col // tn
    assert gate_col % tn == 0

    def body(zf_ref, zs_ref, bf_ref, bs_ref, pf_ref, ps_ref, o_ref):
        gf = _sigmoid(zf_ref[...].astype(F32) + bf_ref[...])
        gs = _sigmoid(zs_ref[...].astype(F32) + bs_ref[...])
        o_ref[...] = (gf * pf_ref[...].astype(F32) + gs * ps_ref[...].astype(F32)).astype(o_ref.dtype)

    blk = pl.BlockSpec((tm, tn), lambda i, j: (i, j))
    return pl.pallas_call(
        body, name=name, grid=(t // tm, nj),
        in_specs=[pl.BlockSpec((tm, tn), lambda i, j: (i, off + j)), pl.BlockSpec((tm, tn), lambda i, j: (i, off + nj + j)),
                  pl.BlockSpec((1, tn), lambda i, j: (0, j)), pl.BlockSpec((1, tn), lambda i, j: (0, nj + j)), blk, blk],
        out_specs=blk,
        out_shape=jax.ShapeDtypeStruct((t, d), BF16),
        compiler_params=_cp(("parallel", "parallel")),
    )(z, z, b_gates, b_gates, pf, ps)


def _gate_merge_bwd(dm, z, gate_col, b_gates, pf, ps, name):
    t, d = pf.shape
    tm, tn = _tile(t, 512, 16), _tile(math.gcd(d, gate_col), 512)
    nj, off = d // tn, gate_col // tn

    def body(dm_ref, zf_ref, zs_ref, bf_ref, bs_ref, pf_ref, ps_ref, dpf_ref, dps_ref, dzf_ref, dzs_ref, dbf_ref, dbs_ref):
        gf = _sigmoid(zf_ref[...].astype(F32) + bf_ref[...])
        gs = _sigmoid(zs_ref[...].astype(F32) + bs_ref[...])
        dmv = dm_ref[...].astype(F32)
        dpf_ref[...] = (dmv * gf).astype(dpf_ref.dtype)
        dps_ref[...] = (dmv * gs).astype(dps_ref.dtype)
        dzf = dmv * pf_ref[...].astype(F32) * gf * (1.0 - gf)
        dzs = dmv * ps_ref[...].astype(F32) * gs * (1.0 - gs)
        dzf_ref[...] = dzf.astype(dzf_ref.dtype)
        dzs_ref[...] = dzs.astype(dzs_ref.dtype)

        @pl.when(pl.program_id(1) == 0)
        def _():
            dbf_ref[...] = jnp.zeros_like(dbf_ref)
            dbs_ref[...] = jnp.zeros_like(dbs_ref)

        dbf_ref[...] += jnp.sum(dzf, axis=0, keepdims=True)
        dbs_ref[...] += jnp.sum(dzs, axis=0, keepdims=True)

    blk = pl.BlockSpec((tm, tn), lambda j, i: (i, j))
    lo = pl.BlockSpec((1, tn), lambda j, i: (0, j))
    hi = pl.BlockSpec((1, tn), lambda j, i: (0, nj + j))
    return pl.pallas_call(
        body, name=name, grid=(nj, t // tm),
        in_specs=[blk, pl.BlockSpec((tm, tn), lambda j, i: (i, off + j)), pl.BlockSpec((tm, tn), lambda j, i: (i, off + nj + j)),
                  lo, hi, blk, blk],
        out_specs=[blk, blk, blk, blk, lo, lo],
        out_shape=[jax.ShapeDtypeStruct((t, d), BF16)] * 4 + [jax.ShapeDtypeStruct((1, d), F32)] * 2,
        compiler_params=_cp(("parallel", "arbitrary")),
    )(dm, z, z, b_gates, b_gates, pf, ps)


def _resid_rms(x, mo, g, name):
    t, d = x.shape
    tm = _tile(t, 256, 16)

    def body(x_ref, mo_ref, g_ref, h_ref, hn_ref, r_ref):
        h = x_ref[...] + mo_ref[...].astype(F32)
        r = lax.rsqrt(jnp.mean(h * h, axis=-1, keepdims=True) + RMS_EPS)
        h_ref[...] = h
        hn_ref[...] = (h * r * g_ref[...]).astype(hn_ref.dtype)
        r_ref[...] = r

    row = pl.BlockSpec((tm, d), lambda i: (i, 0))
    col = pl.BlockSpec((tm, 1), lambda i: (i, 0))
    return pl.pallas_call(
        body, name=name, grid=(t // tm,),
        in_specs=[row, row, pl.BlockSpec((1, d), lambda i: (0, 0))],
        out_specs=[row, row, col],
        out_shape=[jax.ShapeDtypeStruct((t, d), F32), jax.ShapeDtypeStruct((t, d), BF16), jax.ShapeDtypeStruct((t, 1), F32)],
        compiler_params=_cp(("parallel",)),
    )(x, mo, g)


def _swiglu_fwd(gu, name):
    t, f2 = gu.shape
    f = f2 // 2
    tm, tn = _tile(t, 512, 16), _tile(f, 1024)
    nj = f // tn

    def body(g_ref, u_ref, o_ref):
        gate = g_ref[...].astype(F32)
        o_ref[...] = (gate * _sigmoid(gate) * u_ref[...].astype(F32)).astype(o_ref.dtype)

    return pl.pallas_call(
        body, name=name, grid=(t // tm, nj),
        in_specs=[pl.BlockSpec((tm, tn), lambda i, j: (i, j)), pl.BlockSpec((tm, tn), lambda i, j: (i, nj + j))],
        out_specs=pl.BlockSpec((tm, tn), lambda i, j: (i, j)),
        out_shape=jax.ShapeDtypeStruct((t, f), BF16),
        compiler_params=_cp(("parallel", "parallel")),
    )(gu, gu)


def _swiglu_bwd(gu, dact, name, job=None):
    t, f2 = gu.shape
    f = f2 // 2
    tm, tn = _tile(t, 512, 16), _tile(f, 1024)
    nj = f // tn

    def body(g_ref, u_ref, da_ref, dg_ref, du_ref):
        gate = g_ref[...].astype(F32)
        s = _sigmoid(gate)
        da = da_ref[...].astype(F32)
        dg_ref[...] = (da * u_ref[...].astype(F32) * s * (1.0 + gate * (1.0 - s))).astype(dg_ref.dtype)
        du_ref[...] = (da * gate * s).astype(du_ref.dtype)

    lo = pl.BlockSpec((tm, tn), lambda i, j: (i, j))
    return _call(
        body, name=name, grid=(t // tm, nj),
        in_specs=[lo, pl.BlockSpec((tm, tn), lambda i, j: (i, nj + j)), lo],
        out_specs=[lo, lo],
        out_shape=[jax.ShapeDtypeStruct((t, f), BF16)] * 2,
        scratch_shapes=[], semantics=("parallel", "parallel"), operands=(gu, gu, dact), job=job)


def _loss_head(h, dn, target, name):
    t, d = h.shape
    tm = _tile(t, 256, 16)

    def body(h_ref, dn_ref, t_ref, loss_ref, dy_ref, dyb_ref):
        err = h_ref[...] + dn_ref[...].astype(F32) - t_ref[...]
        dy_ref[...] = err * (1.0 / d)
        dyb_ref[...] = (err * (1.0 / d)).astype(dyb_ref.dtype)

        @pl.when(pl.program_id(0) == 0)
        def _():
            loss_ref[...] = jnp.zeros_like(loss_ref)

        loss_ref[...] += 0.5 * jnp.sum(jnp.mean(err * err, axis=-1, keepdims=True))

    row = pl.BlockSpec((tm, d), lambda i: (i, 0))
    return pl.pallas_call(
        body, name=name, grid=(t // tm,),
        in_specs=[row, row, row],
        out_specs=[pl.BlockSpec((8, LANES), lambda i: (0, 0)), row, row],
        out_shape=[jax.ShapeDtypeStruct((8, LANES), F32), jax.ShapeDtypeStruct((t, d), F32), jax.ShapeDtypeStruct((t, d), BF16)],
        compiler_params=_cp(("arbitrary",)),
    )(h, dn, target)


def _qk_prep(z, heads, dh, q_norm, k_norm, name):
    t = z.shape[0]
    tq = _tile(t, 512, 16)
    scale = 1.0 / math.sqrt(dh)

    def body(q_ref, k_ref, gq_ref, gk_ref, qn_ref, kn_ref, rq_ref, rk_ref):
        q = q_ref[...].astype(F32)
        k = k_ref[...].astype(F32)
        rq = lax.rsqrt(jnp.mean(q * q, axis=-1, keepdims=True) + RMS_EPS)
        rk = lax.rsqrt(jnp.mean(k * k, axis=-1, keepdims=True) + RMS_EPS)
        qn_ref[...] = (q * rq * gq_ref[...] * scale).astype(qn_ref.dtype)
        kn_ref[...] = (k * rk * gk_ref[...]).astype(kn_ref.dtype)
        rq_ref[...] = rq
        rk_ref[...] = rk

    blk = pl.BlockSpec((tq, dh), lambda i, h: (i, h))
    vec = pl.BlockSpec((1, dh), lambda i, h: (0, 0))
    col = pl.BlockSpec((None, tq, 1), lambda i, h: (h, i, 0))
    return pl.pallas_call(
        body, name=name, grid=(t // tq, heads),
        in_specs=[blk, pl.BlockSpec((tq, dh), lambda i, h: (i, heads + h)), vec, vec],
        out_specs=[blk, blk, col, col],
        out_shape=[jax.ShapeDtypeStruct((t, heads * dh), BF16)] * 2 + [jax.ShapeDtypeStruct((heads, t, 1), F32)] * 2,
        compiler_params=_cp(("parallel", "parallel")),
    )(z, z, q_norm, k_norm)


def _qk_prep_bwd(dqn, dkn, z, heads, dh, q_norm, k_norm, rq, rk, name):
    t = z.shape[0]
    tq = _tile(t, 512, 16)
    scale = 1.0 / math.sqrt(dh)

    def norm_bwd(dy, xv, r, g):
        xhat = xv * r
        gd = dy * g
        return r * (gd - xhat * jnp.mean(gd * xhat, axis=-1, keepdims=True)), jnp.sum(dy * xhat, axis=0, keepdims=True)

    def body(dqn_ref, dkn_ref, q_ref, k_ref, gq_ref, gk_ref, rq_ref, rk_ref, dq_ref, dk_ref, dgq_ref, dgk_ref):
        dq, dgq = norm_bwd(dqn_ref[...].astype(F32) * scale, q_ref[...].astype(F32), rq_ref[...], gq_ref[...])
        dk, dgk = norm_bwd(dkn_ref[...].astype(F32), k_ref[...].astype(F32), rk_ref[...], gk_ref[...])
        dq_ref[...] = dq.astype(dq_ref.dtype)
        dk_ref[...] = dk.astype(dk_ref.dtype)

        @pl.when((pl.program_id(0) == 0) & (pl.program_id(1) == 0))
        def _():
            dgq_ref[...] = jnp.zeros_like(dgq_ref)
            dgk_ref[...] = jnp.zeros_like(dgk_ref)

        dgq_ref[...] += dgq
        dgk_ref[...] += dgk

    blk = pl.BlockSpec((tq, dh), lambda i, h: (i, h))
    vec = pl.BlockSpec((1, dh), lambda i, h: (0, 0))
    col = pl.BlockSpec((None, tq, 1), lambda i, h: (h, i, 0))
    return pl.pallas_call(
        body, name=name, grid=(t // tq, heads),
        in_specs=[blk, blk, blk, pl.BlockSpec((tq, dh), lambda i, h: (i, heads + h)), vec, vec, col, col],
        out_specs=[blk, blk, vec, vec],
        out_shape=[jax.ShapeDtypeStruct((t, heads * dh), BF16)] * 2 + [jax.ShapeDtypeStruct((1, dh), F32)] * 2,
        compiler_params=_cp(("arbitrary", "arbitrary")),
    )(dqn, dkn, z, z, q_norm, k_norm, rq, rk)


def _tri_ones(n, upper):
    row = lax.broadcasted_iota(jnp.int32, (n, n), 0)
    col = lax.broadcasted_iota(jnp.int32, (n, n), 1)
    return jnp.where((col >= row) if upper else (col <= row), 1.0, 0.0).astype(F32)


def _forget_fwd(f, b, name):
    t, w = f.shape
    blk = _tile(t, 256, 8)
    nb = t // blk

    def body(f_ref, b_ref, out_ref):
        tri = _tri_ones(blk, upper=False)

        def step(i, carry):
            rows = pl.ds(pl.multiple_of(i * blk, blk), blk)
            logf = jax.nn.log_sigmoid(f_ref[rows, :] + b_ref[...])
            acc = jnp.dot(tri, logf, precision=lax.Precision.HIGHEST, preferred_element_type=F32) + carry
            out_ref[rows, :] = acc
            return acc[blk - 1:blk, :]

        lax.fori_loop(0, nb, step, jnp.zeros((1, w), F32))

    return pl.pallas_call(
        body, name=name,
        in_specs=[pl.BlockSpec(memory_space=pltpu.VMEM)] * 2,
        out_specs=pl.BlockSpec(memory_space=pltpu.VMEM),
        out_shape=jax.ShapeDtypeStruct((t, w), F32),
        compiler_params=_cp(),
    )(f, b)


def _forget_bwd(d_query, d_key, f, b, name):
    t, w = f.shape
    blk = _tile(t, 256, 8)
    nb = t // blk

    def body(dq_ref, dk_ref, f_ref, b_ref, df_ref, db_ref):
        tri = _tri_ones(blk, upper=True)

        def step(i, carry):
            suffix, db = carry
            rows = pl.ds(pl.multiple_of((nb - 1 - i) * blk, blk), blk)
            dcum = dq_ref[rows, :] - dk_ref[rows, :]
            dlog = suffix + jnp.dot(tri, dcum, precision=lax.Precision.HIGHEST, preferred_element_type=F32)
            df = dlog * _sigmoid(-(f_ref[rows, :] + b_ref[...]))
            df_ref[rows, :] = df
            return dlog[0:1, :], db + jnp.sum(df, axis=0, keepdims=True)

        _, db = lax.fori_loop(0, nb, step, (jnp.zeros((1, w), F32), jnp.zeros((1, w), F32)))
        db_ref[...] = db

    return pl.pallas_call(
        body, name=name,
        in_specs=[pl.BlockSpec(memory_space=pltpu.VMEM)] * 4,
        out_specs=[pl.BlockSpec(memory_space=pltpu.VMEM)] * 2,
        out_shape=[jax.ShapeDtypeStruct((t, w), F32), jax.ShapeDtypeStruct((1, w), F32)],
        compiler_params=_cp(),
    )(d_query, d_key, f, b)


def _attn_logits(q, k, f_keys, f_first, diagonal):
    s = lax.dot_general(q, k, (((1,), (1,)), ((), ())), preferred_element_type=F32)
    s = s - (f_keys - f_first)
    if diagonal:
        row = lax.broadcasted_iota(jnp.int32, s.shape, 0)
        col = lax.broadcasted_iota(jnp.int32, s.shape, 1)
        s = jnp.where(col <= row, s, MASK_VALUE)
    return s


def _block_at(i, blk):
    return pl.ds(pl.multiple_of(i * blk, blk), blk)


ATTN_BLOCK = 512


def _attn_fwd(qn, kn, v_src, v_col, fcol, frow, heads, dh, name, job=None):
    t = qn.shape[0]
    blk = _tile(t, ATTN_BLOCK)

    def body(q_ref, k_ref, v_ref, fc_ref, fr_ref, o_ref, lse_ref):
        qi = pl.program_id(1)
        q = q_ref[...]
        f_first = fc_ref[0:1, :]

        def block(ki, carry, diagonal):
            m, l, acc = carry
            keys = _block_at(ki, blk)
            s = _attn_logits(q, k_ref[keys, :], fr_ref[:, keys], f_first, diagonal)
            m_new = jnp.maximum(m, jnp.max(s, axis=-1, keepdims=True))
            alpha = jnp.exp(m - m_new)
            p = jnp.exp(s - m_new)
            l = alpha * l + jnp.sum(p, axis=-1, keepdims=True)
            acc = alpha * acc + jnp.dot(p.astype(BF16), v_ref[keys, :].astype(BF16), preferred_element_type=F32)
            return m_new, l, acc

        start = (jnp.full((blk, 1), MASK_VALUE, F32), jnp.zeros((blk, 1), F32), jnp.zeros((blk, dh), F32))
        below = lax.fori_loop(0, qi, lambda ki, carry: block(ki, carry, False), start)
        m, l, acc = block(qi, below, True)
        o_ref[...] = (acc / l).astype(o_ref.dtype)
        lse_ref[...] = m + jnp.log(l)

    qblk = pl.BlockSpec((blk, dh), lambda h, i: (i, h))
    qcol = pl.BlockSpec((None, blk, 1), lambda h, i: (h, i, 0))
    return _call(
        body, name=name, grid=(heads, t // blk),
        in_specs=[qblk,
                  pl.BlockSpec((t, dh), lambda h, i: (0, h)),
                  pl.BlockSpec((t, dh), lambda h, i: (0, v_col + h)),
                  qcol,
                  pl.BlockSpec((None, 1, t), lambda h, i: (h, 0, 0))],
        out_specs=[qblk, qcol],
        out_shape=[jax.ShapeDtypeStruct((t, heads * dh), BF16), jax.ShapeDtypeStruct((heads, t, 1), F32)],
        scratch_shapes=[], semantics=("parallel", "arbitrary"), operands=(qn, kn, v_src, fcol, frow), job=job)


def _attn_delta(o, do, heads, dh, name):
    t = o.shape[0]
    tq = _tile(t, 512, 16)

    def body(o_ref, do_ref, out_ref):
        out_ref[...] = jnp.sum(o_ref[...].astype(F32) * do_ref[...].astype(F32), axis=-1, keepdims=True)

    blk = pl.BlockSpec((tq, dh), lambda i, h: (i, h))
    return pl.pallas_call(
        body, name=name, grid=(t // tq, heads),
        in_specs=[blk, blk],
        out_specs=pl.BlockSpec((None, tq, 1), lambda i, h: (h, i, 0)),
        out_shape=jax.ShapeDtypeStruct((heads, t, 1), F32),
        compiler_params=_cp(("parallel", "parallel")),
    )(o, do)


def _attn_bwd_q(qn, kn, v_src, v_col, do, fcol, frow, lse, delta, heads, dh, name, job=None):
    t = qn.shape[0]
    blk = _tile(t, ATTN_BLOCK)

    def body(q_ref, k_ref, v_ref, do_ref, fc_ref, fr_ref, lse_ref, dl_ref, dq_ref, dfq_ref):
        qi = pl.program_id(1)
        q, dob = q_ref[...], do_ref[...].astype(BF16)
        f_first, lse, dl = fc_ref[0:1, :], lse_ref[...], dl_ref[...]

        def block(ki, carry, diagonal):
            dq, dfq = carry
            keys = _block_at(ki, blk)
            k = k_ref[keys, :]
            p = jnp.exp(_attn_logits(q, k, fr_ref[:, keys], f_first, diagonal) - lse)
            dp = lax.dot_general(dob, v_ref[keys, :].astype(BF16), (((1,), (1,)), ((), ())), preferred_element_type=F32)
            ds = p * (dp - dl)
            return dq + jnp.dot(ds.astype(BF16), k, preferred_element_type=F32), dfq + jnp.sum(ds, axis=-1, keepdims=True)

        start = (jnp.zeros((blk, dh), F32), jnp.zeros((blk, 1), F32))
        below = lax.fori_loop(0, qi, lambda ki, carry: block(ki, carry, False), start)
        dq_ref[...], dfq_ref[...] = block(qi, below, True)

    qblk = pl.BlockSpec((blk, dh), lambda h, i: (i, h))
    qcol = pl.BlockSpec((None, blk, 1), lambda h, i: (h, i, 0))
    return _call(
        body, name=name, grid=(heads, t // blk),
        in_specs=[qblk,
                  pl.BlockSpec((t, dh), lambda h, i: (0, h)),
                  pl.BlockSpec((t, dh), lambda h, i: (0, v_col + h)),
                  qblk, qcol,
                  pl.BlockSpec((None, 1, t), lambda h, i: (h, 0, 0)),
                  qcol, qcol],
        out_specs=[qblk, qcol],
        out_shape=[jax.ShapeDtypeStruct((t, heads * dh), F32), jax.ShapeDtypeStruct((heads, t, 1), F32)],
        scratch_shapes=[], semantics=("parallel", "arbitrary"),
        operands=(qn, kn, v_src, do, fcol, frow, lse, delta), job=job)


def _attn_bwd_kv(qn, kn, v_src, v_col, do, fcol, frow, lse, delta, heads, dh, name, job=None):
    t = qn.shape[0]
    blk = _tile(t, ATTN_BLOCK)
    nq = t // blk
    tn_dims = (((0,), (0,)), ((), ()))

    def body(q_ref, k_ref, v_ref, do_ref, fc_ref, fr_ref, lse_ref, dl_ref, dk_ref, dv_ref, dfk_ref):
        ki = pl.program_id(1)
        k, v, f_keys = k_ref[...], v_ref[...].astype(BF16), fr_ref[...]

        def block(qi, carry, diagonal):
            dk, dv, dfk = carry
            rows = _block_at(qi, blk)
            q, dob = q_ref[rows, :], do_ref[rows, :].astype(BF16)
            f_first = fc_ref[pl.ds(pl.multiple_of(qi * blk, blk), 1), :]
            p = jnp.exp(_attn_logits(q, k, f_keys, f_first, diagonal) - lse_ref[rows, :])
            dp = lax.dot_general(dob, v, (((1,), (1,)), ((), ())), preferred_element_type=F32)
            ds = p * (dp - dl_ref[rows, :])
            dv = dv + lax.dot_general(p.astype(BF16), dob, tn_dims, preferred_element_type=F32)
            dk = dk + lax.dot_general(ds.astype(BF16), q, tn_dims, preferred_element_type=F32)
            return dk, dv, dfk + jnp.sum(ds, axis=0, keepdims=True)

        start = (jnp.zeros((blk, dh), F32), jnp.zeros((blk, dh), F32), jnp.zeros((1, blk), F32))
        dk, dv, dfk = lax.fori_loop(ki + 1, nq, lambda qi, carry: block(qi, carry, False), block(ki, start, True))
        dk_ref[...] = dk
        dv_ref[...] = dv.astype(dv_ref.dtype)
        dfk_ref[...] = dfk

    whole = pl.BlockSpec((t, dh), lambda h, j: (0, h))
    wcol = pl.BlockSpec((None, t, 1), lambda h, j: (h, 0, 0))
    kblk = pl.BlockSpec((blk, dh), lambda h, j: (j, h))
    krow = pl.BlockSpec((None, 1, blk), lambda h, j: (h, 0, j))
    return _call(
        body, name=name, grid=(heads, nq),
        in_specs=[whole, kblk, pl.BlockSpec((blk, dh), lambda h, j: (j, v_col + h)), whole, wcol, krow, wcol, wcol],
        out_specs=[kblk, kblk, krow],
        out_shape=[jax.ShapeDtypeStruct((t, heads * dh), F32), jax.ShapeDtypeStruct((t, heads * dh), BF16),
                   jax.ShapeDtypeStruct((heads, 1, t), F32)],
        scratch_shapes=[], semantics=("parallel", "arbitrary"),
        operands=(qn, kn, v_src, do, fcol, frow, lse, delta), job=job)


TIME_TILE = 8


def _s5_discretize(lam_re, lam_im, log_step, b_re, b_im):
    dt = jnp.exp(log_step)
    mag = jnp.exp(lam_re * dt)
    lb_re = mag * jnp.cos(lam_im * dt)
    lb_im = mag * jnp.sin(lam_im * dt)
    denom = lam_re * lam_re + lam_im * lam_im
    num_re = lb_re - 1.0
    fac_re = (num_re * lam_re + lb_im * lam_im) / denom
    fac_im = (lb_im * lam_re - num_re * lam_im) / denom
    return lb_re, lb_im, fac_re * b_re - fac_im * b_im, fac_re * b_im + fac_im * b_re


def _s5_prep(lam_re, lam_im, log_step, b_re, b_im, name):
    gp, width = b_re.shape

    def body(lr, li, ls, br, bi, o_lr, o_li, o_br, o_bi):
        res = _s5_discretize(lr[...], li[...], ls[...], br[...], bi[...])
        for ref, val in zip((o_lr, o_li, o_br, o_bi), res):
            ref[...] = val

    vm = pl.BlockSpec(memory_space=pltpu.VMEM)
    return pl.pallas_call(
        body, name=name, in_specs=[vm] * 5, out_specs=[vm] * 4,
        out_shape=[jax.ShapeDtypeStruct((gp, 1), F32)] * 2 + [jax.ShapeDtypeStruct((gp, width), F32)] * 2,
        compiler_params=_cp(),
    )(lam_re, lam_im, log_step, b_re, b_im)


def _s5_prep_bwd(lam_re, lam_im, log_step, b_re, b_im, d_lb_re, d_lb_im, d_bb_re, d_bb_im, groups, name):
    gp, width = b_re.shape
    states = gp // groups
    tr = _tile(gp, 512, 8)

    def body(lr, li, ls, br, bi, g_lr, g_li, g_br, g_bi, o_lr, o_li, o_ls, o_br, o_bi):
        _, vjp = jax.vjp(_s5_discretize, lr[...], li[...], ls[...], br[...], bi[...])
        d_lr, d_li, d_ls, d_br, d_bi = vjp((g_lr[...], g_li[...], g_br[...], g_bi[...]))
        o_lr[...] = d_lr
        o_li[...] = d_li
        o_br[...] = d_br
        o_bi[...] = d_bi
        row_group = (pl.program_id(0) * tr + lax.broadcasted_iota(jnp.int32, (tr, groups), 0)) // states
        col_group = lax.broadcasted_iota(jnp.int32, (tr, groups), 1)

        @pl.when(pl.program_id(0) == 0)
        def _():
            o_ls[...] = jnp.zeros_like(o_ls)

        o_ls[...] += jnp.sum(jnp.where(row_group == col_group, d_ls, 0.0), axis=0, keepdims=True)

    col = pl.BlockSpec((tr, 1), lambda i: (i, 0))
    mat = pl.BlockSpec((tr, width), lambda i: (i, 0))
    return pl.pallas_call(
        body, name=name, grid=(gp // tr,),
        in_specs=[col, col, col, mat, mat, col, col, mat, mat],
        out_specs=[col, col, pl.BlockSpec((1, groups), lambda i: (0, 0)), mat, mat],
        out_shape=[jax.ShapeDtypeStruct((gp, 1), F32)] * 2 + [jax.ShapeDtypeStruct((1, groups), F32)]
        + [jax.ShapeDtypeStruct((gp, width), F32)] * 2,
        compiler_params=_cp(("arbitrary",)),
    )(lam_re, lam_im, log_step, b_re, b_im, d_lb_re, d_lb_im, d_bb_re, d_bb_im)


def _shift_time(v, s, reverse):
    row = lax.broadcasted_iota(jnp.int32, v.shape, 0)
    if reverse:
        return jnp.where(row < TIME_TILE - s, pltpu.roll(v, TIME_TILE - s, 0), 0.0)
    return jnp.where(row >= s, pltpu.roll(v, s, 0), 0.0)


def _cmul(ar, ai, br, bi):
    return ar * br - ai * bi, ar * bi + ai * br


def _scan_time(xr_ref, xi_ref, ar, ai, reverse):
    t = xr_ref.shape[0]
    n_tiles = t // TIME_TILE
    powers = [(ar, ai)]
    for _ in range(TIME_TILE - 1):
        powers.append(_cmul(*powers[-1], ar, ai))
    order = powers[::-1] if reverse else powers
    carry_r = jnp.concatenate([p[0] for p in order], axis=0)
    carry_i = jnp.concatenate([p[1] for p in order], axis=0)
    levels = [(1, powers[0]), (2, powers[1]), (4, powers[3])]
    last = 0 if reverse else TIME_TILE - 1

    def tile(i, carry):
        cr, ci = carry
        idx = (n_tiles - 1 - i) if reverse else i
        rows = pl.ds(pl.multiple_of(idx * TIME_TILE, TIME_TILE), TIME_TILE)
        br, bi = xr_ref[rows, :], xi_ref[rows, :]
        for s, (pr, pi) in levels:
            sr, si = _cmul(pr, pi, _shift_time(br, s, reverse), _shift_time(bi, s, reverse))
            br, bi = br + sr, bi + si
        kr, ki = _cmul(carry_r, carry_i, cr, ci)
        br, bi = br + kr, bi + ki
        xr_ref[rows, :] = br
        xi_ref[rows, :] = bi
        return br[last:last + 1, :], bi[last:last + 1, :]

    zero = jnp.zeros_like(ar)
    lax.fori_loop(0, n_tiles, tile, (zero, zero))


def _s5_states(u_ref, bbr_ref, bbi_ref, ar_ref, ai_ref, xr, xi, chunk):
    t = u_ref.shape[0]
    for r0 in range(0, t, chunk):
        rows = pl.ds(r0, chunk)
        xr[rows, :] = jnp.dot(u_ref[rows, :], bbr_ref[...], preferred_element_type=F32)
        xi[rows, :] = jnp.dot(u_ref[rows, :], bbi_ref[...], preferred_element_type=F32)
    _scan_time(xr, xi, ar_ref[...], ai_ref[...], reverse=False)


def _s5_specs(t, nb_lanes, state_lanes):
    tok = pl.BlockSpec((t, nb_lanes), lambda j: (0, j))
    bb = pl.BlockSpec((None, nb_lanes, state_lanes), lambda j: (j, 0, 0))
    cc = pl.BlockSpec((None, state_lanes, nb_lanes), lambda j: (j, 0, 0))
    dvec = pl.BlockSpec((1, nb_lanes), lambda j: (0, j))
    avec = pl.BlockSpec((1, state_lanes), lambda j: (0, j))
    return tok, bb, cc, dvec, avec


def _s5_fwd(u5, bbr, bbi, ccr, cci, dskip, ar, ai, name, job=None):
    t, w = u5.shape
    nb, nb_lanes, state_lanes = bbr.shape
    chunk = _tile(t, 512, 16)

    def body(u_ref, bbr_ref, bbi_ref, cr_ref, ci_ref, d_ref, ar_ref, ai_ref, y_ref, xr, xi):
        _s5_states(u_ref, bbr_ref, bbi_ref, ar_ref, ai_ref, xr, xi, chunk)
        for r0 in range(0, t, chunk):
            rows = pl.ds(r0, chunk)
            y = jnp.dot(xr[rows, :].astype(BF16), cr_ref[...], preferred_element_type=F32)
            y = y - jnp.dot(xi[rows, :].astype(BF16), ci_ref[...], preferred_element_type=F32)
            y_ref[rows, :] = y + d_ref[...] * u_ref[rows, :].astype(F32)

    tok, bb, cc, dvec, avec = _s5_specs(t, nb_lanes, state_lanes)
    return _call(
        body, name=name, grid=(nb,),
        in_specs=[tok, bb, bb, cc, cc, dvec, avec, avec],
        out_specs=[tok],
        out_shape=[jax.ShapeDtypeStruct((t, w), F32)],
        scratch_shapes=[pltpu.VMEM((t, state_lanes), F32)] * 2,
        semantics=("parallel",), operands=(u5, bbr, bbi, ccr, cci, dskip, ar, ai), job=job)


def _s5_bwd(u5, dy, bbr, bbi, ccr, cci, dskip, ar, ai, name, job=None):
    t, w = u5.shape
    nb, nb_lanes, state_lanes = bbr.shape
    chunk = _tile(t, 512, 16)
    nt_dims = (((1,), (1,)), ((), ()))
    tn_dims = (((0,), (0,)), ((), ()))

    def body(u_ref, dy_ref, bbr_ref, bbi_ref, cr_ref, ci_ref, d_ref, ar_ref, ai_ref,
             du_ref, dbbr_ref, dbbi_ref, dcr_ref, dci_ref, dar_ref, dai_ref, dd_ref, xr, xi, gr, gi):
        _s5_states(u_ref, bbr_ref, bbi_ref, ar_ref, ai_ref, xr, xi, chunk)
        for r0 in range(0, t, chunk):
            rows = pl.ds(r0, chunk)
            dyb = dy_ref[rows, :].astype(BF16)
            gr[rows, :] = lax.dot_general(dyb, cr_ref[...], nt_dims, preferred_element_type=F32)
            gi[rows, :] = -lax.dot_general(dyb, ci_ref[...], nt_dims, preferred_element_type=F32)
        _scan_time(gr, gi, ar_ref[...], -ai_ref[...], reverse=True)

        dcr = jnp.zeros((state_lanes, nb_lanes), F32)
        dci = jnp.zeros((state_lanes, nb_lanes), F32)
        dbr = jnp.zeros((nb_lanes, state_lanes), F32)
        dbi = jnp.zeros((nb_lanes, state_lanes), F32)
        dd = jnp.zeros((1, nb_lanes), F32)
        for r0 in range(0, t, chunk):
            rows = pl.ds(r0, chunk)
            u = u_ref[rows, :]
            dyv = dy_ref[rows, :]
            dyb = dyv.astype(BF16)
            lr, li = gr[rows, :].astype(BF16), gi[rows, :].astype(BF16)
            dcr = dcr + lax.dot_general(xr[rows, :].astype(BF16), dyb, tn_dims, preferred_element_type=F32)
            dci = dci - lax.dot_general(xi[rows, :].astype(BF16), dyb, tn_dims, preferred_element_type=F32)
            dbr = dbr + lax.dot_general(u, lr, tn_dims, preferred_element_type=F32)
            dbi = dbi + lax.dot_general(u, li, tn_dims, preferred_element_type=F32)
            du = lax.dot_general(lr, bbr_ref[...], nt_dims, preferred_element_type=F32)
            du = du + lax.dot_general(li, bbi_ref[...], nt_dims, preferred_element_type=F32)
            du_ref[rows, :] = du + d_ref[...] * dyv
            dd = dd + jnp.sum(dyv * u.astype(F32), axis=0, keepdims=True)
        dcr_ref[...] = dcr
        dci_ref[...] = dci
        dbbr_ref[...] = dbr
        dbbi_ref[...] = dbi
        dd_ref[...] = dd

        first_row = lax.broadcasted_iota(jnp.int32, (TIME_TILE, state_lanes), 0) == 0

        def tile(i, carry):
            pr, pi, acc_r, acc_i = carry
            rows = pl.ds(pl.multiple_of(i * TIME_TILE, TIME_TILE), TIME_TILE)
            x_r, x_i, l_r, l_i = xr[rows, :], xi[rows, :], gr[rows, :], gi[rows, :]
            prev_r = jnp.where(first_row, pr, pltpu.roll(x_r, 1, 0))
            prev_i = jnp.where(first_row, pi, pltpu.roll(x_i, 1, 0))
            acc_r = acc_r + l_r * prev_r + l_i * prev_i
            acc_i = acc_i + l_i * prev_r - l_r * prev_i
            return x_r[TIME_TILE - 1:, :], x_i[TIME_TILE - 1:, :], acc_r, acc_i

        zrow = jnp.zeros((1, state_lanes), F32)
        ztile = jnp.zeros((TIME_TILE, state_lanes), F32)
        _, _, acc_r, acc_i = lax.fori_loop(0, t // TIME_TILE, tile, (zrow, zrow, ztile, ztile))
        dar_ref[...] = jnp.sum(acc_r, axis=0, keepdims=True)
        dai_ref[...] = jnp.sum(acc_i, axis=0, keepdims=True)

    tok, bb, cc, dvec, avec = _s5_specs(t, nb_lanes, state_lanes)
    return _call(
        body, name=name, grid=(nb,),
        in_specs=[tok, tok, bb, bb, cc, cc, dvec, avec, avec],
        out_specs=[tok, bb, bb, cc, cc, avec, avec, dvec],
        out_shape=[jax.ShapeDtypeStruct((t, w), F32)]
        + [jax.ShapeDtypeStruct((nb, nb_lanes, state_lanes), F32)] * 2
        + [jax.ShapeDtypeStruct((nb, state_lanes, nb_lanes), F32)] * 2
        + [jax.ShapeDtypeStruct((1, nb * state_lanes), F32)] * 2
        + [jax.ShapeDtypeStruct((1, w), F32)],
        scratch_shapes=[pltpu.VMEM((t, state_lanes), F32)] * 4,
        semantics=("parallel",), operands=(u5, dy, bbr, bbi, ccr, cci, dskip, ar, ai), job=job)


def _gelu(x):
    return 0.5 * x * (1.0 + jnp.tanh(GELU_C * (x + GELU_A * x * x * x)))


def _gelu_grad(x):
    th = jnp.tanh(GELU_C * (x + GELU_A * x * x * x))
    return 0.5 * (1.0 + th) + 0.5 * x * (1.0 - th * th) * GELU_C * (1.0 + 3.0 * GELU_A * x * x)


def _glu_fwd(y5, w, b, name):
    t, width = y5.shape
    tm = _tile(t, 512, 16)

    def body(y_ref, w_ref, b_ref, o_ref):
        g = _gelu(y_ref[...])
        a = jnp.dot(g.astype(BF16), w_ref[...], preferred_element_type=F32) + b_ref[...]
        o_ref[...] = (g * _sigmoid(a)).astype(o_ref.dtype)

    row = pl.BlockSpec((tm, width), lambda i: (i, 0))
    return pl.pallas_call(
        body, name=name, grid=(t // tm,),
        in_specs=[row, pl.BlockSpec((width, width), lambda i: (0, 0)), pl.BlockSpec((1, width), lambda i: (0, 0))],
        out_specs=row,
        out_shape=jax.ShapeDtypeStruct((t, width), BF16),
        compiler_params=_cp(("parallel",)),
    )(y5, w, b)


def _glu_bwd(y5, dout, w, b, name):
    t, width = y5.shape
    tm = _tile(t, 512, 16)

    def body(y_ref, do_ref, w_ref, b_ref, dy_ref, g_ref, da_ref, db_ref):
        y = y_ref[...]
        g = _gelu(y)
        s = _sigmoid(jnp.dot(g.astype(BF16), w_ref[...], preferred_element_type=F32) + b_ref[...])
        dout_v = do_ref[...].astype(F32)
        da = dout_v * g * s * (1.0 - s)
        dg = dout_v * s + lax.dot_general(da.astype(BF16), w_ref[...], (((1,), (1,)), ((), ())),
                                          preferred_element_type=F32)
        dy_ref[...] = dg * _gelu_grad(y)
        g_ref[...] = g.astype(g_ref.dtype)
        da_ref[...] = da.astype(da_ref.dtype)

        @pl.when(pl.program_id(0) == 0)
        def _():
            db_ref[...] = jnp.zeros_like(db_ref)

        db_ref[...] += jnp.sum(da, axis=0, keepdims=True)

    row = pl.BlockSpec((tm, width), lambda i: (i, 0))
    vec = pl.BlockSpec((1, width), lambda i: (0, 0))
    return pl.pallas_call(
        body, name=name, grid=(t // tm,),
        in_specs=[row, row, pl.BlockSpec((width, width), lambda i: (0, 0)), vec],
        out_specs=[row, row, row, vec],
        out_shape=[jax.ShapeDtypeStruct((t, width), F32), jax.ShapeDtypeStruct((t, width), BF16),
                   jax.ShapeDtypeStruct((t, width), BF16), jax.ShapeDtypeStruct((1, width), F32)],
        compiler_params=_cp(("arbitrary",)),
    )(y5, dout, w, b)


def _adamw(w, g, m, v):
    m = ADAM_B1 * m + (1.0 - ADAM_B1) * g
    v = ADAM_B2 * v + (1.0 - ADAM_B2) * (g * g)
    m_hat = m / (1.0 - ADAM_B1 ** ADAM_STEP)
    v_hat = v / (1.0 - ADAM_B2 ** ADAM_STEP)
    return -ADAM_LR * (m_hat / (jnp.sqrt(v_hat) + ADAM_EPS) + ADAM_WD * w), m, v


def _adamw_shard(w, m, v, sums, got, chip, name):
    rows, cols = w.shape
    wide = sums.shape[2]
    tr = _row_tile(rows, wide, target=2**20)

    def body(chip_ref, w_ref, m_ref, v_ref, s_ref, g0_ref, g1_ref, g2_ref, g_out, d_out, m_out, v_out):
        g = s_ref[...].astype(F32) + g0_ref[...].astype(F32) + g1_ref[...].astype(F32) + g2_ref[...].astype(F32)
        g = g[:, :cols]
        delta, m_new, v_new = _adamw(w_ref[...], g, m_ref[...], v_ref[...])
        g_out[...] = g
        d_out[...] = delta
        m_out[...] = m_new
        v_out[...] = v_new

    blk = pl.BlockSpec((tr, cols), lambda i, chip_ref: (i, 0))

    def part(k):
        return pl.BlockSpec((None, tr, wide), lambda i, chip_ref: (k, i, 0))

    return pl.pallas_call(
        body, name=name,
        grid_spec=pltpu.PrefetchScalarGridSpec(
            num_scalar_prefetch=1, grid=(rows // tr,),
            in_specs=[blk, blk, blk, pl.BlockSpec((None, tr, wide), lambda i, chip_ref: (chip_ref[0], i, 0)),
                      part(0), part(1), part(2)],
            out_specs=[blk] * 4),
        out_shape=[jax.ShapeDtypeStruct((rows, cols), F32)] * 4,
        compiler_params=_cp(("parallel",)),
    )(chip, w, m, v, sums, got, got, got)


def _adamw_packed(w, m, v, g, name):
    def body(w_ref, m_ref, v_ref, g_ref, d_out, m_out, v_out):
        delta, m_new, v_new = _adamw(w_ref[...], g_ref[...], m_ref[...], v_ref[...])
        d_out[...] = delta
        m_out[...] = m_new
        v_out[...] = v_new

    vm = pl.BlockSpec(memory_space=pltpu.VMEM)
    return pl.pallas_call(
        body, name=name, in_specs=[vm] * 4, out_specs=[vm] * 3,
        out_shape=[jax.ShapeDtypeStruct(w.shape, F32)] * 3,
        compiler_params=_cp(),
    )(w, m, v, g)


WEIGHTS = ("g_mix", "w_in", "b_fgate", "b_gates", "q_norm", "k_norm", "s5_lambda_re", "s5_lambda_im", "s5_log_step",
           "s5_b_re", "s5_b_im", "s5_c_re", "s5_c_im", "s5_d", "w_glu", "b_glu", "w_proj_fox", "w_proj_s5", "w_out",
           "g_ffn", "w_gate_up", "w_down")
COLUMN_SHARDED = ("w_in", "w_proj_fox", "w_proj_s5", "w_gate_up")
ROW_SHARDED = ("w_glu", "w_out", "w_down")
PACK_ROWS = 8 * LANES

def _pack(arrays):
    flat = jnp.concatenate([a.reshape(-1).astype(F32) for a in arrays])
    flat = jnp.pad(flat, (0, (-flat.shape[0]) % PACK_ROWS))
    return flat.reshape(-1, LANES)


def _unpack(packed, like):
    flat, out, at = packed.reshape(-1), [], 0
    for a in like:
        out.append(flat[at:at + a.size].reshape(a.shape))
        at += a.size
    return out


def _split_rows(rows, shares, unit=16):
    out, first = [], 0
    for k, share in enumerate(shares):
        count = rows - first if k == len(shares) - 1 else max(unit, int(rows * share) // unit * unit)
        out.append((first, count))
        first += count
    assert first == rows and all(c > 0 for _, c in out), (rows, out)
    return out


def _pad_lanes(a):
    return jnp.pad(a, ((0, 0), (0, LANES - a.shape[1])))


def kernel(x, g_mix, w_in, b_fgate, b_gates, q_norm, k_norm, s5_lambda_re, s5_lambda_im, s5_log_step, s5_b_re, s5_b_im,
           s5_c_re, s5_c_im, s5_d, w_glu, b_glu, w_proj_fox, w_proj_s5, w_out, g_ffn, w_gate_up, w_down,
           loss_target, m_g_mix, m_w_in, m_b_fgate, m_b_gates, m_q_norm, m_k_norm, m_s5_lambda_re,
           m_s5_lambda_im, m_s5_log_step, m_s5_b_re, m_s5_b_im, m_s5_c_re, m_s5_c_im, m_s5_d, m_w_glu,
           m_b_glu, m_w_proj_fox, m_w_proj_s5, m_w_out, m_g_ffn, m_w_gate_up, m_w_down, v_g_mix, v_w_in,
           v_b_fgate, v_b_gates, v_q_norm, v_k_norm, v_s5_lambda_re, v_s5_lambda_im, v_s5_log_step, v_s5_b_re,
           v_s5_b_im, v_s5_c_re, v_s5_c_im, v_s5_d, v_w_glu, v_b_glu, v_w_proj_fox, v_w_proj_s5, v_w_out,
           v_g_ffn, v_w_gate_up, v_w_down):
    given = dict(locals())
    weights = {n: given[n] for n in WEIGHTS}
    mom_m = {n: given["m_" + n] for n in WEIGHTS}
    mom_v = {n: given["v_" + n] for n in WEIGHTS}

    pos_x, pos_y, pos_c = _position()
    core = jnp.reshape(pos_c, (1,)).astype(jnp.int32)
    chip = jnp.reshape(2 * pos_x + pos_y, (1,)).astype(jnp.int32)

    xs, target = x[0], loss_target[0]
    t, d = xs.shape
    heads, dh = b_fgate.shape[-1], q_norm.shape[-1]
    fw = heads * dh
    groups, states, gwidth = s5_b_re.shape[1:]
    sw = groups * gwidth
    gp = groups * states
    assert dh == LANES and sw % LANES == 0 and LANES % gwidth == 0
    col_v, col_f, col_s5 = 3 * fw, 3 * fw + heads, 3 * fw + heads + sw

    shard = {n: weights[n][0].astype(BF16) for n in COLUMN_SHARDED + ROW_SHARDED}

    def whole(n, ag):
        if n in COLUMN_SHARDED:
            return ag.transpose(1, 0, 2).reshape(ag.shape[1], N_DEV * ag.shape[2])
        return ag.reshape(N_DEV * ag.shape[1], ag.shape[2])

    full = {}
    c_gu, r_dn = w_gate_up.shape[2], w_down.shape[1]
    assert c_gu == 2 * r_dn
    cp_gu = -(-c_gu // LANES) * LANES
    shard["w_gate_up"] = jnp.pad(shard["w_gate_up"], ((0, 0), (0, cp_gu - c_gu)))

    def down_rows(ag):
        gap = [jnp.zeros((cp_gu - c_gu, d), ag.dtype)] if cp_gu > c_gu else []
        return jnp.concatenate([p for b in range(N_DEV // 2) for p in [ag[2 * b], ag[2 * b + 1]] + gap], axis=0)

    c_in = w_in.shape[2]
    in_cols = N_DEV * c_in

    def in_pieces(lo, hi, take):
        cuts = [(k, max(lo, k * c_in), min(hi, (k + 1) * c_in)) for k in range(N_DEV)]
        return [take(k, a - k * c_in, b - k * c_in) for k, a, b in cuts if a < b]

    ag_in = _all_gather(shard["w_in"], "ag_w_in")

    def from_gathered(k, a, b):
        return ag_in[k][:, a:b]

    w_main = jnp.concatenate(in_pieces(0, col_v, from_gathered) + in_pieces(col_f, in_cols, from_gathered), axis=1)
    w_forget = _pad_lanes(jnp.concatenate(in_pieces(col_v, col_f, from_gathered), axis=1))
    z_s5, z_gate = 3 * fw, 3 * fw + sw

    u, r_mix = _rms_fwd(xs, g_mix, "rms_mix")
    early = ("w_proj_fox", "w_proj_s5", "w_glu", "w_out")
    gu_rows = _split_rows(d, (0.62, 0.17, 0.06, 0.15))

    def gate_up_piece(k, so_far):
        return _GatherJob([shard["w_gate_up"]], rows=gu_rows[k], into=so_far)

    z, gu_buf = _mm(u, w_main, "nn", BF16, "mm_z", job=gate_up_piece(0, None))
    zf = _mm(u, w_forget, "nn", F32, "mm_zf")
    qn, kn, r_q, r_k = _qk_prep(z, heads, dh, q_norm, k_norm, "qk_prep")
    b_forget = _pad_lanes(b_fgate)
    cum = _forget_fwd(zf, b_forget, "forget_fwd")
    cum_t = cum[:, :heads].T
    fcol, frow = cum_t[:, :, None], cum_t[:, None, :]
    (attn, lse), got = _attn_fwd(qn, kn, z, 2 * heads, fcol, frow, heads, dh, "attn_fwd",
                                 job=_GatherJob([shard[n] for n in early]))
    for n, ag in zip(early, got):
        full[n] = whole(n, ag)

    lam_re, lam_im = s5_lambda_re.reshape(gp, 1), s5_lambda_im.reshape(gp, 1)
    log_step = jnp.repeat(s5_log_step.reshape(groups, 1), states, axis=1).reshape(gp, 1)
    b_re, b_im = s5_b_re.reshape(gp, gwidth), s5_b_im.reshape(gp, gwidth)
    lb_re, lb_im, bb_re, bb_im = _s5_prep(lam_re, lam_im, log_step, b_re, b_im, "s5_prep")
    nb, per = sw // LANES, LANES // gwidth
    eye = jnp.eye(per, dtype=F32)

    def diag_b(bb):
        return jnp.einsum("napi,ab->naibp", bb.reshape(nb, per, states, gwidth), eye).reshape(nb, LANES, per * states)

    def diag_c(c):
        return jnp.einsum("naip,ab->nbpai", c.reshape(nb, per, gwidth, states), eye).reshape(nb, per * states, LANES)

    def undiag_b(g):
        return jnp.einsum("naibp,ab->napi", g.reshape(nb, per, gwidth, per, states), eye).reshape(gp, gwidth)

    def undiag_c(g):
        return jnp.einsum("nbpai,ab->naip", g.reshape(nb, per, states, per, gwidth), eye).reshape(1, groups, gwidth, states)

    bbr, bbi = diag_b(bb_re).astype(BF16), diag_b(bb_im).astype(BF16)
    ccr, cci = diag_c(s5_c_re[0]).astype(BF16), diag_c(s5_c_im[0]).astype(BF16)
    a_re, a_im = lb_re.reshape(1, gp), lb_im.reshape(1, gp)
    d_skip = s5_d.reshape(1, sw)
    u5 = z[:, z_s5:z_s5 + sw]
    (y5,), gu_buf = _s5_fwd(u5, bbr, bbi, ccr, cci, d_skip, a_re, a_im, "s5_fwd", job=gate_up_piece(1, gu_buf))
    ssm = _glu_fwd(y5, full["w_glu"], b_glu, "glu_fwd")

    pf, gu_buf = _mm(attn, full["w_proj_fox"], "nn", BF16, "mm_pf", job=gate_up_piece(2, gu_buf))
    ps = _mm(ssm, full["w_proj_s5"], "nn", BF16, "mm_ps")
    merged = _gate_merge_fwd(z, z_gate, b_gates, pf, ps, "merge_fwd")
    mo, gu_buf = _mm(merged, full["w_out"], "nn", F32, "mm_out", job=gate_up_piece(3, gu_buf))
    full["w_gate_up"] = gu_buf[0]
    h, hn, r_ffn = _resid_rms(xs, mo, g_ffn, "resid_rms")
    gu, got = _mm(hn, full["w_gate_up"], "nn", BF16, "mm_gu", job=_GatherJob([shard["w_down"]]), b_stacked=True)
    full["w_down"] = down_rows(got[0])
    act = _swiglu_fwd(gu, "swiglu_fwd")
    dn = _mm(act, full["w_down"], "nn", F32, "mm_down")
    loss_blk, dy, dy_b = _loss_head(h, dn, target, "loss_head")
    loss = lax.psum(loss_blk[0, 0], ("x", "y", "c"))

    grad, sums, from_chips = {}, {}, {}

    def pair_sums(n, parts=None):
        if parts is None:
            g_full = grad[n]
            if n in COLUMN_SHARDED:
                parts = g_full.reshape(g_full.shape[0], N_DEV, g_full.shape[1] // N_DEV).transpose(1, 0, 2)
            else:
                parts = g_full.reshape(N_DEV, g_full.shape[0] // N_DEV, g_full.shape[1])
        got = _swap_with_sibling(parts, "rs_sibling_" + n)
        return _add_sibling(parts, got, core, "rs_add_" + n)

    dact = _mm(dy_b, full["w_down"], "nt", BF16, "mm_dact")
    gw_down = _mm(act, dy_b, "tn", BF16, "mm_gw_down")
    parts = jnp.stack(
        [lax.slice_in_dim(gw_down, cp_gu * (k // 2) + r_dn * (k % 2), cp_gu * (k // 2) + r_dn * (k % 2) + r_dn)
         for k in range(N_DEV)])
    (dgate, dup), (got,) = _swiglu_bwd(gu, dact, "swiglu_bwd", job=_SiblingSwapJob([parts]))
    dgu = jnp.concatenate([dgate, dup], axis=1)
    sums["w_down"] = _add_sibling(parts, got, core, "rs_add_w_down")
    dhn, (from_chips["w_down"],) = _mm(dgu, full["w_gate_up"], "nt", F32, "mm_dhn", job=_ChipSwapJob([sums["w_down"]]),
                                       b_stacked=True)
    parts = _mm(hn, dgu, "tn", BF16, "mm_gw_gu", out_stack=N_DEV)
    dh_, dh_b, grad["g_ffn"] = _rms_bwd([dhn], h, r_ffn, g_ffn, [dy], "rms_ffn_bwd")
    dmerged, (got,) = _mm(dh_b, full["w_out"], "nt", BF16, "mm_dmerged", job=_SiblingSwapJob([parts]))
    sums["w_gate_up"] = _add_sibling(parts, got, core, "rs_add_w_gate_up")
    grad["w_out"] = _mm(merged, dh_b, "tn", BF16, "mm_gw_out")
    dpf, dps, dz_gf, dz_gs, db_gf, db_gs = _gate_merge_bwd(dmerged, z, z_gate, b_gates, pf, ps, "merge_bwd")
    grad["b_gates"] = jnp.concatenate([db_gf, db_gs], axis=1)
    dattn = _mm(dpf, full["w_proj_fox"], "nt", BF16, "mm_dattn")
    grad["w_proj_fox"] = _mm(attn, dpf, "tn", BF16, "mm_gw_pf")
    dssm = _mm(dps, full["w_proj_s5"], "nt", BF16, "mm_dssm")
    grad["w_proj_s5"] = _mm(ssm, dps, "tn", BF16, "mm_gw_ps")

    dy5, g5, da5, grad["b_glu"] = _glu_bwd(y5, dssm, full["w_glu"], b_glu, "glu_bwd")
    grad["w_glu"] = _mm(g5, da5, "tn", BF16, "mm_gw_glu")
    for n in early:
        sums[n] = pair_sums(n)
    (du5, d_bbr, d_bbi, d_ccr, d_cci, d_are, d_aim, d_dskip), got = _s5_bwd(
        u5, dy5, bbr, bbi, ccr, cci, d_skip, a_re, a_im, "s5_bwd", job=_ChipSwapJob([sums[n] for n in early]))
    from_chips.update(zip(early, got))
    d_lre, d_lim, d_lstep, d_bre, d_bim = _s5_prep_bwd(
        lam_re, lam_im, log_step, b_re, b_im, d_are.reshape(gp, 1), d_aim.reshape(gp, 1),
        undiag_b(d_bbr), undiag_b(d_bbi), groups, "s5_prep_bwd")
    grad["s5_lambda_re"], grad["s5_lambda_im"] = d_lre.reshape(1, groups, states), d_lim.reshape(1, groups, states)
    grad["s5_log_step"] = d_lstep
    grad["s5_b_re"], grad["s5_b_im"] = d_bre.reshape(s5_b_re.shape), d_bim.reshape(s5_b_im.shape)
    grad["s5_c_re"], grad["s5_c_im"] = undiag_c(d_ccr), undiag_c(d_cci)
    grad["s5_d"] = d_dskip.reshape(s5_d.shape)

    delta = _attn_delta(attn, dattn, heads, dh, "attn_delta")
    swap_rows = _split_rows(d, (0.36, 0.5, 0.14))

    def gate_up_swap(k, so_far):
        return _ChipSwapJob([sums["w_gate_up"]], rows=swap_rows[k], into=so_far)

    (dqn, df_q), got = _attn_bwd_q(qn, kn, z, 2 * heads, dattn, fcol, frow, lse, delta, heads, dh, "attn_bwd_q",
                              job=gate_up_swap(0, None))
    (dkn, dv, df_k), got = _attn_bwd_kv(qn, kn, z, 2 * heads, dattn, fcol, frow, lse, delta, heads, dh, "attn_bwd_kv",
                                        job=gate_up_swap(1, got))
    dq, dk, grad["q_norm"], grad["k_norm"] = _qk_prep_bwd(dqn, dkn, z, heads, dh, q_norm, k_norm, r_q, r_k, "qk_prep_bwd")
    dzf, db_forget = _forget_bwd(_pad_lanes(df_q[:, :, 0].T), _pad_lanes(df_k[:, 0, :].T), zf, b_forget, "forget_bwd")
    grad["b_fgate"] = db_forget[:, :heads]

    dz = jnp.concatenate([dq, dk, dv, du5.astype(BF16), dz_gf, dz_gs], axis=1)
    gw_main, (from_chips["w_gate_up"],) = _mm(u, dz, "tn", BF16, "mm_gw_main", job=gate_up_swap(2, got))
    gw_forget = _mm(u, dzf, "tn", BF16, "mm_gw_forget")

    def from_grads(k, a, b):
        lo, hi = k * c_in + a, k * c_in + b
        if hi <= col_v:
            return gw_main[:, lo:hi]
        if hi <= col_f:
            return gw_forget[:, lo - col_v:hi - col_v]
        return gw_main[:, lo - heads:hi - heads]

    def in_part(k):
        lo, hi = k * c_in, (k + 1) * c_in
        cuts = [(max(lo, a), min(hi, b)) for a, b in ((0, col_v), (col_v, col_f), (col_f, in_cols))]
        return jnp.concatenate([from_grads(k, a - lo, b - lo) for a, b in cuts if a < b], axis=1)

    sums["w_in"] = pair_sums("w_in", jnp.stack([in_part(k) for k in range(N_DEV)]))
    in_rows = _split_rows(d, (0.8, 0.2))
    du, got = _mm(dz, w_main, "nt", F32, "mm_du", job=_ChipSwapJob([sums["w_in"]], rows=in_rows[0]))
    du_f = _mm(dzf, w_forget, "nt", F32, "mm_du_f")
    (dx, _, grad["g_mix"]), (from_chips["w_in"],) = _rms_bwd(
        [du, du_f], xs, r_mix, g_mix, [dh_], "rms_mix_bwd", job=_ChipSwapJob([sums["w_in"]], rows=in_rows[1], into=got))

    out_g, out_d, out_m, out_v = {}, {}, {}, {}
    for n in COLUMN_SHARDED + ROW_SHARDED:
        res = _adamw_shard(weights[n][0], mom_m[n][0], mom_v[n][0], sums[n], from_chips[n], chip, "adamw_" + n)
        out_g[n], out_d[n], out_m[n], out_v[n] = (r[None] for r in res)

    small = [n for n in WEIGHTS if n not in COLUMN_SHARDED + ROW_SHARDED]
    g_small = _all_reduce_small(_pack([grad[n] for n in small]), "ar_small")
    like = [weights[n] for n in small]
    res = _adamw_packed(_pack(like), _pack([mom_m[n] for n in small]), _pack([mom_v[n] for n in small]), g_small,
                        "adamw_small")
    for store, packed in zip((out_g, out_d, out_m, out_v), (g_small, *res)):
        for n, a in zip(small, _unpack(packed, like)):
            store[n] = a

    return (loss, dx[None], *[out_g[n] for n in WEIGHTS], *[out_d[n] for n in WEIGHTS],
            *[out_m[n] for n in WEIGHTS], *[out_v[n] for n in WEIGHTS])
```
